```python
import jax, jax.numpy as jnp
from jax import lax
import numpy as np

D_MODEL = 2048
BATCH = 8
SEQ = 4096
DEPTH = 2

N_A_LAYERS = DEPTH // 2
N_B_LAYERS = DEPTH - N_A_LAYERS
A_HEADS = 16
A_KEY_DIM = D_MODEL // A_HEADS
A_VAL_DIM = D_MODEL // A_HEADS
A_CHUNK = 16
B_HEADS = 16
B_KV_HEADS = 4
B_HEAD_DIM = D_MODEL // B_HEADS
B_GROUP = B_HEADS // B_KV_HEADS
Q_BLOCK = 128
D_FF = 4 * D_MODEL
NORM_EPS = 1e-6

kernel_name = 'yoco_hgrn2_stickbreaking_adaln_block'


def rms_norm(x, gain):
    xf = x.astype(jnp.float32)
    inv = lax.rsqrt(jnp.mean(xf * xf, axis=-1, keepdims=True) + NORM_EPS)
    return (xf * inv).astype(x.dtype) * gain


def modulate(h, shift, scale):
    return h * (1 + scale[:, None, :]) + shift[:, None, :]


def squared_relu_mlp(h, w1, w2):
    a = jax.nn.relu(h @ w1)
    return (a * a) @ w2


def hgrn2_mixer(h, w_in, lb, out_gain, w_out):
    b, s, _ = h.shape
    proj = h @ w_in
    q, f_logit, i_in, g = jnp.split(proj, 4, axis=-1)
    q = jax.nn.silu(q.astype(jnp.float32))
    f_logit = f_logit.astype(jnp.float32)
    f = lb + (1 - lb) * jax.nn.sigmoid(f_logit)
    log_f = jnp.log(f)
    k = (1 - lb) * jax.nn.sigmoid(-f_logit)
    v = i_in.astype(jnp.float32)
    nc = s // A_CHUNK
    shp = (b, nc, A_CHUNK, A_HEADS, A_KEY_DIM)
    q = q.reshape(shp)
    k = k.reshape(shp)
    log_f = log_f.reshape(shp)
    v = v.reshape(b, nc, A_CHUNK, A_HEADS, A_VAL_DIM)
    cum = jnp.cumsum(log_f, axis=2)
    q_dec = q * jnp.exp(cum)
    k_intra = k * jnp.exp(-cum)
    k_state = k * jnp.exp(cum[:, :, -1:] - cum)
    chunk_decay = jnp.exp(cum[:, :, -1])
    causal = jnp.tril(jnp.ones((A_CHUNK, A_CHUNK), dtype=bool))
    scores = jnp.einsum('bnthk,bnshk->bnhts', q_dec, k_intra)
    scores = jnp.where(causal, scores, 0.0)
    o_intra = jnp.einsum('bnhts,bnshv->bnthv', scores, v)

    def step(state, inp):
        q_c, k_c, v_c, dec_c = inp
        o_c = jnp.einsum('bthk,bhkv->bthv', q_c, state)
        state = dec_c[..., None] * state + jnp.einsum('bthk,bthv->bhkv', k_c, v_c)
        return state, o_c

    xs = (jnp.moveaxis(q_dec, 1, 0), jnp.moveaxis(k_state, 1, 0),
          jnp.moveaxis(v, 1, 0), jnp.moveaxis(chunk_decay, 1, 0))
    s0 = jnp.zeros((b, A_HEADS, A_KEY_DIM, A_VAL_DIM), jnp.float32)
    _, o_inter = lax.scan(step, s0, xs)
    o = (o_intra + jnp.moveaxis(o_inter, 0, 1)).reshape(b, s, A_HEADS, A_VAL_DIM)
    o = o * lax.rsqrt(jnp.mean(o * o, axis=-1, keepdims=True) + NORM_EPS)
    o = o.reshape(b, s, A_HEADS * A_VAL_DIM) * out_gain
    o = o * jax.nn.silu(g.astype(jnp.float32))
    return o.astype(h.dtype) @ w_out


def stick_breaking_mixer(h, w_q, k, v, w_out):
    b, s, _ = h.shape
    q = (h @ w_q).reshape(b, s, B_KV_HEADS, B_GROUP, B_HEAD_DIM)
    scale = B_HEAD_DIM ** -0.5
    outs = []
    for blk in range(s // Q_BLOCK):
        t0 = blk * Q_BLOCK
        t1 = t0 + Q_BLOCK
        q_blk = q[:, t0:t1]
        k_pre = k[:, :t1]
        v_pre = v[:, :t1]
        z = jnp.einsum('btkgd,bskd->bkgts', q_blk, k_pre).astype(jnp.float32) * scale
        t_pos = t0 + jnp.arange(Q_BLOCK)[:, None]
        s_pos = jnp.arange(t1)[None, :]
        mask = s_pos < t_pos
        log_beta = jax.nn.log_sigmoid(z)
        log_rest = jnp.where(mask, log_beta - z, 0.0)
        between = lax.cumsum(log_rest, axis=4, reverse=True) - log_rest
        weights = jnp.where(mask, jnp.exp(log_beta + between), 0.0)
        outs.append(jnp.einsum('bkgts,bskd->btkgd', weights.astype(v.dtype), v_pre))
    o = jnp.concatenate(outs, axis=1).reshape(b, s, B_HEADS * B_HEAD_DIM)
    return o @ w_out


def _fwd_setup_inputs(seed: int = 0) -> dict:
    key = jax.random.key(seed)
    ks = jax.random.split(key, 20)
    d = D_MODEL
    f32 = jnp.float32

    def nrm(k, shape, fan_in, gain=1.0):
        return jax.random.normal(k, shape, f32) * (gain * fan_in ** -0.5)

    def small(k, shape):
        return 0.02 * jax.random.normal(k, shape, f32)

    return {
        'x': jax.random.normal(ks[0], (BATCH, SEQ, d), f32),
        'c': jax.random.normal(ks[1], (BATCH, d), f32),
        'ada_w': nrm(ks[2], (DEPTH, d, 6 * d), d, 0.5),
        'ada_b': small(ks[3], (DEPTH, 6 * d)),
        'norm_mix': 1.0 + small(ks[4], (DEPTH, d)),
        'norm_mlp': 1.0 + small(ks[5], (DEPTH, d)),
        'a_w_in': nrm(ks[6], (N_A_LAYERS, d, 4 * d), d),
        'a_lb_logits': 0.1 * jax.random.normal(ks[7], (N_A_LAYERS + 1, A_HEADS * A_KEY_DIM), f32),
        'a_out_gain': 1.0 + small(ks[8], (N_A_LAYERS, A_HEADS * A_VAL_DIM)),
        'a_w_out': nrm(ks[9], (N_A_LAYERS, A_HEADS * A_VAL_DIM, d), A_HEADS * A_VAL_DIM),
        'kv_ada_w': nrm(ks[10], (d, 2 * d), d, 0.5),
        'kv_ada_b': small(ks[11], (2 * d,)),
        'kv_norm': 1.0 + small(ks[12], (d,)),
        'w_kv': nrm(ks[13], (d, 2 * B_KV_HEADS * B_HEAD_DIM), d),
        'b_w_q': nrm(ks[14], (N_B_LAYERS, d, B_HEADS * B_HEAD_DIM), d),
        'b_w_out': nrm(ks[15], (N_B_LAYERS, B_HEADS * B_HEAD_DIM, d), B_HEADS * B_HEAD_DIM),
        'mlp_w1': nrm(ks[16], (DEPTH, d, D_FF), d),
        'mlp_w2': nrm(ks[17], (DEPTH, D_FF, d), D_FF),
        'final_norm': 1.0 + small(ks[18], (d,)),
    }


def _fwd_reference(x, c, ada_w, ada_b, norm_mix, norm_mlp, a_w_in, a_lb_logits, a_out_gain, a_w_out,
              kv_ada_w, kv_ada_b, kv_norm, w_kv, b_w_q, b_w_out, mlp_w1, mlp_w2, final_norm):
    b, s, _ = x.shape
    c_act = jax.nn.silu(c)
    lb_all = jnp.cumsum(jax.nn.softmax(a_lb_logits.astype(jnp.float32), axis=0), axis=0)
    k_shared = None
    v_shared = None
    for layer in range(DEPTH):
        mod = c_act @ ada_w[layer] + ada_b[layer]
        sh1, sc1, g1, sh2, sc2, g2 = jnp.split(mod, 6, axis=-1)
        h = modulate(rms_norm(x, norm_mix[layer]), sh1, sc1)
        if layer < N_A_LAYERS:
            y = hgrn2_mixer(h, a_w_in[layer], lb_all[layer], a_out_gain[layer], a_w_out[layer])
        else:
            j = layer - N_A_LAYERS
            y = stick_breaking_mixer(h, b_w_q[j], k_shared, v_shared, b_w_out[j])
        x = x + g1[:, None, :] * y
        h = modulate(rms_norm(x, norm_mlp[layer]), sh2, sc2)
        x = x + g2[:, None, :] * squared_relu_mlp(h, mlp_w1[layer], mlp_w2[layer])
        if layer == N_A_LAYERS - 1:
            kv_sh, kv_sc = jnp.split(c_act @ kv_ada_w + kv_ada_b, 2, axis=-1)
            hk = modulate(rms_norm(x, kv_norm), kv_sh, kv_sc)
            kv = (hk @ w_kv).reshape(b, s, 2, B_KV_HEADS, B_HEAD_DIM)
            k_shared = kv[:, :, 0]
            v_shared = kv[:, :, 1]
    return rms_norm(x, final_norm)


import jax as _jax
import jax.numpy as _jnp

TWIN_FORMAT = 'train_step'
FWD_PARAMS = ['x', 'c', 'ada_w', 'ada_b', 'norm_mix', 'norm_mlp', 'a_w_in', 'a_lb_logits', 'a_out_gain', 'a_w_out', 'kv_ada_w', 'kv_ada_b', 'kv_norm', 'w_kv', 'b_w_q', 'b_w_out', 'mlp_w1', 'mlp_w2', 'final_norm']
TWIN_WEIGHTS = ['ada_w', 'ada_b', 'norm_mix', 'norm_mlp', 'a_w_in', 'a_lb_logits', 'a_out_gain', 'a_w_out', 'kv_ada_w', 'kv_ada_b', 'kv_norm', 'w_kv', 'b_w_q', 'b_w_out', 'mlp_w1', 'mlp_w2', 'final_norm']
TWIN_DIFF_INPUT = 'x'
TWIN_INPUTS = ['x', 'c', 'ada_w', 'ada_b', 'norm_mix', 'norm_mlp', 'a_w_in', 'a_lb_logits', 'a_out_gain', 'a_w_out', 'kv_ada_w', 'kv_ada_b', 'kv_norm', 'w_kv', 'b_w_q', 'b_w_out', 'mlp_w1', 'mlp_w2', 'final_norm', 'loss_target', 'm_ada_w', 'm_ada_b', 'm_norm_mix', 'm_norm_mlp', 'm_a_w_in', 'm_a_lb_logits', 'm_a_out_gain', 'm_a_w_out', 'm_kv_ada_w', 'm_kv_ada_b', 'm_kv_norm', 'm_w_kv', 'm_b_w_q', 'm_b_w_out', 'm_mlp_w1', 'm_mlp_w2', 'm_final_norm', 'v_ada_w', 'v_ada_b', 'v_norm_mix', 'v_norm_mlp', 'v_a_w_in', 'v_a_lb_logits', 'v_a_out_gain', 'v_a_w_out', 'v_kv_ada_w', 'v_kv_ada_b', 'v_kv_norm', 'v_w_kv', 'v_b_w_q', 'v_b_w_out', 'v_mlp_w1', 'v_mlp_w2', 'v_final_norm']
TWIN_OUTPUTS = ['loss', 'grad_x', 'grad_ada_w', 'grad_ada_b', 'grad_norm_mix', 'grad_norm_mlp', 'grad_a_w_in', 'grad_a_lb_logits', 'grad_a_out_gain', 'grad_a_w_out', 'grad_kv_ada_w', 'grad_kv_ada_b', 'grad_kv_norm', 'grad_w_kv', 'grad_b_w_q', 'grad_b_w_out', 'grad_mlp_w1', 'grad_mlp_w2', 'grad_final_norm', 'delta_ada_w', 'delta_ada_b', 'delta_norm_mix', 'delta_norm_mlp', 'delta_a_w_in', 'delta_a_lb_logits', 'delta_a_out_gain', 'delta_a_w_out', 'delta_kv_ada_w', 'delta_kv_ada_b', 'delta_kv_norm', 'delta_w_kv', 'delta_b_w_q', 'delta_b_w_out', 'delta_mlp_w1', 'delta_mlp_w2', 'delta_final_norm', 'new_m_ada_w', 'new_m_ada_b', 'new_m_norm_mix', 'new_m_norm_mlp', 'new_m_a_w_in', 'new_m_a_lb_logits', 'new_m_a_out_gain', 'new_m_a_w_out', 'new_m_kv_ada_w', 'new_m_kv_ada_b', 'new_m_kv_norm', 'new_m_w_kv', 'new_m_b_w_q', 'new_m_b_w_out', 'new_m_mlp_w1', 'new_m_mlp_w2', 'new_m_final_norm', 'new_v_ada_w', 'new_v_ada_b', 'new_v_norm_mix', 'new_v_norm_mlp', 'new_v_a_w_in', 'new_v_a_lb_logits', 'new_v_a_out_gain', 'new_v_a_w_out', 'new_v_kv_ada_w', 'new_v_kv_ada_b', 'new_v_kv_norm', 'new_v_w_kv', 'new_v_b_w_q', 'new_v_b_w_out', 'new_v_mlp_w1', 'new_v_mlp_w2', 'new_v_final_norm']
TWIN_LEAF_KINDS = {'loss': 'loss', 'grad_x': 'grad_x', 'grad_ada_w': 'grad_w', 'grad_ada_b': 'grad_w', 'grad_norm_mix': 'grad_w', 'grad_norm_mlp': 'grad_w', 'grad_a_w_in': 'grad_w', 'grad_a_lb_logits': 'grad_w', 'grad_a_out_gain': 'grad_w', 'grad_a_w_out': 'grad_w', 'grad_kv_ada_w': 'grad_w', 'grad_kv_ada_b': 'grad_w', 'grad_kv_norm': 'grad_w', 'grad_w_kv': 'grad_w', 'grad_b_w_q': 'grad_w', 'grad_b_w_out': 'grad_w', 'grad_mlp_w1': 'grad_w', 'grad_mlp_w2': 'grad_w', 'grad_final_norm': 'grad_w', 'delta_ada_w': 'delta_w', 'delta_ada_b': 'delta_w', 'delta_norm_mix': 'delta_w', 'delta_norm_mlp': 'delta_w', 'delta_a_w_in': 'delta_w', 'delta_a_lb_logits': 'delta_w', 'delta_a_out_gain': 'delta_w', 'delta_a_w_out': 'delta_w', 'delta_kv_ada_w': 'delta_w', 'delta_kv_ada_b': 'delta_w', 'delta_kv_norm': 'delta_w', 'delta_w_kv': 'delta_w', 'delta_b_w_q': 'delta_w', 'delta_b_w_out': 'delta_w', 'delta_mlp_w1': 'delta_w', 'delta_mlp_w2': 'delta_w', 'delta_final_norm': 'delta_w', 'new_m_ada_w': 'new_m', 'new_m_ada_b': 'new_m', 'new_m_norm_mix': 'new_m', 'new_m_norm_mlp': 'new_m', 'new_m_a_w_in': 'new_m', 'new_m_a_lb_logits': 'new_m', 'new_m_a_out_gain': 'new_m', 'new_m_a_w_out': 'new_m', 'new_m_kv_ada_w': 'new_m', 'new_m_kv_ada_b': 'new_m', 'new_m_kv_norm': 'new_m', 'new_m_w_kv': 'new_m', 'new_m_b_w_q': 'new_m', 'new_m_b_w_out': 'new_m', 'new_m_mlp_w1': 'new_m', 'new_m_mlp_w2': 'new_m', 'new_m_final_norm': 'new_m', 'new_v_ada_w': 'new_v', 'new_v_ada_b': 'new_v', 'new_v_norm_mix': 'new_v', 'new_v_norm_mlp': 'new_v', 'new_v_a_w_in': 'new_v', 'new_v_a_lb_logits': 'new_v', 'new_v_a_out_gain': 'new_v', 'new_v_a_w_out': 'new_v', 'new_v_kv_ada_w': 'new_v', 'new_v_kv_ada_b': 'new_v', 'new_v_kv_norm': 'new_v', 'new_v_w_kv': 'new_v', 'new_v_b_w_q': 'new_v', 'new_v_b_w_out': 'new_v', 'new_v_mlp_w1': 'new_v', 'new_v_mlp_w2': 'new_v', 'new_v_final_norm': 'new_v'}


def _forward(args):
    return _fwd_reference(*[args[k] for k in FWD_PARAMS])


def _output_shape():
    out = _jax.eval_shape(lambda: _forward(_fwd_setup_inputs(0)))
    return out.shape, out.dtype

N_MICROBATCH = 1
ADAM_LR = 0.001
ADAM_B1 = 0.9
ADAM_B2 = 0.999
ADAM_EPS = 1e-08
ADAM_WD = 0.01
ADAM_STEP = 10
PER_EXAMPLE_BATCH_AXIS = {'x': 0, 'c': 0, 'loss_target': 0}
SHARED_INPUTS = []
_WEIGHT_DTYPES = {'ada_w': _jnp.float32, 'ada_b': _jnp.float32, 'norm_mix': _jnp.float32, 'norm_mlp': _jnp.float32, 'a_w_in': _jnp.float32, 'a_lb_logits': _jnp.float32, 'a_out_gain': _jnp.float32, 'a_w_out': _jnp.float32, 'kv_ada_w': _jnp.float32, 'kv_ada_b': _jnp.float32, 'kv_norm': _jnp.float32, 'w_kv': _jnp.float32, 'b_w_q': _jnp.float32, 'b_w_out': _jnp.float32, 'mlp_w1': _jnp.float32, 'mlp_w2': _jnp.float32, 'final_norm': _jnp.float32}
MOMENT_SCALE = {'ada_w': 4.009111e-02, 'ada_b': 7.192483e-02, 'norm_mix': 1.754439e-02, 'norm_mlp': 4.019781e-02, 'a_w_in': 1.311472e-02, 'a_lb_logits': 1.546537e-03, 'a_out_gain': 1.846450e-02, 'a_w_out': 1.808401e-02, 'kv_ada_w': 1.333263e-02, 'kv_ada_b': 2.246580e-02, 'kv_norm': 1.846851e-02, 'w_kv': 2.888390e-02, 'b_w_q': 8.344489e-03, 'b_w_out': 1.913785e-02, 'mlp_w1': 1.968635e-02, 'mlp_w2': 3.558160e-02, 'final_norm': 1.616008e+01}


def _to_microbatches(a, axis):
    t = _jnp.moveaxis(a, axis, 0)
    t = t.reshape((N_MICROBATCH, t.shape[0] // N_MICROBATCH) + t.shape[1:])
    return _jnp.moveaxis(t, 1, axis + 1)


def setup_inputs(seed: int = 0) -> dict:
    inp = _fwd_setup_inputs(seed)
    key = _jax.random.fold_in(_jax.random.key(seed), 7919)
    shape, _ = _output_shape()
    out = dict(inp)
    out["loss_target"] = _jax.random.normal(_jax.random.fold_in(key, 0), shape, _jnp.float32)
    for i, name in enumerate(TWIN_WEIGHTS):
        w = inp[name].astype(_jnp.float32)
        if MOMENT_SCALE is None:
            s = _jnp.sqrt(_jnp.mean(_jnp.square(w)) + 1e-30)
        else:
            s = MOMENT_SCALE[name]
        km, kv = _jax.random.split(_jax.random.fold_in(key, i + 1))
        out[name] = w
        out["m_" + name] = s * _jax.random.normal(km, w.shape, _jnp.float32)
        out["v_" + name] = (s * s) * _jax.random.uniform(kv, w.shape, _jnp.float32, 0.5, 1.5)
    if N_MICROBATCH > 1:
        for name, axis in PER_EXAMPLE_BATCH_AXIS.items():
            out[name] = _to_microbatches(out[name], axis)
    return {'x': out['x'], 'c': out['c'], 'ada_w': out['ada_w'], 'ada_b': out['ada_b'], 'norm_mix': out['norm_mix'], 'norm_mlp': out['norm_mlp'], 'a_w_in': out['a_w_in'], 'a_lb_logits': out['a_lb_logits'], 'a_out_gain': out['a_out_gain'], 'a_w_out': out['a_w_out'], 'kv_ada_w': out['kv_ada_w'], 'kv_ada_b': out['kv_ada_b'], 'kv_norm': out['kv_norm'], 'w_kv': out['w_kv'], 'b_w_q': out['b_w_q'], 'b_w_out': out['b_w_out'], 'mlp_w1': out['mlp_w1'], 'mlp_w2': out['mlp_w2'], 'final_norm': out['final_norm'], 'loss_target': out['loss_target'], 'm_ada_w': out['m_ada_w'], 'm_ada_b': out['m_ada_b'], 'm_norm_mix': out['m_norm_mix'], 'm_norm_mlp': out['m_norm_mlp'], 'm_a_w_in': out['m_a_w_in'], 'm_a_lb_logits': out['m_a_lb_logits'], 'm_a_out_gain': out['m_a_out_gain'], 'm_a_w_out': out['m_a_w_out'], 'm_kv_ada_w': out['m_kv_ada_w'], 'm_kv_ada_b': out['m_kv_ada_b'], 'm_kv_norm': out['m_kv_norm'], 'm_w_kv': out['m_w_kv'], 'm_b_w_q': out['m_b_w_q'], 'm_b_w_out': out['m_b_w_out'], 'm_mlp_w1': out['m_mlp_w1'], 'm_mlp_w2': out['m_mlp_w2'], 'm_final_norm': out['m_final_norm'], 'v_ada_w': out['v_ada_w'], 'v_ada_b': out['v_ada_b'], 'v_norm_mix': out['v_norm_mix'], 'v_norm_mlp': out['v_norm_mlp'], 'v_a_w_in': out['v_a_w_in'], 'v_a_lb_logits': out['v_a_lb_logits'], 'v_a_out_gain': out['v_a_out_gain'], 'v_a_w_out': out['v_a_w_out'], 'v_kv_ada_w': out['v_kv_ada_w'], 'v_kv_ada_b': out['v_kv_ada_b'], 'v_kv_norm': out['v_kv_norm'], 'v_w_kv': out['v_w_kv'], 'v_b_w_q': out['v_b_w_q'], 'v_b_w_out': out['v_b_w_out'], 'v_mlp_w1': out['v_mlp_w1'], 'v_mlp_w2': out['v_mlp_w2'], 'v_final_norm': out['v_final_norm']}


def _loss(weights, diff, rest, loss_target):
    with _jax.named_scope("forward"):
        args = {**rest, TWIN_DIFF_INPUT: diff, **{k: w.astype(_WEIGHT_DTYPES[k]) for k, w in weights.items()}}
        y = _forward(args)
    with _jax.named_scope("loss_head"):
        err = _jnp.square(y.astype(_jnp.float32) - loss_target)
        return 0.5 * _jnp.sum(_jnp.mean(err, axis=-1)) if err.ndim else 0.5 * err


def _adamw(w, g, m, v):
    m = ADAM_B1 * m + (1.0 - ADAM_B1) * g
    v = ADAM_B2 * v + (1.0 - ADAM_B2) * _jnp.square(g)
    m_hat = m / (1.0 - ADAM_B1 ** ADAM_STEP)
    v_hat = v / (1.0 - ADAM_B2 ** ADAM_STEP)
    delta = -ADAM_LR * (m_hat / (_jnp.sqrt(v_hat) + ADAM_EPS) + ADAM_WD * w)
    return delta, m, v


def reference(x, c, ada_w, ada_b, norm_mix, norm_mlp, a_w_in, a_lb_logits, a_out_gain, a_w_out, kv_ada_w, kv_ada_b, kv_norm, w_kv, b_w_q, b_w_out, mlp_w1, mlp_w2, final_norm, loss_target, m_ada_w, m_ada_b, m_norm_mix, m_norm_mlp, m_a_w_in, m_a_lb_logits, m_a_out_gain, m_a_w_out, m_kv_ada_w, m_kv_ada_b, m_kv_norm, m_w_kv, m_b_w_q, m_b_w_out, m_mlp_w1, m_mlp_w2, m_final_norm, v_ada_w, v_ada_b, v_norm_mix, v_norm_mlp, v_a_w_in, v_a_lb_logits, v_a_out_gain, v_a_w_out, v_kv_ada_w, v_kv_ada_b, v_kv_norm, v_w_kv, v_b_w_q, v_b_w_out, v_mlp_w1, v_mlp_w2, v_final_norm):
    given = dict(x=x, c=c, ada_w=ada_w, ada_b=ada_b, norm_mix=norm_mix, norm_mlp=norm_mlp, a_w_in=a_w_in, a_lb_logits=a_lb_logits, a_out_gain=a_out_gain, a_w_out=a_w_out, kv_ada_w=kv_ada_w, kv_ada_b=kv_ada_b, kv_norm=kv_norm, w_kv=w_kv, b_w_q=b_w_q, b_w_out=b_w_out, mlp_w1=mlp_w1, mlp_w2=mlp_w2, final_norm=final_norm, loss_target=loss_target, m_ada_w=m_ada_w, m_ada_b=m_ada_b, m_norm_mix=m_norm_mix, m_norm_mlp=m_norm_mlp, m_a_w_in=m_a_w_in, m_a_lb_logits=m_a_lb_logits, m_a_out_gain=m_a_out_gain, m_a_w_out=m_a_w_out, m_kv_ada_w=m_kv_ada_w, m_kv_ada_b=m_kv_ada_b, m_kv_norm=m_kv_norm, m_w_kv=m_w_kv, m_b_w_q=m_b_w_q, m_b_w_out=m_b_w_out, m_mlp_w1=m_mlp_w1, m_mlp_w2=m_mlp_w2, m_final_norm=m_final_norm, v_ada_w=v_ada_w, v_ada_b=v_ada_b, v_norm_mix=v_norm_mix, v_norm_mlp=v_norm_mlp, v_a_w_in=v_a_w_in, v_a_lb_logits=v_a_lb_logits, v_a_out_gain=v_a_out_gain, v_a_w_out=v_a_w_out, v_kv_ada_w=v_kv_ada_w, v_kv_ada_b=v_kv_ada_b, v_kv_norm=v_kv_norm, v_w_kv=v_w_kv, v_b_w_q=v_b_w_q, v_b_w_out=v_b_w_out, v_mlp_w1=v_mlp_w1, v_mlp_w2=v_mlp_w2, v_final_norm=v_final_norm)
    weights = {n: given[n] for n in TWIN_WEIGHTS}
    shared = {n: given[n] for n in SHARED_INPUTS}
    per_example = {n: given[n] for n in ['x', 'c']}
    grad_fn = _jax.value_and_grad(_loss, argnums=(0, 1))

    def one_microbatch(ex, loss_target):
        ex = dict(ex)
        diff = ex.pop(TWIN_DIFF_INPUT)
        return grad_fn(weights, diff, {**shared, **ex}, loss_target)

    if N_MICROBATCH == 1:
        loss, (grad_w, grad_x) = one_microbatch(per_example, given["loss_target"])
    else:
        def body(carry, xs):
            loss_sum, grad_sum = carry
            l_k, (gw_k, gx_k) = one_microbatch(xs[0], xs[1])
            with _jax.named_scope("update"):
                return (loss_sum + l_k, _jax.tree.map(_jnp.add, grad_sum, gw_k)), gx_k

        init = (_jnp.zeros((), _jnp.float32), _jax.tree.map(_jnp.zeros_like, weights))
        (loss, grad_w), grad_x = _jax.lax.scan(body, init, (per_example, given["loss_target"]))
    with _jax.named_scope("update"):
        delta_w, new_m, new_v = {}, {}, {}
        for n in TWIN_WEIGHTS:
            delta_w[n], new_m[n], new_v[n] = _adamw(weights[n], grad_w[n], given["m_" + n], given["v_" + n])
    return (loss, grad_x, *[grad_w[n] for n in TWIN_WEIGHTS], *[delta_w[n] for n in TWIN_WEIGHTS],
            *[new_m[n] for n in TWIN_WEIGHTS], *[new_v[n] for n in TWIN_WEIGHTS])
```

```python
import functools

import jax
import jax.numpy as jnp
from jax import lax
from jax.experimental import pallas as pl
from jax.experimental.pallas import tpu as pltpu

F32 = jnp.float32
BF16 = jnp.bfloat16
MESH = pl.DeviceIdType.MESH

HEAD_DIM = 128
KV_GROUP = 4
HGRN_CHUNK = 64
NORM_EPS = 1e-6
N_CHIPS = 4
N_DEV = 8
ROW_TILE = 256
ATTN_TILE = 256
VMEM_LIMIT = 56 * 1024 * 1024

ADAM_LR = 0.001
ADAM_B1 = 0.9
ADAM_B2 = 0.999
ADAM_EPS = 1e-08
ADAM_WD = 0.01
ADAM_STEP = 10

NN = (((1,), (0,)), ((), ()))
NT = (((1,), (1,)), ((), ()))
TN = (((0,), (0,)), ((), ()))


def _dot(a, b, dims=NN, precision=None):
    return lax.dot_general(a, b, dims, preferred_element_type=F32, precision=precision)


def _bdot(a, b, dims=NN):
    return _dot(a.astype(BF16), b.astype(BF16), dims)


def _sigmoid(x):
    return 1.0 / (1.0 + jnp.exp(-x))


def _log_sigmoid(z):
    return jnp.minimum(z, 0.0) - jnp.log(1.0 + jnp.exp(-jnp.abs(z)))


def _split_bf16(x):
    hi = x.astype(BF16)
    lo = (x - hi.astype(F32)).astype(BF16)
    return hi, lo


def _params(sem=None):
    return pltpu.CompilerParams(dimension_semantics=sem, vmem_limit_bytes=VMEM_LIMIT)


def _tile(n, pref):
    t = min(n, pref)
    assert n % t == 0, (n, pref)
    return t


def _mm(name, a, b, a_spec, b_spec, grid, n_red, dims, out_shapes, out_specs,
        acc_shape, epilogue=None, extras=(), extra_specs=()):
    n_extra = len(extras)
    n_out = len(out_shapes)
    if epilogue is None:
        epilogue = lambda acc: (acc,)

    def body(*refs):
        a_ref, b_ref = refs[:2]
        ex_refs = refs[2:2 + n_extra]
        out_refs = refs[2 + n_extra:2 + n_extra + n_out]
        prod = _bdot(a_ref[...], b_ref[...], dims)

        def finish(acc):
            res = epilogue(acc, *[e[...] for e in ex_refs])
            for o_ref, r in zip(out_refs, res):
                o_ref[...] = r.astype(o_ref.dtype)

        if n_red == 0:
            finish(prod)
            return
        acc_ref = refs[-1]
        ids = [pl.program_id(len(grid) - n_red + r) for r in range(n_red)]
        sizes = grid[len(grid) - n_red:]
        first = functools.reduce(jnp.logical_and, [i == 0 for i in ids])
        last = functools.reduce(jnp.logical_and, [i == s - 1 for i, s in zip(ids, sizes)])

        @pl.when(first)
        def _():
            acc_ref[...] = prod

        @pl.when(jnp.logical_not(first))
        def _():
            acc_ref[...] += prod

        @pl.when(last)
        def _():
            finish(acc_ref[...])

    sem = ("parallel",) * (len(grid) - n_red) + ("arbitrary",) * n_red
    out = pl.pallas_call(
        body, name=name, grid=grid,
        in_specs=[a_spec, b_spec, *extra_specs],
        out_specs=list(out_specs),
        out_shape=list(out_shapes),
        scratch_shapes=[pltpu.VMEM(acc_shape, F32)] if n_red else [],
        compiler_params=_params(sem),
    )(a, b, *extras)
    return out


def _sds(shape, dtype):
    return jax.ShapeDtypeStruct(shape, dtype)


def mm_nn_b(name, a, w3, out_dtypes, epilogue=None, extras=(), extra_kinds=()):
    m, k = a.shape
    nb, _, n = w3.shape
    tm, tn = _tile(m, 1024), _tile(n, 512)
    grid = (nb, m // tm, n // tn)
    nt = n // tn
    especs = []
    for kind in extra_kinds:
        if kind == "tile":
            especs.append(pl.BlockSpec((None, tm, tn), lambda j, i, c: (j, i, c)))
        else:
            especs.append(pl.BlockSpec((1, tn), lambda j, i, c: (0, j * nt + c)))
    return _mm(name, a, w3,
               pl.BlockSpec((tm, k), lambda j, i, c: (i, 0)),
               pl.BlockSpec((None, k, tn), lambda j, i, c: (j, 0, c)),
               grid, 0, NN,
               [_sds((nb, m, n), d) for d in out_dtypes],
               [pl.BlockSpec((None, tm, tn), lambda j, i, c: (j, i, c)) for _ in out_dtypes],
               None, epilogue, extras, especs)


def mm_nn_r(name, a3, w3, out_dtypes, epilogue=None, extras=(), extra_kinds=()):
    nb, m, kb = a3.shape
    n = w3.shape[2]
    tm, tn, tk = _tile(m, 1024), _tile(n, 512), _tile(kb, 2048)
    grid = (m // tm, n // tn, nb, kb // tk)
    especs = []
    for kind in extra_kinds:
        if kind == "tile":
            especs.append(pl.BlockSpec((tm, tn), lambda i, c, j, r: (i, c)))
        else:
            especs.append(pl.BlockSpec((1, tn), lambda i, c, j, r: (0, c)))
    return _mm(name, a3, w3,
               pl.BlockSpec((None, tm, tk), lambda i, c, j, r: (j, i, r)),
               pl.BlockSpec((None, tk, tn), lambda i, c, j, r: (j, r, c)),
               grid, 2, NN,
               [_sds((m, n), d) for d in out_dtypes],
               [pl.BlockSpec((tm, tn), lambda i, c, j, r: (i, c)) for _ in out_dtypes],
               (tm, tn), epilogue, extras, especs)


def mm_nt_b(name, a, w3, out_dtypes, epilogue=None, extras=()):
    m, n = a.shape
    nb, kb, _ = w3.shape
    tm, tk = _tile(m, 1024), _tile(kb, 512)
    grid = (nb, m // tm, kb // tk)
    especs = [pl.BlockSpec((None, tm, tk), lambda j, i, c: (j, i, c)) for _ in extras]
    return _mm(name, a, w3,
               pl.BlockSpec((tm, n), lambda j, i, c: (i, 0)),
               pl.BlockSpec((None, tk, n), lambda j, i, c: (j, c, 0)),
               grid, 0, NT,
               [_sds((nb, m, kb), d) for d in out_dtypes],
               [pl.BlockSpec((None, tm, tk), lambda j, i, c: (j, i, c)) for _ in out_dtypes],
               None, epilogue, extras, especs)


def mm_nt_r(name, a3, w3, out_dtype):
    nb, m, n = a3.shape
    k = w3.shape[1]
    tm, tk, tc = _tile(m, 1024), _tile(k, 1024), _tile(n, 2048)
    grid = (m // tm, k // tk, nb, n // tc)
    return _mm(name, a3, w3,
               pl.BlockSpec((None, tm, tc), lambda i, c, j, r: (j, i, r)),
               pl.BlockSpec((None, tk, tc), lambda i, c, j, r: (j, c, r)),
               grid, 2, NT,
               [_sds((m, k), out_dtype)],
               [pl.BlockSpec((tm, tk), lambda i, c, j, r: (i, c))],
               (tm, tk))[0]


def mm_tn(name, a3, d3, out_dtype):
    na, m, kb = a3.shape
    nd, _, n = d3.shape
    nb = max(na, nd)
    tk, tn, tm = _tile(kb, 512), _tile(n, 2048), _tile(m, 1024)
    grid = (nb, kb // tk, n // tn, m // tm)
    ja = (lambda j: j) if na > 1 else (lambda j: 0)
    jd = (lambda j: j) if nd > 1 else (lambda j: 0)
    return _mm(name, a3, d3,
               pl.BlockSpec((None, tm, tk), lambda j, c, e, r: (ja(j), r, c)),
               pl.BlockSpec((None, tm, tn), lambda j, c, e, r: (jd(j), r, e)),
               grid, 1, TN,
               [_sds((nb, kb, n), out_dtype)],
               [pl.BlockSpec((None, tk, tn), lambda j, c, e, r: (j, c, e))],
               (tk, tn))[0]


def _row_spec(ts, d):
    return pl.BlockSpec((ts, d), lambda i: (i, 0))


def _vec_spec(d):
    return pl.BlockSpec((1, d), lambda i: (0, 0))


def norm_mod_fwd(name, x, gain, scale, shift):
    s, d = x.shape
    ts = _tile(s, ROW_TILE)

    def body(x_ref, g_ref, sc_ref, sh_ref, h_ref):
        xv = x_ref[...]
        inv = lax.rsqrt(jnp.mean(xv * xv, axis=-1, keepdims=True) + NORM_EPS)
        h = (xv * inv) * g_ref[...] * (1.0 + sc_ref[...]) + sh_ref[...]
        h_ref[...] = h.astype(h_ref.dtype)

    return pl.pallas_call(
        body, name=name, grid=(s // ts,),
        in_specs=[_row_spec(ts, d), _vec_spec(d), _vec_spec(d), _vec_spec(d)],
        out_specs=_row_spec(ts, d), out_shape=_sds((s, d), BF16),
        compiler_params=_params(("parallel",)),
    )(x, gain, scale, shift)


def norm_mod_bwd(name, dh, x, gain, scale, dres):
    s, d = x.shape
    ts = _tile(s, ROW_TILE)

    def body(dh_ref, x_ref, g_ref, sc_ref, dres_ref, dx_ref, dsh_ref, dsc_ref, dg_ref):
        @pl.when(pl.program_id(0) == 0)
        def _():
            dsh_ref[...] = jnp.zeros_like(dsh_ref)
            dsc_ref[...] = jnp.zeros_like(dsc_ref)
            dg_ref[...] = jnp.zeros_like(dg_ref)

        xv = x_ref[...]
        dhv = dh_ref[...].astype(F32)
        g = g_ref[...]
        inv = lax.rsqrt(jnp.mean(xv * xv, axis=-1, keepdims=True) + NORM_EPS)
        n = xv * inv
        dhn = dhv * (1.0 + sc_ref[...])
        dn = dhn * g
        dx = inv * (dn - n * jnp.mean(dn * n, axis=-1, keepdims=True))
        dx_ref[...] = dres_ref[...] + dx
        dsh_ref[...] += jnp.sum(dhv, axis=0, keepdims=True)
        dsc_ref[...] += jnp.sum(dhv * (n * g), axis=0, keepdims=True)
        dg_ref[...] += jnp.sum(dhn * n, axis=0, keepdims=True)

    return pl.pallas_call(
        body, name=name, grid=(s // ts,),
        in_specs=[_row_spec(ts, d), _row_spec(ts, d), _vec_spec(d), _vec_spec(d), _row_spec(ts, d)],
        out_specs=[_row_spec(ts, d), _vec_spec(d), _vec_spec(d), _vec_spec(d)],
        out_shape=[_sds((s, d), F32), _sds((1, d), F32), _sds((1, d), F32), _sds((1, d), F32)],
        compiler_params=_params(("arbitrary",)),
    )(dh, x, gain, scale, dres)


def gate_bwd(name, dx, y, gate):
    s, d = dx.shape
    ts = _tile(s, ROW_TILE)

    def body(dx_ref, y_ref, g_ref, dy_ref, dg_ref):
        @pl.when(pl.program_id(0) == 0)
        def _():
            dg_ref[...] = jnp.zeros_like(dg_ref)

        dxv = dx_ref[...]
        dy_ref[...] = (dxv * g_ref[...]).astype(dy_ref.dtype)
        dg_ref[...] += jnp.sum(dxv * y_ref[...].astype(F32), axis=0, keepdims=True)

    return pl.pallas_call(
        body, name=name, grid=(s // ts,),
        in_specs=[_row_spec(ts, d), _row_spec(ts, d), _vec_spec(d)],
        out_specs=[_row_spec(ts, d), _vec_spec(d)],
        out_shape=[_sds((s, d), BF16), _sds((1, d), F32)],
        compiler_params=_params(("arbitrary",)),
    )(dx, y, gate)


def final_loss(name, x, gain, target):
    s, d = x.shape
    ts = _tile(s, ROW_TILE)

    def body(x_ref, g_ref, t_ref, dx_ref, dg_ref, loss_ref):
        @pl.when(pl.program_id(0) == 0)
        def _():
            dg_ref[...] = jnp.zeros_like(dg_ref)
            loss_ref[...] = jnp.zeros_like(loss_ref)

        xv = x_ref[...]
        g = g_ref[...]
        inv = lax.rsqrt(jnp.mean(xv * xv, axis=-1, keepdims=True) + NORM_EPS)
        n = xv * inv
        diff = n * g - t_ref[...]
        per_tok = jnp.mean(diff * diff, axis=-1, keepdims=True)
        loss_ref[...] += 0.5 * jnp.sum(per_tok, axis=0, keepdims=True)
        dy = diff * (1.0 / d)
        dg_ref[...] += jnp.sum(dy * n, axis=0, keepdims=True)
        dn = dy * g
        dx_ref[...] = inv * (dn - n * jnp.mean(dn * n, axis=-1, keepdims=True))

    return pl.pallas_call(
        body, name=name, grid=(s // ts,),
        in_specs=[_row_spec(ts, d), _vec_spec(d), _row_spec(ts, d)],
        out_specs=[_row_spec(ts, d), _vec_spec(d), _vec_spec(128)],
        out_shape=[_sds((s, d), F32), _sds((1, d), F32), _sds((1, 128), F32)],
        compiler_params=_params(("arbitrary",)),
    )(x, gain, target)


def _hgrn_chunk_fwd(qr, fl, lbv, tri):
    sg = _sigmoid(fl)
    sgm = _sigmoid(-fl)
    f = lbv + (1.0 - lbv) * sg
    logf = jnp.log(f)
    k = (1.0 - lbv) * sgm
    cum = _dot(tri, logf, precision=lax.Precision.HIGHEST)
    cl = cum[HGRN_CHUNK - 1:HGRN_CHUNK, :]
    e = jnp.exp(cum)
    en = jnp.exp(-cum)
    es = jnp.exp(cl - cum)
    sq = _sigmoid(qr)
    qs = qr * sq
    return dict(sg=sg, sgm=sgm, f=f, k=k, cum=cum, cl=cl, e=e, en=en, es=es, sq=sq, qs=qs,
                qd=qs * e, ki=k * en, ks=k * es, dec=jnp.exp(cl))


def _tri_masks(strict=False):
    r = lax.broadcasted_iota(jnp.int32, (HGRN_CHUNK, HGRN_CHUNK), 0)
    c = lax.broadcasted_iota(jnp.int32, (HGRN_CHUNK, HGRN_CHUNK), 1)
    return (r > c) if strict else (r >= c)


def hgrn_fwd(name, proj3, lb, out_gain):
    _, s, d = proj3.shape
    heads = d // HEAD_DIM
    t_rows = _tile(s, 512)
    n_t = s // t_rows
    n_c = t_rows // HGRN_CHUNK

    def body(q_ref, f_ref, i_ref, g_ref, lb_ref, gain_ref, u_ref, o_ref, st_ref, state):
        @pl.when(pl.program_id(1) == 0)
        def _():
            state[...] = jnp.zeros_like(state)

        causal = _tri_masks()
        tri = causal.astype(F32)
        lbv = lb_ref[...]
        gain = gain_ref[...]
        for ci in range(n_c):
            rows = pl.ds(ci * HGRN_CHUNK, HGRN_CHUNK)
            c = _hgrn_chunk_fwd(q_ref[rows, :], f_ref[rows, :], lbv, tri)
            v = i_ref[rows, :]
            st = state[...]
            st_ref[0, ci] = st
            scores = jnp.where(causal, _bdot(c["qd"], c["ki"], NT), 0.0)
            o = _bdot(scores, v) + _bdot(c["qd"], st, NT)
            state[...] = st * c["dec"] + _bdot(v, c["ks"], TN)
            graw = g_ref[rows, :]
            rms = lax.rsqrt(jnp.mean(o * o, axis=-1, keepdims=True) + NORM_EPS)
            u = o * rms * gain * (graw * _sigmoid(graw))
            o_ref[rows, :] = o
            u_ref[rows, :] = u.astype(u_ref.dtype)

    def pspec(blk):
        return pl.BlockSpec((None, t_rows, HEAD_DIM), lambda h, t: (blk, t, h))

    hspec = pl.BlockSpec((1, HEAD_DIM), lambda h, t: (0, h))
    ospec = pl.BlockSpec((t_rows, HEAD_DIM), lambda h, t: (t, h))
    return pl.pallas_call(
        body, name=name, grid=(heads, n_t),
        in_specs=[pspec(0), pspec(1), pspec(2), pspec(3), hspec, hspec],
        out_specs=[ospec, ospec,
                   pl.BlockSpec((1, n_c, HEAD_DIM, HEAD_DIM), lambda h, t: (h, t, 0, 0))],
        out_shape=[_sds((s, d), BF16), _sds((s, d), F32),
                   _sds((heads, s // HGRN_CHUNK, HEAD_DIM, HEAD_DIM), F32)],
        scratch_shapes=[pltpu.VMEM((HEAD_DIM, HEAD_DIM), F32)],
        compiler_params=_params(("parallel", "arbitrary")),
    )(proj3, proj3, proj3, proj3, lb, out_gain)


def hgrn_bwd(name, proj3, lb, out_gain, o, du, states):
    _, s, d = proj3.shape
    heads = d // HEAD_DIM
    t_rows = _tile(s, 512)
    n_t = s // t_rows
    n_c = t_rows // HGRN_CHUNK

    def body(q_ref, f_ref, i_ref, g_ref, lb_ref, gain_ref, o_ref, du_ref, st_ref,
             dp_ref, dlb_ref, dgain_ref, dstate):
        @pl.when(pl.program_id(1) == 0)
        def _():
            dstate[...] = jnp.zeros_like(dstate)
            dlb_ref[...] = jnp.zeros_like(dlb_ref)
            dgain_ref[...] = jnp.zeros_like(dgain_ref)

        causal = _tri_masks()
        tri = causal.astype(F32)
        tri_t = jnp.logical_not(_tri_masks(strict=True)).astype(F32)
        lbv = lb_ref[...]
        gain = gain_ref[...]
        for ci in reversed(range(n_c)):
            rows = pl.ds(ci * HGRN_CHUNK, HGRN_CHUNK)
            qr = q_ref[rows, :]
            c = _hgrn_chunk_fwd(qr, f_ref[rows, :], lbv, tri)
            v = i_ref[rows, :]
            st = st_ref[0, ci]
            ov = o_ref[rows, :]
            duv = du_ref[rows, :].astype(F32)
            graw = g_ref[rows, :]
            sgg = _sigmoid(graw)
            gate = graw * sgg
            rms = lax.rsqrt(jnp.mean(ov * ov, axis=-1, keepdims=True) + NORM_EPS)
            on = ov * rms
            dgain_ref[...] += jnp.sum(duv * on * gate, axis=0, keepdims=True)
            dgraw = duv * on * gain * (sgg * (1.0 + graw * (1.0 - sgg)))
            don = duv * gain * gate
            do = rms * (don - on * jnp.mean(don * on, axis=-1, keepdims=True))
            qd, ki, ks = c["qd"], c["ki"], c["ks"]
            p = jnp.where(causal, _bdot(qd, ki, NT), 0.0)
            dp = jnp.where(causal, _bdot(do, v, NT), 0.0)
            dst = dstate[...]
            dqd = _bdot(dp, ki) + _bdot(do, st)
            dki = _bdot(dp, qd, TN)
            dv = _bdot(p, do, TN) + _bdot(ks, dst, NT)
            dks = _bdot(v, dst)
            ddec = jnp.sum(dst * st, axis=0, keepdims=True)
            dstate[...] = dst * c["dec"] + _bdot(do, qd, TN)
            dqs = dqd * c["e"]
            dcum = dqd * qd - dki * ki - dks * ks
            dk = dki * c["en"] + dks * c["es"]
            dcl = jnp.sum(dks * ks, axis=0, keepdims=True) + ddec * c["dec"]
            dlogf = _dot(tri_t, dcum, precision=lax.Precision.HIGHEST) + dcl
            df = dlogf / c["f"]
            sg, sgm = c["sg"], c["sgm"]
            one_m_lb = 1.0 - lbv
            dlb_ref[...] += jnp.sum(df * (1.0 - sg) - dk * sgm, axis=0, keepdims=True)
            dfl = df * one_m_lb * sg * (1.0 - sg) - dk * one_m_lb * sgm * (1.0 - sgm)
            sq = c["sq"]
            dqr = dqs * (sq * (1.0 + qr * (1.0 - sq)))
            dp_ref[0, rows, :] = dqr.astype(dp_ref.dtype)
            dp_ref[1, rows, :] = dfl.astype(dp_ref.dtype)
            dp_ref[2, rows, :] = dv.astype(dp_ref.dtype)
            dp_ref[3, rows, :] = dgraw.astype(dp_ref.dtype)

    def pspec(blk):
        return pl.BlockSpec((None, t_rows, HEAD_DIM), lambda h, t: (blk, n_t - 1 - t, h))

    hspec = pl.BlockSpec((1, HEAD_DIM), lambda h, t: (0, h))
    ospec = pl.BlockSpec((t_rows, HEAD_DIM), lambda h, t: (n_t - 1 - t, h))
    return pl.pallas_call(
        body, name=name, grid=(heads, n_t),
        in_specs=[pspec(0), pspec(1), pspec(2), pspec(3), hspec, hspec, ospec, ospec,
                  pl.BlockSpec((1, n_c, HEAD_DIM, HEAD_DIM), lambda h, t: (h, n_t - 1 - t, 0, 0))],
        out_specs=[pl.BlockSpec((4, t_rows, HEAD_DIM), lambda h, t: (0, n_t - 1 - t, h)), hspec, hspec],
        out_shape=[_sds((4, s, d), BF16), _sds((1, d), F32), _sds((1, d), F32)],
        scratch_shapes=[pltpu.VMEM((HEAD_DIM, HEAD_DIM), F32)],
        compiler_params=_params(("parallel", "arbitrary")),
    )(proj3, proj3, proj3, proj3, lb, out_gain, o, du, states)


def attn_fwd(name, q, kv):
    s, dq = q.shape
    kvh = kv.shape[1] // (2 * HEAD_DIM)
    assert dq == kvh * KV_GROUP * HEAD_DIM
    tq = _tile(s, ATTN_TILE)
    scale = HEAD_DIM ** -0.5

    def body(q_ref, k_ref, v_ref, o_ref, tot_ref):
        i = pl.program_id(2)
        qv = q_ref[...]
        r_i = lax.broadcasted_iota(jnp.int32, (tq, tq), 0)
        c_i = lax.broadcasted_iota(jnp.int32, (tq, tq), 1)
        later = (r_i > c_i).astype(BF16)
        mask = c_i < r_i
        ones = jnp.ones((8, tq), BF16)

        def block(j, run, acc, tot, masked):
            rows = pl.ds(pl.multiple_of(j * tq, tq), tq)
            kj = k_ref[rows, :]
            vj = v_ref[rows, :]
            z = _dot(qv, kj, NT) * scale
            lbeta = _log_sigmoid(z)
            lrest = lbeta - z
            if masked:
                lrest = jnp.where(mask, lrest, 0.0)
            hi, lo = _split_bf16(lrest)
            between = _dot(hi, later) + _dot(lo, later) + run
            w = jnp.exp(lbeta + between)
            if masked:
                w = jnp.where(mask, w, 0.0)
            acc = acc + _dot(w.astype(BF16), vj)
            run = run + jnp.sum(lrest, axis=1, keepdims=True)
            tot = tot + _dot(ones, hi, NT) + _dot(ones, lo, NT)
            return run, acc, tot

        carry = block(i, jnp.zeros((tq, 1), F32), jnp.zeros((tq, HEAD_DIM), F32),
                      jnp.zeros((8, tq), F32), True)

        def step(jj, carry):
            return block(i - 1 - jj, carry[0], carry[1], carry[2], False)

        _, acc, tot = lax.fori_loop(0, i, step, carry)
        o_ref[...] = acc
        tot_ref[...] = tot[0:1, :]

    qspec = pl.BlockSpec((tq, HEAD_DIM), lambda kh, g, i: (i, kh * KV_GROUP + g))
    return pl.pallas_call(
        body, name=name, grid=(kvh, KV_GROUP, s // tq),
        in_specs=[qspec,
                  pl.BlockSpec((s, HEAD_DIM), lambda kh, g, i: (0, kh)),
                  pl.BlockSpec((s, HEAD_DIM), lambda kh, g, i: (0, kvh + kh))],
        out_specs=[qspec, pl.BlockSpec((None, 1, tq), lambda kh, g, i: (kh * KV_GROUP + g, 0, i))],
        out_shape=[_sds((s, dq), F32), _sds((dq // HEAD_DIM, 1, s), F32)],
        compiler_params=_params(("parallel", "parallel", "parallel")),
    )(q, kv, kv)


def attn_bwd(name, q, kv, totals, do):
    s, dq_cols = q.shape
    kvh = kv.shape[1] // (2 * HEAD_DIM)
    tq = _tile(s, ATTN_TILE)
    scale = HEAD_DIM ** -0.5

    def body(q_ref, k_ref, v_ref, tot_ref, do_ref, dq_ref, dkv_ref):
        g = pl.program_id(1)
        i = pl.program_id(2)

        @pl.when(jnp.logical_and(g == 0, i == 0))
        def _():
            dkv_ref[...] = jnp.zeros_like(dkv_ref)

        qv = q_ref[...]
        dob = do_ref[...].astype(BF16)
        tot = tot_ref[...]
        r_i = lax.broadcasted_iota(jnp.int32, (tq, tq), 0)
        c_i = lax.broadcasted_iota(jnp.int32, (tq, tq), 1)
        upto = (c_i <= r_i).astype(BF16)
        before = (c_i < r_i).astype(BF16)
        mask = r_i < c_i

        def block(j, run, drun, dq_acc, masked):
            rows = pl.ds(pl.multiple_of(j * tq, tq), tq)
            kj = k_ref[rows, :]
            vj = v_ref[rows, :]
            zt = _dot(kj, qv, NT) * scale
            lbeta = _log_sigmoid(zt)
            lrest_raw = lbeta - zt
            lrest = jnp.where(mask, lrest_raw, 0.0) if masked else lrest_raw
            hi, lo = _split_bf16(lrest)
            between = tot - run - (_dot(upto, hi) + _dot(upto, lo))
            wt = jnp.exp(lbeta + between)
            if masked:
                wt = jnp.where(mask, wt, 0.0)
            dwt = _dot(vj, dob, NT)
            dat = dwt * wt
            ahi, alo = _split_bf16(dat)
            dsum = drun + _dot(before, ahi) + _dot(before, alo)
            dzt = dat * jnp.exp(lrest_raw) - dsum * jnp.exp(lbeta)
            if masked:
                dzt = jnp.where(mask, dzt, 0.0)
            dzb = (dzt * scale).astype(BF16)
            dq_acc = dq_acc + _dot(dzb, kj, TN)
            dkv_ref[0, rows, :] += _dot(dzb, qv)
            dkv_ref[1, rows, :] += _dot(wt.astype(BF16), dob)
            run = run + jnp.sum(lrest, axis=0, keepdims=True)
            drun = drun + jnp.sum(dat, axis=0, keepdims=True)
            return run, drun, dq_acc

        def step(j, carry):
            return block(j, carry[0], carry[1], carry[2], False)

        zrow = jnp.zeros((1, tq), F32)
        carry = lax.fori_loop(0, i, step, (zrow, zrow, jnp.zeros((tq, HEAD_DIM), F32)))
        carry = block(i, carry[0], carry[1], carry[2], True)
        dq_ref[...] = carry[2].astype(dq_ref.dtype)

    qspec = pl.BlockSpec((tq, HEAD_DIM), lambda kh, g, i: (i, kh * KV_GROUP + g))
    return pl.pallas_call(
        body, name=name, grid=(kvh, KV_GROUP, s // tq),
        in_specs=[qspec,
                  pl.BlockSpec((s, HEAD_DIM), lambda kh, g, i: (0, kh)),
                  pl.BlockSpec((s, HEAD_DIM), lambda kh, g, i: (0, kvh + kh)),
                  pl.BlockSpec((None, 1, tq), lambda kh, g, i: (kh * KV_GROUP + g, 0, i)),
                  qspec],
        out_specs=[qspec, pl.BlockSpec((2, s, HEAD_DIM), lambda kh, g, i: (0, 0, kh))],
        out_shape=[_sds((s, dq_cols), BF16), _sds((2, s, kvh * HEAD_DIM), F32)],
        compiler_params=_params(("parallel", "arbitrary", "arbitrary")),
    )(q, kv, kv, totals, do)


def ada_project(name, c_all, w, b):
    bsz, d = c_all.shape
    n = w.shape[1]
    tn = _tile(n, 512)

    def body(c_ref, w_ref, b_ref, o_ref):
        cv = c_ref[...]
        act = cv * _sigmoid(cv)
        o_ref[...] = _bdot(act, w_ref[...]) + b_ref[...]

    return pl.pallas_call(
        body, name=name, grid=(n // tn,),
        in_specs=[pl.BlockSpec((bsz, d), lambda i: (0, 0)),
                  pl.BlockSpec((d, tn), lambda i: (0, i)),
                  pl.BlockSpec((1, tn), lambda i: (0, i))],
        out_specs=pl.BlockSpec((bsz, tn), lambda i: (0, i)),
        out_shape=_sds((bsz, n), F32),
        compiler_params=_params(("parallel",)),
    )(c_all, w, b)


def _adamw_math(w, g, m, v):
    m = ADAM_B1 * m + (1.0 - ADAM_B1) * g
    v = ADAM_B2 * v + (1.0 - ADAM_B2) * (g * g)
    m_hat = m / (1.0 - ADAM_B1 ** ADAM_STEP)
    v_hat = v / (1.0 - ADAM_B2 ** ADAM_STEP)
    delta = -ADAM_LR * (m_hat / (jnp.sqrt(v_hat) + ADAM_EPS) + ADAM_WD * w)
    return delta, m, v


def adamw(name, w, g, m, v):
    r, c = w.shape
    tr = _tile(r, 256)
    tc = _tile(c, 2048)

    def body(w_ref, g_ref, m_ref, v_ref, d_ref, mo_ref, vo_ref):
        delta, mn, vn = _adamw_math(w_ref[...], g_ref[...], m_ref[...], v_ref[...])
        d_ref[...] = delta
        mo_ref[...] = mn
        vo_ref[...] = vn

    spec = pl.BlockSpec((tr, tc), lambda i, j: (i, j))
    return pl.pallas_call(
        body, name=name, grid=(r // tr, c // tc),
        in_specs=[spec] * 4, out_specs=[spec] * 3,
        out_shape=[_sds((r, c), F32)] * 3,
        compiler_params=_params(("parallel", "parallel")),
    )(w, g, m, v)


def ada_grad_adamw(name, c_t, dmod, w, m, v):
    d, n = w.shape
    tr = _tile(d, 256)
    tc = _tile(n, 512)

    def body(a_ref, dm_ref, w_ref, m_ref, v_ref, g_ref, d_ref, mo_ref, vo_ref):
        cv = a_ref[...]
        g = _bdot(cv * _sigmoid(cv), dm_ref[...])
        delta, mn, vn = _adamw_math(w_ref[...], g, m_ref[...], v_ref[...])
        g_ref[...] = g
        d_ref[...] = delta
        mo_ref[...] = mn
        vo_ref[...] = vn

    spec = pl.BlockSpec((tr, tc), lambda i, j: (i, j))
    return pl.pallas_call(
        body, name=name, grid=(d // tr, n // tc),
        in_specs=[pl.BlockSpec((tr, 128), lambda i, j: (i, 0)),
                  pl.BlockSpec((128, tc), lambda i, j: (0, j)), spec, spec, spec],
        out_specs=[spec] * 4, out_shape=[_sds((d, n), F32)] * 4,
        compiler_params=_params(("parallel", "parallel")),
    )(c_t, dmod, w, m, v)


def device_sum(name, gathered):
    _, r, c = gathered.shape

    def body(g_ref, o_ref):
        acc = g_ref[0]
        for dev in range(1, N_DEV):
            acc = acc + g_ref[dev]
        o_ref[...] = acc

    return pl.pallas_call(
        body, name=name,
        in_specs=[pl.BlockSpec(memory_space=pltpu.VMEM)],
        out_specs=pl.BlockSpec(memory_space=pltpu.VMEM),
        out_shape=_sds((r, c), F32),
    )(gathered)


def lower_bound_fwd(name, logits):
    _, d = logits.shape

    def body(l_ref, o_ref):
        l0 = l_ref[0:1, :]
        l1 = l_ref[1:2, :]
        mx = jnp.maximum(l0, l1)
        e0 = jnp.exp(l0 - mx)
        e1 = jnp.exp(l1 - mx)
        o_ref[...] = e0 / (e0 + e1)

    return pl.pallas_call(
        body, name=name,
        in_specs=[pl.BlockSpec(memory_space=pltpu.VMEM)],
        out_specs=pl.BlockSpec(memory_space=pltpu.VMEM),
        out_shape=_sds((1, d), F32),
    )(logits)


def lower_bound_bwd(name, logits, dlb):
    _, d = logits.shape

    def body(l_ref, dlb_ref, o_ref):
        l0 = l_ref[0:1, :]
        l1 = l_ref[1:2, :]
        mx = jnp.maximum(l0, l1)
        e0 = jnp.exp(l0 - mx)
        e1 = jnp.exp(l1 - mx)
        p0 = e0 / (e0 + e1)
        p1 = e1 / (e0 + e1)
        g = dlb_ref[...] * p0 * p1
        o_ref[0:1, :] = g
        o_ref[1:2, :] = -g

    return pl.pallas_call(
        body, name=name,
        in_specs=[pl.BlockSpec(memory_space=pltpu.VMEM)] * 2,
        out_specs=pl.BlockSpec(memory_space=pltpu.VMEM),
        out_shape=_sds((2, d), F32),
    )(logits, dlb)


HBM = pl.BlockSpec(memory_space=pltpu.HBM)


def _position():
    return lax.axis_index("x"), lax.axis_index("y"), lax.axis_index("c")


def _remote(src, dst, send_sem, recv_sem, device):
    return pltpu.make_async_remote_copy(src_ref=src, dst_ref=dst, send_sem=send_sem, recv_sem=recv_sem,
                                        device_id=device, device_id_type=MESH)


def allgather8(name, x):
    r, c = x.shape

    def body(x_ref, out_ref, send_sems, recv_sems):
        px, py, pc = _position()
        me = 4 * px + 2 * py + pc
        out_ref[me] = x_ref[...]
        copies = []
        for k in range(1, N_DEV):
            peer = (1 - px if k & 4 else px, 1 - py if k & 2 else py, 1 - pc if k & 1 else pc)
            cp = _remote(x_ref, out_ref.at[me], send_sems.at[k - 1], recv_sems.at[k - 1], peer)
            cp.start()
            copies.append(cp)
        for cp in copies:
            cp.wait()

    return pl.pallas_call(
        body, name=name,
        in_specs=[pl.BlockSpec(memory_space=pltpu.VMEM)],
        out_specs=pl.BlockSpec(memory_space=pltpu.VMEM),
        out_shape=_sds((N_DEV, r, c), x.dtype),
        scratch_shapes=[pltpu.SemaphoreType.DMA((N_DEV - 1,)), pltpu.SemaphoreType.DMA((N_DEV - 1,))],
    )(x)


def _other_chips(px, py):
    return [(1 - px, py), (px, 1 - py), (1 - px, 1 - py)]


def gather_weights(name, shards):
    n = len(shards)

    def body(*refs):
        ins, outs = refs[:n], refs[n:2 * n]
        send_sems, recv_sems, local_sems = refs[2 * n:]
        px, py, pc = _position()
        chip = 2 * px + py
        sibling = (px, py, 1 - pc)
        others = _other_chips(px, py)

        def half(a, h):
            rows = ins[a].shape[0] // 2
            return pl.ds(h * rows, rows)

        own = []
        for a in range(n):
            cp = pltpu.make_async_copy(ins[a], outs[a].at[chip], local_sems.at[a])
            cp.start()
            own.append(cp)
        sends = []
        for a in range(n):
            for j, (ox, oy) in enumerate(others):
                cp = _remote(ins[a].at[half(a, pc)], outs[a].at[chip, half(a, pc)],
                             send_sems.at[a, j], recv_sems.at[a, j], (ox, oy, pc))
                cp.start()
                sends.append(cp)
        passed = []
        for a in range(n):
            for j, (ox, oy) in enumerate(others):
                landed = outs[a].at[2 * ox + oy, half(a, pc)]
                _remote(landed, landed, send_sems.at[a, j], recv_sems.at[a, j], sibling).wait_recv()
                cp = _remote(landed, landed, send_sems.at[a, 3 + j], recv_sems.at[a, 3 + j], sibling)
                cp.start()
                passed.append(cp)
        for a in range(n):
            for j, (ox, oy) in enumerate(others):
                landed = outs[a].at[2 * ox + oy, half(a, 1 - pc)]
                _remote(landed, landed, send_sems.at[a, 3 + j], recv_sems.at[a, 3 + j], sibling).wait_recv()
        for cp in sends + passed:
            cp.wait_send()
        for cp in own:
            cp.wait()

    return pl.pallas_call(
        body, name=name,
        in_specs=[HBM] * n, out_specs=[HBM] * n,
        out_shape=[_sds((N_CHIPS,) + s.shape, s.dtype) for s in shards],
        scratch_shapes=[pltpu.SemaphoreType.DMA((n, 6)), pltpu.SemaphoreType.DMA((n, 6)),
                        pltpu.SemaphoreType.DMA((n,))],
    )(*shards)


def sibling_exchange(name, grads):
    n = len(grads)

    def body(*refs):
        ins, outs = refs[:n], refs[n:2 * n]
        send_sems, recv_sems = refs[2 * n:]
        px, py, pc = _position()
        sibling = (px, py, 1 - pc)
        copies = []
        for a in range(n):
            rows = ins[a].shape[1] // 2
            src = ins[a].at[:, pl.ds((1 - pc) * rows, rows), :]
            cp = _remote(src, outs[a], send_sems.at[a], recv_sems.at[a], sibling)
            cp.start()
            copies.append(cp)
        for cp in copies:
            cp.wait()

    return pl.pallas_call(
        body, name=name,
        in_specs=[HBM] * n, out_specs=[HBM] * n,
        out_shape=[_sds((g.shape[0], g.shape[1] // 2, g.shape[2]), g.dtype) for g in grads],
        scratch_shapes=[pltpu.SemaphoreType.DMA((n,)), pltpu.SemaphoreType.DMA((n,))],
    )(*grads)


def pair_add(name, g, recv, core):
    nb, rows, cols = g.shape
    half = rows // 2
    tr = _tile(half, 256)
    steps = half // tr

    def body(core_ref, g_ref, r_ref, o_ref):
        del core_ref
        o_ref[...] = (g_ref[...].astype(F32) + r_ref[...].astype(F32)).astype(o_ref.dtype)

    return pl.pallas_call(
        body, name=name,
        grid_spec=pltpu.PrefetchScalarGridSpec(
            num_scalar_prefetch=1, grid=(nb, steps),
            in_specs=[pl.BlockSpec((None, tr, cols), lambda j, i, core_ref: (j, core_ref[0] * steps + i, 0)),
                      pl.BlockSpec((None, tr, cols), lambda j, i, core_ref: (j, i, 0))],
            out_specs=pl.BlockSpec((None, tr, cols), lambda j, i, core_ref: (j, i, 0))),
        out_shape=_sds((nb, half, cols), g.dtype),
        compiler_params=_params(("parallel", "parallel")),
    )(core, g, recv)


def chip_scatter(name, parts):
    n = len(parts)

    def body(*refs):
        ins, outs = refs[:n], refs[n:2 * n]
        send_sems, recv_sems, local_sems = refs[2 * n:]
        px, py, pc = _position()
        chip = 2 * px + py
        others = _other_chips(px, py)
        own = []
        copies = []
        for a in range(n):
            cp = pltpu.make_async_copy(ins[a].at[chip], outs[a].at[chip], local_sems.at[a])
            cp.start()
            own.append(cp)
            for j, (ox, oy) in enumerate(others):
                cp = _remote(ins[a].at[2 * ox + oy], outs[a].at[chip],
                             send_sems.at[a, j], recv_sems.at[a, j], (ox, oy, pc))
                cp.start()
                copies.append(cp)
        for cp in copies:
            cp.wait()
        for cp in own:
            cp.wait()

    return pl.pallas_call(
        body, name=name,
        in_specs=[HBM] * n, out_specs=[HBM] * n,
        out_shape=[_sds(p.shape, p.dtype) for p in parts],
        scratch_shapes=[pltpu.SemaphoreType.DMA((n, 3)), pltpu.SemaphoreType.DMA((n, 3)),
                        pltpu.SemaphoreType.DMA((n,))],
    )(*parts)


def chip_sum(name, recv):
    nb, half, cols = recv.shape
    tr = _tile(half, 256)

    def body(r_ref, o_ref):
        acc = r_ref[0].astype(F32)
        for j in range(1, nb):
            acc = acc + r_ref[j].astype(F32)
        o_ref[...] = acc

    return pl.pallas_call(
        body, name=name, grid=(half // tr,),
        in_specs=[pl.BlockSpec((nb, tr, cols), lambda i: (0, i, 0))],
        out_specs=pl.BlockSpec((tr, cols), lambda i: (i, 0)),
        out_shape=_sds((half, cols), F32),
        compiler_params=_params(("parallel",)),
    )(recv)


def sibling_share(name, halves, out_shapes, places):
    n = len(halves)
    n_out = len(out_shapes)

    def body(*refs):
        ins, outs = refs[:n], refs[n:n + n_out]
        send_sems, recv_sems, local_sems = refs[n + n_out:]
        px, py, pc = _position()
        sibling = (px, py, 1 - pc)
        copies = []
        own = []
        for a in range(n):
            oi, lead = places[a]
            half = ins[a].shape[0]
            slab = outs[oi] if lead is None else outs[oi].at[lead]
            mine = slab.at[pl.ds(pc * half, half)]
            theirs = slab.at[pl.ds((1 - pc) * half, half)]
            cp = pltpu.make_async_copy(ins[a], mine, local_sems.at[a])
            cp.start()
            own.append(cp)
            cp = _remote(ins[a], mine, send_sems.at[a], recv_sems.at[a], sibling)
            cp.start()
            copies.append((cp, _remote(theirs, theirs, send_sems.at[a], recv_sems.at[a], sibling)))
        for cp, arrival in copies:
            cp.wait_send()
            arrival.wait_recv()
        for cp in own:
            cp.wait()

    return pl.pallas_call(
        body, name=name,
        in_specs=[HBM] * n, out_specs=[HBM] * n_out,
        out_shape=list(out_shapes),
        scratch_shapes=[pltpu.SemaphoreType.DMA((n,)), pltpu.SemaphoreType.DMA((n,)),
                        pltpu.SemaphoreType.DMA((n,))],
    )(*halves)


def _row(a, i):
    return a[i:i + 1]


def _relu_sq(acc):
    r = jnp.maximum(acc, 0.0)
    return r, r * r


def _residual(acc, res, gate):
    return res + gate * acc, acc


def _mlp_fwd(tag, x, gain, mod, w1, w2):
    h = norm_mod_fwd(tag + "_mlp_norm", x, gain, _row(mod, 4), _row(mod, 3))
    r3, a3 = mm_nn_b(tag + "_mlp_up", h, w1, [BF16, BF16], _relu_sq)
    x_out, m = mm_nn_r(tag + "_mlp_down", a3, w2, [F32, BF16], _residual,
                       (x, _row(mod, 5)), ("tile", "row"))
    return x_out, (h, r3, a3, m)


def _mlp_bwd(tag, dx_out, x, gain, mod, w1, w2, saved):
    h, r3, a3, m = saved
    dm, dgate = gate_bwd(tag + "_mlp_gate_bwd", dx_out, m, _row(mod, 5))
    dz3 = mm_nt_b(tag + "_mlp_down_dgrad", dm, w2, [BF16],
                  lambda acc, r: (acc * (2.0 * r.astype(F32)),), (r3,))[0]
    gw2 = mm_tn(tag + "_mlp_w2_grad", a3, dm[None], BF16)
    gw1 = mm_tn(tag + "_mlp_w1_grad", h[None], dz3, BF16)
    dh = mm_nt_r(tag + "_mlp_up_dgrad", dz3, w1, F32)
    dx, dsh, dsc, dgain = norm_mod_bwd(tag + "_mlp_norm_bwd", dh, x, gain, _row(mod, 4), dx_out)
    return dx, gw1, gw2, (dsh, dsc, dgate), dgain


def local_step(x, target, mod0, mod1, kvmod, norm_mix, norm_mlp, kv_norm, final_norm, lb, out_gain, w):
    h1 = norm_mod_fwd("l0_mix_norm", x, _row(norm_mix, 0), _row(mod0, 1), _row(mod0, 0))
    proj3 = mm_nn_b("l0_in_proj", h1, w["a_in"], [F32])[0]
    u, o_h, states = hgrn_fwd("l0_hgrn_fwd", proj3, lb, out_gain)
    x1, y0 = mm_nn_b("l0_out_proj", u, w["a_out"][None], [F32, BF16], _residual,
                     (x[None], _row(mod0, 2)), ("tile", "row"))
    x1, y0 = x1[0], y0[0]
    x2, mlp0 = _mlp_fwd("l0", x1, _row(norm_mlp, 0), mod0, w["w1_0"], w["w2_0"])
    hk = norm_mod_fwd("kv_norm", x2, kv_norm, _row(kvmod, 1), _row(kvmod, 0))
    kv = mm_nn_b("kv_proj", hk, w["kv"][None], [BF16])[0][0]
    h3 = norm_mod_fwd("l1_mix_norm", x2, _row(norm_mix, 1), _row(mod1, 1), _row(mod1, 0))
    q = mm_nn_b("l1_q_proj", h3, w["bq"][None], [BF16])[0][0]
    o_a, totals = attn_fwd("l1_attn_fwd", q, kv)
    x3, y1 = mm_nn_b("l1_out_proj", o_a, w["bo"][None], [F32, BF16], _residual,
                     (x2[None], _row(mod1, 2)), ("tile", "row"))
    x3, y1 = x3[0], y1[0]
    x4, mlp1 = _mlp_fwd("l1", x3, _row(norm_mlp, 1), mod1, w["w1_1"], w["w2_1"])
    dx4, d_final, loss = final_loss("final_loss", x4, final_norm, target)

    grads = {}
    dx3, grads["w1_1"], grads["w2_1"], dmlp1, d_nmlp1 = _mlp_bwd(
        "l1", dx4, x3, _row(norm_mlp, 1), mod1, w["w1_1"], w["w2_1"], mlp1)
    dy1, dgate1 = gate_bwd("l1_mix_gate_bwd", dx3, y1, _row(mod1, 2))
    do_a = mm_nt_b("l1_out_dgrad", dy1, w["bo"][None], [F32])[0][0]
    grads["bo"] = mm_tn("l1_out_grad", o_a[None], dy1[None], BF16)[0]
    dq, dkv3 = attn_bwd("l1_attn_bwd", q, kv, totals, do_a)
    grads["bq"] = mm_tn("l1_q_grad", h3[None], dq[None], BF16)[0]
    dh3 = mm_nt_b("l1_q_dgrad", dq, w["bq"][None], [F32])[0][0]
    dx2, dsh, dsc, d_nmix1 = norm_mod_bwd("l1_mix_norm_bwd", dh3, x2, _row(norm_mix, 1), _row(mod1, 1), dx3)
    dmod1 = jnp.concatenate([dsh, dsc, dgate1, *dmlp1], axis=0)
    dkv = jnp.concatenate([dkv3[0], dkv3[1]], axis=1).astype(BF16)
    grads["kv"] = mm_tn("kv_grad", hk[None], dkv[None], BF16)[0]
    dhk = mm_nt_b("kv_dgrad", dkv, w["kv"][None], [F32])[0][0]
    dx2, dsh, dsc, d_nkv = norm_mod_bwd("kv_norm_bwd", dhk, x2, kv_norm, _row(kvmod, 1), dx2)
    dkvmod = jnp.concatenate([dsh, dsc], axis=0)
    dx1, grads["w1_0"], grads["w2_0"], dmlp0, d_nmlp0 = _mlp_bwd(
        "l0", dx2, x1, _row(norm_mlp, 0), mod0, w["w1_0"], w["w2_0"], mlp0)
    dy0, dgate0 = gate_bwd("l0_mix_gate_bwd", dx1, y0, _row(mod0, 2))
    du = mm_nt_b("l0_out_dgrad", dy0, w["a_out"][None], [F32])[0][0]
    grads["a_out"] = mm_tn("l0_out_grad", u[None], dy0[None], BF16)[0]
    dproj3, d_lb, d_out_gain = hgrn_bwd("l0_hgrn_bwd", proj3, lb, out_gain, o_h, du, states)
    grads["a_in"] = mm_tn("l0_in_grad", h1[None], dproj3, BF16)
    dh1 = mm_nt_r("l0_in_dgrad", dproj3, w["a_in"], F32)
    dx0, dsh, dsc, d_nmix0 = norm_mod_bwd("l0_mix_norm_bwd", dh1, x, _row(norm_mix, 0), _row(mod0, 1), dx1)
    dmod0 = jnp.concatenate([dsh, dsc, dgate0, *dmlp0], axis=0)
    small = dict(norm_mix=(d_nmix0, d_nmix1), norm_mlp=(d_nmlp0, d_nmlp1), kv_norm=d_nkv,
                 final_norm=d_final, lb=d_lb, out_gain=d_out_gain)
    return loss, dx0, grads, dmod0, dmod1, dkvmod, small


BIG = ("a_in", "w1_0", "w1_1", "w2_0", "w2_1", "a_out", "bq", "bo", "kv")


def kernel(x, c, ada_w, ada_b, norm_mix, norm_mlp, a_w_in, a_lb_logits, a_out_gain, a_w_out, kv_ada_w, kv_ada_b, kv_norm, w_kv, b_w_q, b_w_out, mlp_w1, mlp_w2, final_norm, loss_target, m_ada_w, m_ada_b, m_norm_mix, m_norm_mlp, m_a_w_in, m_a_lb_logits, m_a_out_gain, m_a_w_out, m_kv_ada_w, m_kv_ada_b, m_kv_norm, m_w_kv, m_b_w_q, m_b_w_out, m_mlp_w1, m_mlp_w2, m_final_norm, v_ada_w, v_ada_b, v_norm_mix, v_norm_mlp, v_a_w_in, v_a_lb_logits, v_a_out_gain, v_a_w_out, v_kv_ada_w, v_kv_ada_b, v_kv_norm, v_w_kv, v_b_w_q, v_b_w_out, v_mlp_w1, v_mlp_w2, v_final_norm):
    d = x.shape[-1]
    px, py, pc = _position()
    me = 4 * px + 2 * py + pc
    chip = 2 * px + py
    n_ada = ada_w.shape[2]
    n_kvada = kv_ada_w.shape[1]
    shard_cols = d // N_CHIPS

    def as_rows(a):
        return a.reshape(-1, shard_cols)

    pack1 = jnp.concatenate([as_rows(c), a_lb_logits, a_out_gain,
                             jnp.zeros((1, shard_cols), F32)], axis=0)
    got1 = allgather8("gather_cond", pack1)
    n_c = d // shard_cols
    c_all = got1[:, :n_c, :].reshape(N_DEV, d)
    per_chip = got1[0::2]
    logits_full = jnp.swapaxes(per_chip[:, n_c:n_c + 2, :], 0, 1).reshape(2, d)
    out_gain_full = per_chip[:, n_c + 2, :].reshape(1, d)
    lb = lower_bound_fwd("lower_bound", logits_full)

    bias0 = lax.dynamic_slice_in_dim(ada_b, chip * n_ada, n_ada, axis=1)
    bias_kv = lax.dynamic_slice_in_dim(kv_ada_b.reshape(1, -1), chip * n_kvada, n_kvada, axis=1)
    mod_part = jnp.concatenate([
        ada_project("ada_proj_0", c_all, ada_w[0], bias0[0:1]),
        ada_project("ada_proj_1", c_all, ada_w[1], bias0[1:2]),
        ada_project("ada_proj_kv", c_all, kv_ada_w, bias_kv)], axis=1)
    got2 = allgather8("gather_mod", mod_part)
    mine = lax.dynamic_index_in_dim(got2[0::2], me, axis=1, keepdims=False)
    mod0 = mine[:, :n_ada].reshape(6, d)
    mod1 = mine[:, n_ada:2 * n_ada].reshape(6, d)
    kvmod = mine[:, 2 * n_ada:].reshape(2, d)

    shards = dict(a_in=a_w_in[0], w1_0=mlp_w1[0], w1_1=mlp_w1[1], w2_0=mlp_w2[0], w2_1=mlp_w2[1],
                  a_out=a_w_out[0], bq=b_w_q[0], bo=b_w_out[0], kv=w_kv)
    gathered = gather_weights("gather_weights", [shards[k].astype(BF16) for k in BIG])
    w = {}
    for k, g in zip(BIG, gathered):
        w[k] = g if k in ("a_in", "w1_0", "w1_1", "w2_0", "w2_1") else g.reshape(-1, g.shape[-1])

    loss, dx0, grads, dmod0, dmod1, dkvmod, small = local_step(
        x[0], loss_target[0], mod0, mod1, kvmod, norm_mix, norm_mlp, kv_norm.reshape(1, d),
        final_norm.reshape(1, d), lb, out_gain_full, w)

    loss_row = jnp.concatenate([loss, jnp.zeros((1, shard_cols - loss.shape[1]), F32)], axis=1)
    rows = [as_rows(dmod0), as_rows(dmod1), as_rows(dkvmod),
            as_rows(small["norm_mix"][0]), as_rows(small["norm_mix"][1]),
            as_rows(small["norm_mlp"][0]), as_rows(small["norm_mlp"][1]),
            as_rows(small["kv_norm"]), as_rows(small["final_norm"]),
            as_rows(small["lb"]), as_rows(small["out_gain"]), loss_row]
    n_rows = sum(r.shape[0] for r in rows)
    pad = (-n_rows) % 8
    pack3 = jnp.concatenate(rows + [jnp.zeros((pad, shard_cols), F32)], axis=0)
    got3 = allgather8("gather_small_grads", pack3)
    total = device_sum("sum_small_grads", got3)
    n_mod_rows = (12 * d + 2 * d) // shard_cols
    n_gain_rows = 6 * n_c
    loss_out = total[n_mod_rows + n_gain_rows + 2 * n_c, 0]

    dmod_all = got3[:, :n_mod_rows, :].reshape(N_DEV, 14 * d)
    act_t = jnp.zeros((d, 128), F32).at[:, :N_DEV].set(c_all.T)

    def ada_update(name, lo, width, wgt, m_, v_):
        part = lax.dynamic_slice_in_dim(dmod_all, lo + chip * width, width, axis=1)
        part = jnp.zeros((128, width), F32).at[:N_DEV].set(part)
        return ada_grad_adamw(name, act_t, part, wgt, m_, v_)

    ada0 = ada_update("ada_update_0", 0, n_ada, ada_w[0], m_ada_w[0], v_ada_w[0])
    ada1 = ada_update("ada_update_1", 6 * d, n_ada, ada_w[1], m_ada_w[1], v_ada_w[1])
    g_ada_w, d_ada_w, nm_ada_w, nv_ada_w = (jnp.stack([a0, a1]) for a0, a1 in zip(ada0, ada1))
    g_kv_ada_w, d_kv_ada_w, nm_kv_ada_w, nv_kv_ada_w = ada_update(
        "ada_update_kv", 12 * d, n_kvada, kv_ada_w, m_kv_ada_w, v_kv_ada_w)

    glist = []
    for k in BIG:
        g = grads[k]
        glist.append(g if g.ndim == 3 else g.reshape(N_CHIPS, g.shape[0] // N_CHIPS, g.shape[1]))
    from_sibling = sibling_exchange("grad_sibling_exchange", glist)
    core = pc.astype(jnp.int32).reshape(1)
    chip_parts = [pair_add("grad_pair_add_" + k, g, r, core) for k, g, r in zip(BIG, glist, from_sibling)]
    from_chips = chip_scatter("grad_chip_scatter", chip_parts)
    halves = [chip_sum("grad_chip_sum_" + k, r) for k, r in zip(BIG, from_chips)]
    out_shapes = [_sds(a_w_in.shape, F32), _sds(mlp_w1.shape, F32), _sds(mlp_w2.shape, F32),
                  _sds(a_w_out.shape, F32), _sds(b_w_q.shape, F32), _sds(b_w_out.shape, F32),
                  _sds(w_kv.shape, F32)]
    places = [(0, 0), (1, 0), (1, 1), (2, 0), (2, 1), (3, 0), (4, 0), (5, 0), (6, None)]
    g_a_w_in, g_mlp_w1, g_mlp_w2, g_a_w_out, g_b_w_q, g_b_w_out, g_w_kv = sibling_share(
        "grad_sibling_share", halves, out_shapes, places)

    def update(name, wgt, g, m_, v_):
        shape = wgt.shape
        f = lambda a: a.reshape(-1, shape[-1])
        return tuple(o.reshape(shape) for o in adamw(name, f(wgt), f(g), f(m_), f(v_)))

    u_a_w_in = update("adamw_a_w_in", a_w_in, g_a_w_in, m_a_w_in, v_a_w_in)
    u_mlp_w1 = update("adamw_mlp_w1", mlp_w1, g_mlp_w1, m_mlp_w1, v_mlp_w1)
    u_mlp_w2 = update("adamw_mlp_w2", mlp_w2, g_mlp_w2, m_mlp_w2, v_mlp_w2)
    u_a_w_out = update("adamw_a_w_out", a_w_out, g_a_w_out, m_a_w_out, v_a_w_out)
    u_b_w_q = update("adamw_b_w_q", b_w_q, g_b_w_q, m_b_w_q, v_b_w_q)
    u_b_w_out = update("adamw_b_w_out", b_w_out, g_b_w_out, m_b_w_out, v_b_w_out)
    u_w_kv = update("adamw_w_kv", w_kv, g_w_kv, m_w_kv, v_w_kv)

    base = n_mod_rows + n_gain_rows
    d_lb_mine = lax.dynamic_slice_in_dim(total, base + chip, 1, axis=0)
    d_gain_mine = lax.dynamic_slice_in_dim(total, base + n_c + chip, 1, axis=0)
    d_logits = lower_bound_bwd("lower_bound_bwd", a_lb_logits, d_lb_mine)

    def pack_small(ada_b_, kv_ada_b_, norm_mix_, norm_mlp_, kv_norm_, final_norm_, lbl_, gain_):
        parts = [as_rows(ada_b_), as_rows(kv_ada_b_), as_rows(norm_mix_), as_rows(norm_mlp_),
                 as_rows(kv_norm_), as_rows(final_norm_), lbl_, gain_]
        n = sum(p.shape[0] for p in parts)
        return jnp.concatenate(parts + [jnp.zeros(((-n) % 8, shard_cols), F32)], axis=0)

    w_small = pack_small(ada_b, kv_ada_b, norm_mix, norm_mlp, kv_norm, final_norm, a_lb_logits, a_out_gain)
    m_small = pack_small(m_ada_b, m_kv_ada_b, m_norm_mix, m_norm_mlp, m_kv_norm, m_final_norm,
                         m_a_lb_logits, m_a_out_gain)
    v_small = pack_small(v_ada_b, v_kv_ada_b, v_norm_mix, v_norm_mlp, v_kv_norm, v_final_norm,
                         v_a_lb_logits, v_a_out_gain)
    n_small = w_small.shape[0]
    g_small = jnp.concatenate([total[:base], d_logits, d_gain_mine,
                               jnp.zeros((n_small - base - 3, shard_cols), F32)], axis=0)
    small_out = (g_small,) + tuple(adamw("adamw_small", w_small, g_small, m_small, v_small))

    def unpack_small(p):
        out, r0 = [], 0
        for ref in (ada_b, kv_ada_b, norm_mix, norm_mlp, kv_norm, final_norm, a_lb_logits, a_out_gain):
            nr = ref.size // shard_cols
            out.append(p[r0:r0 + nr].reshape(ref.shape))
            r0 += nr
        return out

    sm = [unpack_small(p) for p in small_out]

    def leaves(kind, big_ada, big_kv_ada):
        ada_b_, kv_ada_b_, norm_mix_, norm_mlp_, kv_norm_, final_norm_, lbl_, gain_ = sm[kind]
        pick = (lambda u, g: g) if kind == 0 else (lambda u, g: u[kind - 1])
        return [big_ada, ada_b_, norm_mix_, norm_mlp_, pick(u_a_w_in, g_a_w_in), lbl_, gain_,
                pick(u_a_w_out, g_a_w_out), big_kv_ada, kv_ada_b_, kv_norm_, pick(u_w_kv, g_w_kv),
                pick(u_b_w_q, g_b_w_q), pick(u_b_w_out, g_b_w_out), pick(u_mlp_w1, g_mlp_w1),
                pick(u_mlp_w2, g_mlp_w2), final_norm_]

    return (loss_out, dx0[None],
            *leaves(0, g_ada_w, g_kv_ada_w), *leaves(1, d_ada_w, d_kv_ada_w),
            *leaves(2, nm_ada_w, nm_kv_ada_w), *leaves(3, nv_ada_w, nv_kv_ada_w))
```

```python
import functools

import jax
import jax.numpy as jnp
from jax import lax
from jax.experimental import pallas as pl
from jax.experimental.pallas import tpu as pltpu

F32 = jnp.float32
BF16 = jnp.bfloat16
MESH = pl.DeviceIdType.MESH

HEAD_DIM = 128
KV_GROUP = 4
HGRN_CHUNK = 64
NORM_EPS = 1e-6
N_CHIPS = 4
N_DEV = 8
ROW_TILE = 256
ATTN_TILE = 256
VMEM_LIMIT = 56 * 1024 * 1024
DEAD_LOG_WEIGHT = -110.0

ADAM_LR = 0.001
ADAM_B1 = 0.9
ADAM_B2 = 0.999
ADAM_EPS = 1e-08
ADAM_WD = 0.01
ADAM_STEP = 10

NN = (((1,), (0,)), ((), ()))
NT = (((1,), (1,)), ((), ()))
TN = (((0,), (0,)), ((), ()))


def _dot(a, b, dims=NN, precision=None):
    return lax.dot_general(a, b, dims, preferred_element_type=F32, precision=precision)


def _bdot(a, b, dims=NN):
    return _dot(a.astype(BF16), b.astype(BF16), dims)


def _sigmoid(x):
    return 1.0 / (1.0 + jnp.exp(-x))


def _log_sigmoid(z):
    return jnp.minimum(z, 0.0) - jnp.log(1.0 + jnp.exp(-jnp.abs(z)))


def _split_bf16(x):
    hi = x.astype(BF16)
    lo = (x - hi.astype(F32)).astype(BF16)
    return hi, lo


def _params(sem=None):
    return pltpu.CompilerParams(dimension_semantics=sem, vmem_limit_bytes=VMEM_LIMIT)


def _tile(n, pref):
    t = min(n, pref)
    assert n % t == 0, (n, pref)
    return t


def _mm(name, a, b, a_spec, b_spec, grid, n_red, dims, out_shapes, out_specs,
        acc_shape, epilogue=None, extras=(), extra_specs=()):
    n_extra = len(extras)
    n_out = len(out_shapes)
    if epilogue is None:
        epilogue = lambda acc: (acc,)

    def body(*refs):
        a_ref, b_ref = refs[:2]
        ex_refs = refs[2:2 + n_extra]
        out_refs = refs[2 + n_extra:2 + n_extra + n_out]
        prod = _bdot(a_ref[...], b_ref[...], dims)

        def finish(acc):
            res = epilogue(acc, *[e[...] for e in ex_refs])
            for o_ref, r in zip(out_refs, res):
                o_ref[...] = r.astype(o_ref.dtype)

        if n_red == 0:
            finish(prod)
            return
        acc_ref = refs[-1]
        ids = [pl.program_id(len(grid) - n_red + r) for r in range(n_red)]
        sizes = grid[len(grid) - n_red:]
        first = functools.reduce(jnp.logical_and, [i == 0 for i in ids])
        last = functools.reduce(jnp.logical_and, [i == s - 1 for i, s in zip(ids, sizes)])

        @pl.when(first)
        def _():
            acc_ref[...] = prod

        @pl.when(jnp.logical_not(first))
        def _():
            acc_ref[...] += prod

        @pl.when(last)
        def _():
            finish(acc_ref[...])

    sem = ("parallel",) * (len(grid) - n_red) + ("arbitrary",) * n_red
    out = pl.pallas_call(
        body, name=name, grid=grid,
        in_specs=[a_spec, b_spec, *extra_specs],
        out_specs=list(out_specs),
        out_shape=list(out_shapes),
        scratch_shapes=[pltpu.VMEM(acc_shape, F32)] if n_red else [],
        compiler_params=_params(sem),
    )(a, b, *extras)
    return out


def _sds(shape, dtype):
    return jax.ShapeDtypeStruct(shape, dtype)


def mm_nn_b(name, a, w3, out_dtypes, epilogue=None, extras=(), extra_kinds=()):
    m, k = a.shape
    nb, _, n = w3.shape
    tm, tn = _tile(m, 1024), _tile(n, 512)
    grid = (nb, m // tm, n // tn)
    nt = n // tn
    especs = []
    for kind in extra_kinds:
        if kind == "tile":
            especs.append(pl.BlockSpec((None, tm, tn), lambda j, i, c: (j, i, c)))
        else:
            especs.append(pl.BlockSpec((1, tn), lambda j, i, c: (0, j * nt + c)))
    return _mm(name, a, w3,
               pl.BlockSpec((tm, k), lambda j, i, c: (i, 0)),
               pl.BlockSpec((None, k, tn), lambda j, i, c: (j, 0, c)),
               grid, 0, NN,
               [_sds((nb, m, n), d) for d in out_dtypes],
               [pl.BlockSpec((None, tm, tn), lambda j, i, c: (j, i, c)) for _ in out_dtypes],
               None, epilogue, extras, especs)


def mm_nn_r(name, a3, w3, out_dtypes, epilogue=None, extras=(), extra_kinds=()):
    nb, m, kb = a3.shape
    n = w3.shape[2]
    tm, tn, tk = _tile(m, 1024), _tile(n, 512), _tile(kb, 2048)
    grid = (m // tm, n // tn, nb, kb // tk)
    especs = []
    for kind in extra_kinds:
        if kind == "tile":
            especs.append(pl.BlockSpec((tm, tn), lambda i, c, j, r: (i, c)))
        else:
            especs.append(pl.BlockSpec((1, tn), lambda i, c, j, r: (0, c)))
    return _mm(name, a3, w3,
               pl.BlockSpec((None, tm, tk), lambda i, c, j, r: (j, i, r)),
               pl.BlockSpec((None, tk, tn), lambda i, c, j, r: (j, r, c)),
               grid, 2, NN,
               [_sds((m, n), d) for d in out_dtypes],
               [pl.BlockSpec((tm, tn), lambda i, c, j, r: (i, c)) for _ in out_dtypes],
               (tm, tn), epilogue, extras, especs)


def mm_nt_b(name, a, w3, out_dtypes, epilogue=None, extras=()):
    m, n = a.shape
    nb, kb, _ = w3.shape
    tm, tk = _tile(m, 1024), _tile(kb, 512)
    grid = (nb, m // tm, kb // tk)
    especs = [pl.BlockSpec((None, tm, tk), lambda j, i, c: (j, i, c)) for _ in extras]
    return _mm(name, a, w3,
               pl.BlockSpec((tm, n), lambda j, i, c: (i, 0)),
               pl.BlockSpec((None, tk, n), lambda j, i, c: (j, c, 0)),
               grid, 0, NT,
               [_sds((nb, m, kb), d) for d in out_dtypes],
               [pl.BlockSpec((None, tm, tk), lambda j, i, c: (j, i, c)) for _ in out_dtypes],
               None, epilogue, extras, especs)


def mm_nt_r(name, a3, w3, out_dtype):
    nb, m, n = a3.shape
    k = w3.shape[1]
    tm, tk, tc = _tile(m, 1024), _tile(k, 1024), _tile(n, 2048)
    grid = (m // tm, k // tk, nb, n // tc)
    return _mm(name, a3, w3,
               pl.BlockSpec((None, tm, tc), lambda i, c, j, r: (j, i, r)),
               pl.BlockSpec((None, tk, tc), lambda i, c, j, r: (j, c, r)),
               grid, 2, NT,
               [_sds((m, k), out_dtype)],
               [pl.BlockSpec((tm, tk), lambda i, c, j, r: (i, c))],
               (tm, tk))[0]


def mm_tn(name, a3, d3, out_dtype):
    na, m, kb = a3.shape
    nd, _, n = d3.shape
    nb = max(na, nd)
    tk, tn, tm = _tile(kb, 512), _tile(n, 2048), _tile(m, 1024)
    grid = (nb, kb // tk, n // tn, m // tm)
    ja = (lambda j: j) if na > 1 else (lambda j: 0)
    jd = (lambda j: j) if nd > 1 else (lambda j: 0)
    return _mm(name, a3, d3,
               pl.BlockSpec((None, tm, tk), lambda j, c, e, r: (ja(j), r, c)),
               pl.BlockSpec((None, tm, tn), lambda j, c, e, r: (jd(j), r, e)),
               grid, 1, TN,
               [_sds((nb, kb, n), out_dtype)],
               [pl.BlockSpec((None, tk, tn), lambda j, c, e, r: (j, c, e))],
               (tk, tn))[0]


def _row_spec(ts, d):
    return pl.BlockSpec((ts, d), lambda i: (i, 0))


def _vec_spec(d):
    return pl.BlockSpec((1, d), lambda i: (0, 0))


def norm_mod_fwd(name, x, gain, scale, shift):
    s, d = x.shape
    ts = _tile(s, ROW_TILE)

    def body(x_ref, g_ref, sc_ref, sh_ref, h_ref):
        xv = x_ref[...]
        inv = lax.rsqrt(jnp.mean(xv * xv, axis=-1, keepdims=True) + NORM_EPS)
        h = (xv * inv) * g_ref[...] * (1.0 + sc_ref[...]) + sh_ref[...]
        h_ref[...] = h.astype(h_ref.dtype)

    return pl.pallas_call(
        body, name=name, grid=(s // ts,),
        in_specs=[_row_spec(ts, d), _vec_spec(d), _vec_spec(d), _vec_spec(d)],
        out_specs=_row_spec(ts, d), out_shape=_sds((s, d), BF16),
        compiler_params=_params(("parallel",)),
    )(x, gain, scale, shift)


def norm_mod_bwd(name, dh, x, gain, scale, dres):
    s, d = x.shape
    ts = _tile(s, ROW_TILE)

    def body(dh_ref, x_ref, g_ref, sc_ref, dres_ref, dx_ref, dsh_ref, dsc_ref, dg_ref):
        @pl.when(pl.program_id(0) == 0)
        def _():
            dsh_ref[...] = jnp.zeros_like(dsh_ref)
            dsc_ref[...] = jnp.zeros_like(dsc_ref)
            dg_ref[...] = jnp.zeros_like(dg_ref)

        xv = x_ref[...]
        dhv = dh_ref[...].astype(F32)
        g = g_ref[...]
        inv = lax.rsqrt(jnp.mean(xv * xv, axis=-1, keepdims=True) + NORM_EPS)
        n = xv * inv
        dhn = dhv * (1.0 + sc_ref[...])
        dn = dhn * g
        dx = inv * (dn - n * jnp.mean(dn * n, axis=-1, keepdims=True))
        dx_ref[...] = dres_ref[...] + dx
        dsh_ref[...] += jnp.sum(dhv, axis=0, keepdims=True)
        dsc_ref[...] += jnp.sum(dhv * (n * g), axis=0, keepdims=True)
        dg_ref[...] += jnp.sum(dhn * n, axis=0, keepdims=True)

    return pl.pallas_call(
        body, name=name, grid=(s // ts,),
        in_specs=[_row_spec(ts, d), _row_spec(ts, d), _vec_spec(d), _vec_spec(d), _row_spec(ts, d)],
        out_specs=[_row_spec(ts, d), _vec_spec(d), _vec_spec(d), _vec_spec(d)],
        out_shape=[_sds((s, d), F32), _sds((1, d), F32), _sds((1, d), F32), _sds((1, d), F32)],
        compiler_params=_params(("arbitrary",)),
    )(dh, x, gain, scale, dres)


def gate_bwd(name, dx, y, gate):
    s, d = dx.shape
    ts = _tile(s, ROW_TILE)

    def body(dx_ref, y_ref, g_ref, dy_ref, dg_ref):
        @pl.when(pl.program_id(0) == 0)
        def _():
            dg_ref[...] = jnp.zeros_like(dg_ref)

        dxv = dx_ref[...]
        dy_ref[...] = (dxv * g_ref[...]).astype(dy_ref.dtype)
        dg_ref[...] += jnp.sum(dxv * y_ref[...].astype(F32), axis=0, keepdims=True)

    return pl.pallas_call(
        body, name=name, grid=(s // ts,),
        in_specs=[_row_spec(ts, d), _row_spec(ts, d), _vec_spec(d)],
        out_specs=[_row_spec(ts, d), _vec_spec(d)],
        out_shape=[_sds((s, d), BF16), _sds((1, d), F32)],
        compiler_params=_params(("arbitrary",)),
    )(dx, y, gate)


def final_loss(name, x, gain, target):
    s, d = x.shape
    ts = _tile(s, ROW_TILE)

    def body(x_ref, g_ref, t_ref, dx_ref, dg_ref, loss_ref):
        @pl.when(pl.program_id(0) == 0)
        def _():
            dg_ref[...] = jnp.zeros_like(dg_ref)
            loss_ref[...] = jnp.zeros_like(loss_ref)

        xv = x_ref[...]
        g = g_ref[...]
        inv = lax.rsqrt(jnp.mean(xv * xv, axis=-1, keepdims=True) + NORM_EPS)
        n = xv * inv
        diff = n * g - t_ref[...]
        per_tok = jnp.mean(diff * diff, axis=-1, keepdims=True)
        loss_ref[...] += 0.5 * jnp.sum(per_tok, axis=0, keepdims=True)
        dy = diff * (1.0 / d)
        dg_ref[...] += jnp.sum(dy * n, axis=0, keepdims=True)
        dn = dy * g
        dx_ref[...] = inv * (dn - n * jnp.mean(dn * n, axis=-1, keepdims=True))

    return pl.pallas_call(
        body, name=name, grid=(s // ts,),
        in_specs=[_row_spec(ts, d), _vec_spec(d), _row_spec(ts, d)],
        out_specs=[_row_spec(ts, d), _vec_spec(d), _vec_spec(128)],
        out_shape=[_sds((s, d), F32), _sds((1, d), F32), _sds((1, 128), F32)],
        compiler_params=_params(("arbitrary",)),
    )(x, gain, target)


def _hgrn_chunk_fwd(qr, fl, lbv, tri):
    sg = _sigmoid(fl)
    sgm = _sigmoid(-fl)
    f = lbv + (1.0 - lbv) * sg
    logf = jnp.log(f)
    k = (1.0 - lbv) * sgm
    cum = _dot(tri, logf, precision=lax.Precision.HIGHEST)
    cl = cum[HGRN_CHUNK - 1:HGRN_CHUNK, :]
    e = jnp.exp(cum)
    en = jnp.exp(-cum)
    es = jnp.exp(cl - cum)
    sq = _sigmoid(qr)
    qs = qr * sq
    return dict(sg=sg, sgm=sgm, f=f, k=k, cum=cum, cl=cl, e=e, en=en, es=es, sq=sq, qs=qs,
                qd=qs * e, ki=k * en, ks=k * es, dec=jnp.exp(cl))


def _tri_masks(strict=False):
    r = lax.broadcasted_iota(jnp.int32, (HGRN_CHUNK, HGRN_CHUNK), 0)
    c = lax.broadcasted_iota(jnp.int32, (HGRN_CHUNK, HGRN_CHUNK), 1)
    return (r > c) if strict else (r >= c)


def hgrn_fwd(name, proj3, lb, out_gain):
    _, s, d = proj3.shape
    heads = d // HEAD_DIM
    t_rows = _tile(s, 512)
    n_t = s // t_rows
    n_c = t_rows // HGRN_CHUNK

    def body(q_ref, f_ref, i_ref, g_ref, lb_ref, gain_ref, u_ref, o_ref, st_ref, state):
        @pl.when(pl.program_id(1) == 0)
        def _():
            state[...] = jnp.zeros_like(state)

        causal = _tri_masks()
        tri = causal.astype(F32)
        lbv = lb_ref[...]
        gain = gain_ref[...]
        for ci in range(n_c):
            rows = pl.ds(ci * HGRN_CHUNK, HGRN_CHUNK)
            c = _hgrn_chunk_fwd(q_ref[rows, :], f_ref[rows, :], lbv, tri)
            v = i_ref[rows, :]
            st = state[...]
            st_ref[0, ci] = st
            scores = jnp.where(causal, _bdot(c["qd"], c["ki"], NT), 0.0)
            o = _bdot(scores, v) + _bdot(c["qd"], st, NT)
            state[...] = st * c["dec"] + _bdot(v, c["ks"], TN)
            graw = g_ref[rows, :]
            rms = lax.rsqrt(jnp.mean(o * o, axis=-1, keepdims=True) + NORM_EPS)
            u = o * rms * gain * (graw * _sigmoid(graw))
            o_ref[rows, :] = o
            u_ref[rows, :] = u.astype(u_ref.dtype)

    def pspec(blk):
        return pl.BlockSpec((None, t_rows, HEAD_DIM), lambda h, t: (blk, t, h))

    hspec = pl.BlockSpec((1, HEAD_DIM), lambda h, t: (0, h))
    ospec = pl.BlockSpec((t_rows, HEAD_DIM), lambda h, t: (t, h))
    return pl.pallas_call(
        body, name=name, grid=(heads, n_t),
        in_specs=[pspec(0), pspec(1), pspec(2), pspec(3), hspec, hspec],
        out_specs=[ospec, ospec,
                   pl.BlockSpec((1, n_c, HEAD_DIM, HEAD_DIM), lambda h, t: (h, t, 0, 0))],
        out_shape=[_sds((s, d), BF16), _sds((s, d), F32),
                   _sds((heads, s // HGRN_CHUNK, HEAD_DIM, HEAD_DIM), F32)],
        scratch_shapes=[pltpu.VMEM((HEAD_DIM, HEAD_DIM), F32)],
        compiler_params=_params(("parallel", "arbitrary")),
    )(proj3, proj3, proj3, proj3, lb, out_gain)


def hgrn_bwd(name, proj3, lb, out_gain, o, du, states):
    _, s, d = proj3.shape
    heads = d // HEAD_DIM
    t_rows = _tile(s, 512)
    n_t = s // t_rows
    n_c = t_rows // HGRN_CHUNK

    def body(q_ref, f_ref, i_ref, g_ref, lb_ref, gain_ref, o_ref, du_ref, st_ref,
             dp_ref, dlb_ref, dgain_ref, dstate):
        @pl.when(pl.program_id(1) == 0)
        def _():
            dstate[...] = jnp.zeros_like(dstate)
            dlb_ref[...] = jnp.zeros_like(dlb_ref)
            dgain_ref[...] = jnp.zeros_like(dgain_ref)

        causal = _tri_masks()
        tri = causal.astype(F32)
        tri_t = jnp.logical_not(_tri_masks(strict=True)).astype(F32)
        lbv = lb_ref[...]
        gain = gain_ref[...]
        for ci in reversed(range(n_c)):
            rows = pl.ds(ci * HGRN_CHUNK, HGRN_CHUNK)
            qr = q_ref[rows, :]
            c = _hgrn_chunk_fwd(qr, f_ref[rows, :], lbv, tri)
            v = i_ref[rows, :]
            st = st_ref[0, ci]
            ov = o_ref[rows, :]
            duv = du_ref[rows, :].astype(F32)
            graw = g_ref[rows, :]
            sgg = _sigmoid(graw)
            gate = graw * sgg
            rms = lax.rsqrt(jnp.mean(ov * ov, axis=-1, keepdims=True) + NORM_EPS)
            on = ov * rms
            dgain_ref[...] += jnp.sum(duv * on * gate, axis=0, keepdims=True)
            dgraw = duv * on * gain * (sgg * (1.0 + graw * (1.0 - sgg)))
            don = duv * gain * gate
            do = rms * (don - on * jnp.mean(don * on, axis=-1, keepdims=True))
            qd, ki, ks = c["qd"], c["ki"], c["ks"]
            p = jnp.where(causal, _bdot(qd, ki, NT), 0.0)
            dp = jnp.where(causal, _bdot(do, v, NT), 0.0)
            dst = dstate[...]
            dqd = _bdot(dp, ki) + _bdot(do, st)
            dki = _bdot(dp, qd, TN)
            dv = _bdot(p, do, TN) + _bdot(ks, dst, NT)
            dks = _bdot(v, dst)
            ddec = jnp.sum(dst * st, axis=0, keepdims=True)
            dstate[...] = dst * c["dec"] + _bdot(do, qd, TN)
            dqs = dqd * c["e"]
            dcum = dqd * qd - dki * ki - dks * ks
            dk = dki * c["en"] + dks * c["es"]
            dcl = jnp.sum(dks * ks, axis=0, keepdims=True) + ddec * c["dec"]
            dlogf = _dot(tri_t, dcum, precision=lax.Precision.HIGHEST) + dcl
            df = dlogf / c["f"]
            sg, sgm = c["sg"], c["sgm"]
            one_m_lb = 1.0 - lbv
            dlb_ref[...] += jnp.sum(df * (1.0 - sg) - dk * sgm, axis=0, keepdims=True)
            dfl = df * one_m_lb * sg * (1.0 - sg) - dk * one_m_lb * sgm * (1.0 - sgm)
            sq = c["sq"]
            dqr = dqs * (sq * (1.0 + qr * (1.0 - sq)))
            dp_ref[0, rows, :] = dqr.astype(dp_ref.dtype)
            dp_ref[1, rows, :] = dfl.astype(dp_ref.dtype)
            dp_ref[2, rows, :] = dv.astype(dp_ref.dtype)
            dp_ref[3, rows, :] = dgraw.astype(dp_ref.dtype)

    def pspec(blk):
        return pl.BlockSpec((None, t_rows, HEAD_DIM), lambda h, t: (blk, n_t - 1 - t, h))

    hspec = pl.BlockSpec((1, HEAD_DIM), lambda h, t: (0, h))
    ospec = pl.BlockSpec((t_rows, HEAD_DIM), lambda h, t: (n_t - 1 - t, h))
    return pl.pallas_call(
        body, name=name, grid=(heads, n_t),
        in_specs=[pspec(0), pspec(1), pspec(2), pspec(3), hspec, hspec, ospec, ospec,
                  pl.BlockSpec((1, n_c, HEAD_DIM, HEAD_DIM), lambda h, t: (h, n_t - 1 - t, 0, 0))],
        out_specs=[pl.BlockSpec((4, t_rows, HEAD_DIM), lambda h, t: (0, n_t - 1 - t, h)), hspec, hspec],
        out_shape=[_sds((4, s, d), BF16), _sds((1, d), F32), _sds((1, d), F32)],
        scratch_shapes=[pltpu.VMEM((HEAD_DIM, HEAD_DIM), F32)],
        compiler_params=_params(("parallel", "arbitrary")),
    )(proj3, proj3, proj3, proj3, lb, out_gain, o, du, states)


def attn_fwd(name, q, kv):
    s, dq = q.shape
    kvh = kv.shape[1] // (2 * HEAD_DIM)
    assert dq == kvh * KV_GROUP * HEAD_DIM
    tq = _tile(s, ATTN_TILE)
    scale = HEAD_DIM ** -0.5

    def body(q_ref, k_ref, v_ref, o_ref, tot_ref, cnt_ref):
        i = pl.program_id(1)
        heads = range(KV_GROUP)
        qs = [q_ref[:, g * HEAD_DIM:(g + 1) * HEAD_DIM] for g in heads]
        r_i = lax.broadcasted_iota(jnp.int32, (tq, tq), 0)
        c_i = lax.broadcasted_iota(jnp.int32, (tq, tq), 1)
        later = (r_i > c_i).astype(BF16)
        later2 = jnp.concatenate([later, later], axis=0)
        mask = c_i < r_i
        ones2 = jnp.ones((8, 2 * tq), BF16)

        def block(j, carry, masked):
            rows = pl.ds(pl.multiple_of(j * tq, tq), tq)
            kj = k_ref[rows, :]
            vj = v_ref[rows, :]
            new = []
            for g in heads:
                run, acc, tot = carry[g]
                z = _dot(qs[g], kj, NT) * scale
                lbeta = _log_sigmoid(z)
                lrest = lbeta - z
                if masked:
                    lrest = jnp.where(mask, lrest, 0.0)
                hl = jnp.concatenate(_split_bf16(lrest), axis=1)
                w = jnp.exp(lbeta + (_dot(hl, later2) + run))
                if masked:
                    w = jnp.where(mask, w, 0.0)
                acc = acc + _dot(w.astype(BF16), vj)
                run = run + jnp.sum(lrest, axis=1, keepdims=True)
                tot = tot + _dot(ones2, hl, NT)
                new.append((run, acc, tot))
            return tuple(new)

        def alive(carry):
            top = carry[0][0]
            for g in heads[1:]:
                top = jnp.maximum(top, carry[g][0])
            return jnp.max(top) > DEAD_LOG_WEIGHT

        zero = (jnp.zeros((tq, 1), F32), jnp.zeros((tq, HEAD_DIM), F32), jnp.zeros((8, tq), F32))
        carry = block(i, (zero,) * KV_GROUP, True)

        def step(state):
            jj, _, cr = state
            cr = block(i - 1 - jj, cr, False)
            return jj + 1, alive(cr), cr

        done, _, carry = lax.while_loop(lambda st: jnp.logical_and(st[0] < i, st[1]), step,
                                        (jnp.int32(0), alive(carry), carry))
        for g in heads:
            o_ref[:, g * HEAD_DIM:(g + 1) * HEAD_DIM] = carry[g][1]
            tot_ref[g] = carry[g][2][0:1, :]
        cnt_ref[pl.program_id(0), i] = done

    group = KV_GROUP * HEAD_DIM
    return pl.pallas_call(
        body, name=name, grid=(kvh, s // tq),
        in_specs=[pl.BlockSpec((tq, group), lambda kh, i: (i, kh)),
                  pl.BlockSpec((s, HEAD_DIM), lambda kh, i: (0, kh)),
                  pl.BlockSpec((s, HEAD_DIM), lambda kh, i: (0, kvh + kh))],
        out_specs=[pl.BlockSpec((tq, group), lambda kh, i: (i, kh)),
                   pl.BlockSpec((KV_GROUP, 1, tq), lambda kh, i: (kh, 0, i)),
                   pl.BlockSpec(memory_space=pltpu.SMEM)],
        out_shape=[_sds((s, dq), F32), _sds((dq // HEAD_DIM, 1, s), F32), _sds((kvh, s // tq), jnp.int32)],
        compiler_params=_params(("arbitrary", "arbitrary")),
    )(q, kv, kv)


def attn_bwd(name, q, kv, totals, visited, do):
    s, dq_cols = q.shape
    kvh = kv.shape[1] // (2 * HEAD_DIM)
    tq = _tile(s, ATTN_TILE)
    scale = HEAD_DIM ** -0.5

    def body(cnt_ref, q_ref, k_ref, v_ref, tot_ref, do_ref, dq_ref, dkv_ref):
        i = pl.program_id(1)
        first = i - jnp.clip(cnt_ref[pl.program_id(0), i], 0, i)

        @pl.when(i == 0)
        def _():
            dkv_ref[...] = jnp.zeros_like(dkv_ref)

        heads = range(KV_GROUP)
        qs = [q_ref[:, g * HEAD_DIM:(g + 1) * HEAD_DIM] for g in heads]
        dobs = [do_ref[:, g * HEAD_DIM:(g + 1) * HEAD_DIM].astype(BF16) for g in heads]
        tots = [tot_ref[g] for g in heads]
        q_all = jnp.concatenate(qs, axis=0)
        do_all = jnp.concatenate(dobs, axis=0)
        r_i = lax.broadcasted_iota(jnp.int32, (tq, tq), 0)
        c_i = lax.broadcasted_iota(jnp.int32, (tq, tq), 1)
        upto = (c_i <= r_i).astype(BF16)
        before = (c_i < r_i).astype(BF16)
        upto2 = jnp.concatenate([upto, upto], axis=1)
        before2 = jnp.concatenate([before, before], axis=1)
        mask = r_i < c_i

        def block(j, carry, masked):
            rows = pl.ds(pl.multiple_of(j * tq, tq), tq)
            kj = k_ref[rows, :]
            vj = v_ref[rows, :]
            new, dzs, wts = [], [], []
            for g in heads:
                run, drun, dq_acc = carry[g]
                zt = _dot(kj, qs[g], NT) * scale
                lbeta = _log_sigmoid(zt)
                lrest_raw = lbeta - zt
                lrest = jnp.where(mask, lrest_raw, 0.0) if masked else lrest_raw
                hl = jnp.concatenate(_split_bf16(lrest), axis=0)
                wt = jnp.exp(lbeta + (tots[g] - run - _dot(upto2, hl)))
                if masked:
                    wt = jnp.where(mask, wt, 0.0)
                dat = _dot(vj, dobs[g], NT) * wt
                dsum = drun + _dot(before2, jnp.concatenate(_split_bf16(dat), axis=0))
                dzt = dat * jnp.exp(lrest_raw) - dsum * jnp.exp(lbeta)
                if masked:
                    dzt = jnp.where(mask, dzt, 0.0)
                dzb = (dzt * scale).astype(BF16)
                dq_acc = dq_acc + _dot(dzb, kj, TN)
                dzs.append(dzb)
                wts.append(wt.astype(BF16))
                new.append((run + jnp.sum(lrest, axis=0, keepdims=True),
                            drun + jnp.sum(dat, axis=0, keepdims=True), dq_acc))
            dkv_ref[0, rows, :] += _dot(jnp.concatenate(dzs, axis=1), q_all)
            dkv_ref[1, rows, :] += _dot(jnp.concatenate(wts, axis=1), do_all)
            return tuple(new)

        zrow = jnp.zeros((1, tq), F32)
        carry = ((zrow, zrow, jnp.zeros((tq, HEAD_DIM), F32)),) * KV_GROUP
        carry = lax.fori_loop(first, i, lambda j, cr: block(j, cr, False), carry)
        carry = block(i, carry, True)
        for g in heads:
            dq_ref[:, g * HEAD_DIM:(g + 1) * HEAD_DIM] = carry[g][2].astype(dq_ref.dtype)

    group = KV_GROUP * HEAD_DIM
    qspec = pl.BlockSpec((tq, group), lambda kh, i, cnt: (i, kh))
    return pl.pallas_call(
        body, name=name,
        grid_spec=pltpu.PrefetchScalarGridSpec(
            num_scalar_prefetch=1, grid=(kvh, s // tq),
            in_specs=[qspec,
                      pl.BlockSpec((s, HEAD_DIM), lambda kh, i, cnt: (0, kh)),
                      pl.BlockSpec((s, HEAD_DIM), lambda kh, i, cnt: (0, kvh + kh)),
                      pl.BlockSpec((KV_GROUP, 1, tq), lambda kh, i, cnt: (kh, 0, i)),
                      qspec],
            out_specs=[qspec, pl.BlockSpec((2, s, HEAD_DIM), lambda kh, i, cnt: (0, 0, kh))]),
        out_shape=[_sds((s, dq_cols), BF16), _sds((2, s, kvh * HEAD_DIM), F32)],
        compiler_params=_params(("parallel", "arbitrary")),
    )(visited, q, kv, kv, totals, do)


def ada_project(name, c_all, w, b):
    bsz, d = c_all.shape
    n = w.shape[1]
    tn = _tile(n, 512)

    def body(c_ref, w_ref, b_ref, o_ref):
        cv = c_ref[...]
        act = cv * _sigmoid(cv)
        o_ref[...] = _bdot(act, w_ref[...]) + b_ref[...]

    return pl.pallas_call(
        body, name=name, grid=(n // tn,),
        in_specs=[pl.BlockSpec((bsz, d), lambda i: (0, 0)),
                  pl.BlockSpec((d, tn), lambda i: (0, i)),
                  pl.BlockSpec((1, tn), lambda i: (0, i))],
        out_specs=pl.BlockSpec((bsz, tn), lambda i: (0, i)),
        out_shape=_sds((bsz, n), F32),
        compiler_params=_params(("parallel",)),
    )(c_all, w, b)


def _adamw_math(w, g, m, v):
    m = ADAM_B1 * m + (1.0 - ADAM_B1) * g
    v = ADAM_B2 * v + (1.0 - ADAM_B2) * (g * g)
    m_hat = m / (1.0 - ADAM_B1 ** ADAM_STEP)
    v_hat = v / (1.0 - ADAM_B2 ** ADAM_STEP)
    delta = -ADAM_LR * (m_hat / (jnp.sqrt(v_hat) + ADAM_EPS) + ADAM_WD * w)
    return delta, m, v


def adamw(name, w, g, m, v):
    r, c = w.shape
    tr = _tile(r, 256)
    tc = _tile(c, 2048)

    def body(w_ref, g_ref, m_ref, v_ref, d_ref, mo_ref, vo_ref):
        delta, mn, vn = _adamw_math(w_ref[...], g_ref[...], m_ref[...], v_ref[...])
        d_ref[...] = delta
        mo_ref[...] = mn
        vo_ref[...] = vn

    spec = pl.BlockSpec((tr, tc), lambda i, j: (i, j))
    return pl.pallas_call(
        body, name=name, grid=(r // tr, c // tc),
        in_specs=[spec] * 4, out_specs=[spec] * 3,
        out_shape=[_sds((r, c), F32)] * 3,
        compiler_params=_params(("parallel", "parallel")),
    )(w, g, m, v)


def ada_grad_adamw(name, c_t, dmod, w, m, v):
    layers, d, n = w.shape
    tr = _tile(d, 256)
    tc = _tile(n, 512)

    def body(a_ref, dm_ref, w_ref, m_ref, v_ref, g_ref, d_ref, mo_ref, vo_ref):
        cv = a_ref[...]
        g = _bdot(cv * _sigmoid(cv), dm_ref[...])
        delta, mn, vn = _adamw_math(w_ref[...], g, m_ref[...], v_ref[...])
        g_ref[...] = g
        d_ref[...] = delta
        mo_ref[...] = mn
        vo_ref[...] = vn

    spec = pl.BlockSpec((None, tr, tc), lambda l, i, j: (l, i, j))
    return pl.pallas_call(
        body, name=name, grid=(layers, d // tr, n // tc),
        in_specs=[pl.BlockSpec((tr, 128), lambda l, i, j: (i, 0)),
                  pl.BlockSpec((None, 128, tc), lambda l, i, j: (l, 0, j)), spec, spec, spec],
        out_specs=[spec] * 4, out_shape=[_sds((layers, d, n), F32)] * 4,
        compiler_params=_params(("parallel", "parallel", "parallel")),
    )(c_t, dmod, w, m, v)


def device_sum(name, gathered):
    _, r, c = gathered.shape

    def body(g_ref, o_ref):
        acc = g_ref[0]
        for dev in range(1, N_DEV):
            acc = acc + g_ref[dev]
        o_ref[...] = acc

    return pl.pallas_call(
        body, name=name,
        in_specs=[pl.BlockSpec(memory_space=pltpu.VMEM)],
        out_specs=pl.BlockSpec(memory_space=pltpu.VMEM),
        out_shape=_sds((r, c), F32),
    )(gathered)


def lower_bound_fwd(name, logits):
    _, d = logits.shape

    def body(l_ref, o_ref):
        l0 = l_ref[0:1, :]
        l1 = l_ref[1:2, :]
        mx = jnp.maximum(l0, l1)
        e0 = jnp.exp(l0 - mx)
        e1 = jnp.exp(l1 - mx)
        o_ref[...] = e0 / (e0 + e1)

    return pl.pallas_call(
        body, name=name,
        in_specs=[pl.BlockSpec(memory_space=pltpu.VMEM)],
        out_specs=pl.BlockSpec(memory_space=pltpu.VMEM),
        out_shape=_sds((1, d), F32),
    )(logits)


def lower_bound_bwd(name, logits, dlb):
    _, d = logits.shape

    def body(l_ref, dlb_ref, o_ref):
        l0 = l_ref[0:1, :]
        l1 = l_ref[1:2, :]
        mx = jnp.maximum(l0, l1)
        e0 = jnp.exp(l0 - mx)
        e1 = jnp.exp(l1 - mx)
        p0 = e0 / (e0 + e1)
        p1 = e1 / (e0 + e1)
        g = dlb_ref[...] * p0 * p1
        o_ref[0:1, :] = g
        o_ref[1:2, :] = -g

    return pl.pallas_call(
        body, name=name,
        in_specs=[pl.BlockSpec(memory_space=pltpu.VMEM)] * 2,
        out_specs=pl.BlockSpec(memory_space=pltpu.VMEM),
        out_shape=_sds((2, d), F32),
    )(logits, dlb)


HBM = pl.BlockSpec(memory_space=pltpu.HBM)


def _position():
    return lax.axis_index("x"), lax.axis_index("y"), lax.axis_index("c")


def _remote(src, dst, send_sem, recv_sem, device):
    return pltpu.make_async_remote_copy(src_ref=src, dst_ref=dst, send_sem=send_sem, recv_sem=recv_sem,
                                        device_id=device, device_id_type=MESH)


def allgather8(name, x):
    r, c = x.shape

    def body(x_ref, out_ref, send_sems, recv_sems):
        px, py, pc = _position()
        me = 4 * px + 2 * py + pc
        out_ref[me] = x_ref[...]
        copies = []
        for k in range(1, N_DEV):
            peer = (1 - px if k & 4 else px, 1 - py if k & 2 else py, 1 - pc if k & 1 else pc)
            cp = _remote(x_ref, out_ref.at[me], send_sems.at[k - 1], recv_sems.at[k - 1], peer)
            cp.start()
            copies.append(cp)
        for cp in copies:
            cp.wait()

    return pl.pallas_call(
        body, name=name,
        in_specs=[pl.BlockSpec(memory_space=pltpu.VMEM)],
        out_specs=pl.BlockSpec(memory_space=pltpu.VMEM),
        out_shape=_sds((N_DEV, r, c), x.dtype),
        scratch_shapes=[pltpu.SemaphoreType.DMA((N_DEV - 1,)), pltpu.SemaphoreType.DMA((N_DEV - 1,))],
    )(x)


def _other_chips(px, py):
    return [(1 - px, py), (px, 1 - py), (1 - px, 1 - py)]


def gather_weights(name, shards):
    n = len(shards)

    def body(*refs):
        ins, outs = refs[:n], refs[n:2 * n]
        send_sems, recv_sems = refs[2 * n:]
        px, py, pc = _position()
        chip = 2 * px + py
        sibling = (px, py, 1 - pc)
        others = _other_chips(px, py)

        def half(a, h):
            rows = ins[a].shape[0] // 2
            return pl.ds(h * rows, rows)

        own = []
        for a in range(n):
            cp = _remote(ins[a], outs[a].at[chip], send_sems.at[a, 6], recv_sems.at[a, 6], sibling)
            cp.start()
            own.append(cp)
        sends = []
        for a in range(n):
            for j, (ox, oy) in enumerate(others):
                cp = _remote(ins[a].at[half(a, pc)], outs[a].at[chip, half(a, pc)],
                             send_sems.at[a, j], recv_sems.at[a, j], (ox, oy, pc))
                cp.start()
                sends.append(cp)
        passed = []
        for a in range(n):
            for j, (ox, oy) in enumerate(others):
                landed = outs[a].at[2 * ox + oy, half(a, pc)]
                _remote(landed, landed, send_sems.at[a, j], recv_sems.at[a, j], sibling).wait_recv()
                cp = _remote(landed, landed, send_sems.at[a, 3 + j], recv_sems.at[a, 3 + j], sibling)
                cp.start()
                passed.append(cp)
        for a in range(n):
            for j, (ox, oy) in enumerate(others):
                landed = outs[a].at[2 * ox + oy, half(a, 1 - pc)]
                _remote(landed, landed, send_sems.at[a, 3 + j], recv_sems.at[a, 3 + j], sibling).wait_recv()
        for cp in sends + passed:
            cp.wait_send()
        for cp in own:
            cp.wait()

    return pl.pallas_call(
        body, name=name,
        in_specs=[HBM] * n, out_specs=[HBM] * n,
        out_shape=[_sds((N_CHIPS,) + s.shape, s.dtype) for s in shards],
        scratch_shapes=[pltpu.SemaphoreType.DMA((n, 7)), pltpu.SemaphoreType.DMA((n, 7))],
    )(*shards)


def sibling_exchange(name, grads):
    n = len(grads)

    def body(*refs):
        ins, outs = refs[:n], refs[n:2 * n]
        send_sems, recv_sems = refs[2 * n:]
        px, py, pc = _position()
        sibling = (px, py, 1 - pc)
        copies = []
        for a in range(n):
            rows = ins[a].shape[1] // 2
            src = ins[a].at[:, pl.ds((1 - pc) * rows, rows), :]
            cp = _remote(src, outs[a], send_sems.at[a], recv_sems.at[a], sibling)
            cp.start()
            copies.append(cp)
        for cp in copies:
            cp.wait()

    return pl.pallas_call(
        body, name=name,
        in_specs=[HBM] * n, out_specs=[HBM] * n,
        out_shape=[_sds((g.shape[0], g.shape[1] // 2, g.shape[2]), g.dtype) for g in grads],
        scratch_shapes=[pltpu.SemaphoreType.DMA((n,)), pltpu.SemaphoreType.DMA((n,))],
    )(*grads)


def pair_add(name, g, recv, core):
    nb, rows, cols = g.shape
    half = rows // 2
    tr = _tile(half, 256)
    steps = half // tr

    def body(core_ref, g_ref, r_ref, o_ref):
        del core_ref
        o_ref[...] = (g_ref[...].astype(F32) + r_ref[...].astype(F32)).astype(o_ref.dtype)

    return pl.pallas_call(
        body, name=name,
        grid_spec=pltpu.PrefetchScalarGridSpec(
            num_scalar_prefetch=1, grid=(nb, steps),
            in_specs=[pl.BlockSpec((None, tr, cols), lambda j, i, core_ref: (j, core_ref[0] * steps + i, 0)),
                      pl.BlockSpec((None, tr, cols), lambda j, i, core_ref: (j, i, 0))],
            out_specs=pl.BlockSpec((None, tr, cols), lambda j, i, core_ref: (j, i, 0))),
        out_shape=_sds((nb, half, cols), g.dtype),
        compiler_params=_params(("parallel", "parallel")),
    )(core, g, recv)


def chip_scatter(name, parts):
    n = len(parts)

    def body(*refs):
        ins, outs = refs[:n], refs[n:2 * n]
        send_sems, recv_sems = refs[2 * n:]
        px, py, pc = _position()
        copies = []
        for a in range(n):
            for j, (ox, oy) in enumerate(_other_chips(px, py)):
                cp = _remote(ins[a].at[2 * ox + oy], outs[a].at[j],
                             send_sems.at[a, j], recv_sems.at[a, j], (ox, oy, pc))
                cp.start()
                copies.append(cp)
        for cp in copies:
            cp.wait()

    return pl.pallas_call(
        body, name=name,
        in_specs=[HBM] * n, out_specs=[HBM] * n,
        out_shape=[_sds((N_CHIPS - 1,) + p.shape[1:], p.dtype) for p in parts],
        scratch_shapes=[pltpu.SemaphoreType.DMA((n, 3)), pltpu.SemaphoreType.DMA((n, 3))],
    )(*parts)


def chip_sum(name, part, recv, where, out_shape, lead, dest=None):
    _, half, cols = part.shape
    tr = _tile(half, 256)
    steps = half // tr

    def body(where_ref, p_ref, r_ref, *rest):
        o_ref = rest[-1]
        acc = p_ref[...].astype(F32)
        for j in range(N_CHIPS - 1):
            acc = acc + r_ref[j].astype(F32)
        o_ref[...] = acc

    if lead is None:
        ospec = pl.BlockSpec((tr, cols), lambda i, w: (w[1] * steps + i, 0))
    else:
        ospec = pl.BlockSpec((None, tr, cols), lambda i, w: (lead, w[1] * steps + i, 0))
    in_specs = [pl.BlockSpec((None, tr, cols), lambda i, w: (w[0], i, 0)),
                pl.BlockSpec((N_CHIPS - 1, tr, cols), lambda i, w: (0, i, 0))]
    operands = [where, part, recv]
    aliases = {}
    if dest is not None:
        in_specs.append(pl.BlockSpec(memory_space=pl.ANY))
        operands.append(dest)
        aliases = {3: 0}
    return pl.pallas_call(
        body, name=name,
        grid_spec=pltpu.PrefetchScalarGridSpec(num_scalar_prefetch=1, grid=(steps,),
                                               in_specs=in_specs, out_specs=ospec),
        out_shape=out_shape, input_output_aliases=aliases,
        compiler_params=_params(("parallel",)),
    )(*operands)


def sibling_share(name, slabs, places):
    n = len(slabs)
    k = len(places)

    def body(*refs):
        outs = refs[n:2 * n]
        send_sems, recv_sems = refs[2 * n:]
        px, py, pc = _position()
        sibling = (px, py, 1 - pc)
        copies = []
        for a, (oi, lead, half) in enumerate(places):
            slab = outs[oi] if lead is None else outs[oi].at[lead]
            mine = slab.at[pl.ds(pc * half, half)]
            theirs = slab.at[pl.ds((1 - pc) * half, half)]
            cp = _remote(mine, mine, send_sems.at[a], recv_sems.at[a], sibling)
            cp.start()
            copies.append((cp, _remote(theirs, theirs, send_sems.at[a], recv_sems.at[a], sibling)))
        for cp, arrival in copies:
            cp.wait_send()
            arrival.wait_recv()

    return pl.pallas_call(
        body, name=name,
        in_specs=[HBM] * n, out_specs=[HBM] * n,
        out_shape=[_sds(s.shape, s.dtype) for s in slabs],
        input_output_aliases={a: a for a in range(n)},
        scratch_shapes=[pltpu.SemaphoreType.DMA((k,)), pltpu.SemaphoreType.DMA((k,))],
    )(*slabs)


def _row(a, i):
    return a[i:i + 1]


def _relu_sq(acc):
    r = jnp.maximum(acc, 0.0)
    return r, r * r


def _residual(acc, res, gate):
    return res + gate * acc, acc


def _mlp_fwd(tag, x, gain, mod, w1, w2):
    h = norm_mod_fwd(tag + "_mlp_norm", x, gain, _row(mod, 4), _row(mod, 3))
    r3, a3 = mm_nn_b(tag + "_mlp_up", h, w1, [BF16, BF16], _relu_sq)
    x_out, m = mm_nn_r(tag + "_mlp_down", a3, w2, [F32, BF16], _residual,
                       (x, _row(mod, 5)), ("tile", "row"))
    return x_out, (h, r3, a3, m)


def _mlp_bwd(tag, dx_out, x, gain, mod, w1, w2, saved):
    h, r3, a3, m = saved
    dm, dgate = gate_bwd(tag + "_mlp_gate_bwd", dx_out, m, _row(mod, 5))
    dz3 = mm_nt_b(tag + "_mlp_down_dgrad", dm, w2, [BF16],
                  lambda acc, r: (acc * (2.0 * r.astype(F32)),), (r3,))[0]
    gw2 = mm_tn(tag + "_mlp_w2_grad", a3, dm[None], BF16)
    gw1 = mm_tn(tag + "_mlp_w1_grad", h[None], dz3, BF16)
    dh = mm_nt_r(tag + "_mlp_up_dgrad", dz3, w1, F32)
    dx, dsh, dsc, dgain = norm_mod_bwd(tag + "_mlp_norm_bwd", dh, x, gain, _row(mod, 4), dx_out)
    return dx, gw1, gw2, (dsh, dsc, dgate), dgain


def local_step(x, target, mod0, mod1, kvmod, norm_mix, norm_mlp, kv_norm, final_norm, lb, out_gain, w):
    h1 = norm_mod_fwd("l0_mix_norm", x, _row(norm_mix, 0), _row(mod0, 1), _row(mod0, 0))
    proj3 = mm_nn_b("l0_in_proj", h1, w["a_in"], [F32])[0]
    u, o_h, states = hgrn_fwd("l0_hgrn_fwd", proj3, lb, out_gain)
    x1, y0 = mm_nn_b("l0_out_proj", u, w["a_out"][None], [F32, BF16], _residual,
                     (x[None], _row(mod0, 2)), ("tile", "row"))
    x1, y0 = x1[0], y0[0]
    x2, mlp0 = _mlp_fwd("l0", x1, _row(norm_mlp, 0), mod0, w["w1_0"], w["w2_0"])
    hk = norm_mod_fwd("kv_norm", x2, kv_norm, _row(kvmod, 1), _row(kvmod, 0))
    kv = mm_nn_b("kv_proj", hk, w["kv"][None], [BF16])[0][0]
    h3 = norm_mod_fwd("l1_mix_norm", x2, _row(norm_mix, 1), _row(mod1, 1), _row(mod1, 0))
    q = mm_nn_b("l1_q_proj", h3, w["bq"][None], [BF16])[0][0]
    o_a, totals, visited = attn_fwd("l1_attn_fwd", q, kv)
    x3, y1 = mm_nn_b("l1_out_proj", o_a, w["bo"][None], [F32, BF16], _residual,
                     (x2[None], _row(mod1, 2)), ("tile", "row"))
    x3, y1 = x3[0], y1[0]
    x4, mlp1 = _mlp_fwd("l1", x3, _row(norm_mlp, 1), mod1, w["w1_1"], w["w2_1"])
    dx4, d_final, loss = final_loss("final_loss", x4, final_norm, target)

    grads = {}
    dx3, grads["w1_1"], grads["w2_1"], dmlp1, d_nmlp1 = _mlp_bwd(
        "l1", dx4, x3, _row(norm_mlp, 1), mod1, w["w1_1"], w["w2_1"], mlp1)
    dy1, dgate1 = gate_bwd("l1_mix_gate_bwd", dx3, y1, _row(mod1, 2))
    do_a = mm_nt_b("l1_out_dgrad", dy1, w["bo"][None], [F32])[0][0]
    grads["bo"] = mm_tn("l1_out_grad", o_a[None], dy1[None], BF16)[0]
    dq, dkv3 = attn_bwd("l1_attn_bwd", q, kv, totals, visited, do_a)
    grads["bq"] = mm_tn("l1_q_grad", h3[None], dq[None], BF16)[0]
    dh3 = mm_nt_b("l1_q_dgrad", dq, w["bq"][None], [F32])[0][0]
    dx2, dsh, dsc, d_nmix1 = norm_mod_bwd("l1_mix_norm_bwd", dh3, x2, _row(norm_mix, 1), _row(mod1, 1), dx3)
    dmod1 = jnp.concatenate([dsh, dsc, dgate1, *dmlp1], axis=0)
    dkv = jnp.concatenate([dkv3[0], dkv3[1]], axis=1).astype(BF16)
    grads["kv"] = mm_tn("kv_grad", hk[None], dkv[None], BF16)[0]
    dhk = mm_nt_b("kv_dgrad", dkv, w["kv"][None], [F32])[0][0]
    dx2, dsh, dsc, d_nkv = norm_mod_bwd("kv_norm_bwd", dhk, x2, kv_norm, _row(kvmod, 1), dx2)
    dkvmod = jnp.concatenate([dsh, dsc], axis=0)
    dx1, grads["w1_0"], grads["w2_0"], dmlp0, d_nmlp0 = _mlp_bwd(
        "l0", dx2, x1, _row(norm_mlp, 0), mod0, w["w1_0"], w["w2_0"], mlp0)
    dy0, dgate0 = gate_bwd("l0_mix_gate_bwd", dx1, y0, _row(mod0, 2))
    du = mm_nt_b("l0_out_dgrad", dy0, w["a_out"][None], [F32])[0][0]
    grads["a_out"] = mm_tn("l0_out_grad", u[None], dy0[None], BF16)[0]
    dproj3, d_lb, d_out_gain = hgrn_bwd("l0_hgrn_bwd", proj3, lb, out_gain, o_h, du, states)
    grads["a_in"] = mm_tn("l0_in_grad", h1[None], dproj3, BF16)
    dh1 = mm_nt_r("l0_in_dgrad", dproj3, w["a_in"], F32)
    dx0, dsh, dsc, d_nmix0 = norm_mod_bwd("l0_mix_norm_bwd", dh1, x, _row(norm_mix, 0), _row(mod0, 1), dx1)
    dmod0 = jnp.concatenate([dsh, dsc, dgate0, *dmlp0], axis=0)
    small = dict(norm_mix=(d_nmix0, d_nmix1), norm_mlp=(d_nmlp0, d_nmlp1), kv_norm=d_nkv,
                 final_norm=d_final, lb=d_lb, out_gain=d_out_gain)
    return loss, dx0, grads, dmod0, dmod1, dkvmod, small


BIG = ("a_in", "w1_0", "w1_1", "w2_0", "w2_1", "a_out", "bq", "bo", "kv")


def kernel(x, c, ada_w, ada_b, norm_mix, norm_mlp, a_w_in, a_lb_logits, a_out_gain, a_w_out, kv_ada_w, kv_ada_b, kv_norm, w_kv, b_w_q, b_w_out, mlp_w1, mlp_w2, final_norm, loss_target, m_ada_w, m_ada_b, m_norm_mix, m_norm_mlp, m_a_w_in, m_a_lb_logits, m_a_out_gain, m_a_w_out, m_kv_ada_w, m_kv_ada_b, m_kv_norm, m_w_kv, m_b_w_q, m_b_w_out, m_mlp_w1, m_mlp_w2, m_final_norm, v_ada_w, v_ada_b, v_norm_mix, v_norm_mlp, v_a_w_in, v_a_lb_logits, v_a_out_gain, v_a_w_out, v_kv_ada_w, v_kv_ada_b, v_kv_norm, v_w_kv, v_b_w_q, v_b_w_out, v_mlp_w1, v_mlp_w2, v_final_norm):
    d = x.shape[-1]
    px, py, pc = _position()
    me = 4 * px + 2 * py + pc
    chip = 2 * px + py
    n_ada = ada_w.shape[2]
    n_kvada = kv_ada_w.shape[1]
    shard_cols = d // N_CHIPS

    def as_rows(a):
        return a.reshape(-1, shard_cols)

    pack1 = jnp.concatenate([as_rows(c), a_lb_logits, a_out_gain,
                             jnp.zeros((1, shard_cols), F32)], axis=0)
    got1 = allgather8("gather_cond", pack1)
    n_c = d // shard_cols
    c_all = got1[:, :n_c, :].reshape(N_DEV, d)
    per_chip = got1[0::2]
    logits_full = jnp.swapaxes(per_chip[:, n_c:n_c + 2, :], 0, 1).reshape(2, d)
    out_gain_full = per_chip[:, n_c + 2, :].reshape(1, d)
    lb = lower_bound_fwd("lower_bound", logits_full)

    bias0 = lax.dynamic_slice_in_dim(ada_b, chip * n_ada, n_ada, axis=1)
    bias_kv = lax.dynamic_slice_in_dim(kv_ada_b.reshape(1, -1), chip * n_kvada, n_kvada, axis=1)
    mod_part = jnp.concatenate([
        ada_project("ada_proj_0", c_all, ada_w[0], bias0[0:1]),
        ada_project("ada_proj_1", c_all, ada_w[1], bias0[1:2]),
        ada_project("ada_proj_kv", c_all, kv_ada_w, bias_kv)], axis=1)
    got2 = allgather8("gather_mod", mod_part)
    mine = lax.dynamic_index_in_dim(got2[0::2], me, axis=1, keepdims=False)
    mod0 = mine[:, :n_ada].reshape(6, d)
    mod1 = mine[:, n_ada:2 * n_ada].reshape(6, d)
    kvmod = mine[:, 2 * n_ada:].reshape(2, d)

    shards = dict(a_in=a_w_in[0], w1_0=mlp_w1[0], w1_1=mlp_w1[1], w2_0=mlp_w2[0], w2_1=mlp_w2[1],
                  a_out=a_w_out[0], bq=b_w_q[0], bo=b_w_out[0], kv=w_kv)
    gathered = gather_weights("gather_weights", [shards[k].astype(BF16) for k in BIG])
    w = {}
    for k, g in zip(BIG, gathered):
        w[k] = g if k in ("a_in", "w1_0", "w1_1", "w2_0", "w2_1") else g.reshape(-1, g.shape[-1])

    loss, dx0, grads, dmod0, dmod1, dkvmod, small = local_step(
        x[0], loss_target[0], mod0, mod1, kvmod, norm_mix, norm_mlp, kv_norm.reshape(1, d),
        final_norm.reshape(1, d), lb, out_gain_full, w)

    loss_row = jnp.concatenate([loss, jnp.zeros((1, shard_cols - loss.shape[1]), F32)], axis=1)
    rows = [as_rows(dmod0), as_rows(dmod1), as_rows(dkvmod),
            as_rows(small["norm_mix"][0]), as_rows(small["norm_mix"][1]),
            as_rows(small["norm_mlp"][0]), as_rows(small["norm_mlp"][1]),
            as_rows(small["kv_norm"]), as_rows(small["final_norm"]),
            as_rows(small["lb"]), as_rows(small["out_gain"]), loss_row]
    n_rows = sum(r.shape[0] for r in rows)
    pad = (-n_rows) % 8
    pack3 = jnp.concatenate(rows + [jnp.zeros((pad, shard_cols), F32)], axis=0)
    got3 = allgather8("gather_small_grads", pack3)
    total = device_sum("sum_small_grads", got3)
    n_mod_rows = (12 * d + 2 * d) // shard_cols
    n_gain_rows = 6 * n_c
    loss_out = total[n_mod_rows + n_gain_rows + 2 * n_c, 0]

    dmod_all = got3[:, :n_mod_rows, :].reshape(N_DEV, 14 * d)
    act_t = jnp.zeros((d, 128), F32).at[:, :N_DEV].set(c_all.T)

    def dmod_cols(lo, width):
        part = lax.dynamic_slice_in_dim(dmod_all, lo + chip * width, width, axis=1)
        return jnp.zeros((128, width), F32).at[:N_DEV].set(part)

    g_ada_w, d_ada_w, nm_ada_w, nv_ada_w = ada_grad_adamw(
        "ada_update", act_t, jnp.stack([dmod_cols(0, n_ada), dmod_cols(6 * d, n_ada)]), ada_w, m_ada_w, v_ada_w)
    g_kv_ada_w, d_kv_ada_w, nm_kv_ada_w, nv_kv_ada_w = (a[0] for a in ada_grad_adamw(
        "ada_update_kv", act_t, dmod_cols(12 * d, n_kvada)[None], kv_ada_w[None], m_kv_ada_w[None],
        v_kv_ada_w[None]))

    glist = []
    for k in BIG:
        g = grads[k]
        glist.append(g if g.ndim == 3 else g.reshape(N_CHIPS, g.shape[0] // N_CHIPS, g.shape[1]))
    from_sibling = sibling_exchange("grad_sibling_exchange", glist)
    core = pc.astype(jnp.int32).reshape(1)
    chip_parts = [pair_add("grad_pair_add_" + k, g, r, core) for k, g, r in zip(BIG, glist, from_sibling)]
    from_chips = chip_scatter("grad_chip_scatter", chip_parts)
    where = jnp.stack([chip, pc]).astype(jnp.int32)
    out_shapes = [_sds(a_w_in.shape, F32), _sds(mlp_w1.shape, F32), _sds(mlp_w2.shape, F32),
                  _sds(a_w_out.shape, F32), _sds(b_w_q.shape, F32), _sds(b_w_out.shape, F32),
                  _sds(w_kv.shape, F32)]
    targets = [(0, 0), (1, 0), (1, 1), (2, 0), (2, 1), (3, 0), (4, 0), (5, 0), (6, None)]
    slabs = [None] * len(out_shapes)
    places = []
    for k, part, recv, (oi, lead) in zip(BIG, chip_parts, from_chips, targets):
        slabs[oi] = chip_sum("grad_chip_sum_" + k, part, recv, where, out_shapes[oi], lead, slabs[oi])
        places.append((oi, lead, part.shape[1]))
    g_a_w_in, g_mlp_w1, g_mlp_w2, g_a_w_out, g_b_w_q, g_b_w_out, g_w_kv = sibling_share(
        "grad_sibling_share", slabs, places)

    def update(name, wgt, g, m_, v_):
        shape = wgt.shape
        f = lambda a: a.reshape(-1, shape[-1])
        return tuple(o.reshape(shape) for o in adamw(name, f(wgt), f(g), f(m_), f(v_)))

    u_a_w_in = update("adamw_a_w_in", a_w_in, g_a_w_in, m_a_w_in, v_a_w_in)
    u_mlp_w1 = update("adamw_mlp_w1", mlp_w1, g_mlp_w1, m_mlp_w1, v_mlp_w1)
    u_mlp_w2 = update("adamw_mlp_w2", mlp_w2, g_mlp_w2, m_mlp_w2, v_mlp_w2)
    u_a_w_out = update("adamw_a_w_out", a_w_out, g_a_w_out, m_a_w_out, v_a_w_out)
    u_b_w_q = update("adamw_b_w_q", b_w_q, g_b_w_q, m_b_w_q, v_b_w_q)
    u_b_w_out = update("adamw_b_w_out", b_w_out, g_b_w_out, m_b_w_out, v_b_w_out)
    u_w_kv = update("adamw_w_kv", w_kv, g_w_kv, m_w_kv, v_w_kv)

    base = n_mod_rows + n_gain_rows
    d_lb_mine = lax.dynamic_slice_in_dim(total, base + chip, 1, axis=0)
    d_gain_mine = lax.dynamic_slice_in_dim(total, base + n_c + chip, 1, axis=0)
    d_logits = lower_bound_bwd("lower_bound_bwd", a_lb_logits, d_lb_mine)

    def pack_small(ada_b_, kv_ada_b_, norm_mix_, norm_mlp_, kv_norm_, final_norm_, lbl_, gain_):
        parts = [as_rows(ada_b_), as_rows(kv_ada_b_), as_rows(norm_mix_), as_rows(norm_mlp_),
                 as_rows(kv_norm_), as_rows(final_norm_), lbl_, gain_]
        n = sum(p.shape[0] for p in parts)
        return jnp.concatenate(parts + [jnp.zeros(((-n) % 8, shard_cols), F32)], axis=0)

    w_small = pack_small(ada_b, kv_ada_b, norm_mix, norm_mlp, kv_norm, final_norm, a_lb_logits, a_out_gain)
    m_small = pack_small(m_ada_b, m_kv_ada_b, m_norm_mix, m_norm_mlp, m_kv_norm, m_final_norm,
                         m_a_lb_logits, m_a_out_gain)
    v_small = pack_small(v_ada_b, v_kv_ada_b, v_norm_mix, v_norm_mlp, v_kv_norm, v_final_norm,
                         v_a_lb_logits, v_a_out_gain)
    n_small = w_small.shape[0]
    g_small = jnp.concatenate([total[:base], d_logits, d_gain_mine,
                               jnp.zeros((n_small - base - 3, shard_cols), F32)], axis=0)
    small_out = (g_small,) + tuple(adamw("adamw_small", w_small, g_small, m_small, v_small))

    def unpack_small(p):
        out, r0 = [], 0
        for ref in (ada_b, kv_ada_b, norm_mix, norm_mlp, kv_norm, final_norm, a_lb_logits, a_out_gain):
            nr = ref.size // shard_cols
            out.append(p[r0:r0 + nr].reshape(ref.shape))
            r0 += nr
        return out

    sm = [unpack_small(p) for p in small_out]

    def leaves(kind, big_ada, big_kv_ada):
        ada_b_, kv_ada_b_, norm_mix_, norm_mlp_, kv_norm_, final_norm_, lbl_, gain_ = sm[kind]
        pick = (lambda u, g: g) if kind == 0 else (lambda u, g: u[kind - 1])
        return [big_ada, ada_b_, norm_mix_, norm_mlp_, pick(u_a_w_in, g_a_w_in), lbl_, gain_,
                pick(u_a_w_out, g_a_w_out), big_kv_ada, kv_ada_b_, kv_norm_, pick(u_w_kv, g_w_kv),
                pick(u_b_w_q, g_b_w_q), pick(u_b_w_out, g_b_w_out), pick(u_mlp_w1, g_mlp_w1),
                pick(u_mlp_w2, g_mlp_w2), final_norm_]

    return (loss_out, dx0[None],
            *leaves(0, g_ada_w, g_kv_ada_w), *leaves(1, d_ada_w, d_kv_ada_w),
            *leaves(2, nm_ada_w, nm_kv_ada_w), *leaves(3, nv_ada_w, nv_kv_ada_w))
```

```python
import functools

import jax
import jax.numpy as jnp
from jax import lax
from jax.experimental import pallas as pl
from jax.experimental.pallas import tpu as pltpu

F32 = jnp.float32
BF16 = jnp.bfloat16
MESH = pl.DeviceIdType.MESH

HEAD_DIM = 128
KV_GROUP = 4
HGRN_CHUNK = 64
NORM_EPS = 1e-6
N_CHIPS = 4
N_DEV = 8
ROW_TILE = 256
ATTN_TILE = 256
VMEM_LIMIT = 56 * 1024 * 1024
DEAD_LOG_WEIGHT = -110.0

ADAM_LR = 0.001
ADAM_B1 = 0.9
ADAM_B2 = 0.999
ADAM_EPS = 1e-08
ADAM_WD = 0.01
ADAM_STEP = 10

NN = (((1,), (0,)), ((), ()))
NT = (((1,), (1,)), ((), ()))
TN = (((0,), (0,)), ((), ()))


def _dot(a, b, dims=NN, precision=None):
    return lax.dot_general(a, b, dims, preferred_element_type=F32, precision=precision)


def _bdot(a, b, dims=NN):
    return _dot(a.astype(BF16), b.astype(BF16), dims)


def _sigmoid(x):
    return 1.0 / (1.0 + jnp.exp(-x))


def _log_sigmoid(z):
    return jnp.minimum(z, 0.0) - jnp.log(1.0 + jnp.exp(-jnp.abs(z)))


def _split_bf16(x):
    hi = x.astype(BF16)
    lo = (x - hi.astype(F32)).astype(BF16)
    return hi, lo


def _params(sem=None):
    return pltpu.CompilerParams(dimension_semantics=sem, vmem_limit_bytes=VMEM_LIMIT)


def _tile(n, pref):
    t = min(n, pref)
    assert n % t == 0, (n, pref)
    return t


def _mm(name, a, b, a_spec, b_spec, grid, n_red, dims, out_shapes, out_specs,
        acc_shape, epilogue=None, extras=(), extra_specs=(), comm=None):
    n_extra = len(extras)
    n_out = len(out_shapes)
    n_cin = len(comm["operands"]) if comm else 0
    n_cout = len(comm["out_shapes"]) if comm else 0
    if epilogue is None:
        epilogue = lambda acc: (acc,)

    def body(*refs):
        a_ref, b_ref = refs[:2]
        ex_refs = refs[2:2 + n_extra]
        cin_refs = refs[2 + n_extra:2 + n_extra + n_cin]
        out_refs = refs[2 + n_extra + n_cin:2 + n_extra + n_cin + n_out]
        cout_refs = refs[2 + n_extra + n_cin + n_out:2 + n_extra + n_cin + n_out + n_cout]
        scratch = refs[2 + n_extra + n_cin + n_out + n_cout:]
        pids = [pl.program_id(ax) for ax in range(len(grid))]
        if comm:
            send_sems, recv_sems = scratch[-2:]

            @pl.when(functools.reduce(jnp.logical_and, [p == 0 for p in pids]))
            def _():
                comm["start"](cin_refs, cout_refs, send_sems, recv_sems)

        prod = _bdot(a_ref[...], b_ref[...], dims)

        def finish(acc):
            res = epilogue(acc, *[e[...] for e in ex_refs])
            for o_ref, r in zip(out_refs, res):
                o_ref[...] = r.astype(o_ref.dtype)

        if n_red == 0:
            finish(prod)
        else:
            acc_ref = scratch[0]
            ids = pids[len(grid) - n_red:]
            sizes = grid[len(grid) - n_red:]
            first = functools.reduce(jnp.logical_and, [i == 0 for i in ids])
            last = functools.reduce(jnp.logical_and, [i == s - 1 for i, s in zip(ids, sizes)])

            @pl.when(first)
            def _():
                acc_ref[...] = prod

            @pl.when(jnp.logical_not(first))
            def _():
                acc_ref[...] += prod

            @pl.when(last)
            def _():
                finish(acc_ref[...])

        if comm:
            @pl.when(functools.reduce(jnp.logical_and, [p == s - 1 for p, s in zip(pids, grid)]))
            def _():
                comm["finish"](cin_refs, cout_refs, send_sems, recv_sems)

    if comm:
        sem = ("arbitrary",) * len(grid)
    else:
        sem = ("parallel",) * (len(grid) - n_red) + ("arbitrary",) * n_red
    scratch_shapes = [pltpu.VMEM(acc_shape, F32)] if n_red else []
    if comm:
        scratch_shapes += [pltpu.SemaphoreType.DMA(comm["sems"]), pltpu.SemaphoreType.DMA(comm["sems"])]
    out = pl.pallas_call(
        body, name=name, grid=grid,
        in_specs=[a_spec, b_spec, *extra_specs] + [HBM] * n_cin,
        out_specs=list(out_specs) + [HBM] * n_cout,
        out_shape=list(out_shapes) + (list(comm["out_shapes"]) if comm else []),
        scratch_shapes=scratch_shapes,
        compiler_params=_params(sem),
    )(a, b, *extras, *(comm["operands"] if comm else ()))
    return (out[:n_out], out[n_out:]) if comm else out


def _sds(shape, dtype):
    return jax.ShapeDtypeStruct(shape, dtype)


def mm_nn_b(name, a, w3, out_dtypes, epilogue=None, extras=(), extra_kinds=(), comm=None):
    m, k = a.shape
    nb, _, n = w3.shape
    tm, tn = _tile(m, 1024), _tile(n, 512)
    grid = (nb, m // tm, n // tn)
    nt = n // tn
    especs = []
    for kind in extra_kinds:
        if kind == "tile":
            especs.append(pl.BlockSpec((None, tm, tn), lambda j, i, c: (j, i, c)))
        else:
            especs.append(pl.BlockSpec((1, tn), lambda j, i, c: (0, j * nt + c)))
    return _mm(name, a, w3,
               pl.BlockSpec((tm, k), lambda j, i, c: (i, 0)),
               pl.BlockSpec((None, k, tn), lambda j, i, c: (j, 0, c)),
               grid, 0, NN,
               [_sds((nb, m, n), d) for d in out_dtypes],
               [pl.BlockSpec((None, tm, tn), lambda j, i, c: (j, i, c)) for _ in out_dtypes],
               None, epilogue, extras, especs, comm)


def mm_nn_r(name, a3, w3, out_dtypes, epilogue=None, extras=(), extra_kinds=(), comm=None):
    nb, m, kb = a3.shape
    n = w3.shape[2]
    tm, tn, tk = _tile(m, 1024), _tile(n, 512), _tile(kb, 2048)
    grid = (m // tm, n // tn, nb, kb // tk)
    especs = []
    for kind in extra_kinds:
        if kind == "tile":
            especs.append(pl.BlockSpec((tm, tn), lambda i, c, j, r: (i, c)))
        else:
            especs.append(pl.BlockSpec((1, tn), lambda i, c, j, r: (0, c)))
    return _mm(name, a3, w3,
               pl.BlockSpec((None, tm, tk), lambda i, c, j, r: (j, i, r)),
               pl.BlockSpec((None, tk, tn), lambda i, c, j, r: (j, r, c)),
               grid, 2, NN,
               [_sds((m, n), d) for d in out_dtypes],
               [pl.BlockSpec((tm, tn), lambda i, c, j, r: (i, c)) for _ in out_dtypes],
               (tm, tn), epilogue, extras, especs, comm)


def mm_nt_b(name, a, w3, out_dtypes, epilogue=None, extras=(), comm=None):
    m, n = a.shape
    nb, kb, _ = w3.shape
    tm, tk = _tile(m, 1024), _tile(kb, 512)
    grid = (nb, m // tm, kb // tk)
    especs = [pl.BlockSpec((None, tm, tk), lambda j, i, c: (j, i, c)) for _ in extras]
    return _mm(name, a, w3,
               pl.BlockSpec((tm, n), lambda j, i, c: (i, 0)),
               pl.BlockSpec((None, tk, n), lambda j, i, c: (j, c, 0)),
               grid, 0, NT,
               [_sds((nb, m, kb), d) for d in out_dtypes],
               [pl.BlockSpec((None, tm, tk), lambda j, i, c: (j, i, c)) for _ in out_dtypes],
               None, epilogue, extras, especs, comm)


def _single(res, comm):
    return res[0] if comm is None else (res[0][0], res[1])


def mm_nt_r(name, a3, w3, out_dtype, comm=None):
    nb, m, n = a3.shape
    k = w3.shape[1]
    tm, tk, tc = _tile(m, 1024), _tile(k, 1024), _tile(n, 2048)
    grid = (m // tm, k // tk, nb, n // tc)
    return _single(_mm(name, a3, w3,
                       pl.BlockSpec((None, tm, tc), lambda i, c, j, r: (j, i, r)),
                       pl.BlockSpec((None, tk, tc), lambda i, c, j, r: (j, c, r)),
                       grid, 2, NT,
                       [_sds((m, k), out_dtype)],
                       [pl.BlockSpec((tm, tk), lambda i, c, j, r: (i, c))],
                       (tm, tk), comm=comm), comm)


def mm_tn(name, a3, d3, out_dtype, comm=None):
    na, m, kb = a3.shape
    nd, _, n = d3.shape
    nb = max(na, nd)
    tk, tn, tm = _tile(kb, 512), _tile(n, 2048), _tile(m, 1024)
    grid = (nb, kb // tk, n // tn, m // tm)
    ja = (lambda j: j) if na > 1 else (lambda j: 0)
    jd = (lambda j: j) if nd > 1 else (lambda j: 0)
    return _single(_mm(name, a3, d3,
                       pl.BlockSpec((None, tm, tk), lambda j, c, e, r: (ja(j), r, c)),
                       pl.BlockSpec((None, tm, tn), lambda j, c, e, r: (jd(j), r, e)),
                       grid, 1, TN,
                       [_sds((nb, kb, n), out_dtype)],
                       [pl.BlockSpec((None, tk, tn), lambda j, c, e, r: (j, c, e))],
                       (tk, tn), comm=comm), comm)


def _row_spec(ts, d):
    return pl.BlockSpec((ts, d), lambda i: (i, 0))


def _vec_spec(d):
    return pl.BlockSpec((1, d), lambda i: (0, 0))


def norm_mod_fwd(name, x, gain, scale, shift):
    s, d = x.shape
    ts = _tile(s, ROW_TILE)

    def body(x_ref, g_ref, sc_ref, sh_ref, h_ref):
        xv = x_ref[...]
        inv = lax.rsqrt(jnp.mean(xv * xv, axis=-1, keepdims=True) + NORM_EPS)
        h = (xv * inv) * g_ref[...] * (1.0 + sc_ref[...]) + sh_ref[...]
        h_ref[...] = h.astype(h_ref.dtype)

    return pl.pallas_call(
        body, name=name, grid=(s // ts,),
        in_specs=[_row_spec(ts, d), _vec_spec(d), _vec_spec(d), _vec_spec(d)],
        out_specs=_row_spec(ts, d), out_shape=_sds((s, d), BF16),
        compiler_params=_params(("parallel",)),
    )(x, gain, scale, shift)


def norm_mod_bwd(name, dh, x, gain, scale, dres):
    s, d = x.shape
    ts = _tile(s, ROW_TILE)

    def body(dh_ref, x_ref, g_ref, sc_ref, dres_ref, dx_ref, dsh_ref, dsc_ref, dg_ref):
        @pl.when(pl.program_id(0) == 0)
        def _():
            dsh_ref[...] = jnp.zeros_like(dsh_ref)
            dsc_ref[...] = jnp.zeros_like(dsc_ref)
            dg_ref[...] = jnp.zeros_like(dg_ref)

        xv = x_ref[...]
        dhv = dh_ref[...].astype(F32)
        g = g_ref[...]
        inv = lax.rsqrt(jnp.mean(xv * xv, axis=-1, keepdims=True) + NORM_EPS)
        n = xv * inv
        dhn = dhv * (1.0 + sc_ref[...])
        dn = dhn * g
        dx = inv * (dn - n * jnp.mean(dn * n, axis=-1, keepdims=True))
        dx_ref[...] = dres_ref[...] + dx
        dsh_ref[...] += jnp.sum(dhv, axis=0, keepdims=True)
        dsc_ref[...] += jnp.sum(dhv * (n * g), axis=0, keepdims=True)
        dg_ref[...] += jnp.sum(dhn * n, axis=0, keepdims=True)

    return pl.pallas_call(
        body, name=name, grid=(s // ts,),
        in_specs=[_row_spec(ts, d), _row_spec(ts, d), _vec_spec(d), _vec_spec(d), _row_spec(ts, d)],
        out_specs=[_row_spec(ts, d), _vec_spec(d), _vec_spec(d), _vec_spec(d)],
        out_shape=[_sds((s, d), F32), _sds((1, d), F32), _sds((1, d), F32), _sds((1, d), F32)],
        compiler_params=_params(("arbitrary",)),
    )(dh, x, gain, scale, dres)


def gate_bwd(name, dx, y, gate):
    s, d = dx.shape
    ts = _tile(s, ROW_TILE)

    def body(dx_ref, y_ref, g_ref, dy_ref, dg_ref):
        @pl.when(pl.program_id(0) == 0)
        def _():
            dg_ref[...] = jnp.zeros_like(dg_ref)

        dxv = dx_ref[...]
        dy_ref[...] = (dxv * g_ref[...]).astype(dy_ref.dtype)
        dg_ref[...] += jnp.sum(dxv * y_ref[...].astype(F32), axis=0, keepdims=True)

    return pl.pallas_call(
        body, name=name, grid=(s // ts,),
        in_specs=[_row_spec(ts, d), _row_spec(ts, d), _vec_spec(d)],
        out_specs=[_row_spec(ts, d), _vec_spec(d)],
        out_shape=[_sds((s, d), BF16), _sds((1, d), F32)],
        compiler_params=_params(("arbitrary",)),
    )(dx, y, gate)


def final_loss(name, x, gain, target):
    s, d = x.shape
    ts = _tile(s, ROW_TILE)

    def body(x_ref, g_ref, t_ref, dx_ref, dg_ref, loss_ref):
        @pl.when(pl.program_id(0) == 0)
        def _():
            dg_ref[...] = jnp.zeros_like(dg_ref)
            loss_ref[...] = jnp.zeros_like(loss_ref)

        xv = x_ref[...]
        g = g_ref[...]
        inv = lax.rsqrt(jnp.mean(xv * xv, axis=-1, keepdims=True) + NORM_EPS)
        n = xv * inv
        diff = n * g - t_ref[...]
        per_tok = jnp.mean(diff * diff, axis=-1, keepdims=True)
        loss_ref[...] += 0.5 * jnp.sum(per_tok, axis=0, keepdims=True)
        dy = diff * (1.0 / d)
        dg_ref[...] += jnp.sum(dy * n, axis=0, keepdims=True)
        dn = dy * g
        dx_ref[...] = inv * (dn - n * jnp.mean(dn * n, axis=-1, keepdims=True))

    return pl.pallas_call(
        body, name=name, grid=(s // ts,),
        in_specs=[_row_spec(ts, d), _vec_spec(d), _row_spec(ts, d)],
        out_specs=[_row_spec(ts, d), _vec_spec(d), _vec_spec(128)],
        out_shape=[_sds((s, d), F32), _sds((1, d), F32), _sds((1, 128), F32)],
        compiler_params=_params(("arbitrary",)),
    )(x, gain, target)


def _hgrn_chunk_fwd(qr, fl, lbv, tri):
    sg = _sigmoid(fl)
    sgm = _sigmoid(-fl)
    f = lbv + (1.0 - lbv) * sg
    logf = jnp.log(f)
    k = (1.0 - lbv) * sgm
    cum = _dot(tri, logf, precision=lax.Precision.HIGHEST)
    cl = cum[HGRN_CHUNK - 1:HGRN_CHUNK, :]
    e = jnp.exp(cum)
    en = jnp.exp(-cum)
    es = jnp.exp(cl - cum)
    sq = _sigmoid(qr)
    qs = qr * sq
    return dict(sg=sg, sgm=sgm, f=f, k=k, cum=cum, cl=cl, e=e, en=en, es=es, sq=sq, qs=qs,
                qd=qs * e, ki=k * en, ks=k * es, dec=jnp.exp(cl))


def _tri_masks(strict=False):
    r = lax.broadcasted_iota(jnp.int32, (HGRN_CHUNK, HGRN_CHUNK), 0)
    c = lax.broadcasted_iota(jnp.int32, (HGRN_CHUNK, HGRN_CHUNK), 1)
    return (r > c) if strict else (r >= c)


def hgrn_fwd(name, proj3, lb, out_gain):
    _, s, d = proj3.shape
    heads = d // HEAD_DIM
    t_rows = _tile(s, 512)
    n_t = s // t_rows
    n_c = t_rows // HGRN_CHUNK

    def body(q_ref, f_ref, i_ref, g_ref, lb_ref, gain_ref, u_ref, o_ref, st_ref, state):
        @pl.when(pl.program_id(1) == 0)
        def _():
            state[...] = jnp.zeros_like(state)

        causal = _tri_masks()
        tri = causal.astype(F32)
        lbv = lb_ref[...]
        gain = gain_ref[...]
        for ci in range(n_c):
            rows = pl.ds(ci * HGRN_CHUNK, HGRN_CHUNK)
            c = _hgrn_chunk_fwd(q_ref[rows, :], f_ref[rows, :], lbv, tri)
            v = i_ref[rows, :]
            st = state[...]
            st_ref[0, ci] = st
            scores = jnp.where(causal, _bdot(c["qd"], c["ki"], NT), 0.0)
            o = _bdot(scores, v) + _bdot(c["qd"], st, NT)
            state[...] = st * c["dec"] + _bdot(v, c["ks"], TN)
            graw = g_ref[rows, :]
            rms = lax.rsqrt(jnp.mean(o * o, axis=-1, keepdims=True) + NORM_EPS)
            u = o * rms * gain * (graw * _sigmoid(graw))
            o_ref[rows, :] = o
            u_ref[rows, :] = u.astype(u_ref.dtype)

    def pspec(blk):
        return pl.BlockSpec((None, t_rows, HEAD_DIM), lambda h, t: (blk, t, h))

    hspec = pl.BlockSpec((1, HEAD_DIM), lambda h, t: (0, h))
    ospec = pl.BlockSpec((t_rows, HEAD_DIM), lambda h, t: (t, h))
    return pl.pallas_call(
        body, name=name, grid=(heads, n_t),
        in_specs=[pspec(0), pspec(1), pspec(2), pspec(3), hspec, hspec],
        out_specs=[ospec, ospec,
                   pl.BlockSpec((1, n_c, HEAD_DIM, HEAD_DIM), lambda h, t: (h, t, 0, 0))],
        out_shape=[_sds((s, d), BF16), _sds((s, d), F32),
                   _sds((heads, s // HGRN_CHUNK, HEAD_DIM, HEAD_DIM), F32)],
        scratch_shapes=[pltpu.VMEM((HEAD_DIM, HEAD_DIM), F32)],
        compiler_params=_params(("parallel", "arbitrary")),
    )(proj3, proj3, proj3, proj3, lb, out_gain)


def hgrn_bwd(name, proj3, lb, out_gain, o, du, states):
    _, s, d = proj3.shape
    heads = d // HEAD_DIM
    t_rows = _tile(s, 512)
    n_t = s // t_rows
    n_c = t_rows // HGRN_CHUNK

    def body(q_ref, f_ref, i_ref, g_ref, lb_ref, gain_ref, o_ref, du_ref, st_ref,
             dp_ref, dlb_ref, dgain_ref, dstate):
        @pl.when(pl.program_id(1) == 0)
        def _():
            dstate[...] = jnp.zeros_like(dstate)
            dlb_ref[...] = jnp.zeros_like(dlb_ref)
            dgain_ref[...] = jnp.zeros_like(dgain_ref)

        causal = _tri_masks()
        tri = causal.astype(F32)
        tri_t = jnp.logical_not(_tri_masks(strict=True)).astype(F32)
        lbv = lb_ref[...]
        gain = gain_ref[...]
        for ci in reversed(range(n_c)):
            rows = pl.ds(ci * HGRN_CHUNK, HGRN_CHUNK)
            qr = q_ref[rows, :]
            c = _hgrn_chunk_fwd(qr, f_ref[rows, :], lbv, tri)
            v = i_ref[rows, :]
            st = st_ref[0, ci]
            ov = o_ref[rows, :]
            duv = du_ref[rows, :].astype(F32)
            graw = g_ref[rows, :]
            sgg = _sigmoid(graw)
            gate = graw * sgg
            rms = lax.rsqrt(jnp.mean(ov * ov, axis=-1, keepdims=True) + NORM_EPS)
            on = ov * rms
            dgain_ref[...] += jnp.sum(duv * on * gate, axis=0, keepdims=True)
            dgraw = duv * on * gain * (sgg * (1.0 + graw * (1.0 - sgg)))
            don = duv * gain * gate
            do = rms * (don - on * jnp.mean(don * on, axis=-1, keepdims=True))
            qd, ki, ks = c["qd"], c["ki"], c["ks"]
            p = jnp.where(causal, _bdot(qd, ki, NT), 0.0)
            dp = jnp.where(causal, _bdot(do, v, NT), 0.0)
            dst = dstate[...]
            dqd = _bdot(dp, ki) + _bdot(do, st)
            dki = _bdot(dp, qd, TN)
            dv = _bdot(p, do, TN) + _bdot(ks, dst, NT)
            dks = _bdot(v, dst)
            ddec = jnp.sum(dst * st, axis=0, keepdims=True)
            dstate[...] = dst * c["dec"] + _bdot(do, qd, TN)
            dqs = dqd * c["e"]
            dcum = dqd * qd - dki * ki - dks * ks
            dk = dki * c["en"] + dks * c["es"]
            dcl = jnp.sum(dks * ks, axis=0, keepdims=True) + ddec * c["dec"]
            dlogf = _dot(tri_t, dcum, precision=lax.Precision.HIGHEST) + dcl
            df = dlogf / c["f"]
            sg, sgm = c["sg"], c["sgm"]
            one_m_lb = 1.0 - lbv
            dlb_ref[...] += jnp.sum(df * (1.0 - sg) - dk * sgm, axis=0, keepdims=True)
            dfl = df * one_m_lb * sg * (1.0 - sg) - dk * one_m_lb * sgm * (1.0 - sgm)
            sq = c["sq"]
            dqr = dqs * (sq * (1.0 + qr * (1.0 - sq)))
            dp_ref[0, rows, :] = dqr.astype(dp_ref.dtype)
            dp_ref[1, rows, :] = dfl.astype(dp_ref.dtype)
            dp_ref[2, rows, :] = dv.astype(dp_ref.dtype)
            dp_ref[3, rows, :] = dgraw.astype(dp_ref.dtype)

    def pspec(blk):
        return pl.BlockSpec((None, t_rows, HEAD_DIM), lambda h, t: (blk, n_t - 1 - t, h))

    hspec = pl.BlockSpec((1, HEAD_DIM), lambda h, t: (0, h))
    ospec = pl.BlockSpec((t_rows, HEAD_DIM), lambda h, t: (n_t - 1 - t, h))
    return pl.pallas_call(
        body, name=name, grid=(heads, n_t),
        in_specs=[pspec(0), pspec(1), pspec(2), pspec(3), hspec, hspec, ospec, ospec,
                  pl.BlockSpec((1, n_c, HEAD_DIM, HEAD_DIM), lambda h, t: (h, n_t - 1 - t, 0, 0))],
        out_specs=[pl.BlockSpec((4, t_rows, HEAD_DIM), lambda h, t: (0, n_t - 1 - t, h)), hspec, hspec],
        out_shape=[_sds((4, s, d), BF16), _sds((1, d), F32), _sds((1, d), F32)],
        scratch_shapes=[pltpu.VMEM((HEAD_DIM, HEAD_DIM), F32)],
        compiler_params=_params(("parallel", "arbitrary")),
    )(proj3, proj3, proj3, proj3, lb, out_gain, o, du, states)


def attn_fwd(name, q, kv):
    s, dq = q.shape
    kvh = kv.shape[1] // (2 * HEAD_DIM)
    assert dq == kvh * KV_GROUP * HEAD_DIM
    tq = _tile(s, ATTN_TILE)
    scale = HEAD_DIM ** -0.5

    def body(q_ref, k_ref, v_ref, o_ref, tot_ref, cnt_ref):
        i = pl.program_id(1)
        heads = range(KV_GROUP)
        qs = [q_ref[:, g * HEAD_DIM:(g + 1) * HEAD_DIM] for g in heads]
        r_i = lax.broadcasted_iota(jnp.int32, (tq, tq), 0)
        c_i = lax.broadcasted_iota(jnp.int32, (tq, tq), 1)
        later = (r_i > c_i).astype(BF16)
        later2 = jnp.concatenate([later, later], axis=0)
        mask = c_i < r_i
        ones2 = jnp.ones((8, 2 * tq), BF16)

        def block(j, carry, masked):
            rows = pl.ds(pl.multiple_of(j * tq, tq), tq)
            kj = k_ref[rows, :]
            vj = v_ref[rows, :]
            new = []
            for g in heads:
                run, acc, tot = carry[g]
                z = _dot(qs[g], kj, NT) * scale
                lbeta = _log_sigmoid(z)
                lrest = lbeta - z
                if masked:
                    lrest = jnp.where(mask, lrest, 0.0)
                hl = jnp.concatenate(_split_bf16(lrest), axis=1)
                w = jnp.exp(lbeta + (_dot(hl, later2) + run))
                if masked:
                    w = jnp.where(mask, w, 0.0)
                acc = acc + _dot(w.astype(BF16), vj)
                run = run + jnp.sum(lrest, axis=1, keepdims=True)
                tot = tot + _dot(ones2, hl, NT)
                new.append((run, acc, tot))
            return tuple(new)

        def alive(carry):
            top = carry[0][0]
            for g in heads[1:]:
                top = jnp.maximum(top, carry[g][0])
            return jnp.max(top) > DEAD_LOG_WEIGHT

        zero = (jnp.zeros((tq, 1), F32), jnp.zeros((tq, HEAD_DIM), F32), jnp.zeros((8, tq), F32))
        carry = block(i, (zero,) * KV_GROUP, True)

        def step(state):
            jj, _, cr = state
            cr = block(i - 1 - jj, cr, False)
            return jj + 1, alive(cr), cr

        done, _, carry = lax.while_loop(lambda st: jnp.logical_and(st[0] < i, st[1]), step,
                                        (jnp.int32(0), alive(carry), carry))
        for g in heads:
            o_ref[:, g * HEAD_DIM:(g + 1) * HEAD_DIM] = carry[g][1]
            tot_ref[g] = carry[g][2][0:1, :]
        cnt_ref[pl.program_id(0), i] = done

    group = KV_GROUP * HEAD_DIM
    return pl.pallas_call(
        body, name=name, grid=(kvh, s // tq),
        in_specs=[pl.BlockSpec((tq, group), lambda kh, i: (i, kh)),
                  pl.BlockSpec((s, HEAD_DIM), lambda kh, i: (0, kh)),
                  pl.BlockSpec((s, HEAD_DIM), lambda kh, i: (0, kvh + kh))],
        out_specs=[pl.BlockSpec((tq, group), lambda kh, i: (i, kh)),
                   pl.BlockSpec((KV_GROUP, 1, tq), lambda kh, i: (kh, 0, i)),
                   pl.BlockSpec(memory_space=pltpu.SMEM)],
        out_shape=[_sds((s, dq), F32), _sds((dq // HEAD_DIM, 1, s), F32), _sds((kvh, s // tq), jnp.int32)],
        compiler_params=_params(("arbitrary", "arbitrary")),
    )(q, kv, kv)


def attn_bwd(name, q, kv, totals, visited, do):
    s, dq_cols = q.shape
    kvh = kv.shape[1] // (2 * HEAD_DIM)
    tq = _tile(s, ATTN_TILE)
    scale = HEAD_DIM ** -0.5

    def body(cnt_ref, q_ref, k_ref, v_ref, tot_ref, do_ref, dq_ref, dkv_ref):
        i = pl.program_id(1)
        first = i - jnp.clip(cnt_ref[pl.program_id(0), i], 0, i)

        @pl.when(i == 0)
        def _():
            dkv_ref[...] = jnp.zeros_like(dkv_ref)

        heads = range(KV_GROUP)
        qs = [q_ref[:, g * HEAD_DIM:(g + 1) * HEAD_DIM] for g in heads]
        dobs = [do_ref[:, g * HEAD_DIM:(g + 1) * HEAD_DIM].astype(BF16) for g in heads]
        tots = [tot_ref[g] for g in heads]
        q_all = jnp.concatenate(qs, axis=0)
        do_all = jnp.concatenate(dobs, axis=0)
        r_i = lax.broadcasted_iota(jnp.int32, (tq, tq), 0)
        c_i = lax.broadcasted_iota(jnp.int32, (tq, tq), 1)
        upto = (c_i <= r_i).astype(BF16)
        before = (c_i < r_i).astype(BF16)
        upto2 = jnp.concatenate([upto, upto], axis=1)
        before2 = jnp.concatenate([before, before], axis=1)
        mask = r_i < c_i

        def block(j, carry, masked):
            rows = pl.ds(pl.multiple_of(j * tq, tq), tq)
            kj = k_ref[rows, :]
            vj = v_ref[rows, :]
            new, dzs, wts = [], [], []
            for g in heads:
                run, drun, dq_acc = carry[g]
                zt = _dot(kj, qs[g], NT) * scale
                lbeta = _log_sigmoid(zt)
                lrest_raw = lbeta - zt
                lrest = jnp.where(mask, lrest_raw, 0.0) if masked else lrest_raw
                hl = jnp.concatenate(_split_bf16(lrest), axis=0)
                wt = jnp.exp(lbeta + (tots[g] - run - _dot(upto2, hl)))
                if masked:
                    wt = jnp.where(mask, wt, 0.0)
                dat = _dot(vj, dobs[g], NT) * wt
                dsum = drun + _dot(before2, jnp.concatenate(_split_bf16(dat), axis=0))
                dzt = dat * jnp.exp(lrest_raw) - dsum * jnp.exp(lbeta)
                if masked:
                    dzt = jnp.where(mask, dzt, 0.0)
                dzb = (dzt * scale).astype(BF16)
                dq_acc = dq_acc + _dot(dzb, kj, TN)
                dzs.append(dzb)
                wts.append(wt.astype(BF16))
                new.append((run + jnp.sum(lrest, axis=0, keepdims=True),
                            drun + jnp.sum(dat, axis=0, keepdims=True), dq_acc))
            dkv_ref[0, rows, :] += _dot(jnp.concatenate(dzs, axis=1), q_all)
            dkv_ref[1, rows, :] += _dot(jnp.concatenate(wts, axis=1), do_all)
            return tuple(new)

        zrow = jnp.zeros((1, tq), F32)
        carry = ((zrow, zrow, jnp.zeros((tq, HEAD_DIM), F32)),) * KV_GROUP
        carry = lax.fori_loop(first, i, lambda j, cr: block(j, cr, False), carry)
        carry = block(i, carry, True)
        for g in heads:
            dq_ref[:, g * HEAD_DIM:(g + 1) * HEAD_DIM] = carry[g][2].astype(dq_ref.dtype)

    group = KV_GROUP * HEAD_DIM
    qspec = pl.BlockSpec((tq, group), lambda kh, i, cnt: (i, kh))
    return pl.pallas_call(
        body, name=name,
        grid_spec=pltpu.PrefetchScalarGridSpec(
            num_scalar_prefetch=1, grid=(kvh, s // tq),
            in_specs=[qspec,
                      pl.BlockSpec((s, HEAD_DIM), lambda kh, i, cnt: (0, kh)),
                      pl.BlockSpec((s, HEAD_DIM), lambda kh, i, cnt: (0, kvh + kh)),
                      pl.BlockSpec((KV_GROUP, 1, tq), lambda kh, i, cnt: (kh, 0, i)),
                      qspec],
            out_specs=[qspec, pl.BlockSpec((2, s, HEAD_DIM), lambda kh, i, cnt: (0, 0, kh))]),
        out_shape=[_sds((s, dq_cols), BF16), _sds((2, s, kvh * HEAD_DIM), F32)],
        compiler_params=_params(("parallel", "arbitrary")),
    )(visited, q, kv, kv, totals, do)


def ada_project(name, c_all, w, b):
    bsz, d = c_all.shape
    n = w.shape[1]
    tn = _tile(n, 512)

    def body(c_ref, w_ref, b_ref, o_ref):
        cv = c_ref[...]
        act = cv * _sigmoid(cv)
        o_ref[...] = _bdot(act, w_ref[...]) + b_ref[...]

    return pl.pallas_call(
        body, name=name, grid=(n // tn,),
        in_specs=[pl.BlockSpec((bsz, d), lambda i: (0, 0)),
                  pl.BlockSpec((d, tn), lambda i: (0, i)),
                  pl.BlockSpec((1, tn), lambda i: (0, i))],
        out_specs=pl.BlockSpec((bsz, tn), lambda i: (0, i)),
        out_shape=_sds((bsz, n), F32),
        compiler_params=_params(("parallel",)),
    )(c_all, w, b)


def _adamw_math(w, g, m, v):
    m = ADAM_B1 * m + (1.0 - ADAM_B1) * g
    v = ADAM_B2 * v + (1.0 - ADAM_B2) * (g * g)
    m_hat = m / (1.0 - ADAM_B1 ** ADAM_STEP)
    v_hat = v / (1.0 - ADAM_B2 ** ADAM_STEP)
    delta = -ADAM_LR * (m_hat / (jnp.sqrt(v_hat) + ADAM_EPS) + ADAM_WD * w)
    return delta, m, v


def adamw(name, w, g, m, v):
    r, c = w.shape
    tr = _tile(r, 256)
    tc = _tile(c, 2048)

    def body(w_ref, g_ref, m_ref, v_ref, d_ref, mo_ref, vo_ref):
        delta, mn, vn = _adamw_math(w_ref[...], g_ref[...], m_ref[...], v_ref[...])
        d_ref[...] = delta
        mo_ref[...] = mn
        vo_ref[...] = vn

    spec = pl.BlockSpec((tr, tc), lambda i, j: (i, j))
    return pl.pallas_call(
        body, name=name, grid=(r // tr, c // tc),
        in_specs=[spec] * 4, out_specs=[spec] * 3,
        out_shape=[_sds((r, c), F32)] * 3,
        compiler_params=_params(("parallel", "parallel")),
    )(w, g, m, v)


def ada_grad_adamw(name, c_t, dmod, w, m, v):
    layers, d, n = w.shape
    tr = _tile(d, 256)
    tc = _tile(n, 512)

    def body(a_ref, dm_ref, w_ref, m_ref, v_ref, g_ref, d_ref, mo_ref, vo_ref):
        cv = a_ref[...]
        g = _bdot(cv * _sigmoid(cv), dm_ref[...])
        delta, mn, vn = _adamw_math(w_ref[...], g, m_ref[...], v_ref[...])
        g_ref[...] = g
        d_ref[...] = delta
        mo_ref[...] = mn
        vo_ref[...] = vn

    spec = pl.BlockSpec((None, tr, tc), lambda l, i, j: (l, i, j))
    return pl.pallas_call(
        body, name=name, grid=(layers, d // tr, n // tc),
        in_specs=[pl.BlockSpec((tr, 128), lambda l, i, j: (i, 0)),
                  pl.BlockSpec((None, 128, tc), lambda l, i, j: (l, 0, j)), spec, spec, spec],
        out_specs=[spec] * 4, out_shape=[_sds((layers, d, n), F32)] * 4,
        compiler_params=_params(("parallel", "parallel", "parallel")),
    )(c_t, dmod, w, m, v)


def device_sum(name, gathered):
    _, r, c = gathered.shape

    def body(g_ref, o_ref):
        acc = g_ref[0]
        for dev in range(1, N_DEV):
            acc = acc + g_ref[dev]
        o_ref[...] = acc

    return pl.pallas_call(
        body, name=name,
        in_specs=[pl.BlockSpec(memory_space=pltpu.VMEM)],
        out_specs=pl.BlockSpec(memory_space=pltpu.VMEM),
        out_shape=_sds((r, c), F32),
    )(gathered)


def lower_bound_fwd(name, logits):
    _, d = logits.shape

    def body(l_ref, o_ref):
        l0 = l_ref[0:1, :]
        l1 = l_ref[1:2, :]
        mx = jnp.maximum(l0, l1)
        e0 = jnp.exp(l0 - mx)
        e1 = jnp.exp(l1 - mx)
        o_ref[...] = e0 / (e0 + e1)

    return pl.pallas_call(
        body, name=name,
        in_specs=[pl.BlockSpec(memory_space=pltpu.VMEM)],
        out_specs=pl.BlockSpec(memory_space=pltpu.VMEM),
        out_shape=_sds((1, d), F32),
    )(logits)


def lower_bound_bwd(name, logits, dlb):
    _, d = logits.shape

    def body(l_ref, dlb_ref, o_ref):
        l0 = l_ref[0:1, :]
        l1 = l_ref[1:2, :]
        mx = jnp.maximum(l0, l1)
        e0 = jnp.exp(l0 - mx)
        e1 = jnp.exp(l1 - mx)
        p0 = e0 / (e0 + e1)
        p1 = e1 / (e0 + e1)
        g = dlb_ref[...] * p0 * p1
        o_ref[0:1, :] = g
        o_ref[1:2, :] = -g

    return pl.pallas_call(
        body, name=name,
        in_specs=[pl.BlockSpec(memory_space=pltpu.VMEM)] * 2,
        out_specs=pl.BlockSpec(memory_space=pltpu.VMEM),
        out_shape=_sds((2, d), F32),
    )(logits, dlb)


HBM = pl.BlockSpec(memory_space=pltpu.HBM)


def _position():
    return lax.axis_index("x"), lax.axis_index("y"), lax.axis_index("c")


def _remote(src, dst, send_sem, recv_sem, device):
    return pltpu.make_async_remote_copy(src_ref=src, dst_ref=dst, send_sem=send_sem, recv_sem=recv_sem,
                                        device_id=device, device_id_type=MESH)


def allgather8(name, x):
    r, c = x.shape

    def body(x_ref, out_ref, send_sems, recv_sems):
        px, py, pc = _position()
        me = 4 * px + 2 * py + pc
        out_ref[me] = x_ref[...]
        copies = []
        for k in range(1, N_DEV):
            peer = (1 - px if k & 4 else px, 1 - py if k & 2 else py, 1 - pc if k & 1 else pc)
            cp = _remote(x_ref, out_ref.at[me], send_sems.at[k - 1], recv_sems.at[k - 1], peer)
            cp.start()
            copies.append(cp)
        for cp in copies:
            cp.wait()

    return pl.pallas_call(
        body, name=name,
        in_specs=[pl.BlockSpec(memory_space=pltpu.VMEM)],
        out_specs=pl.BlockSpec(memory_space=pltpu.VMEM),
        out_shape=_sds((N_DEV, r, c), x.dtype),
        scratch_shapes=[pltpu.SemaphoreType.DMA((N_DEV - 1,)), pltpu.SemaphoreType.DMA((N_DEV - 1,))],
    )(x)


def _other_chips(px, py):
    return [(1 - px, py), (px, 1 - py), (1 - px, 1 - py)]


def gather_weights(name, shards):
    n = len(shards)
    hook = gather_hook(shards)

    def body(*refs):
        ins, outs = refs[:n], refs[n:2 * n]
        send_sems, recv_sems = refs[2 * n:]
        hook["start"](ins, outs, send_sems, recv_sems)
        hook["finish"](ins, outs, send_sems, recv_sems)

    return pl.pallas_call(
        body, name=name,
        in_specs=[HBM] * n, out_specs=[HBM] * n,
        out_shape=hook["out_shapes"],
        scratch_shapes=[pltpu.SemaphoreType.DMA(hook["sems"]), pltpu.SemaphoreType.DMA(hook["sems"])],
    )(*shards)


def gather_hook(shards):
    n = len(shards)

    def copies(ins, outs, send_sems, recv_sems):
        px, py, pc = _position()
        chip = 2 * px + py
        sibling = (px, py, 1 - pc)
        own, sends, arrivals, passes, pass_arrivals = [], [], [], [], []
        for a in range(n):
            rows = ins[a].shape[0] // 2
            mine, theirs = pl.ds(pc * rows, rows), pl.ds((1 - pc) * rows, rows)
            own.append(_remote(ins[a], outs[a].at[chip], send_sems.at[a, 6], recv_sems.at[a, 6], sibling))
            for j, (ox, oy) in enumerate(_other_chips(px, py)):
                sends.append(_remote(ins[a].at[mine], outs[a].at[chip, mine],
                                     send_sems.at[a, j], recv_sems.at[a, j], (ox, oy, pc)))
                landed = outs[a].at[2 * ox + oy, mine]
                arrivals.append(_remote(landed, landed, send_sems.at[a, j], recv_sems.at[a, j], sibling))
                passes.append(_remote(landed, landed, send_sems.at[a, 3 + j], recv_sems.at[a, 3 + j], sibling))
                via = outs[a].at[2 * ox + oy, theirs]
                pass_arrivals.append(_remote(via, via, send_sems.at[a, 3 + j], recv_sems.at[a, 3 + j], sibling))
        return own, sends, arrivals, passes, pass_arrivals

    def start(ins, outs, send_sems, recv_sems):
        own, sends, _, _, _ = copies(ins, outs, send_sems, recv_sems)
        for cp in own + sends:
            cp.start()

    def finish(ins, outs, send_sems, recv_sems):
        own, sends, arrivals, passes, pass_arrivals = copies(ins, outs, send_sems, recv_sems)
        for arrival, onward in zip(arrivals, passes):
            arrival.wait_recv()
            onward.start()
        for arrival in pass_arrivals:
            arrival.wait_recv()
        for cp in sends + passes:
            cp.wait_send()
        for cp in own:
            cp.wait()

    return dict(operands=list(shards), out_shapes=[_sds((N_CHIPS,) + s.shape, s.dtype) for s in shards],
                sems=(n, 7), start=start, finish=finish)


def sibling_exchange(name, grads):
    n = len(grads)

    def body(*refs):
        ins, outs = refs[:n], refs[n:2 * n]
        send_sems, recv_sems = refs[2 * n:]
        px, py, pc = _position()
        sibling = (px, py, 1 - pc)
        copies = []
        for a in range(n):
            rows = ins[a].shape[1] // 2
            src = ins[a].at[:, pl.ds((1 - pc) * rows, rows), :]
            cp = _remote(src, outs[a], send_sems.at[a], recv_sems.at[a], sibling)
            cp.start()
            copies.append(cp)
        for cp in copies:
            cp.wait()

    return pl.pallas_call(
        body, name=name,
        in_specs=[HBM] * n, out_specs=[HBM] * n,
        out_shape=[_sds((g.shape[0], g.shape[1] // 2, g.shape[2]), g.dtype) for g in grads],
        scratch_shapes=[pltpu.SemaphoreType.DMA((n,)), pltpu.SemaphoreType.DMA((n,))],
    )(*grads)


def pair_add(name, g, recv, core):
    nb, rows, cols = g.shape
    half = rows // 2
    tr = _tile(half, 256)
    steps = half // tr

    def body(core_ref, g_ref, r_ref, o_ref):
        del core_ref
        o_ref[...] = (g_ref[...].astype(F32) + r_ref[...].astype(F32)).astype(o_ref.dtype)

    return pl.pallas_call(
        body, name=name,
        grid_spec=pltpu.PrefetchScalarGridSpec(
            num_scalar_prefetch=1, grid=(nb, steps),
            in_specs=[pl.BlockSpec((None, tr, cols), lambda j, i, core_ref: (j, core_ref[0] * steps + i, 0)),
                      pl.BlockSpec((None, tr, cols), lambda j, i, core_ref: (j, i, 0))],
            out_specs=pl.BlockSpec((None, tr, cols), lambda j, i, core_ref: (j, i, 0))),
        out_shape=_sds((nb, half, cols), g.dtype),
        compiler_params=_params(("parallel", "parallel")),
    )(core, g, recv)


def scatter_hook(parts):
    n = len(parts)

    def copies(ins, outs, send_sems, recv_sems):
        px, py, pc = _position()
        return [_remote(ins[a].at[2 * ox + oy], outs[a].at[j], send_sems.at[a, j], recv_sems.at[a, j], (ox, oy, pc))
                for a in range(n) for j, (ox, oy) in enumerate(_other_chips(px, py))]

    def start(ins, outs, send_sems, recv_sems):
        for cp in copies(ins, outs, send_sems, recv_sems):
            cp.start()

    def finish(ins, outs, send_sems, recv_sems):
        for cp in copies(ins, outs, send_sems, recv_sems):
            cp.wait()

    return dict(operands=list(parts), out_shapes=[_sds((N_CHIPS - 1,) + p.shape[1:], p.dtype) for p in parts],
                sems=(n, 3), start=start, finish=finish)


def chip_sum(name, part, recv, where, out_shape, lead, dest=None):
    _, half, cols = part.shape
    tr = _tile(half, 256)
    steps = half // tr

    def body(where_ref, p_ref, r_ref, *rest):
        o_ref = rest[-1]
        acc = p_ref[...].astype(F32)
        for j in range(N_CHIPS - 1):
            acc = acc + r_ref[j].astype(F32)
        o_ref[...] = acc

    if lead is None:
        ospec = pl.BlockSpec((tr, cols), lambda i, w: (w[1] * steps + i, 0))
    else:
        ospec = pl.BlockSpec((None, tr, cols), lambda i, w: (lead, w[1] * steps + i, 0))
    in_specs = [pl.BlockSpec((None, tr, cols), lambda i, w: (w[0], i, 0)),
                pl.BlockSpec((N_CHIPS - 1, tr, cols), lambda i, w: (0, i, 0))]
    operands = [where, part, recv]
    aliases = {}
    if dest is not None:
        in_specs.append(pl.BlockSpec(memory_space=pl.ANY))
        operands.append(dest)
        aliases = {3: 0}
    return pl.pallas_call(
        body, name=name,
        grid_spec=pltpu.PrefetchScalarGridSpec(num_scalar_prefetch=1, grid=(steps,),
                                               in_specs=in_specs, out_specs=ospec),
        out_shape=out_shape, input_output_aliases=aliases,
        compiler_params=_params(("parallel",)),
    )(*operands)


def sibling_share(name, slabs, places):
    n = len(slabs)
    k = len(places)

    def body(*refs):
        outs = refs[n:2 * n]
        send_sems, recv_sems = refs[2 * n:]
        px, py, pc = _position()
        sibling = (px, py, 1 - pc)
        copies = []
        for a, (oi, lead, half) in enumerate(places):
            slab = outs[oi] if lead is None else outs[oi].at[lead]
            mine = slab.at[pl.ds(pc * half, half)]
            theirs = slab.at[pl.ds((1 - pc) * half, half)]
            cp = _remote(mine, mine, send_sems.at[a], recv_sems.at[a], sibling)
            cp.start()
            copies.append((cp, _remote(theirs, theirs, send_sems.at[a], recv_sems.at[a], sibling)))
        for cp, arrival in copies:
            cp.wait_send()
            arrival.wait_recv()

    return pl.pallas_call(
        body, name=name,
        in_specs=[HBM] * n, out_specs=[HBM] * n,
        out_shape=[_sds(s.shape, s.dtype) for s in slabs],
        input_output_aliases={a: a for a in range(n)},
        scratch_shapes=[pltpu.SemaphoreType.DMA((k,)), pltpu.SemaphoreType.DMA((k,))],
    )(*slabs)


def _row(a, i):
    return a[i:i + 1]


def _relu_sq(acc):
    r = jnp.maximum(acc, 0.0)
    return r, r * r


def _residual(acc, res, gate):
    return res + gate * acc, acc


def _with_comm(res, comm):
    return res if comm else (res, [])


def _blocked(g):
    return g if g.ndim == 3 else g.reshape(N_CHIPS, g.shape[0] // N_CHIPS, g.shape[1])


def _pre_reduce(tag, named, core):
    glist = [_blocked(g) for _, g in named]
    recv = sibling_exchange("grad_exchange_" + tag, glist)
    return [pair_add("grad_pair_add_" + k, g, r, core) for (k, _), g, r in zip(named, glist, recv)]


def _mlp_fwd(tag, x, gain, mod, w1, w2_shard, down_comm=None):
    h = norm_mod_fwd(tag + "_mlp_norm", x, gain, _row(mod, 4), _row(mod, 3))
    (r3, a3), (w2,) = mm_nn_b(tag + "_mlp_up", h, w1, [BF16, BF16], _relu_sq, comm=gather_hook([w2_shard]))
    (x_out, m), got = _with_comm(mm_nn_r(tag + "_mlp_down", a3, w2, [F32, BF16], _residual,
                                         (x, _row(mod, 5)), ("tile", "row"), comm=down_comm), down_comm)
    return x_out, (h, r3, a3, m), w2, got


def _mlp_bwd(tag, dx_out, x, gain, mod, w1, w2, saved, core, down_comm=None):
    h, r3, a3, m = saved
    dm, dgate = gate_bwd(tag + "_mlp_gate_bwd", dx_out, m, _row(mod, 5))
    (dz3,), got = _with_comm(mm_nt_b(tag + "_mlp_down_dgrad", dm, w2, [BF16],
                                     lambda acc, r: (acc * (2.0 * r.astype(F32)),), (r3,), comm=down_comm),
                             down_comm)
    gw2 = mm_tn(tag + "_mlp_w2_grad", a3, dm[None], BF16)
    part2 = _pre_reduce(tag + "_w2", [(tag + "_w2", gw2)], core)
    gw1, recv2 = mm_tn(tag + "_mlp_w1_grad", h[None], dz3, BF16, comm=scatter_hook(part2))
    part1 = _pre_reduce(tag + "_w1", [(tag + "_w1", gw1)], core)
    dh, recv1 = mm_nt_r(tag + "_mlp_up_dgrad", dz3, w1, F32, comm=scatter_hook(part1))
    dx, dsh, dsc, dgain = norm_mod_bwd(tag + "_mlp_norm_bwd", dh, x, gain, _row(mod, 4), dx_out)
    return dx, (part1[0], recv1[0]), (part2[0], recv2[0]), (dsh, dsc, dgate), dgain, got


def local_step(x, target, mod0, mod1, kvmod, norm_mix, norm_mlp, kv_norm, final_norm, lb, out_gain,
               w, shards, core):
    w = dict(w)

    def flat(g):
        return g.reshape(-1, g.shape[-1])

    h1 = norm_mod_fwd("l0_mix_norm", x, _row(norm_mix, 0), _row(mod0, 1), _row(mod0, 0))
    (proj3,), (w["w1_0"],) = mm_nn_b("l0_in_proj", h1, w["a_in"], [F32], comm=gather_hook([shards["w1_0"]]))
    u, o_h, states = hgrn_fwd("l0_hgrn_fwd", proj3, lb, out_gain)
    (x1, y0), (got,) = mm_nn_b("l0_out_proj", u, w["a_out"][None], [F32, BF16], _residual,
                               (x[None], _row(mod0, 2)), ("tile", "row"), comm=gather_hook([shards["bq"]]))
    x1, y0, w["bq"] = x1[0], y0[0], flat(got)
    x2, mlp0, w["w2_0"], (w["w1_1"], got) = _mlp_fwd(
        "l0", x1, _row(norm_mlp, 0), mod0, w["w1_0"], shards["w2_0"],
        gather_hook([shards["w1_1"], shards["kv"]]))
    w["kv"] = flat(got)
    hk = norm_mod_fwd("kv_norm", x2, kv_norm, _row(kvmod, 1), _row(kvmod, 0))
    kv = mm_nn_b("kv_proj", hk, w["kv"][None], [BF16])[0][0]
    h3 = norm_mod_fwd("l1_mix_norm", x2, _row(norm_mix, 1), _row(mod1, 1), _row(mod1, 0))
    (q,), (got,) = mm_nn_b("l1_q_proj", h3, w["bq"][None], [BF16], comm=gather_hook([shards["bo"]]))
    q, w["bo"] = q[0], flat(got)
    o_a, totals, visited = attn_fwd("l1_attn_fwd", q, kv)
    x3, y1 = mm_nn_b("l1_out_proj", o_a, w["bo"][None], [F32, BF16], _residual,
                     (x2[None], _row(mod1, 2)), ("tile", "row"))
    x3, y1 = x3[0], y1[0]
    x4, mlp1, w["w2_1"], _ = _mlp_fwd("l1", x3, _row(norm_mlp, 1), mod1, w["w1_1"], shards["w2_1"])
    dx4, d_final, loss = final_loss("final_loss", x4, final_norm, target)

    reduce = {}
    dx3, reduce["w1_1"], reduce["w2_1"], dmlp1, d_nmlp1, _ = _mlp_bwd(
        "l1", dx4, x3, _row(norm_mlp, 1), mod1, w["w1_1"], w["w2_1"], mlp1, core)
    dy1, dgate1 = gate_bwd("l1_mix_gate_bwd", dx3, y1, _row(mod1, 2))
    do_a = mm_nt_b("l1_out_dgrad", dy1, w["bo"][None], [F32])[0][0]
    g_bo = mm_tn("l1_out_grad", o_a[None], dy1[None], BF16)[0]
    dq, dkv3 = attn_bwd("l1_attn_bwd", q, kv, totals, visited, do_a)
    g_bq = mm_tn("l1_q_grad", h3[None], dq[None], BF16)[0]
    dh3 = mm_nt_b("l1_q_dgrad", dq, w["bq"][None], [F32])[0][0]
    dx2, dsh, dsc, d_nmix1 = norm_mod_bwd("l1_mix_norm_bwd", dh3, x2, _row(norm_mix, 1), _row(mod1, 1), dx3)
    dmod1 = jnp.concatenate([dsh, dsc, dgate1, *dmlp1], axis=0)
    dkv = jnp.concatenate([dkv3[0], dkv3[1]], axis=1).astype(BF16)
    g_kv = mm_tn("kv_grad", hk[None], dkv[None], BF16)[0]
    dhk = mm_nt_b("kv_dgrad", dkv, w["kv"][None], [F32])[0][0]
    dx2, dsh, dsc, d_nkv = norm_mod_bwd("kv_norm_bwd", dhk, x2, kv_norm, _row(kvmod, 1), dx2)
    dkvmod = jnp.concatenate([dsh, dsc], axis=0)
    attn_parts = _pre_reduce("attn", [("bo", g_bo), ("bq", g_bq), ("kv", g_kv)], core)
    dx1, reduce["w1_0"], reduce["w2_0"], dmlp0, d_nmlp0, attn_recv = _mlp_bwd(
        "l0", dx2, x1, _row(norm_mlp, 0), mod0, w["w1_0"], w["w2_0"], mlp0, core, scatter_hook(attn_parts))
    for k, part, recv in zip(("bo", "bq", "kv"), attn_parts, attn_recv):
        reduce[k] = (part, recv)
    dy0, dgate0 = gate_bwd("l0_mix_gate_bwd", dx1, y0, _row(mod0, 2))
    du = mm_nt_b("l0_out_dgrad", dy0, w["a_out"][None], [F32])[0][0]
    g_a_out = mm_tn("l0_out_grad", u[None], dy0[None], BF16)[0]
    dproj3, d_lb, d_out_gain = hgrn_bwd("l0_hgrn_bwd", proj3, lb, out_gain, o_h, du, states)
    part = _pre_reduce("a_out", [("a_out", g_a_out)], core)
    g_a_in, recv = mm_tn("l0_in_grad", h1[None], dproj3, BF16, comm=scatter_hook(part))
    reduce["a_out"] = (part[0], recv[0])
    part = _pre_reduce("a_in", [("a_in", g_a_in)], core)
    dh1, recv = mm_nt_r("l0_in_dgrad", dproj3, w["a_in"], F32, comm=scatter_hook(part))
    reduce["a_in"] = (part[0], recv[0])
    dx0, dsh, dsc, d_nmix0 = norm_mod_bwd("l0_mix_norm_bwd", dh1, x, _row(norm_mix, 0), _row(mod0, 1), dx1)
    dmod0 = jnp.concatenate([dsh, dsc, dgate0, *dmlp0], axis=0)
    small = dict(norm_mix=(d_nmix0, d_nmix1), norm_mlp=(d_nmlp0, d_nmlp1), kv_norm=d_nkv,
                 final_norm=d_final, lb=d_lb, out_gain=d_out_gain)
    return loss, dx0, reduce, dmod0, dmod1, dkvmod, small


BIG = ("a_in", "w1_0", "w1_1", "w2_0", "w2_1", "a_out", "bq", "bo", "kv")


def kernel(x, c, ada_w, ada_b, norm_mix, norm_mlp, a_w_in, a_lb_logits, a_out_gain, a_w_out, kv_ada_w, kv_ada_b, kv_norm, w_kv, b_w_q, b_w_out, mlp_w1, mlp_w2, final_norm, loss_target, m_ada_w, m_ada_b, m_norm_mix, m_norm_mlp, m_a_w_in, m_a_lb_logits, m_a_out_gain, m_a_w_out, m_kv_ada_w, m_kv_ada_b, m_kv_norm, m_w_kv, m_b_w_q, m_b_w_out, m_mlp_w1, m_mlp_w2, m_final_norm, v_ada_w, v_ada_b, v_norm_mix, v_norm_mlp, v_a_w_in, v_a_lb_logits, v_a_out_gain, v_a_w_out, v_kv_ada_w, v_kv_ada_b, v_kv_norm, v_w_kv, v_b_w_q, v_b_w_out, v_mlp_w1, v_mlp_w2, v_final_norm):
    d = x.shape[-1]
    px, py, pc = _position()
    me = 4 * px + 2 * py + pc
    chip = 2 * px + py
    n_ada = ada_w.shape[2]
    n_kvada = kv_ada_w.shape[1]
    shard_cols = d // N_CHIPS

    def as_rows(a):
        return a.reshape(-1, shard_cols)

    pack1 = jnp.concatenate([as_rows(c), a_lb_logits, a_out_gain,
                             jnp.zeros((1, shard_cols), F32)], axis=0)
    got1 = allgather8("gather_cond", pack1)
    n_c = d // shard_cols
    c_all = got1[:, :n_c, :].reshape(N_DEV, d)
    per_chip = got1[0::2]
    logits_full = jnp.swapaxes(per_chip[:, n_c:n_c + 2, :], 0, 1).reshape(2, d)
    out_gain_full = per_chip[:, n_c + 2, :].reshape(1, d)
    lb = lower_bound_fwd("lower_bound", logits_full)

    bias0 = lax.dynamic_slice_in_dim(ada_b, chip * n_ada, n_ada, axis=1)
    bias_kv = lax.dynamic_slice_in_dim(kv_ada_b.reshape(1, -1), chip * n_kvada, n_kvada, axis=1)
    mod_part = jnp.concatenate([
        ada_project("ada_proj_0", c_all, ada_w[0], bias0[0:1]),
        ada_project("ada_proj_1", c_all, ada_w[1], bias0[1:2]),
        ada_project("ada_proj_kv", c_all, kv_ada_w, bias_kv)], axis=1)
    got2 = allgather8("gather_mod", mod_part)
    mine = lax.dynamic_index_in_dim(got2[0::2], me, axis=1, keepdims=False)
    mod0 = mine[:, :n_ada].reshape(6, d)
    mod1 = mine[:, n_ada:2 * n_ada].reshape(6, d)
    kvmod = mine[:, 2 * n_ada:].reshape(2, d)

    shards = dict(a_in=a_w_in[0], w1_0=mlp_w1[0], w1_1=mlp_w1[1], w2_0=mlp_w2[0], w2_1=mlp_w2[1],
                  a_out=a_w_out[0], bq=b_w_q[0], bo=b_w_out[0], kv=w_kv)
    shards = {k: s.astype(BF16) for k, s in shards.items()}
    g_in, g_out = gather_weights("gather_first_weights", [shards["a_in"], shards["a_out"]])
    core = pc.astype(jnp.int32).reshape(1)

    loss, dx0, reduce, dmod0, dmod1, dkvmod, small = local_step(
        x[0], loss_target[0], mod0, mod1, kvmod, norm_mix, norm_mlp, kv_norm.reshape(1, d),
        final_norm.reshape(1, d), lb, out_gain_full,
        dict(a_in=g_in, a_out=g_out.reshape(-1, g_out.shape[-1])), shards, core)

    loss_row = jnp.concatenate([loss, jnp.zeros((1, shard_cols - loss.shape[1]), F32)], axis=1)
    rows = [as_rows(dmod0), as_rows(dmod1), as_rows(dkvmod),
            as_rows(small["norm_mix"][0]), as_rows(small["norm_mix"][1]),
            as_rows(small["norm_mlp"][0]), as_rows(small["norm_mlp"][1]),
            as_rows(small["kv_norm"]), as_rows(small["final_norm"]),
            as_rows(small["lb"]), as_rows(small["out_gain"]), loss_row]
    n_rows = sum(r.shape[0] for r in rows)
    pad = (-n_rows) % 8
    pack3 = jnp.concatenate(rows + [jnp.zeros((pad, shard_cols), F32)], axis=0)
    got3 = allgather8("gather_small_grads", pack3)
    total = device_sum("sum_small_grads", got3)
    n_mod_rows = (12 * d + 2 * d) // shard_cols
    n_gain_rows = 6 * n_c
    loss_out = total[n_mod_rows + n_gain_rows + 2 * n_c, 0]

    dmod_all = got3[:, :n_mod_rows, :].reshape(N_DEV, 14 * d)
    act_t = jnp.zeros((d, 128), F32).at[:, :N_DEV].set(c_all.T)

    def dmod_cols(lo, width):
        part = lax.dynamic_slice_in_dim(dmod_all, lo + chip * width, width, axis=1)
        return jnp.zeros((128, width), F32).at[:N_DEV].set(part)

    g_ada_w, d_ada_w, nm_ada_w, nv_ada_w = ada_grad_adamw(
        "ada_update", act_t, jnp.stack([dmod_cols(0, n_ada), dmod_cols(6 * d, n_ada)]), ada_w, m_ada_w, v_ada_w)
    g_kv_ada_w, d_kv_ada_w, nm_kv_ada_w, nv_kv_ada_w = (a[0] for a in ada_grad_adamw(
        "ada_update_kv", act_t, dmod_cols(12 * d, n_kvada)[None], kv_ada_w[None], m_kv_ada_w[None],
        v_kv_ada_w[None]))

    chip_parts = [reduce[k][0] for k in BIG]
    from_chips = [reduce[k][1] for k in BIG]
    where = jnp.stack([chip, pc]).astype(jnp.int32)
    out_shapes = [_sds(a_w_in.shape, F32), _sds(mlp_w1.shape, F32), _sds(mlp_w2.shape, F32),
                  _sds(a_w_out.shape, F32), _sds(b_w_q.shape, F32), _sds(b_w_out.shape, F32),
                  _sds(w_kv.shape, F32)]
    targets = [(0, 0), (1, 0), (1, 1), (2, 0), (2, 1), (3, 0), (4, 0), (5, 0), (6, None)]
    slabs = [None] * len(out_shapes)
    places = []
    for k, part, recv, (oi, lead) in zip(BIG, chip_parts, from_chips, targets):
        slabs[oi] = chip_sum("grad_chip_sum_" + k, part, recv, where, out_shapes[oi], lead, slabs[oi])
        places.append((oi, lead, part.shape[1]))
    g_a_w_in, g_mlp_w1, g_mlp_w2, g_a_w_out, g_b_w_q, g_b_w_out, g_w_kv = sibling_share(
        "grad_sibling_share", slabs, places)

    def update(name, wgt, g, m_, v_):
        shape = wgt.shape
        f = lambda a: a.reshape(-1, shape[-1])
        return tuple(o.reshape(shape) for o in adamw(name, f(wgt), f(g), f(m_), f(v_)))

    u_a_w_in = update("adamw_a_w_in", a_w_in, g_a_w_in, m_a_w_in, v_a_w_in)
    u_mlp_w1 = update("adamw_mlp_w1", mlp_w1, g_mlp_w1, m_mlp_w1, v_mlp_w1)
    u_mlp_w2 = update("adamw_mlp_w2", mlp_w2, g_mlp_w2, m_mlp_w2, v_mlp_w2)
    u_a_w_out = update("adamw_a_w_out", a_w_out, g_a_w_out, m_a_w_out, v_a_w_out)
    u_b_w_q = update("adamw_b_w_q", b_w_q, g_b_w_q, m_b_w_q, v_b_w_q)
    u_b_w_out = update("adamw_b_w_out", b_w_out, g_b_w_out, m_b_w_out, v_b_w_out)
    u_w_kv = update("adamw_w_kv", w_kv, g_w_kv, m_w_kv, v_w_kv)

    base = n_mod_rows + n_gain_rows
    d_lb_mine = lax.dynamic_slice_in_dim(total, base + chip, 1, axis=0)
    d_gain_mine = lax.dynamic_slice_in_dim(total, base + n_c + chip, 1, axis=0)
    d_logits = lower_bound_bwd("lower_bound_bwd", a_lb_logits, d_lb_mine)

    def pack_small(ada_b_, kv_ada_b_, norm_mix_, norm_mlp_, kv_norm_, final_norm_, lbl_, gain_):
        parts = [as_rows(ada_b_), as_rows(kv_ada_b_), as_rows(norm_mix_), as_rows(norm_mlp_),
                 as_rows(kv_norm_), as_rows(final_norm_), lbl_, gain_]
        n = sum(p.shape[0] for p in parts)
        return jnp.concatenate(parts + [jnp.zeros(((-n) % 8, shard_cols), F32)], axis=0)

    w_small = pack_small(ada_b, kv_ada_b, norm_mix, norm_mlp, kv_norm, final_norm, a_lb_logits, a_out_gain)
    m_small = pack_small(m_ada_b, m_kv_ada_b, m_norm_mix, m_norm_mlp, m_kv_norm, m_final_norm,
                         m_a_lb_logits, m_a_out_gain)
    v_small = pack_small(v_ada_b, v_kv_ada_b, v_norm_mix, v_norm_mlp, v_kv_norm, v_final_norm,
                         v_a_lb_logits, v_a_out_gain)
    n_small = w_small.shape[0]
    g_small = jnp.concatenate([total[:base], d_logits, d_gain_mine,
                               jnp.zeros((n_small - base - 3, shard_cols), F32)], axis=0)
    small_out = (g_small,) + tuple(adamw("adamw_small", w_small, g_small, m_small, v_small))

    def unpack_small(p):
        out, r0 = [], 0
        for ref in (ada_b, kv_ada_b, norm_mix, norm_mlp, kv_norm, final_norm, a_lb_logits, a_out_gain):
            nr = ref.size // shard_cols
            out.append(p[r0:r0 + nr].reshape(ref.shape))
            r0 += nr
        return out

    sm = [unpack_small(p) for p in small_out]

    def leaves(kind, big_ada, big_kv_ada):
        ada_b_, kv_ada_b_, norm_mix_, norm_mlp_, kv_norm_, final_norm_, lbl_, gain_ = sm[kind]
        pick = (lambda u, g: g) if kind == 0 else (lambda u, g: u[kind - 1])
        return [big_ada, ada_b_, norm_mix_, norm_mlp_, pick(u_a_w_in, g_a_w_in), lbl_, gain_,
                pick(u_a_w_out, g_a_w_out), big_kv_ada, kv_ada_b_, kv_norm_, pick(u_w_kv, g_w_kv),
                pick(u_b_w_q, g_b_w_q), pick(u_b_w_out, g_b_w_out), pick(u_mlp_w1, g_mlp_w1),
                pick(u_mlp_w2, g_mlp_w2), final_norm_]

    return (loss_out, dx0[None],
            *leaves(0, g_ada_w, g_kv_ada_w), *leaves(1, d_ada_w, d_kv_ada_w),
            *leaves(2, nm_ada_w, nm_kv_ada_w), *leaves(3, nv_ada_w, nv_kv_ada_w))
```

```python
import functools

import jax
import jax.numpy as jnp
from jax import lax
from jax.experimental import pallas as pl
from jax.experimental.pallas import tpu as pltpu

F32 = jnp.float32
BF16 = jnp.bfloat16
MESH = pl.DeviceIdType.MESH

HEAD_DIM = 128
KV_GROUP = 4
HGRN_CHUNK = 64
HGRN_HEADS = 4
HGRN_ROWS = 256
NORM_EPS = 1e-6
N_CHIPS = 4
N_DEV = 8
ROW_TILE = 256
ATTN_TILE = 256
VMEM_LIMIT = 56 * 1024 * 1024
DEAD_LOG_WEIGHT = -110.0

ADAM_LR = 0.001
ADAM_B1 = 0.9
ADAM_B2 = 0.999
ADAM_EPS = 1e-08
ADAM_WD = 0.01
ADAM_STEP = 10

NN = (((1,), (0,)), ((), ()))
NT = (((1,), (1,)), ((), ()))
TN = (((0,), (0,)), ((), ()))


def _dot(a, b, dims=NN, precision=None):
    return lax.dot_general(a, b, dims, preferred_element_type=F32, precision=precision)


def _bdot(a, b, dims=NN):
    return _dot(a.astype(BF16), b.astype(BF16), dims)


def _sigmoid(x):
    return 1.0 / (1.0 + jnp.exp(-x))


def _log_sigmoid(z):
    return jnp.minimum(z, 0.0) - jnp.log(1.0 + jnp.exp(-jnp.abs(z)))


def _split_bf16(x):
    hi = x.astype(BF16)
    lo = (x - hi.astype(F32)).astype(BF16)
    return hi, lo


def _params(sem=None):
    return pltpu.CompilerParams(dimension_semantics=sem, vmem_limit_bytes=VMEM_LIMIT)


def _tile(n, pref):
    t = min(n, pref)
    assert n % t == 0, (n, pref)
    return t


def _mm(name, a, b, a_spec, b_spec, grid, n_red, dims, out_shapes, out_specs,
        acc_shape, epilogue=None, extras=(), extra_specs=(), comm=None):
    n_extra = len(extras)
    n_out = len(out_shapes)
    n_cin = len(comm["operands"]) if comm else 0
    n_cout = len(comm["out_shapes"]) if comm else 0
    if epilogue is None:
        epilogue = lambda acc: (acc,)

    def body(*refs):
        a_ref, b_ref = refs[:2]
        ex_refs = refs[2:2 + n_extra]
        cin_refs = refs[2 + n_extra:2 + n_extra + n_cin]
        out_refs = refs[2 + n_extra + n_cin:2 + n_extra + n_cin + n_out]
        cout_refs = refs[2 + n_extra + n_cin + n_out:2 + n_extra + n_cin + n_out + n_cout]
        scratch = refs[2 + n_extra + n_cin + n_out + n_cout:]
        pids = [pl.program_id(ax) for ax in range(len(grid))]
        if comm:
            send_sems, recv_sems = scratch[-2:]

            @pl.when(functools.reduce(jnp.logical_and, [p == 0 for p in pids]))
            def _():
                comm["start"](cin_refs, cout_refs, send_sems, recv_sems)

        prod = _bdot(a_ref[...], b_ref[...], dims)

        def finish(acc):
            res = epilogue(acc, *[e[...] for e in ex_refs])
            for o_ref, r in zip(out_refs, res):
                o_ref[...] = r.astype(o_ref.dtype)

        if n_red == 0:
            finish(prod)
        else:
            acc_ref = scratch[0]
            ids = pids[len(grid) - n_red:]
            sizes = grid[len(grid) - n_red:]
            first = functools.reduce(jnp.logical_and, [i == 0 for i in ids])
            last = functools.reduce(jnp.logical_and, [i == s - 1 for i, s in zip(ids, sizes)])

            @pl.when(first)
            def _():
                acc_ref[...] = prod

            @pl.when(jnp.logical_not(first))
            def _():
                acc_ref[...] += prod

            @pl.when(last)
            def _():
                finish(acc_ref[...])

        if comm:
            @pl.when(functools.reduce(jnp.logical_and, [p == s - 1 for p, s in zip(pids, grid)]))
            def _():
                comm["finish"](cin_refs, cout_refs, send_sems, recv_sems)

    if comm:
        sem = ("arbitrary",) * len(grid)
    else:
        sem = ("parallel",) * (len(grid) - n_red) + ("arbitrary",) * n_red
    scratch_shapes = [pltpu.VMEM(acc_shape, F32)] if n_red else []
    if comm:
        scratch_shapes += [pltpu.SemaphoreType.DMA(comm["sems"]), pltpu.SemaphoreType.DMA(comm["sems"])]
    out = pl.pallas_call(
        body, name=name, grid=grid,
        in_specs=[a_spec, b_spec, *extra_specs] + [HBM] * n_cin,
        out_specs=list(out_specs) + [HBM] * n_cout,
        out_shape=list(out_shapes) + (list(comm["out_shapes"]) if comm else []),
        scratch_shapes=scratch_shapes,
        compiler_params=_params(sem),
    )(a, b, *extras, *(comm["operands"] if comm else ()))
    return (out[:n_out], out[n_out:]) if comm else out


def _sds(shape, dtype):
    return jax.ShapeDtypeStruct(shape, dtype)


def mm_nn_b(name, a, w3, out_dtypes, epilogue=None, extras=(), extra_kinds=(), comm=None):
    m, k = a.shape
    nb, _, n = w3.shape
    tm, tn = _tile(m, 1024), _tile(n, 512)
    grid = (nb, m // tm, n // tn)
    nt = n // tn
    especs = []
    for kind in extra_kinds:
        if kind == "tile":
            especs.append(pl.BlockSpec((None, tm, tn), lambda j, i, c: (j, i, c)))
        else:
            especs.append(pl.BlockSpec((1, tn), lambda j, i, c: (0, j * nt + c)))
    return _mm(name, a, w3,
               pl.BlockSpec((tm, k), lambda j, i, c: (i, 0)),
               pl.BlockSpec((None, k, tn), lambda j, i, c: (j, 0, c)),
               grid, 0, NN,
               [_sds((nb, m, n), d) for d in out_dtypes],
               [pl.BlockSpec((None, tm, tn), lambda j, i, c: (j, i, c)) for _ in out_dtypes],
               None, epilogue, extras, especs, comm)


def mm_nn_r(name, a3, w3, out_dtypes, epilogue=None, extras=(), extra_kinds=(), comm=None):
    nb, m, kb = a3.shape
    n = w3.shape[2]
    tm, tn, tk = _tile(m, 1024), _tile(n, 512), _tile(kb, 2048)
    grid = (m // tm, n // tn, nb, kb // tk)
    especs = []
    for kind in extra_kinds:
        if kind == "tile":
            especs.append(pl.BlockSpec((tm, tn), lambda i, c, j, r: (i, c)))
        else:
            especs.append(pl.BlockSpec((1, tn), lambda i, c, j, r: (0, c)))
    return _mm(name, a3, w3,
               pl.BlockSpec((None, tm, tk), lambda i, c, j, r: (j, i, r)),
               pl.BlockSpec((None, tk, tn), lambda i, c, j, r: (j, r, c)),
               grid, 2, NN,
               [_sds((m, n), d) for d in out_dtypes],
               [pl.BlockSpec((tm, tn), lambda i, c, j, r: (i, c)) for _ in out_dtypes],
               (tm, tn), epilogue, extras, especs, comm)


def mm_nt_b(name, a, w3, out_dtypes, epilogue=None, extras=(), comm=None):
    m, n = a.shape
    nb, kb, _ = w3.shape
    tm, tk = _tile(m, 1024), _tile(kb, 512)
    grid = (nb, m // tm, kb // tk)
    especs = [pl.BlockSpec((None, tm, tk), lambda j, i, c: (j, i, c)) for _ in extras]
    return _mm(name, a, w3,
               pl.BlockSpec((tm, n), lambda j, i, c: (i, 0)),
               pl.BlockSpec((None, tk, n), lambda j, i, c: (j, c, 0)),
               grid, 0, NT,
               [_sds((nb, m, kb), d) for d in out_dtypes],
               [pl.BlockSpec((None, tm, tk), lambda j, i, c: (j, i, c)) for _ in out_dtypes],
               None, epilogue, extras, especs, comm)


def _single(res, comm):
    return res[0] if comm is None else (res[0][0], res[1])


def mm_nt_r(name, a3, w3, out_dtype, comm=None):
    nb, m, n = a3.shape
    k = w3.shape[1]
    tm, tk, tc = _tile(m, 1024), _tile(k, 1024), _tile(n, 2048)
    grid = (m // tm, k // tk, nb, n // tc)
    return _single(_mm(name, a3, w3,
                       pl.BlockSpec((None, tm, tc), lambda i, c, j, r: (j, i, r)),
                       pl.BlockSpec((None, tk, tc), lambda i, c, j, r: (j, c, r)),
                       grid, 2, NT,
                       [_sds((m, k), out_dtype)],
                       [pl.BlockSpec((tm, tk), lambda i, c, j, r: (i, c))],
                       (tm, tk), comm=comm), comm)


def mm_tn(name, a3, d3, out_dtype, comm=None):
    na, m, kb = a3.shape
    nd, _, n = d3.shape
    nb = max(na, nd)
    tk, tn, tm = _tile(kb, 512), _tile(n, 2048), _tile(m, 1024)
    grid = (nb, kb // tk, n // tn, m // tm)
    ja = (lambda j: j) if na > 1 else (lambda j: 0)
    jd = (lambda j: j) if nd > 1 else (lambda j: 0)
    return _single(_mm(name, a3, d3,
                       pl.BlockSpec((None, tm, tk), lambda j, c, e, r: (ja(j), r, c)),
                       pl.BlockSpec((None, tm, tn), lambda j, c, e, r: (jd(j), r, e)),
                       grid, 1, TN,
                       [_sds((nb, kb, n), out_dtype)],
                       [pl.BlockSpec((None, tk, tn), lambda j, c, e, r: (j, c, e))],
                       (tk, tn), comm=comm), comm)


def _row_spec(ts, d):
    return pl.BlockSpec((ts, d), lambda i: (i, 0))


def _vec_spec(d):
    return pl.BlockSpec((1, d), lambda i: (0, 0))


def norm_mod_fwd(name, x, gain, scale, shift):
    s, d = x.shape
    ts = _tile(s, ROW_TILE)

    def body(x_ref, g_ref, sc_ref, sh_ref, h_ref):
        xv = x_ref[...]
        inv = lax.rsqrt(jnp.mean(xv * xv, axis=-1, keepdims=True) + NORM_EPS)
        h = (xv * inv) * g_ref[...] * (1.0 + sc_ref[...]) + sh_ref[...]
        h_ref[...] = h.astype(h_ref.dtype)

    return pl.pallas_call(
        body, name=name, grid=(s // ts,),
        in_specs=[_row_spec(ts, d), _vec_spec(d), _vec_spec(d), _vec_spec(d)],
        out_specs=_row_spec(ts, d), out_shape=_sds((s, d), BF16),
        compiler_params=_params(("parallel",)),
    )(x, gain, scale, shift)


def norm_mod_bwd(name, dh, x, gain, scale, dres):
    s, d = x.shape
    ts = _tile(s, ROW_TILE)

    def body(dh_ref, x_ref, g_ref, sc_ref, dres_ref, dx_ref, dsh_ref, dsc_ref, dg_ref):
        @pl.when(pl.program_id(0) == 0)
        def _():
            dsh_ref[...] = jnp.zeros_like(dsh_ref)
            dsc_ref[...] = jnp.zeros_like(dsc_ref)
            dg_ref[...] = jnp.zeros_like(dg_ref)

        xv = x_ref[...]
        dhv = dh_ref[...].astype(F32)
        g = g_ref[...]
        inv = lax.rsqrt(jnp.mean(xv * xv, axis=-1, keepdims=True) + NORM_EPS)
        n = xv * inv
        dhn = dhv * (1.0 + sc_ref[...])
        dn = dhn * g
        dx = inv * (dn - n * jnp.mean(dn * n, axis=-1, keepdims=True))
        dx_ref[...] = dres_ref[...] + dx
        dsh_ref[...] += jnp.sum(dhv, axis=0, keepdims=True)
        dsc_ref[...] += jnp.sum(dhv * (n * g), axis=0, keepdims=True)
        dg_ref[...] += jnp.sum(dhn * n, axis=0, keepdims=True)

    return pl.pallas_call(
        body, name=name, grid=(s // ts,),
        in_specs=[_row_spec(ts, d), _row_spec(ts, d), _vec_spec(d), _vec_spec(d), _row_spec(ts, d)],
        out_specs=[_row_spec(ts, d), _vec_spec(d), _vec_spec(d), _vec_spec(d)],
        out_shape=[_sds((s, d), F32), _sds((1, d), F32), _sds((1, d), F32), _sds((1, d), F32)],
        compiler_params=_params(("arbitrary",)),
    )(dh, x, gain, scale, dres)


def gate_bwd(name, dx, y, gate):
    s, d = dx.shape
    ts = _tile(s, ROW_TILE)

    def body(dx_ref, y_ref, g_ref, dy_ref, dg_ref):
        @pl.when(pl.program_id(0) == 0)
        def _():
            dg_ref[...] = jnp.zeros_like(dg_ref)

        dxv = dx_ref[...]
        dy_ref[...] = (dxv * g_ref[...]).astype(dy_ref.dtype)
        dg_ref[...] += jnp.sum(dxv * y_ref[...].astype(F32), axis=0, keepdims=True)

    return pl.pallas_call(
        body, name=name, grid=(s // ts,),
        in_specs=[_row_spec(ts, d), _row_spec(ts, d), _vec_spec(d)],
        out_specs=[_row_spec(ts, d), _vec_spec(d)],
        out_shape=[_sds((s, d), BF16), _sds((1, d), F32)],
        compiler_params=_params(("arbitrary",)),
    )(dx, y, gate)


def final_loss(name, x, gain, target):
    s, d = x.shape
    ts = _tile(s, ROW_TILE)

    def body(x_ref, g_ref, t_ref, dx_ref, dg_ref, loss_ref):
        @pl.when(pl.program_id(0) == 0)
        def _():
            dg_ref[...] = jnp.zeros_like(dg_ref)
            loss_ref[...] = jnp.zeros_like(loss_ref)

        xv = x_ref[...]
        g = g_ref[...]
        inv = lax.rsqrt(jnp.mean(xv * xv, axis=-1, keepdims=True) + NORM_EPS)
        n = xv * inv
        diff = n * g - t_ref[...]
        per_tok = jnp.mean(diff * diff, axis=-1, keepdims=True)
        loss_ref[...] += 0.5 * jnp.sum(per_tok, axis=0, keepdims=True)
        dy = diff * (1.0 / d)
        dg_ref[...] += jnp.sum(dy * n, axis=0, keepdims=True)
        dn = dy * g
        dx_ref[...] = inv * (dn - n * jnp.mean(dn * n, axis=-1, keepdims=True))

    return pl.pallas_call(
        body, name=name, grid=(s // ts,),
        in_specs=[_row_spec(ts, d), _vec_spec(d), _row_spec(ts, d)],
        out_specs=[_row_spec(ts, d), _vec_spec(d), _vec_spec(128)],
        out_shape=[_sds((s, d), F32), _sds((1, d), F32), _sds((1, 128), F32)],
        compiler_params=_params(("arbitrary",)),
    )(x, gain, target)


def _hgrn_chunk_fwd(qr, fl, lbv, tri):
    sg = _sigmoid(fl)
    sgm = _sigmoid(-fl)
    f = lbv + (1.0 - lbv) * sg
    logf = jnp.log(f)
    k = (1.0 - lbv) * sgm
    cum = _dot(tri, logf, precision=lax.Precision.HIGHEST)
    cl = cum[HGRN_CHUNK - 1:HGRN_CHUNK, :]
    e = jnp.exp(cum)
    en = jnp.exp(-cum)
    es = jnp.exp(cl - cum)
    sq = _sigmoid(qr)
    qs = qr * sq
    return dict(sg=sg, sgm=sgm, f=f, k=k, cum=cum, cl=cl, e=e, en=en, es=es, sq=sq, qs=qs,
                qd=qs * e, ki=k * en, ks=k * es, dec=jnp.exp(cl))


def _tri_masks(strict=False):
    r = lax.broadcasted_iota(jnp.int32, (HGRN_CHUNK, HGRN_CHUNK), 0)
    c = lax.broadcasted_iota(jnp.int32, (HGRN_CHUNK, HGRN_CHUNK), 1)
    return (r > c) if strict else (r >= c)


def hgrn_fwd(name, proj3, lb, out_gain):
    _, s, d = proj3.shape
    heads = d // HEAD_DIM
    t_rows = _tile(s, HGRN_ROWS)
    n_t = s // t_rows
    n_c = t_rows // HGRN_CHUNK
    width = HGRN_HEADS * HEAD_DIM

    def body(q_ref, f_ref, i_ref, g_ref, lb_ref, gain_ref, u_ref, o_ref, st_ref, state):
        @pl.when(pl.program_id(1) == 0)
        def _():
            state[...] = jnp.zeros_like(state)

        causal = _tri_masks()
        tri = causal.astype(F32)
        hs = range(HGRN_HEADS)
        col = [pl.ds(hh * HEAD_DIM, HEAD_DIM) for hh in hs]
        for ci in range(n_c):
            rows = pl.ds(ci * HGRN_CHUNK, HGRN_CHUNK)
            c = [_hgrn_chunk_fwd(q_ref[rows, col[hh]], f_ref[rows, col[hh]], lb_ref[:, col[hh]], tri) for hh in hs]
            v = [i_ref[rows, col[hh]] for hh in hs]
            st = [state[hh] for hh in hs]
            scores = [jnp.where(causal, _bdot(c[hh]["qd"], c[hh]["ki"], NT), 0.0) for hh in hs]
            inter = [_bdot(c[hh]["qd"], st[hh], NT) for hh in hs]
            update = [_bdot(v[hh], c[hh]["ks"], TN) for hh in hs]
            o = [_bdot(scores[hh], v[hh]) + inter[hh] for hh in hs]
            for hh in hs:
                st_ref[hh, ci] = st[hh]
                state[hh] = st[hh] * c[hh]["dec"] + update[hh]
            for hh in hs:
                graw = g_ref[rows, col[hh]]
                rms = lax.rsqrt(jnp.mean(o[hh] * o[hh], axis=-1, keepdims=True) + NORM_EPS)
                u = o[hh] * rms * gain_ref[:, col[hh]] * (graw * _sigmoid(graw))
                o_ref[rows, col[hh]] = o[hh]
                u_ref[rows, col[hh]] = u.astype(u_ref.dtype)

    def pspec(blk):
        return pl.BlockSpec((None, t_rows, width), lambda h, t: (blk, t, h))

    hspec = pl.BlockSpec((1, width), lambda h, t: (0, h))
    ospec = pl.BlockSpec((t_rows, width), lambda h, t: (t, h))
    return pl.pallas_call(
        body, name=name, grid=(heads // HGRN_HEADS, n_t),
        in_specs=[pspec(0), pspec(1), pspec(2), pspec(3), hspec, hspec],
        out_specs=[ospec, ospec,
                   pl.BlockSpec((HGRN_HEADS, n_c, HEAD_DIM, HEAD_DIM), lambda h, t: (h, t, 0, 0))],
        out_shape=[_sds((s, d), BF16), _sds((s, d), F32),
                   _sds((heads, s // HGRN_CHUNK, HEAD_DIM, HEAD_DIM), F32)],
        scratch_shapes=[pltpu.VMEM((HGRN_HEADS, HEAD_DIM, HEAD_DIM), F32)],
        compiler_params=_params(("parallel", "arbitrary")),
    )(proj3, proj3, proj3, proj3, lb, out_gain)


def hgrn_bwd(name, proj3, lb, out_gain, o, du, states):
    _, s, d = proj3.shape
    heads = d // HEAD_DIM
    t_rows = _tile(s, HGRN_ROWS)
    n_t = s // t_rows
    n_c = t_rows // HGRN_CHUNK
    width = HGRN_HEADS * HEAD_DIM

    def body(q_ref, f_ref, i_ref, g_ref, lb_ref, gain_ref, o_ref, du_ref, st_ref,
             dp_ref, dlb_ref, dgain_ref, dstate):
        @pl.when(pl.program_id(1) == 0)
        def _():
            dstate[...] = jnp.zeros_like(dstate)
            dlb_ref[...] = jnp.zeros_like(dlb_ref)
            dgain_ref[...] = jnp.zeros_like(dgain_ref)

        causal = _tri_masks()
        tri = causal.astype(F32)
        tri_t = jnp.logical_not(_tri_masks(strict=True)).astype(F32)
        hs = range(HGRN_HEADS)
        col = [pl.ds(hh * HEAD_DIM, HEAD_DIM) for hh in hs]
        lbv = [lb_ref[:, col[hh]] for hh in hs]
        gain = [gain_ref[:, col[hh]] for hh in hs]
        for ci in reversed(range(n_c)):
            rows = pl.ds(ci * HGRN_CHUNK, HGRN_CHUNK)
            qr = [q_ref[rows, col[hh]] for hh in hs]
            c = [_hgrn_chunk_fwd(qr[hh], f_ref[rows, col[hh]], lbv[hh], tri) for hh in hs]
            v = [i_ref[rows, col[hh]] for hh in hs]
            st = [st_ref[hh, ci] for hh in hs]
            dst = [dstate[hh] for hh in hs]
            do, dgraw = [], []
            for hh in hs:
                ov = o_ref[rows, col[hh]]
                duv = du_ref[rows, col[hh]].astype(F32)
                graw = g_ref[rows, col[hh]]
                sgg = _sigmoid(graw)
                gate = graw * sgg
                rms = lax.rsqrt(jnp.mean(ov * ov, axis=-1, keepdims=True) + NORM_EPS)
                on = ov * rms
                dgain_ref[:, col[hh]] += jnp.sum(duv * on * gate, axis=0, keepdims=True)
                dgraw.append(duv * on * gain[hh] * (sgg * (1.0 + graw * (1.0 - sgg))))
                don = duv * gain[hh] * gate
                do.append(rms * (don - on * jnp.mean(don * on, axis=-1, keepdims=True)))
            qd = [c[hh]["qd"] for hh in hs]
            ki = [c[hh]["ki"] for hh in hs]
            ks = [c[hh]["ks"] for hh in hs]
            p = [jnp.where(causal, _bdot(qd[hh], ki[hh], NT), 0.0) for hh in hs]
            dp = [jnp.where(causal, _bdot(do[hh], v[hh], NT), 0.0) for hh in hs]
            from_state = [_bdot(do[hh], st[hh]) for hh in hs]
            dks = [_bdot(v[hh], dst[hh]) for hh in hs]
            dv_state = [_bdot(ks[hh], dst[hh], NT) for hh in hs]
            dstate_new = [_bdot(do[hh], qd[hh], TN) for hh in hs]
            dqd = [_bdot(dp[hh], ki[hh]) + from_state[hh] for hh in hs]
            dki = [_bdot(dp[hh], qd[hh], TN) for hh in hs]
            dv = [_bdot(p[hh], do[hh], TN) + dv_state[hh] for hh in hs]
            ddec = [jnp.sum(dst[hh] * st[hh], axis=0, keepdims=True) for hh in hs]
            for hh in hs:
                dstate[hh] = dst[hh] * c[hh]["dec"] + dstate_new[hh]
            dcum = [dqd[hh] * qd[hh] - dki[hh] * ki[hh] - dks[hh] * ks[hh] for hh in hs]
            dcl = [jnp.sum(dks[hh] * ks[hh], axis=0, keepdims=True) + ddec[hh] * c[hh]["dec"] for hh in hs]
            dlogf = [_dot(tri_t, dcum[hh], precision=lax.Precision.HIGHEST) + dcl[hh] for hh in hs]
            for hh in hs:
                ch = c[hh]
                dqs = dqd[hh] * ch["e"]
                dk = dki[hh] * ch["en"] + dks[hh] * ch["es"]
                df = dlogf[hh] / ch["f"]
                sg, sgm, sq = ch["sg"], ch["sgm"], ch["sq"]
                one_m_lb = 1.0 - lbv[hh]
                dlb_ref[:, col[hh]] += jnp.sum(df * (1.0 - sg) - dk * sgm, axis=0, keepdims=True)
                dfl = df * one_m_lb * sg * (1.0 - sg) - dk * one_m_lb * sgm * (1.0 - sgm)
                dqr = dqs * (sq * (1.0 + qr[hh] * (1.0 - sq)))
                dp_ref[0, rows, col[hh]] = dqr.astype(dp_ref.dtype)
                dp_ref[1, rows, col[hh]] = dfl.astype(dp_ref.dtype)
                dp_ref[2, rows, col[hh]] = dv[hh].astype(dp_ref.dtype)
                dp_ref[3, rows, col[hh]] = dgraw[hh].astype(dp_ref.dtype)

    def pspec(blk):
        return pl.BlockSpec((None, t_rows, width), lambda h, t: (blk, n_t - 1 - t, h))

    hspec = pl.BlockSpec((1, width), lambda h, t: (0, h))
    ospec = pl.BlockSpec((t_rows, width), lambda h, t: (n_t - 1 - t, h))
    return pl.pallas_call(
        body, name=name, grid=(heads // HGRN_HEADS, n_t),
        in_specs=[pspec(0), pspec(1), pspec(2), pspec(3), hspec, hspec, ospec, ospec,
                  pl.BlockSpec((HGRN_HEADS, n_c, HEAD_DIM, HEAD_DIM), lambda h, t: (h, n_t - 1 - t, 0, 0))],
        out_specs=[pl.BlockSpec((4, t_rows, width), lambda h, t: (0, n_t - 1 - t, h)), hspec, hspec],
        out_shape=[_sds((4, s, d), BF16), _sds((1, d), F32), _sds((1, d), F32)],
        scratch_shapes=[pltpu.VMEM((HGRN_HEADS, HEAD_DIM, HEAD_DIM), F32)],
        compiler_params=_params(("parallel", "arbitrary")),
    )(proj3, proj3, proj3, proj3, lb, out_gain, o, du, states)


def attn_fwd(name, q, kv):
    s, dq = q.shape
    kvh = kv.shape[1] // (2 * HEAD_DIM)
    assert dq == kvh * KV_GROUP * HEAD_DIM
    tq = _tile(s, ATTN_TILE)
    scale = HEAD_DIM ** -0.5

    def body(q_ref, k_ref, v_ref, o_ref, tot_ref, cnt_ref):
        i = pl.program_id(1)
        heads = range(KV_GROUP)
        qs = [q_ref[:, g * HEAD_DIM:(g + 1) * HEAD_DIM] for g in heads]
        r_i = lax.broadcasted_iota(jnp.int32, (tq, tq), 0)
        c_i = lax.broadcasted_iota(jnp.int32, (tq, tq), 1)
        later = (r_i > c_i).astype(BF16)
        later2 = jnp.concatenate([later, later], axis=0)
        mask = c_i < r_i
        ones2 = jnp.ones((8, 2 * tq), BF16)

        def block(j, carry, masked):
            rows = pl.ds(pl.multiple_of(j * tq, tq), tq)
            kj = k_ref[rows, :]
            vj = v_ref[rows, :]
            z = [_dot(qs[g], kj, NT) * scale for g in heads]
            lbeta = [_log_sigmoid(z[g]) for g in heads]
            lrest = [lbeta[g] - z[g] for g in heads]
            if masked:
                lrest = [jnp.where(mask, lrest[g], 0.0) for g in heads]
            hl = [jnp.concatenate(_split_bf16(lrest[g]), axis=1) for g in heads]
            between = [_dot(hl[g], later2) + carry[g][0] for g in heads]
            sums = [_dot(ones2, hl[g], NT) for g in heads]
            w = [jnp.exp(lbeta[g] + between[g]) for g in heads]
            if masked:
                w = [jnp.where(mask, w[g], 0.0) for g in heads]
            pv = [_dot(w[g].astype(BF16), vj) for g in heads]
            return tuple((carry[g][0] + jnp.sum(lrest[g], axis=1, keepdims=True), carry[g][1] + pv[g],
                          carry[g][2] + sums[g]) for g in heads)

        def alive(carry):
            top = carry[0][0]
            for g in heads[1:]:
                top = jnp.maximum(top, carry[g][0])
            return jnp.max(top) > DEAD_LOG_WEIGHT

        zero = (jnp.zeros((tq, 1), F32), jnp.zeros((tq, HEAD_DIM), F32), jnp.zeros((8, tq), F32))
        carry = block(i, (zero,) * KV_GROUP, True)

        def step(state):
            jj, _, cr = state
            cr = block(i - 1 - jj, cr, False)
            return jj + 1, alive(cr), cr

        done, _, carry = lax.while_loop(lambda st: jnp.logical_and(st[0] < i, st[1]), step,
                                        (jnp.int32(0), alive(carry), carry))
        for g in heads:
            o_ref[:, g * HEAD_DIM:(g + 1) * HEAD_DIM] = carry[g][1]
            tot_ref[g] = carry[g][2][0:1, :]
        cnt_ref[pl.program_id(0), i] = done

    group = KV_GROUP * HEAD_DIM
    return pl.pallas_call(
        body, name=name, grid=(kvh, s // tq),
        in_specs=[pl.BlockSpec((tq, group), lambda kh, i: (i, kh)),
                  pl.BlockSpec((s, HEAD_DIM), lambda kh, i: (0, kh)),
                  pl.BlockSpec((s, HEAD_DIM), lambda kh, i: (0, kvh + kh))],
        out_specs=[pl.BlockSpec((tq, group), lambda kh, i: (i, kh)),
                   pl.BlockSpec((KV_GROUP, 1, tq), lambda kh, i: (kh, 0, i)),
                   pl.BlockSpec(memory_space=pltpu.SMEM)],
        out_shape=[_sds((s, dq), F32), _sds((dq // HEAD_DIM, 1, s), F32), _sds((kvh, s // tq), jnp.int32)],
        compiler_params=_params(("arbitrary", "arbitrary")),
    )(q, kv, kv)


def attn_bwd(name, q, kv, totals, visited, do):
    s, dq_cols = q.shape
    kvh = kv.shape[1] // (2 * HEAD_DIM)
    tq = _tile(s, ATTN_TILE)
    scale = HEAD_DIM ** -0.5

    def body(cnt_ref, q_ref, k_ref, v_ref, tot_ref, do_ref, dq_ref, dkv_ref):
        i = pl.program_id(1)
        first = i - jnp.clip(cnt_ref[pl.program_id(0), i], 0, i)

        @pl.when(i == 0)
        def _():
            dkv_ref[...] = jnp.zeros_like(dkv_ref)

        heads = range(KV_GROUP)
        qs = [q_ref[:, g * HEAD_DIM:(g + 1) * HEAD_DIM] for g in heads]
        dobs = [do_ref[:, g * HEAD_DIM:(g + 1) * HEAD_DIM].astype(BF16) for g in heads]
        tots = [tot_ref[g] for g in heads]
        q_all = jnp.concatenate(qs, axis=0)
        do_all = jnp.concatenate(dobs, axis=0)
        r_i = lax.broadcasted_iota(jnp.int32, (tq, tq), 0)
        c_i = lax.broadcasted_iota(jnp.int32, (tq, tq), 1)
        upto = (c_i <= r_i).astype(BF16)
        before = (c_i < r_i).astype(BF16)
        upto2 = jnp.concatenate([upto, upto], axis=1)
        before2 = jnp.concatenate([before, before], axis=1)
        mask = r_i < c_i

        def block(j, carry, masked):
            rows = pl.ds(pl.multiple_of(j * tq, tq), tq)
            kj = k_ref[rows, :]
            vj = v_ref[rows, :]
            zt = [_dot(kj, qs[g], NT) * scale for g in heads]
            dwt = [_dot(vj, dobs[g], NT) for g in heads]
            lbeta = [_log_sigmoid(zt[g]) for g in heads]
            lrest_raw = [lbeta[g] - zt[g] for g in heads]
            lrest = [jnp.where(mask, lrest_raw[g], 0.0) for g in heads] if masked else lrest_raw
            hl = [jnp.concatenate(_split_bf16(lrest[g]), axis=0) for g in heads]
            upto_sum = [_dot(upto2, hl[g]) for g in heads]
            wt = [jnp.exp(lbeta[g] + (tots[g] - carry[g][0] - upto_sum[g])) for g in heads]
            if masked:
                wt = [jnp.where(mask, wt[g], 0.0) for g in heads]
            dat = [dwt[g] * wt[g] for g in heads]
            earlier = [_dot(before2, jnp.concatenate(_split_bf16(dat[g]), axis=0)) for g in heads]
            dzt = [dat[g] * jnp.exp(lrest_raw[g]) - (carry[g][1] + earlier[g]) * jnp.exp(lbeta[g]) for g in heads]
            if masked:
                dzt = [jnp.where(mask, dzt[g], 0.0) for g in heads]
            dzb = [(dzt[g] * scale).astype(BF16) for g in heads]
            dq_new = [_dot(dzb[g], kj, TN) for g in heads]
            dkv_ref[0, rows, :] += _dot(jnp.concatenate(dzb, axis=1), q_all)
            dkv_ref[1, rows, :] += _dot(jnp.concatenate([wt[g].astype(BF16) for g in heads], axis=1), do_all)
            return tuple((carry[g][0] + jnp.sum(lrest[g], axis=0, keepdims=True),
                          carry[g][1] + jnp.sum(dat[g], axis=0, keepdims=True),
                          carry[g][2] + dq_new[g]) for g in heads)

        zrow = jnp.zeros((1, tq), F32)
        carry = ((zrow, zrow, jnp.zeros((tq, HEAD_DIM), F32)),) * KV_GROUP
        carry = lax.fori_loop(first, i, lambda j, cr: block(j, cr, False), carry)
        carry = block(i, carry, True)
        for g in heads:
            dq_ref[:, g * HEAD_DIM:(g + 1) * HEAD_DIM] = carry[g][2].astype(dq_ref.dtype)

    group = KV_GROUP * HEAD_DIM
    qspec = pl.BlockSpec((tq, group), lambda kh, i, cnt: (i, kh))
    return pl.pallas_call(
        body, name=name,
        grid_spec=pltpu.PrefetchScalarGridSpec(
            num_scalar_prefetch=1, grid=(kvh, s // tq),
            in_specs=[qspec,
                      pl.BlockSpec((s, HEAD_DIM), lambda kh, i, cnt: (0, kh)),
                      pl.BlockSpec((s, HEAD_DIM), lambda kh, i, cnt: (0, kvh + kh)),
                      pl.BlockSpec((KV_GROUP, 1, tq), lambda kh, i, cnt: (kh, 0, i)),
                      qspec],
            out_specs=[qspec, pl.BlockSpec((2, s, HEAD_DIM), lambda kh, i, cnt: (0, 0, kh))]),
        out_shape=[_sds((s, dq_cols), BF16), _sds((2, s, kvh * HEAD_DIM), F32)],
        compiler_params=_params(("parallel", "arbitrary")),
    )(visited, q, kv, kv, totals, do)


def ada_project(name, c_all, w, b):
    bsz, d = c_all.shape
    n = w.shape[1]
    tn = _tile(n, 512)

    def body(c_ref, w_ref, b_ref, o_ref):
        cv = c_ref[...]
        act = cv * _sigmoid(cv)
        o_ref[...] = _bdot(act, w_ref[...]) + b_ref[...]

    return pl.pallas_call(
        body, name=name, grid=(n // tn,),
        in_specs=[pl.BlockSpec((bsz, d), lambda i: (0, 0)),
                  pl.BlockSpec((d, tn), lambda i: (0, i)),
                  pl.BlockSpec((1, tn), lambda i: (0, i))],
        out_specs=pl.BlockSpec((bsz, tn), lambda i: (0, i)),
        out_shape=_sds((bsz, n), F32),
        compiler_params=_params(("parallel",)),
    )(c_all, w, b)


def _adamw_math(w, g, m, v):
    m = ADAM_B1 * m + (1.0 - ADAM_B1) * g
    v = ADAM_B2 * v + (1.0 - ADAM_B2) * (g * g)
    m_hat = m / (1.0 - ADAM_B1 ** ADAM_STEP)
    v_hat = v / (1.0 - ADAM_B2 ** ADAM_STEP)
    delta = -ADAM_LR * (m_hat / (jnp.sqrt(v_hat) + ADAM_EPS) + ADAM_WD * w)
    return delta, m, v


def adamw(name, w, g, m, v):
    r, c = w.shape
    tr = _tile(r, 256)
    tc = _tile(c, 2048)

    def body(w_ref, g_ref, m_ref, v_ref, d_ref, mo_ref, vo_ref):
        delta, mn, vn = _adamw_math(w_ref[...], g_ref[...], m_ref[...], v_ref[...])
        d_ref[...] = delta
        mo_ref[...] = mn
        vo_ref[...] = vn

    spec = pl.BlockSpec((tr, tc), lambda i, j: (i, j))
    return pl.pallas_call(
        body, name=name, grid=(r // tr, c // tc),
        in_specs=[spec] * 4, out_specs=[spec] * 3,
        out_shape=[_sds((r, c), F32)] * 3,
        compiler_params=_params(("parallel", "parallel")),
    )(w, g, m, v)


def ada_grad_adamw(name, c_t, dmod, w, m, v):
    layers, d, n = w.shape
    tr = _tile(d, 256)
    tc = _tile(n, 512)

    def body(a_ref, dm_ref, w_ref, m_ref, v_ref, g_ref, d_ref, mo_ref, vo_ref):
        cv = a_ref[...]
        g = _bdot(cv * _sigmoid(cv), dm_ref[...])
        delta, mn, vn = _adamw_math(w_ref[...], g, m_ref[...], v_ref[...])
        g_ref[...] = g
        d_ref[...] = delta
        mo_ref[...] = mn
        vo_ref[...] = vn

    spec = pl.BlockSpec((None, tr, tc), lambda l, i, j: (l, i, j))
    return pl.pallas_call(
        body, name=name, grid=(layers, d // tr, n // tc),
        in_specs=[pl.BlockSpec((tr, 128), lambda l, i, j: (i, 0)),
                  pl.BlockSpec((None, 128, tc), lambda l, i, j: (l, 0, j)), spec, spec, spec],
        out_specs=[spec] * 4, out_shape=[_sds((layers, d, n), F32)] * 4,
        compiler_params=_params(("parallel", "parallel", "parallel")),
    )(c_t, dmod, w, m, v)


def device_sum(name, gathered):
    _, r, c = gathered.shape

    def body(g_ref, o_ref):
        acc = g_ref[0]
        for dev in range(1, N_DEV):
            acc = acc + g_ref[dev]
        o_ref[...] = acc

    return pl.pallas_call(
        body, name=name,
        in_specs=[pl.BlockSpec(memory_space=pltpu.VMEM)],
        out_specs=pl.BlockSpec(memory_space=pltpu.VMEM),
        out_shape=_sds((r, c), F32),
    )(gathered)


def lower_bound_fwd(name, logits):
    _, d = logits.shape

    def body(l_ref, o_ref):
        l0 = l_ref[0:1, :]
        l1 = l_ref[1:2, :]
        mx = jnp.maximum(l0, l1)
        e0 = jnp.exp(l0 - mx)
        e1 = jnp.exp(l1 - mx)
        o_ref[...] = e0 / (e0 + e1)

    return pl.pallas_call(
        body, name=name,
        in_specs=[pl.BlockSpec(memory_space=pltpu.VMEM)],
        out_specs=pl.BlockSpec(memory_space=pltpu.VMEM),
        out_shape=_sds((1, d), F32),
    )(logits)


def lower_bound_bwd(name, logits, dlb):
    _, d = logits.shape

    def body(l_ref, dlb_ref, o_ref):
        l0 = l_ref[0:1, :]
        l1 = l_ref[1:2, :]
        mx = jnp.maximum(l0, l1)
        e0 = jnp.exp(l0 - mx)
        e1 = jnp.exp(l1 - mx)
        p0 = e0 / (e0 + e1)
        p1 = e1 / (e0 + e1)
        g = dlb_ref[...] * p0 * p1
        o_ref[0:1, :] = g
        o_ref[1:2, :] = -g

    return pl.pallas_call(
        body, name=name,
        in_specs=[pl.BlockSpec(memory_space=pltpu.VMEM)] * 2,
        out_specs=pl.BlockSpec(memory_space=pltpu.VMEM),
        out_shape=_sds((2, d), F32),
    )(logits, dlb)


HBM = pl.BlockSpec(memory_space=pltpu.HBM)


def _position():
    return lax.axis_index("x"), lax.axis_index("y"), lax.axis_index("c")


def _remote(src, dst, send_sem, recv_sem, device):
    return pltpu.make_async_remote_copy(src_ref=src, dst_ref=dst, send_sem=send_sem, recv_sem=recv_sem,
                                        device_id=device, device_id_type=MESH)


def allgather8(name, x):
    r, c = x.shape

    def body(x_ref, out_ref, send_sems, recv_sems):
        px, py, pc = _position()
        me = 4 * px + 2 * py + pc
        out_ref[me] = x_ref[...]
        copies = []
        for k in range(1, N_DEV):
            peer = (1 - px if k & 4 else px, 1 - py if k & 2 else py, 1 - pc if k & 1 else pc)
            cp = _remote(x_ref, out_ref.at[me], send_sems.at[k - 1], recv_sems.at[k - 1], peer)
            cp.start()
            copies.append(cp)
        for cp in copies:
            cp.wait()

    return pl.pallas_call(
        body, name=name,
        in_specs=[pl.BlockSpec(memory_space=pltpu.VMEM)],
        out_specs=pl.BlockSpec(memory_space=pltpu.VMEM),
        out_shape=_sds((N_DEV, r, c), x.dtype),
        scratch_shapes=[pltpu.SemaphoreType.DMA((N_DEV - 1,)), pltpu.SemaphoreType.DMA((N_DEV - 1,))],
    )(x)


def _other_chips(px, py):
    return [(1 - px, py), (px, 1 - py), (1 - px, 1 - py)]


def gather_weights(name, shards):
    n = len(shards)
    hook = gather_hook(shards)

    def body(*refs):
        ins, outs = refs[:n], refs[n:2 * n]
        send_sems, recv_sems = refs[2 * n:]
        hook["start"](ins, outs, send_sems, recv_sems)
        hook["finish"](ins, outs, send_sems, recv_sems)

    return pl.pallas_call(
        body, name=name,
        in_specs=[HBM] * n, out_specs=[HBM] * n,
        out_shape=hook["out_shapes"],
        scratch_shapes=[pltpu.SemaphoreType.DMA(hook["sems"]), pltpu.SemaphoreType.DMA(hook["sems"])],
    )(*shards)


def gather_hook(shards):
    n = len(shards)

    def copies(ins, outs, send_sems, recv_sems):
        px, py, pc = _position()
        chip = 2 * px + py
        sibling = (px, py, 1 - pc)
        own, sends, arrivals, passes, pass_arrivals = [], [], [], [], []
        for a in range(n):
            rows = ins[a].shape[0] // 2
            mine, theirs = pl.ds(pc * rows, rows), pl.ds((1 - pc) * rows, rows)
            own.append(_remote(ins[a], outs[a].at[chip], send_sems.at[a, 6], recv_sems.at[a, 6], sibling))
            for j, (ox, oy) in enumerate(_other_chips(px, py)):
                sends.append(_remote(ins[a].at[mine], outs[a].at[chip, mine],
                                     send_sems.at[a, j], recv_sems.at[a, j], (ox, oy, pc)))
                landed = outs[a].at[2 * ox + oy, mine]
                arrivals.append(_remote(landed, landed, send_sems.at[a, j], recv_sems.at[a, j], sibling))
                passes.append(_remote(landed, landed, send_sems.at[a, 3 + j], recv_sems.at[a, 3 + j], sibling))
                via = outs[a].at[2 * ox + oy, theirs]
                pass_arrivals.append(_remote(via, via, send_sems.at[a, 3 + j], recv_sems.at[a, 3 + j], sibling))
        return own, sends, arrivals, passes, pass_arrivals

    def start(ins, outs, send_sems, recv_sems):
        own, sends, _, _, _ = copies(ins, outs, send_sems, recv_sems)
        for cp in own + sends:
            cp.start()

    def finish(ins, outs, send_sems, recv_sems):
        own, sends, arrivals, passes, pass_arrivals = copies(ins, outs, send_sems, recv_sems)
        for arrival, onward in zip(arrivals, passes):
            arrival.wait_recv()
            onward.start()
        for arrival in pass_arrivals:
            arrival.wait_recv()
        for cp in sends + passes:
            cp.wait_send()
        for cp in own:
            cp.wait()

    return dict(operands=list(shards), out_shapes=[_sds((N_CHIPS,) + s.shape, s.dtype) for s in shards],
                sems=(n, 7), start=start, finish=finish)


def sibling_exchange(name, grads):
    n = len(grads)

    def body(*refs):
        ins, outs = refs[:n], refs[n:2 * n]
        send_sems, recv_sems = refs[2 * n:]
        px, py, pc = _position()
        sibling = (px, py, 1 - pc)
        copies = []
        for a in range(n):
            rows = ins[a].shape[1] // 2
            src = ins[a].at[:, pl.ds((1 - pc) * rows, rows), :]
            cp = _remote(src, outs[a], send_sems.at[a], recv_sems.at[a], sibling)
            cp.start()
            copies.append(cp)
        for cp in copies:
            cp.wait()

    return pl.pallas_call(
        body, name=name,
        in_specs=[HBM] * n, out_specs=[HBM] * n,
        out_shape=[_sds((g.shape[0], g.shape[1] // 2, g.shape[2]), g.dtype) for g in grads],
        scratch_shapes=[pltpu.SemaphoreType.DMA((n,)), pltpu.SemaphoreType.DMA((n,))],
    )(*grads)


def pair_add(name, g, recv, core):
    nb, rows, cols = g.shape
    half = rows // 2
    tr = _tile(half, 256)
    steps = half // tr

    def body(core_ref, g_ref, r_ref, o_ref):
        del core_ref
        o_ref[...] = (g_ref[...].astype(F32) + r_ref[...].astype(F32)).astype(o_ref.dtype)

    return pl.pallas_call(
        body, name=name,
        grid_spec=pltpu.PrefetchScalarGridSpec(
            num_scalar_prefetch=1, grid=(nb, steps),
            in_specs=[pl.BlockSpec((None, tr, cols), lambda j, i, core_ref: (j, core_ref[0] * steps + i, 0)),
                      pl.BlockSpec((None, tr, cols), lambda j, i, core_ref: (j, i, 0))],
            out_specs=pl.BlockSpec((None, tr, cols), lambda j, i, core_ref: (j, i, 0))),
        out_shape=_sds((nb, half, cols), g.dtype),
        compiler_params=_params(("parallel", "parallel")),
    )(core, g, recv)


def scatter_hook(parts):
    n = len(parts)

    def copies(ins, outs, send_sems, recv_sems):
        px, py, pc = _position()
        return [_remote(ins[a].at[2 * ox + oy], outs[a].at[j], send_sems.at[a, j], recv_sems.at[a, j], (ox, oy, pc))
                for a in range(n) for j, (ox, oy) in enumerate(_other_chips(px, py))]

    def start(ins, outs, send_sems, recv_sems):
        for cp in copies(ins, outs, send_sems, recv_sems):
            cp.start()

    def finish(ins, outs, send_sems, recv_sems):
        for cp in copies(ins, outs, send_sems, recv_sems):
            cp.wait()

    return dict(operands=list(parts), out_shapes=[_sds((N_CHIPS - 1,) + p.shape[1:], p.dtype) for p in parts],
                sems=(n, 3), start=start, finish=finish)


def chip_sum(name, part, recv, where, out_shape, lead, dest=None):
    _, half, cols = part.shape
    tr = _tile(half, 256)
    steps = half // tr

    def body(where_ref, p_ref, r_ref, *rest):
        o_ref = rest[-1]
        acc = p_ref[...].astype(F32)
        for j in range(N_CHIPS - 1):
            acc = acc + r_ref[j].astype(F32)
        o_ref[...] = acc

    if lead is None:
        ospec = pl.BlockSpec((tr, cols), lambda i, w: (w[1] * steps + i, 0))
    else:
        ospec = pl.BlockSpec((None, tr, cols), lambda i, w: (lead, w[1] * steps + i, 0))
    in_specs = [pl.BlockSpec((None, tr, cols), lambda i, w: (w[0], i, 0)),
                pl.BlockSpec((N_CHIPS - 1, tr, cols), lambda i, w: (0, i, 0))]
    operands = [where, part, recv]
    aliases = {}
    if dest is not None:
        in_specs.append(pl.BlockSpec(memory_space=pl.ANY))
        operands.append(dest)
        aliases = {3: 0}
    return pl.pallas_call(
        body, name=name,
        grid_spec=pltpu.PrefetchScalarGridSpec(num_scalar_prefetch=1, grid=(steps,),
                                               in_specs=in_specs, out_specs=ospec),
        out_shape=out_shape, input_output_aliases=aliases,
        compiler_params=_params(("parallel",)),
    )(*operands)


def sibling_share(name, slabs, places):
    n = len(slabs)
    k = len(places)

    def body(*refs):
        outs = refs[n:2 * n]
        send_sems, recv_sems = refs[2 * n:]
        px, py, pc = _position()
        sibling = (px, py, 1 - pc)
        copies = []
        for a, (oi, lead, half) in enumerate(places):
            slab = outs[oi] if lead is None else outs[oi].at[lead]
            mine = slab.at[pl.ds(pc * half, half)]
            theirs = slab.at[pl.ds((1 - pc) * half, half)]
            cp = _remote(mine, mine, send_sems.at[a], recv_sems.at[a], sibling)
            cp.start()
            copies.append((cp, _remote(theirs, theirs, send_sems.at[a], recv_sems.at[a], sibling)))
        for cp, arrival in copies:
            cp.wait_send()
            arrival.wait_recv()

    return pl.pallas_call(
        body, name=name,
        in_specs=[HBM] * n, out_specs=[HBM] * n,
        out_shape=[_sds(s.shape, s.dtype) for s in slabs],
        input_output_aliases={a: a for a in range(n)},
        scratch_shapes=[pltpu.SemaphoreType.DMA((k,)), pltpu.SemaphoreType.DMA((k,))],
    )(*slabs)


def _row(a, i):
    return a[i:i + 1]


def _relu_sq(acc):
    r = jnp.maximum(acc, 0.0)
    return r, r * r


def _residual(acc, res, gate):
    return res + gate * acc, acc


def _with_comm(res, comm):
    return res if comm else (res, [])


def _blocked(g):
    return g if g.ndim == 3 else g.reshape(N_CHIPS, g.shape[0] // N_CHIPS, g.shape[1])


def _pre_reduce(tag, named, core):
    glist = [_blocked(g) for _, g in named]
    recv = sibling_exchange("grad_exchange_" + tag, glist)
    return [pair_add("grad_pair_add_" + k, g, r, core) for (k, _), g, r in zip(named, glist, recv)]


def _mlp_fwd(tag, x, gain, mod, w1, w2_shard, down_comm=None):
    h = norm_mod_fwd(tag + "_mlp_norm", x, gain, _row(mod, 4), _row(mod, 3))
    (r3, a3), (w2,) = mm_nn_b(tag + "_mlp_up", h, w1, [BF16, BF16], _relu_sq, comm=gather_hook([w2_shard]))
    (x_out, m), got = _with_comm(mm_nn_r(tag + "_mlp_down", a3, w2, [F32, BF16], _residual,
                                         (x, _row(mod, 5)), ("tile", "row"), comm=down_comm), down_comm)
    return x_out, (h, r3, a3, m), w2, got


def _mlp_bwd(tag, dx_out, x, gain, mod, w1, w2, saved, core, down_comm=None):
    h, r3, a3, m = saved
    dm, dgate = gate_bwd(tag + "_mlp_gate_bwd", dx_out, m, _row(mod, 5))
    (dz3,), got = _with_comm(mm_nt_b(tag + "_mlp_down_dgrad", dm, w2, [BF16],
                                     lambda acc, r: (acc * (2.0 * r.astype(F32)),), (r3,), comm=down_comm),
                             down_comm)
    gw2 = mm_tn(tag + "_mlp_w2_grad", a3, dm[None], BF16)
    part2 = _pre_reduce(tag + "_w2", [(tag + "_w2", gw2)], core)
    gw1, recv2 = mm_tn(tag + "_mlp_w1_grad", h[None], dz3, BF16, comm=scatter_hook(part2))
    part1 = _pre_reduce(tag + "_w1", [(tag + "_w1", gw1)], core)
    dh, recv1 = mm_nt_r(tag + "_mlp_up_dgrad", dz3, w1, F32, comm=scatter_hook(part1))
    dx, dsh, dsc, dgain = norm_mod_bwd(tag + "_mlp_norm_bwd", dh, x, gain, _row(mod, 4), dx_out)
    return dx, (part1[0], recv1[0]), (part2[0], recv2[0]), (dsh, dsc, dgate), dgain, got


def local_step(x, target, mod0, mod1, kvmod, norm_mix, norm_mlp, kv_norm, final_norm, lb, out_gain,
               w, shards, core):
    w = dict(w)

    def flat(g):
        return g.reshape(-1, g.shape[-1])

    h1 = norm_mod_fwd("l0_mix_norm", x, _row(norm_mix, 0), _row(mod0, 1), _row(mod0, 0))
    (proj3,), (w["w1_0"],) = mm_nn_b("l0_in_proj", h1, w["a_in"], [F32], comm=gather_hook([shards["w1_0"]]))
    u, o_h, states = hgrn_fwd("l0_hgrn_fwd", proj3, lb, out_gain)
    (x1, y0), (got,) = mm_nn_b("l0_out_proj", u, w["a_out"][None], [F32, BF16], _residual,
                               (x[None], _row(mod0, 2)), ("tile", "row"), comm=gather_hook([shards["bq"]]))
    x1, y0, w["bq"] = x1[0], y0[0], flat(got)
    x2, mlp0, w["w2_0"], (w["w1_1"], got) = _mlp_fwd(
        "l0", x1, _row(norm_mlp, 0), mod0, w["w1_0"], shards["w2_0"],
        gather_hook([shards["w1_1"], shards["kv"]]))
    w["kv"] = flat(got)
    hk = norm_mod_fwd("kv_norm", x2, kv_norm, _row(kvmod, 1), _row(kvmod, 0))
    kv = mm_nn_b("kv_proj", hk, w["kv"][None], [BF16])[0][0]
    h3 = norm_mod_fwd("l1_mix_norm", x2, _row(norm_mix, 1), _row(mod1, 1), _row(mod1, 0))
    (q,), (got,) = mm_nn_b("l1_q_proj", h3, w["bq"][None], [BF16], comm=gather_hook([shards["bo"]]))
    q, w["bo"] = q[0], flat(got)
    o_a, totals, visited = attn_fwd("l1_attn_fwd", q, kv)
    x3, y1 = mm_nn_b("l1_out_proj", o_a, w["bo"][None], [F32, BF16], _residual,
                     (x2[None], _row(mod1, 2)), ("tile", "row"))
    x3, y1 = x3[0], y1[0]
    x4, mlp1, w["w2_1"], _ = _mlp_fwd("l1", x3, _row(norm_mlp, 1), mod1, w["w1_1"], shards["w2_1"])
    dx4, d_final, loss = final_loss("final_loss", x4, final_norm, target)

    reduce = {}
    dx3, reduce["w1_1"], reduce["w2_1"], dmlp1, d_nmlp1, _ = _mlp_bwd(
        "l1", dx4, x3, _row(norm_mlp, 1), mod1, w["w1_1"], w["w2_1"], mlp1, core)
    dy1, dgate1 = gate_bwd("l1_mix_gate_bwd", dx3, y1, _row(mod1, 2))
    do_a = mm_nt_b("l1_out_dgrad", dy1, w["bo"][None], [F32])[0][0]
    g_bo = mm_tn("l1_out_grad", o_a[None], dy1[None], BF16)[0]
    dq, dkv3 = attn_bwd("l1_attn_bwd", q, kv, totals, visited, do_a)
    g_bq = mm_tn("l1_q_grad", h3[None], dq[None], BF16)[0]
    dh3 = mm_nt_b("l1_q_dgrad", dq, w["bq"][None], [F32])[0][0]
    dx2, dsh, dsc, d_nmix1 = norm_mod_bwd("l1_mix_norm_bwd", dh3, x2, _row(norm_mix, 1), _row(mod1, 1), dx3)
    dmod1 = jnp.concatenate([dsh, dsc, dgate1, *dmlp1], axis=0)
    dkv = jnp.concatenate([dkv3[0], dkv3[1]], axis=1).astype(BF16)
    g_kv = mm_tn("kv_grad", hk[None], dkv[None], BF16)[0]
    dhk = mm_nt_b("kv_dgrad", dkv, w["kv"][None], [F32])[0][0]
    dx2, dsh, dsc, d_nkv = norm_mod_bwd("kv_norm_bwd", dhk, x2, kv_norm, _row(kvmod, 1), dx2)
    dkvmod = jnp.concatenate([dsh, dsc], axis=0)
    attn_parts = _pre_reduce("attn", [("bo", g_bo), ("bq", g_bq), ("kv", g_kv)], core)
    dx1, reduce["w1_0"], reduce["w2_0"], dmlp0, d_nmlp0, attn_recv = _mlp_bwd(
        "l0", dx2, x1, _row(norm_mlp, 0), mod0, w["w1_0"], w["w2_0"], mlp0, core, scatter_hook(attn_parts))
    for k, part, recv in zip(("bo", "bq", "kv"), attn_parts, attn_recv):
        reduce[k] = (part, recv)
    dy0, dgate0 = gate_bwd("l0_mix_gate_bwd", dx1, y0, _row(mod0, 2))
    du = mm_nt_b("l0_out_dgrad", dy0, w["a_out"][None], [F32])[0][0]
    g_a_out = mm_tn("l0_out_grad", u[None], dy0[None], BF16)[0]
    dproj3, d_lb, d_out_gain = hgrn_bwd("l0_hgrn_bwd", proj3, lb, out_gain, o_h, du, states)
    part = _pre_reduce("a_out", [("a_out", g_a_out)], core)
    g_a_in, recv = mm_tn("l0_in_grad", h1[None], dproj3, BF16, comm=scatter_hook(part))
    reduce["a_out"] = (part[0], recv[0])
    part = _pre_reduce("a_in", [("a_in", g_a_in)], core)
    dh1, recv = mm_nt_r("l0_in_dgrad", dproj3, w["a_in"], F32, comm=scatter_hook(part))
    reduce["a_in"] = (part[0], recv[0])
    dx0, dsh, dsc, d_nmix0 = norm_mod_bwd("l0_mix_norm_bwd", dh1, x, _row(norm_mix, 0), _row(mod0, 1), dx1)
    dmod0 = jnp.concatenate([dsh, dsc, dgate0, *dmlp0], axis=0)
    small = dict(norm_mix=(d_nmix0, d_nmix1), norm_mlp=(d_nmlp0, d_nmlp1), kv_norm=d_nkv,
                 final_norm=d_final, lb=d_lb, out_gain=d_out_gain)
    return loss, dx0, reduce, dmod0, dmod1, dkvmod, small


BIG = ("a_in", "w1_0", "w1_1", "w2_0", "w2_1", "a_out", "bq", "bo", "kv")


def kernel(x, c, ada_w, ada_b, norm_mix, norm_mlp, a_w_in, a_lb_logits, a_out_gain, a_w_out, kv_ada_w, kv_ada_b, kv_norm, w_kv, b_w_q, b_w_out, mlp_w1, mlp_w2, final_norm, loss_target, m_ada_w, m_ada_b, m_norm_mix, m_norm_mlp, m_a_w_in, m_a_lb_logits, m_a_out_gain, m_a_w_out, m_kv_ada_w, m_kv_ada_b, m_kv_norm, m_w_kv, m_b_w_q, m_b_w_out, m_mlp_w1, m_mlp_w2, m_final_norm, v_ada_w, v_ada_b, v_norm_mix, v_norm_mlp, v_a_w_in, v_a_lb_logits, v_a_out_gain, v_a_w_out, v_kv_ada_w, v_kv_ada_b, v_kv_norm, v_w_kv, v_b_w_q, v_b_w_out, v_mlp_w1, v_mlp_w2, v_final_norm):
    d = x.shape[-1]
    px, py, pc = _position()
    me = 4 * px + 2 * py + pc
    chip = 2 * px + py
    n_ada = ada_w.shape[2]
    n_kvada = kv_ada_w.shape[1]
    shard_cols = d // N_CHIPS

    def as_rows(a):
        return a.reshape(-1, shard_cols)

    pack1 = jnp.concatenate([as_rows(c), a_lb_logits, a_out_gain,
                             jnp.zeros((1, shard_cols), F32)], axis=0)
    got1 = allgather8("gather_cond", pack1)
    n_c = d // shard_cols
    c_all = got1[:, :n_c, :].reshape(N_DEV, d)
    per_chip = got1[0::2]
    logits_full = jnp.swapaxes(per_chip[:, n_c:n_c + 2, :], 0, 1).reshape(2, d)
    out_gain_full = per_chip[:, n_c + 2, :].reshape(1, d)
    lb = lower_bound_fwd("lower_bound", logits_full)

    bias0 = lax.dynamic_slice_in_dim(ada_b, chip * n_ada, n_ada, axis=1)
    bias_kv = lax.dynamic_slice_in_dim(kv_ada_b.reshape(1, -1), chip * n_kvada, n_kvada, axis=1)
    mod_part = jnp.concatenate([
        ada_project("ada_proj_0", c_all, ada_w[0], bias0[0:1]),
        ada_project("ada_proj_1", c_all, ada_w[1], bias0[1:2]),
        ada_project("ada_proj_kv", c_all, kv_ada_w, bias_kv)], axis=1)
    got2 = allgather8("gather_mod", mod_part)
    mine = lax.dynamic_index_in_dim(got2[0::2], me, axis=1, keepdims=False)
    mod0 = mine[:, :n_ada].reshape(6, d)
    mod1 = mine[:, n_ada:2 * n_ada].reshape(6, d)
    kvmod = mine[:, 2 * n_ada:].reshape(2, d)

    shards = dict(a_in=a_w_in[0], w1_0=mlp_w1[0], w1_1=mlp_w1[1], w2_0=mlp_w2[0], w2_1=mlp_w2[1],
                  a_out=a_w_out[0], bq=b_w_q[0], bo=b_w_out[0], kv=w_kv)
    shards = {k: s.astype(BF16) for k, s in shards.items()}
    g_in, g_out = gather_weights("gather_first_weights", [shards["a_in"], shards["a_out"]])
    core = pc.astype(jnp.int32).reshape(1)

    loss, dx0, reduce, dmod0, dmod1, dkvmod, small = local_step(
        x[0], loss_target[0], mod0, mod1, kvmod, norm_mix, norm_mlp, kv_norm.reshape(1, d),
        final_norm.reshape(1, d), lb, out_gain_full,
        dict(a_in=g_in, a_out=g_out.reshape(-1, g_out.shape[-1])), shards, core)

    loss_row = jnp.concatenate([loss, jnp.zeros((1, shard_cols - loss.shape[1]), F32)], axis=1)
    rows = [as_rows(dmod0), as_rows(dmod1), as_rows(dkvmod),
            as_rows(small["norm_mix"][0]), as_rows(small["norm_mix"][1]),
            as_rows(small["norm_mlp"][0]), as_rows(small["norm_mlp"][1]),
            as_rows(small["kv_norm"]), as_rows(small["final_norm"]),
            as_rows(small["lb"]), as_rows(small["out_gain"]), loss_row]
    n_rows = sum(r.shape[0] for r in rows)
    pad = (-n_rows) % 8
    pack3 = jnp.concatenate(rows + [jnp.zeros((pad, shard_cols), F32)], axis=0)
    got3 = allgather8("gather_small_grads", pack3)
    total = device_sum("sum_small_grads", got3)
    n_mod_rows = (12 * d + 2 * d) // shard_cols
    n_gain_rows = 6 * n_c
    loss_out = total[n_mod_rows + n_gain_rows + 2 * n_c, 0]

    dmod_all = got3[:, :n_mod_rows, :].reshape(N_DEV, 14 * d)
    act_t = jnp.zeros((d, 128), F32).at[:, :N_DEV].set(c_all.T)

    def dmod_cols(lo, width):
        part = lax.dynamic_slice_in_dim(dmod_all, lo + chip * width, width, axis=1)
        return jnp.zeros((128, width), F32).at[:N_DEV].set(part)

    g_ada_w, d_ada_w, nm_ada_w, nv_ada_w = ada_grad_adamw(
        "ada_update", act_t, jnp.stack([dmod_cols(0, n_ada), dmod_cols(6 * d, n_ada)]), ada_w, m_ada_w, v_ada_w)
    g_kv_ada_w, d_kv_ada_w, nm_kv_ada_w, nv_kv_ada_w = (a[0] for a in ada_grad_adamw(
        "ada_update_kv", act_t, dmod_cols(12 * d, n_kvada)[None], kv_ada_w[None], m_kv_ada_w[None],
        v_kv_ada_w[None]))

    chip_parts = [reduce[k][0] for k in BIG]
    from_chips = [reduce[k][1] for k in BIG]
    where = jnp.stack([chip, pc]).astype(jnp.int32)
    out_shapes = [_sds(a_w_in.shape, F32), _sds(mlp_w1.shape, F32), _sds(mlp_w2.shape, F32),
                  _sds(a_w_out.shape, F32), _sds(b_w_q.shape, F32), _sds(b_w_out.shape, F32),
                  _sds(w_kv.shape, F32)]
    targets = [(0, 0), (1, 0), (1, 1), (2, 0), (2, 1), (3, 0), (4, 0), (5, 0), (6, None)]
    slabs = [None] * len(out_shapes)
    places = []
    for k, part, recv, (oi, lead) in zip(BIG, chip_parts, from_chips, targets):
        slabs[oi] = chip_sum("grad_chip_sum_" + k, part, recv, where, out_shapes[oi], lead, slabs[oi])
        places.append((oi, lead, part.shape[1]))
    g_a_w_in, g_mlp_w1, g_mlp_w2, g_a_w_out, g_b_w_q, g_b_w_out, g_w_kv = sibling_share(
        "grad_sibling_share", slabs, places)

    def update(name, wgt, g, m_, v_):
        shape = wgt.shape
        f = lambda a: a.reshape(-1, shape[-1])
        return tuple(o.reshape(shape) for o in adamw(name, f(wgt), f(g), f(m_), f(v_)))

    u_a_w_in = update("adamw_a_w_in", a_w_in, g_a_w_in, m_a_w_in, v_a_w_in)
    u_mlp_w1 = update("adamw_mlp_w1", mlp_w1, g_mlp_w1, m_mlp_w1, v_mlp_w1)
    u_mlp_w2 = update("adamw_mlp_w2", mlp_w2, g_mlp_w2, m_mlp_w2, v_mlp_w2)
    u_a_w_out = update("adamw_a_w_out", a_w_out, g_a_w_out, m_a_w_out, v_a_w_out)
    u_b_w_q = update("adamw_b_w_q", b_w_q, g_b_w_q, m_b_w_q, v_b_w_q)
    u_b_w_out = update("adamw_b_w_out", b_w_out, g_b_w_out, m_b_w_out, v_b_w_out)
    u_w_kv = update("adamw_w_kv", w_kv, g_w_kv, m_w_kv, v_w_kv)

    base = n_mod_rows + n_gain_rows
    d_lb_mine = lax.dynamic_slice_in_dim(total, base + chip, 1, axis=0)
    d_gain_mine = lax.dynamic_slice_in_dim(total, base + n_c + chip, 1, axis=0)
    d_logits = lower_bound_bwd("lower_bound_bwd", a_lb_logits, d_lb_mine)

    def pack_small(ada_b_, kv_ada_b_, norm_mix_, norm_mlp_, kv_norm_, final_norm_, lbl_, gain_):
        parts = [as_rows(ada_b_), as_rows(kv_ada_b_), as_rows(norm_mix_), as_rows(norm_mlp_),
                 as_rows(kv_norm_), as_rows(final_norm_), lbl_, gain_]
        n = sum(p.shape[0] for p in parts)
        return jnp.concatenate(parts + [jnp.zeros(((-n) % 8, shard_cols), F32)], axis=0)

    w_small = pack_small(ada_b, kv_ada_b, norm_mix, norm_mlp, kv_norm, final_norm, a_lb_logits, a_out_gain)
    m_small = pack_small(m_ada_b, m_kv_ada_b, m_norm_mix, m_norm_mlp, m_kv_norm, m_final_norm,
                         m_a_lb_logits, m_a_out_gain)
    v_small = pack_small(v_ada_b, v_kv_ada_b, v_norm_mix, v_norm_mlp, v_kv_norm, v_final_norm,
                         v_a_lb_logits, v_a_out_gain)
    n_small = w_small.shape[0]
    g_small = jnp.concatenate([total[:base], d_logits, d_gain_mine,
                               jnp.zeros((n_small - base - 3, shard_cols), F32)], axis=0)
    small_out = (g_small,) + tuple(adamw("adamw_small", w_small, g_small, m_small, v_small))

    def unpack_small(p):
        out, r0 = [], 0
        for ref in (ada_b, kv_ada_b, norm_mix, norm_mlp, kv_norm, final_norm, a_lb_logits, a_out_gain):
            nr = ref.size // shard_cols
            out.append(p[r0:r0 + nr].reshape(ref.shape))
            r0 += nr
        return out

    sm = [unpack_small(p) for p in small_out]

    def leaves(kind, big_ada, big_kv_ada):
        ada_b_, kv_ada_b_, norm_mix_, norm_mlp_, kv_norm_, final_norm_, lbl_, gain_ = sm[kind]
        pick = (lambda u, g: g) if kind == 0 else (lambda u, g: u[kind - 1])
        return [big_ada, ada_b_, norm_mix_, norm_mlp_, pick(u_a_w_in, g_a_w_in), lbl_, gain_,
                pick(u_a_w_out, g_a_w_out), big_kv_ada, kv_ada_b_, kv_norm_, pick(u_w_kv, g_w_kv),
                pick(u_b_w_q, g_b_w_q), pick(u_b_w_out, g_b_w_out), pick(u_mlp_w1, g_mlp_w1),
                pick(u_mlp_w2, g_mlp_w2), final_norm_]

    return (loss_out, dx0[None],
            *leaves(0, g_ada_w, g_kv_ada_w), *leaves(1, d_ada_w, d_kv_ada_w),
            *leaves(2, nm_ada_w, nm_kv_ada_w), *leaves(3, nv_ada_w, nv_kv_ada_w))
```

```python
import functools

import jax
import jax.numpy as jnp
from jax import lax
from jax.experimental import pallas as pl
from jax.experimental.pallas import tpu as pltpu

F32 = jnp.float32
BF16 = jnp.bfloat16
MESH = pl.DeviceIdType.MESH

HEAD_DIM = 128
KV_GROUP = 4
HGRN_CHUNK = 64
HGRN_HEADS = 4
HGRN_ROWS = 256
NORM_EPS = 1e-6
N_CHIPS = 4
N_DEV = 8
ROW_TILE = 256
ATTN_TILE = 256
VMEM_LIMIT = 56 * 1024 * 1024
DEAD_LOG_WEIGHT = -110.0

ADAM_LR = 0.001
ADAM_B1 = 0.9
ADAM_B2 = 0.999
ADAM_EPS = 1e-08
ADAM_WD = 0.01
ADAM_STEP = 10

NN = (((1,), (0,)), ((), ()))
NT = (((1,), (1,)), ((), ()))
TN = (((0,), (0,)), ((), ()))


def _dot(a, b, dims=NN, precision=None):
    return lax.dot_general(a, b, dims, preferred_element_type=F32, precision=precision)


def _bdot(a, b, dims=NN):
    return _dot(a.astype(BF16), b.astype(BF16), dims)


def _sigmoid(x):
    return 1.0 / (1.0 + jnp.exp(-x))


def _log_sigmoid(z):
    return jnp.minimum(z, 0.0) - jnp.log(1.0 + jnp.exp(-jnp.abs(z)))


def _split_bf16(x):
    hi = x.astype(BF16)
    lo = (x - hi.astype(F32)).astype(BF16)
    return hi, lo


def _params(sem=None):
    return pltpu.CompilerParams(dimension_semantics=sem, vmem_limit_bytes=VMEM_LIMIT)


def _tile(n, pref):
    t = min(n, pref)
    assert n % t == 0, (n, pref)
    return t


def _mm(name, a, b, a_spec, b_spec, grid, n_red, dims, out_shapes, out_specs,
        acc_shape, epilogue=None, extras=(), extra_specs=(), comm=None):
    n_extra = len(extras)
    n_out = len(out_shapes)
    n_cin = len(comm["operands"]) if comm else 0
    n_cout = len(comm["out_shapes"]) if comm else 0
    if epilogue is None:
        epilogue = lambda acc: (acc,)

    def body(*refs):
        a_ref, b_ref = refs[:2]
        ex_refs = refs[2:2 + n_extra]
        cin_refs = refs[2 + n_extra:2 + n_extra + n_cin]
        out_refs = refs[2 + n_extra + n_cin:2 + n_extra + n_cin + n_out]
        cout_refs = refs[2 + n_extra + n_cin + n_out:2 + n_extra + n_cin + n_out + n_cout]
        scratch = refs[2 + n_extra + n_cin + n_out + n_cout:]
        pids = [pl.program_id(ax) for ax in range(len(grid))]
        if comm:
            send_sems, recv_sems = scratch[-2:]

            @pl.when(functools.reduce(jnp.logical_and, [p == 0 for p in pids]))
            def _():
                comm["start"](cin_refs, cout_refs, send_sems, recv_sems)

        prod = _bdot(a_ref[...], b_ref[...], dims)

        def finish(acc):
            res = epilogue(acc, *[e[...] for e in ex_refs])
            for o_ref, r in zip(out_refs, res):
                o_ref[...] = r.astype(o_ref.dtype)

        if n_red == 0:
            finish(prod)
        else:
            acc_ref = scratch[0]
            ids = pids[len(grid) - n_red:]
            sizes = grid[len(grid) - n_red:]
            first = functools.reduce(jnp.logical_and, [i == 0 for i in ids])
            last = functools.reduce(jnp.logical_and, [i == s - 1 for i, s in zip(ids, sizes)])

            @pl.when(first)
            def _():
                acc_ref[...] = prod

            @pl.when(jnp.logical_not(first))
            def _():
                acc_ref[...] += prod

            @pl.when(last)
            def _():
                finish(acc_ref[...])

        if comm:
            @pl.when(functools.reduce(jnp.logical_and, [p == s - 1 for p, s in zip(pids, grid)]))
            def _():
                comm["finish"](cin_refs, cout_refs, send_sems, recv_sems)

    if comm:
        sem = ("arbitrary",) * len(grid)
    else:
        sem = ("parallel",) * (len(grid) - n_red) + ("arbitrary",) * n_red
    scratch_shapes = [pltpu.VMEM(acc_shape, F32)] if n_red else []
    if comm:
        scratch_shapes += [pltpu.SemaphoreType.DMA(comm["sems"]), pltpu.SemaphoreType.DMA(comm["sems"])]
    out = pl.pallas_call(
        body, name=name, grid=grid,
        in_specs=[a_spec, b_spec, *extra_specs] + [HBM] * n_cin,
        out_specs=list(out_specs) + [HBM] * n_cout,
        out_shape=list(out_shapes) + (list(comm["out_shapes"]) if comm else []),
        scratch_shapes=scratch_shapes,
        compiler_params=_params(sem),
    )(a, b, *extras, *(comm["operands"] if comm else ()))
    return (out[:n_out], out[n_out:]) if comm else out


def _sds(shape, dtype):
    return jax.ShapeDtypeStruct(shape, dtype)


def mm_nn_b(name, a, w3, out_dtypes, epilogue=None, extras=(), extra_kinds=(), comm=None):
    m, k = a.shape
    nb, _, n = w3.shape
    tm, tn = _tile(m, 1024), _tile(n, 512)
    grid = (nb, m // tm, n // tn)
    nt = n // tn
    especs = []
    for kind in extra_kinds:
        if kind == "tile":
            especs.append(pl.BlockSpec((None, tm, tn), lambda j, i, c: (j, i, c)))
        else:
            especs.append(pl.BlockSpec((1, tn), lambda j, i, c: (0, j * nt + c)))
    return _mm(name, a, w3,
               pl.BlockSpec((tm, k), lambda j, i, c: (i, 0)),
               pl.BlockSpec((None, k, tn), lambda j, i, c: (j, 0, c)),
               grid, 0, NN,
               [_sds((nb, m, n), d) for d in out_dtypes],
               [pl.BlockSpec((None, tm, tn), lambda j, i, c: (j, i, c)) for _ in out_dtypes],
               None, epilogue, extras, especs, comm)


def mm_nn_r(name, a3, w3, out_dtypes, epilogue=None, extras=(), extra_kinds=(), comm=None):
    nb, m, kb = a3.shape
    n = w3.shape[2]
    tm, tn, tk = _tile(m, 1024), _tile(n, 512), _tile(kb, 2048)
    grid = (m // tm, n // tn, nb, kb // tk)
    especs = []
    for kind in extra_kinds:
        if kind == "tile":
            especs.append(pl.BlockSpec((tm, tn), lambda i, c, j, r: (i, c)))
        else:
            especs.append(pl.BlockSpec((1, tn), lambda i, c, j, r: (0, c)))
    return _mm(name, a3, w3,
               pl.BlockSpec((None, tm, tk), lambda i, c, j, r: (j, i, r)),
               pl.BlockSpec((None, tk, tn), lambda i, c, j, r: (j, r, c)),
               grid, 2, NN,
               [_sds((m, n), d) for d in out_dtypes],
               [pl.BlockSpec((tm, tn), lambda i, c, j, r: (i, c)) for _ in out_dtypes],
               (tm, tn), epilogue, extras, especs, comm)


def mm_nt_b(name, a, w3, out_dtypes, epilogue=None, extras=(), comm=None):
    m, n = a.shape
    nb, kb, _ = w3.shape
    tm, tk = _tile(m, 1024), _tile(kb, 512)
    grid = (nb, m // tm, kb // tk)
    especs = [pl.BlockSpec((None, tm, tk), lambda j, i, c: (j, i, c)) for _ in extras]
    return _mm(name, a, w3,
               pl.BlockSpec((tm, n), lambda j, i, c: (i, 0)),
               pl.BlockSpec((None, tk, n), lambda j, i, c: (j, c, 0)),
               grid, 0, NT,
               [_sds((nb, m, kb), d) for d in out_dtypes],
               [pl.BlockSpec((None, tm, tk), lambda j, i, c: (j, i, c)) for _ in out_dtypes],
               None, epilogue, extras, especs, comm)


def _single(res, comm):
    return res[0] if comm is None else (res[0][0], res[1])


def mm_nt_r(name, a3, w3, out_dtype, comm=None):
    nb, m, n = a3.shape
    k = w3.shape[1]
    tm, tk, tc = _tile(m, 1024), _tile(k, 1024), _tile(n, 2048)
    grid = (m // tm, k // tk, nb, n // tc)
    return _single(_mm(name, a3, w3,
                       pl.BlockSpec((None, tm, tc), lambda i, c, j, r: (j, i, r)),
                       pl.BlockSpec((None, tk, tc), lambda i, c, j, r: (j, c, r)),
                       grid, 2, NT,
                       [_sds((m, k), out_dtype)],
                       [pl.BlockSpec((tm, tk), lambda i, c, j, r: (i, c))],
                       (tm, tk), comm=comm), comm)


def mm_tn(name, a3, d3, out_dtype, comm=None):
    na, m, kb = a3.shape
    nd, _, n = d3.shape
    nb = max(na, nd)
    tk, tn, tm = _tile(kb, 512), _tile(n, 2048), _tile(m, 1024)
    grid = (nb, kb // tk, n // tn, m // tm)
    ja = (lambda j: j) if na > 1 else (lambda j: 0)
    jd = (lambda j: j) if nd > 1 else (lambda j: 0)
    return _single(_mm(name, a3, d3,
                       pl.BlockSpec((None, tm, tk), lambda j, c, e, r: (ja(j), r, c)),
                       pl.BlockSpec((None, tm, tn), lambda j, c, e, r: (jd(j), r, e)),
                       grid, 1, TN,
                       [_sds((nb, kb, n), out_dtype)],
                       [pl.BlockSpec((None, tk, tn), lambda j, c, e, r: (j, c, e))],
                       (tk, tn), comm=comm), comm)


def _row_spec(ts, d):
    return pl.BlockSpec((ts, d), lambda i: (i, 0))


def _vec_spec(d):
    return pl.BlockSpec((1, d), lambda i: (0, 0))


def norm_mod_fwd(name, x, gain, scale, shift):
    s, d = x.shape
    ts = _tile(s, ROW_TILE)

    def body(x_ref, g_ref, sc_ref, sh_ref, h_ref):
        xv = x_ref[...]
        inv = lax.rsqrt(jnp.mean(xv * xv, axis=-1, keepdims=True) + NORM_EPS)
        h = (xv * inv) * g_ref[...] * (1.0 + sc_ref[...]) + sh_ref[...]
        h_ref[...] = h.astype(h_ref.dtype)

    return pl.pallas_call(
        body, name=name, grid=(s // ts,),
        in_specs=[_row_spec(ts, d), _vec_spec(d), _vec_spec(d), _vec_spec(d)],
        out_specs=_row_spec(ts, d), out_shape=_sds((s, d), BF16),
        compiler_params=_params(("parallel",)),
    )(x, gain, scale, shift)


def _gate_bwd(dxv, y_ref, gate_ref, dy_ref, dgate_ref):
    dy_ref[...] = (dxv * gate_ref[...]).astype(dy_ref.dtype)
    dgate_ref[...] += jnp.sum(dxv * y_ref[...].astype(F32), axis=0, keepdims=True)


def norm_mod_bwd(name, dh, x, gain, scale, dres, branch=None):
    s, d = x.shape
    ts = _tile(s, ROW_TILE)

    def body(dh_ref, x_ref, g_ref, sc_ref, dres_ref, *rest):
        if branch:
            y_ref, gate_ref, dx_ref, dsh_ref, dsc_ref, dg_ref, dy_ref, dgate_ref = rest
        else:
            dx_ref, dsh_ref, dsc_ref, dg_ref = rest

        @pl.when(pl.program_id(0) == 0)
        def _():
            dsh_ref[...] = jnp.zeros_like(dsh_ref)
            dsc_ref[...] = jnp.zeros_like(dsc_ref)
            dg_ref[...] = jnp.zeros_like(dg_ref)
            if branch:
                dgate_ref[...] = jnp.zeros_like(dgate_ref)

        xv = x_ref[...]
        dhv = dh_ref[...].astype(F32)
        g = g_ref[...]
        inv = lax.rsqrt(jnp.mean(xv * xv, axis=-1, keepdims=True) + NORM_EPS)
        n = xv * inv
        dhn = dhv * (1.0 + sc_ref[...])
        dn = dhn * g
        dx = dres_ref[...] + inv * (dn - n * jnp.mean(dn * n, axis=-1, keepdims=True))
        dx_ref[...] = dx
        dsh_ref[...] += jnp.sum(dhv, axis=0, keepdims=True)
        dsc_ref[...] += jnp.sum(dhv * (n * g), axis=0, keepdims=True)
        dg_ref[...] += jnp.sum(dhn * n, axis=0, keepdims=True)
        if branch:
            _gate_bwd(dx, y_ref, gate_ref, dy_ref, dgate_ref)

    row, vec = _row_spec(ts, d), _vec_spec(d)
    return pl.pallas_call(
        body, name=name, grid=(s // ts,),
        in_specs=[row, row, vec, vec, row] + ([row, vec] if branch else []),
        out_specs=[row, vec, vec, vec] + ([row, vec] if branch else []),
        out_shape=[_sds((s, d), F32), _sds((1, d), F32), _sds((1, d), F32), _sds((1, d), F32)]
        + ([_sds((s, d), BF16), _sds((1, d), F32)] if branch else []),
        compiler_params=_params(("arbitrary",)),
    )(dh, x, gain, scale, dres, *(branch or ()))


def final_loss(name, x, gain, target, y, gate):
    s, d = x.shape
    ts = _tile(s, ROW_TILE)

    def body(x_ref, g_ref, t_ref, y_ref, gate_ref, dx_ref, dg_ref, loss_ref, dy_ref, dgate_ref):
        @pl.when(pl.program_id(0) == 0)
        def _():
            dg_ref[...] = jnp.zeros_like(dg_ref)
            loss_ref[...] = jnp.zeros_like(loss_ref)
            dgate_ref[...] = jnp.zeros_like(dgate_ref)

        xv = x_ref[...]
        g = g_ref[...]
        inv = lax.rsqrt(jnp.mean(xv * xv, axis=-1, keepdims=True) + NORM_EPS)
        n = xv * inv
        diff = n * g - t_ref[...]
        per_tok = jnp.mean(diff * diff, axis=-1, keepdims=True)
        loss_ref[...] += 0.5 * jnp.sum(per_tok, axis=0, keepdims=True)
        dout = diff * (1.0 / d)
        dg_ref[...] += jnp.sum(dout * n, axis=0, keepdims=True)
        dn = dout * g
        dx = inv * (dn - n * jnp.mean(dn * n, axis=-1, keepdims=True))
        dx_ref[...] = dx
        _gate_bwd(dx, y_ref, gate_ref, dy_ref, dgate_ref)

    row, vec = _row_spec(ts, d), _vec_spec(d)
    return pl.pallas_call(
        body, name=name, grid=(s // ts,),
        in_specs=[row, vec, row, row, vec],
        out_specs=[row, vec, _vec_spec(128), row, vec],
        out_shape=[_sds((s, d), F32), _sds((1, d), F32), _sds((1, 128), F32), _sds((s, d), BF16), _sds((1, d), F32)],
        compiler_params=_params(("arbitrary",)),
    )(x, gain, target, y, gate)


def _hgrn_chunk_fwd(qr, fl, lbv, tri):
    sg = _sigmoid(fl)
    sgm = _sigmoid(-fl)
    f = lbv + (1.0 - lbv) * sg
    logf = jnp.log(f)
    k = (1.0 - lbv) * sgm
    cum = _dot(tri, logf, precision=lax.Precision.HIGHEST)
    cl = cum[HGRN_CHUNK - 1:HGRN_CHUNK, :]
    e = jnp.exp(cum)
    en = jnp.exp(-cum)
    es = jnp.exp(cl - cum)
    sq = _sigmoid(qr)
    qs = qr * sq
    return dict(sg=sg, sgm=sgm, f=f, k=k, cum=cum, cl=cl, e=e, en=en, es=es, sq=sq, qs=qs,
                qd=qs * e, ki=k * en, ks=k * es, dec=jnp.exp(cl))


def _tri_masks(strict=False):
    r = lax.broadcasted_iota(jnp.int32, (HGRN_CHUNK, HGRN_CHUNK), 0)
    c = lax.broadcasted_iota(jnp.int32, (HGRN_CHUNK, HGRN_CHUNK), 1)
    return (r > c) if strict else (r >= c)


def _carried(comm, grid, n_in, n_out):
    if not comm:
        return [], [], [], [], lambda refs: (refs, lambda: None, lambda: None)
    n_cin, n_cout = len(comm["operands"]), len(comm["out_shapes"])
    sems = [pltpu.SemaphoreType.DMA(comm["sems"]), pltpu.SemaphoreType.DMA(comm["sems"])]

    def split(refs):
        ins, cin = refs[:n_in], refs[n_in:n_in + n_cin]
        outs = refs[n_in + n_cin:n_in + n_cin + n_out]
        cout = refs[n_in + n_cin + n_out:n_in + n_cin + n_out + n_cout]
        scratch = refs[n_in + n_cin + n_out + n_cout:]
        send_sems, recv_sems = scratch[-2:]
        pids = [pl.program_id(ax) for ax in range(len(grid))]

        def start():
            @pl.when(functools.reduce(jnp.logical_and, [p == 0 for p in pids]))
            def _():
                comm["start"](cin, cout, send_sems, recv_sems)

        def finish():
            @pl.when(functools.reduce(jnp.logical_and, [p == n - 1 for p, n in zip(pids, grid)]))
            def _():
                comm["finish"](cin, cout, send_sems, recv_sems)

        return ins + outs + scratch[:-2], start, finish

    return [HBM] * n_cin, [HBM] * n_cout, list(comm["out_shapes"]), sems, split


def hgrn_fwd(name, proj3, lb, out_gain, comm=None):
    _, s, d = proj3.shape
    heads = d // HEAD_DIM
    t_rows = _tile(s, HGRN_ROWS)
    n_t = s // t_rows
    n_c = t_rows // HGRN_CHUNK
    width = HGRN_HEADS * HEAD_DIM
    grid = (heads // HGRN_HEADS, n_t)
    c_in, c_out, c_shapes, c_sems, split = _carried(comm, grid, 6, 3)

    def body(*refs):
        (q_ref, f_ref, i_ref, g_ref, lb_ref, gain_ref, u_ref, o_ref, st_ref, state), start, finish = split(refs)
        start()
        compute(q_ref, f_ref, i_ref, g_ref, lb_ref, gain_ref, u_ref, o_ref, st_ref, state)
        finish()

    def compute(q_ref, f_ref, i_ref, g_ref, lb_ref, gain_ref, u_ref, o_ref, st_ref, state):
        @pl.when(pl.program_id(1) == 0)
        def _():
            state[...] = jnp.zeros_like(state)

        causal = _tri_masks()
        tri = causal.astype(F32)
        hs = range(HGRN_HEADS)
        col = [pl.ds(hh * HEAD_DIM, HEAD_DIM) for hh in hs]
        for ci in range(n_c):
            rows = pl.ds(ci * HGRN_CHUNK, HGRN_CHUNK)
            c = [_hgrn_chunk_fwd(q_ref[rows, col[hh]], f_ref[rows, col[hh]], lb_ref[:, col[hh]], tri) for hh in hs]
            v = [i_ref[rows, col[hh]] for hh in hs]
            st = [state[hh] for hh in hs]
            scores = [jnp.where(causal, _bdot(c[hh]["qd"], c[hh]["ki"], NT), 0.0) for hh in hs]
            inter = [_bdot(c[hh]["qd"], st[hh], NT) for hh in hs]
            update = [_bdot(v[hh], c[hh]["ks"], TN) for hh in hs]
            o = [_bdot(scores[hh], v[hh]) + inter[hh] for hh in hs]
            for hh in hs:
                st_ref[hh, ci] = st[hh]
                state[hh] = st[hh] * c[hh]["dec"] + update[hh]
            for hh in hs:
                graw = g_ref[rows, col[hh]]
                rms = lax.rsqrt(jnp.mean(o[hh] * o[hh], axis=-1, keepdims=True) + NORM_EPS)
                u = o[hh] * rms * gain_ref[:, col[hh]] * (graw * _sigmoid(graw))
                o_ref[rows, col[hh]] = o[hh]
                u_ref[rows, col[hh]] = u.astype(u_ref.dtype)

    def pspec(blk):
        return pl.BlockSpec((None, t_rows, width), lambda h, t: (blk, t, h))

    hspec = pl.BlockSpec((1, width), lambda h, t: (0, h))
    ospec = pl.BlockSpec((t_rows, width), lambda h, t: (t, h))
    out = pl.pallas_call(
        body, name=name, grid=grid,
        in_specs=[pspec(0), pspec(1), pspec(2), pspec(3), hspec, hspec] + c_in,
        out_specs=[ospec, ospec,
                   pl.BlockSpec((HGRN_HEADS, n_c, HEAD_DIM, HEAD_DIM), lambda h, t: (h, t, 0, 0))] + c_out,
        out_shape=[_sds((s, d), BF16), _sds((s, d), F32),
                   _sds((heads, s // HGRN_CHUNK, HEAD_DIM, HEAD_DIM), F32)] + c_shapes,
        scratch_shapes=[pltpu.VMEM((HGRN_HEADS, HEAD_DIM, HEAD_DIM), F32)] + c_sems,
        compiler_params=_params(("arbitrary", "arbitrary")),
    )(proj3, proj3, proj3, proj3, lb, out_gain, *(comm["operands"] if comm else ()))
    return (out[:3], out[3:]) if comm else out


def hgrn_bwd(name, proj3, lb, out_gain, o, du, states):
    _, s, d = proj3.shape
    heads = d // HEAD_DIM
    t_rows = _tile(s, HGRN_ROWS)
    n_t = s // t_rows
    n_c = t_rows // HGRN_CHUNK
    width = HGRN_HEADS * HEAD_DIM

    def body(q_ref, f_ref, i_ref, g_ref, lb_ref, gain_ref, o_ref, du_ref, st_ref,
             dp_ref, dlb_ref, dgain_ref, dstate):
        @pl.when(pl.program_id(1) == 0)
        def _():
            dstate[...] = jnp.zeros_like(dstate)
            dlb_ref[...] = jnp.zeros_like(dlb_ref)
            dgain_ref[...] = jnp.zeros_like(dgain_ref)

        causal = _tri_masks()
        tri = causal.astype(F32)
        tri_t = jnp.logical_not(_tri_masks(strict=True)).astype(F32)
        hs = range(HGRN_HEADS)
        col = [pl.ds(hh * HEAD_DIM, HEAD_DIM) for hh in hs]
        lbv = [lb_ref[:, col[hh]] for hh in hs]
        gain = [gain_ref[:, col[hh]] for hh in hs]
        for ci in reversed(range(n_c)):
            rows = pl.ds(ci * HGRN_CHUNK, HGRN_CHUNK)
            qr = [q_ref[rows, col[hh]] for hh in hs]
            c = [_hgrn_chunk_fwd(qr[hh], f_ref[rows, col[hh]], lbv[hh], tri) for hh in hs]
            v = [i_ref[rows, col[hh]] for hh in hs]
            st = [st_ref[hh, ci] for hh in hs]
            dst = [dstate[hh] for hh in hs]
            do, dgraw = [], []
            for hh in hs:
                ov = o_ref[rows, col[hh]]
                duv = du_ref[rows, col[hh]].astype(F32)
                graw = g_ref[rows, col[hh]]
                sgg = _sigmoid(graw)
                gate = graw * sgg
                rms = lax.rsqrt(jnp.mean(ov * ov, axis=-1, keepdims=True) + NORM_EPS)
                on = ov * rms
                dgain_ref[:, col[hh]] += jnp.sum(duv * on * gate, axis=0, keepdims=True)
                dgraw.append(duv * on * gain[hh] * (sgg * (1.0 + graw * (1.0 - sgg))))
                don = duv * gain[hh] * gate
                do.append(rms * (don - on * jnp.mean(don * on, axis=-1, keepdims=True)))
            qd = [c[hh]["qd"] for hh in hs]
            ki = [c[hh]["ki"] for hh in hs]
            ks = [c[hh]["ks"] for hh in hs]
            p = [jnp.where(causal, _bdot(qd[hh], ki[hh], NT), 0.0) for hh in hs]
            dp = [jnp.where(causal, _bdot(do[hh], v[hh], NT), 0.0) for hh in hs]
            from_state = [_bdot(do[hh], st[hh]) for hh in hs]
            dks = [_bdot(v[hh], dst[hh]) for hh in hs]
            dv_state = [_bdot(ks[hh], dst[hh], NT) for hh in hs]
            dstate_new = [_bdot(do[hh], qd[hh], TN) for hh in hs]
            dqd = [_bdot(dp[hh], ki[hh]) + from_state[hh] for hh in hs]
            dki = [_bdot(dp[hh], qd[hh], TN) for hh in hs]
            dv = [_bdot(p[hh], do[hh], TN) + dv_state[hh] for hh in hs]
            ddec = [jnp.sum(dst[hh] * st[hh], axis=0, keepdims=True) for hh in hs]
            for hh in hs:
                dstate[hh] = dst[hh] * c[hh]["dec"] + dstate_new[hh]
            dcum = [dqd[hh] * qd[hh] - dki[hh] * ki[hh] - dks[hh] * ks[hh] for hh in hs]
            dcl = [jnp.sum(dks[hh] * ks[hh], axis=0, keepdims=True) + ddec[hh] * c[hh]["dec"] for hh in hs]
            dlogf = [_dot(tri_t, dcum[hh], precision=lax.Precision.HIGHEST) + dcl[hh] for hh in hs]
            for hh in hs:
                ch = c[hh]
                dqs = dqd[hh] * ch["e"]
                dk = dki[hh] * ch["en"] + dks[hh] * ch["es"]
                df = dlogf[hh] / ch["f"]
                sg, sgm, sq = ch["sg"], ch["sgm"], ch["sq"]
                one_m_lb = 1.0 - lbv[hh]
                dlb_ref[:, col[hh]] += jnp.sum(df * (1.0 - sg) - dk * sgm, axis=0, keepdims=True)
                dfl = df * one_m_lb * sg * (1.0 - sg) - dk * one_m_lb * sgm * (1.0 - sgm)
                dqr = dqs * (sq * (1.0 + qr[hh] * (1.0 - sq)))
                dp_ref[0, rows, col[hh]] = dqr.astype(dp_ref.dtype)
                dp_ref[1, rows, col[hh]] = dfl.astype(dp_ref.dtype)
                dp_ref[2, rows, col[hh]] = dv[hh].astype(dp_ref.dtype)
                dp_ref[3, rows, col[hh]] = dgraw[hh].astype(dp_ref.dtype)

    def pspec(blk):
        return pl.BlockSpec((None, t_rows, width), lambda h, t: (blk, n_t - 1 - t, h))

    hspec = pl.BlockSpec((1, width), lambda h, t: (0, h))
    ospec = pl.BlockSpec((t_rows, width), lambda h, t: (n_t - 1 - t, h))
    return pl.pallas_call(
        body, name=name, grid=(heads // HGRN_HEADS, n_t),
        in_specs=[pspec(0), pspec(1), pspec(2), pspec(3), hspec, hspec, ospec, ospec,
                  pl.BlockSpec((HGRN_HEADS, n_c, HEAD_DIM, HEAD_DIM), lambda h, t: (h, n_t - 1 - t, 0, 0))],
        out_specs=[pl.BlockSpec((4, t_rows, width), lambda h, t: (0, n_t - 1 - t, h)), hspec, hspec],
        out_shape=[_sds((4, s, d), BF16), _sds((1, d), F32), _sds((1, d), F32)],
        scratch_shapes=[pltpu.VMEM((HGRN_HEADS, HEAD_DIM, HEAD_DIM), F32)],
        compiler_params=_params(("parallel", "arbitrary")),
    )(proj3, proj3, proj3, proj3, lb, out_gain, o, du, states)


def attn_fwd(name, q, kv, comm=None):
    s, dq = q.shape
    kvh = kv.shape[1] // (2 * HEAD_DIM)
    assert dq == kvh * KV_GROUP * HEAD_DIM
    tq = _tile(s, ATTN_TILE)
    scale = HEAD_DIM ** -0.5

    grid = (kvh, s // tq)
    c_in, c_out, c_shapes, c_sems, split = _carried(comm, grid, 3, 3)

    def body(*refs):
        (q_ref, k_ref, v_ref, o_ref, tot_ref, cnt_ref), start, finish = split(refs)
        start()
        compute(q_ref, k_ref, v_ref, o_ref, tot_ref, cnt_ref)
        finish()

    def compute(q_ref, k_ref, v_ref, o_ref, tot_ref, cnt_ref):
        i = pl.program_id(1)
        heads = range(KV_GROUP)
        qs = [q_ref[:, g * HEAD_DIM:(g + 1) * HEAD_DIM] for g in heads]
        r_i = lax.broadcasted_iota(jnp.int32, (tq, tq), 0)
        c_i = lax.broadcasted_iota(jnp.int32, (tq, tq), 1)
        later = (r_i > c_i).astype(BF16)
        later2 = jnp.concatenate([later, later], axis=0)
        mask = c_i < r_i
        ones2 = jnp.ones((8, 2 * tq), BF16)

        def block(j, carry, masked):
            rows = pl.ds(pl.multiple_of(j * tq, tq), tq)
            kj = k_ref[rows, :]
            vj = v_ref[rows, :]
            z = [_dot(qs[g], kj, NT) * scale for g in heads]
            lbeta = [_log_sigmoid(z[g]) for g in heads]
            lrest = [lbeta[g] - z[g] for g in heads]
            if masked:
                lrest = [jnp.where(mask, lrest[g], 0.0) for g in heads]
            hl = [jnp.concatenate(_split_bf16(lrest[g]), axis=1) for g in heads]
            between = [_dot(hl[g], later2) + carry[g][0] for g in heads]
            sums = [_dot(ones2, hl[g], NT) for g in heads]
            w = [jnp.exp(lbeta[g] + between[g]) for g in heads]
            if masked:
                w = [jnp.where(mask, w[g], 0.0) for g in heads]
            pv = [_dot(w[g].astype(BF16), vj) for g in heads]
            return tuple((carry[g][0] + jnp.sum(lrest[g], axis=1, keepdims=True), carry[g][1] + pv[g],
                          carry[g][2] + sums[g]) for g in heads)

        def alive(carry):
            top = carry[0][0]
            for g in heads[1:]:
                top = jnp.maximum(top, carry[g][0])
            return jnp.max(top) > DEAD_LOG_WEIGHT

        zero = (jnp.zeros((tq, 1), F32), jnp.zeros((tq, HEAD_DIM), F32), jnp.zeros((8, tq), F32))
        carry = block(i, (zero,) * KV_GROUP, True)

        def step(state):
            jj, _, cr = state
            cr = block(i - 1 - jj, cr, False)
            return jj + 1, alive(cr), cr

        done, _, carry = lax.while_loop(lambda st: jnp.logical_and(st[0] < i, st[1]), step,
                                        (jnp.int32(0), alive(carry), carry))
        for g in heads:
            o_ref[:, g * HEAD_DIM:(g + 1) * HEAD_DIM] = carry[g][1]
            tot_ref[g] = carry[g][2][0:1, :]
        cnt_ref[pl.program_id(0), i] = done

    group = KV_GROUP * HEAD_DIM
    out = pl.pallas_call(
        body, name=name, grid=grid,
        in_specs=[pl.BlockSpec((tq, group), lambda kh, i: (i, kh)),
                  pl.BlockSpec((s, HEAD_DIM), lambda kh, i: (0, kh)),
                  pl.BlockSpec((s, HEAD_DIM), lambda kh, i: (0, kvh + kh))] + c_in,
        out_specs=[pl.BlockSpec((tq, group), lambda kh, i: (i, kh)),
                   pl.BlockSpec((KV_GROUP, 1, tq), lambda kh, i: (kh, 0, i)),
                   pl.BlockSpec(memory_space=pltpu.SMEM)] + c_out,
        out_shape=[_sds((s, dq), F32), _sds((dq // HEAD_DIM, 1, s), F32),
                   _sds((kvh, s // tq), jnp.int32)] + c_shapes,
        scratch_shapes=c_sems,
        compiler_params=_params(("arbitrary", "arbitrary")),
    )(q, kv, kv, *(comm["operands"] if comm else ()))
    return (out[:3], out[3:]) if comm else out


def attn_bwd(name, q, kv, totals, visited, do):
    s, dq_cols = q.shape
    kvh = kv.shape[1] // (2 * HEAD_DIM)
    tq = _tile(s, ATTN_TILE)
    scale = HEAD_DIM ** -0.5

    def body(cnt_ref, q_ref, k_ref, v_ref, tot_ref, do_ref, dq_ref, dkv_ref):
        i = pl.program_id(1)
        first = i - jnp.clip(cnt_ref[pl.program_id(0), i], 0, i)

        @pl.when(i == 0)
        def _():
            dkv_ref[...] = jnp.zeros_like(dkv_ref)

        heads = range(KV_GROUP)
        qs = [q_ref[:, g * HEAD_DIM:(g + 1) * HEAD_DIM] for g in heads]
        dobs = [do_ref[:, g * HEAD_DIM:(g + 1) * HEAD_DIM].astype(BF16) for g in heads]
        tots = [tot_ref[g] for g in heads]
        q_all = jnp.concatenate(qs, axis=0)
        do_all = jnp.concatenate(dobs, axis=0)
        r_i = lax.broadcasted_iota(jnp.int32, (tq, tq), 0)
        c_i = lax.broadcasted_iota(jnp.int32, (tq, tq), 1)
        upto = (c_i <= r_i).astype(BF16)
        before = (c_i < r_i).astype(BF16)
        upto2 = jnp.concatenate([upto, upto], axis=1)
        before2 = jnp.concatenate([before, before], axis=1)
        mask = r_i < c_i

        def block(j, carry, masked):
            rows = pl.ds(pl.multiple_of(j * tq, tq), tq)
            kj = k_ref[rows, :]
            vj = v_ref[rows, :]
            zt = [_dot(kj, qs[g], NT) * scale for g in heads]
            dwt = [_dot(vj, dobs[g], NT) for g in heads]
            lbeta = [_log_sigmoid(zt[g]) for g in heads]
            lrest_raw = [lbeta[g] - zt[g] for g in heads]
            lrest = [jnp.where(mask, lrest_raw[g], 0.0) for g in heads] if masked else lrest_raw
            hl = [jnp.concatenate(_split_bf16(lrest[g]), axis=0) for g in heads]
            upto_sum = [_dot(upto2, hl[g]) for g in heads]
            wt = [jnp.exp(lbeta[g] + (tots[g] - carry[g][0] - upto_sum[g])) for g in heads]
            if masked:
                wt = [jnp.where(mask, wt[g], 0.0) for g in heads]
            dat = [dwt[g] * wt[g] for g in heads]
            earlier = [_dot(before2, jnp.concatenate(_split_bf16(dat[g]), axis=0)) for g in heads]
            dzt = [dat[g] * jnp.exp(lrest_raw[g]) - (carry[g][1] + earlier[g]) * jnp.exp(lbeta[g]) for g in heads]
            if masked:
                dzt = [jnp.where(mask, dzt[g], 0.0) for g in heads]
            dzb = [(dzt[g] * scale).astype(BF16) for g in heads]
            dq_new = [_dot(dzb[g], kj, TN) for g in heads]
            dkv_ref[0, rows, :] += _dot(jnp.concatenate(dzb, axis=1), q_all)
            dkv_ref[1, rows, :] += _dot(jnp.concatenate([wt[g].astype(BF16) for g in heads], axis=1), do_all)
            return tuple((carry[g][0] + jnp.sum(lrest[g], axis=0, keepdims=True),
                          carry[g][1] + jnp.sum(dat[g], axis=0, keepdims=True),
                          carry[g][2] + dq_new[g]) for g in heads)

        zrow = jnp.zeros((1, tq), F32)
        carry = ((zrow, zrow, jnp.zeros((tq, HEAD_DIM), F32)),) * KV_GROUP
        carry = lax.fori_loop(first, i, lambda j, cr: block(j, cr, False), carry)
        carry = block(i, carry, True)
        for g in heads:
            dq_ref[:, g * HEAD_DIM:(g + 1) * HEAD_DIM] = carry[g][2].astype(dq_ref.dtype)

    group = KV_GROUP * HEAD_DIM
    qspec = pl.BlockSpec((tq, group), lambda kh, i, cnt: (i, kh))
    return pl.pallas_call(
        body, name=name,
        grid_spec=pltpu.PrefetchScalarGridSpec(
            num_scalar_prefetch=1, grid=(kvh, s // tq),
            in_specs=[qspec,
                      pl.BlockSpec((s, HEAD_DIM), lambda kh, i, cnt: (0, kh)),
                      pl.BlockSpec((s, HEAD_DIM), lambda kh, i, cnt: (0, kvh + kh)),
                      pl.BlockSpec((KV_GROUP, 1, tq), lambda kh, i, cnt: (kh, 0, i)),
                      qspec],
            out_specs=[qspec, pl.BlockSpec((2, s, HEAD_DIM), lambda kh, i, cnt: (0, 0, kh))]),
        out_shape=[_sds((s, dq_cols), BF16), _sds((2, s, kvh * HEAD_DIM), F32)],
        compiler_params=_params(("parallel", "arbitrary")),
    )(visited, q, kv, kv, totals, do)


def ada_project(name, c_all, w, b):
    bsz, d = c_all.shape
    n = w.shape[1]
    tn = _tile(n, 512)

    def body(c_ref, w_ref, b_ref, o_ref):
        cv = c_ref[...]
        act = cv * _sigmoid(cv)
        o_ref[...] = _bdot(act, w_ref[...]) + b_ref[...]

    return pl.pallas_call(
        body, name=name, grid=(n // tn,),
        in_specs=[pl.BlockSpec((bsz, d), lambda i: (0, 0)),
                  pl.BlockSpec((d, tn), lambda i: (0, i)),
                  pl.BlockSpec((1, tn), lambda i: (0, i))],
        out_specs=pl.BlockSpec((bsz, tn), lambda i: (0, i)),
        out_shape=_sds((bsz, n), F32),
        compiler_params=_params(("parallel",)),
    )(c_all, w, b)


def _adamw_math(w, g, m, v):
    m = ADAM_B1 * m + (1.0 - ADAM_B1) * g
    v = ADAM_B2 * v + (1.0 - ADAM_B2) * (g * g)
    m_hat = m / (1.0 - ADAM_B1 ** ADAM_STEP)
    v_hat = v / (1.0 - ADAM_B2 ** ADAM_STEP)
    delta = -ADAM_LR * (m_hat / (jnp.sqrt(v_hat) + ADAM_EPS) + ADAM_WD * w)
    return delta, m, v


def adamw(name, w, g, m, v):
    r, c = w.shape
    tr = _tile(r, 256)
    tc = _tile(c, 2048)

    def body(w_ref, g_ref, m_ref, v_ref, d_ref, mo_ref, vo_ref):
        delta, mn, vn = _adamw_math(w_ref[...], g_ref[...], m_ref[...], v_ref[...])
        d_ref[...] = delta
        mo_ref[...] = mn
        vo_ref[...] = vn

    spec = pl.BlockSpec((tr, tc), lambda i, j: (i, j))
    return pl.pallas_call(
        body, name=name, grid=(r // tr, c // tc),
        in_specs=[spec] * 4, out_specs=[spec] * 3,
        out_shape=[_sds((r, c), F32)] * 3,
        compiler_params=_params(("parallel", "parallel")),
    )(w, g, m, v)


def ada_grad_adamw(name, c_t, dmod, w, m, v):
    layers, d, n = w.shape
    tr = _tile(d, 256)
    tc = _tile(n, 512)

    def body(a_ref, dm_ref, w_ref, m_ref, v_ref, g_ref, d_ref, mo_ref, vo_ref):
        cv = a_ref[...]
        g = _bdot(cv * _sigmoid(cv), dm_ref[...])
        delta, mn, vn = _adamw_math(w_ref[...], g, m_ref[...], v_ref[...])
        g_ref[...] = g
        d_ref[...] = delta
        mo_ref[...] = mn
        vo_ref[...] = vn

    spec = pl.BlockSpec((None, tr, tc), lambda l, i, j: (l, i, j))
    return pl.pallas_call(
        body, name=name, grid=(layers, d // tr, n // tc),
        in_specs=[pl.BlockSpec((tr, 128), lambda l, i, j: (i, 0)),
                  pl.BlockSpec((None, 128, tc), lambda l, i, j: (l, 0, j)), spec, spec, spec],
        out_specs=[spec] * 4, out_shape=[_sds((layers, d, n), F32)] * 4,
        compiler_params=_params(("parallel", "parallel", "parallel")),
    )(c_t, dmod, w, m, v)


def device_sum(name, gathered):
    _, r, c = gathered.shape

    def body(g_ref, o_ref):
        acc = g_ref[0]
        for dev in range(1, N_DEV):
            acc = acc + g_ref[dev]
        o_ref[...] = acc

    return pl.pallas_call(
        body, name=name,
        in_specs=[pl.BlockSpec(memory_space=pltpu.VMEM)],
        out_specs=pl.BlockSpec(memory_space=pltpu.VMEM),
        out_shape=_sds((r, c), F32),
    )(gathered)


def lower_bound_fwd(name, logits):
    _, d = logits.shape

    def body(l_ref, o_ref):
        l0 = l_ref[0:1, :]
        l1 = l_ref[1:2, :]
        mx = jnp.maximum(l0, l1)
        e0 = jnp.exp(l0 - mx)
        e1 = jnp.exp(l1 - mx)
        o_ref[...] = e0 / (e0 + e1)

    return pl.pallas_call(
        body, name=name,
        in_specs=[pl.BlockSpec(memory_space=pltpu.VMEM)],
        out_specs=pl.BlockSpec(memory_space=pltpu.VMEM),
        out_shape=_sds((1, d), F32),
    )(logits)


def lower_bound_bwd(name, logits, dlb):
    _, d = logits.shape

    def body(l_ref, dlb_ref, o_ref):
        l0 = l_ref[0:1, :]
        l1 = l_ref[1:2, :]
        mx = jnp.maximum(l0, l1)
        e0 = jnp.exp(l0 - mx)
        e1 = jnp.exp(l1 - mx)
        p0 = e0 / (e0 + e1)
        p1 = e1 / (e0 + e1)
        g = dlb_ref[...] * p0 * p1
        o_ref[0:1, :] = g
        o_ref[1:2, :] = -g

    return pl.pallas_call(
        body, name=name,
        in_specs=[pl.BlockSpec(memory_space=pltpu.VMEM)] * 2,
        out_specs=pl.BlockSpec(memory_space=pltpu.VMEM),
        out_shape=_sds((2, d), F32),
    )(logits, dlb)


HBM = pl.BlockSpec(memory_space=pltpu.HBM)


def _position():
    return lax.axis_index("x"), lax.axis_index("y"), lax.axis_index("c")


def _remote(src, dst, send_sem, recv_sem, device):
    return pltpu.make_async_remote_copy(src_ref=src, dst_ref=dst, send_sem=send_sem, recv_sem=recv_sem,
                                        device_id=device, device_id_type=MESH)


def allgather8(name, x):
    r, c = x.shape

    def body(x_ref, out_ref, send_sems, recv_sems):
        px, py, pc = _position()
        me = 4 * px + 2 * py + pc
        out_ref[me] = x_ref[...]
        copies = []
        for k in range(1, N_DEV):
            peer = (1 - px if k & 4 else px, 1 - py if k & 2 else py, 1 - pc if k & 1 else pc)
            cp = _remote(x_ref, out_ref.at[me], send_sems.at[k - 1], recv_sems.at[k - 1], peer)
            cp.start()
            copies.append(cp)
        for cp in copies:
            cp.wait()

    return pl.pallas_call(
        body, name=name,
        in_specs=[pl.BlockSpec(memory_space=pltpu.VMEM)],
        out_specs=pl.BlockSpec(memory_space=pltpu.VMEM),
        out_shape=_sds((N_DEV, r, c), x.dtype),
        scratch_shapes=[pltpu.SemaphoreType.DMA((N_DEV - 1,)), pltpu.SemaphoreType.DMA((N_DEV - 1,))],
    )(x)


def _other_chips(px, py):
    return [(1 - px, py), (px, 1 - py), (1 - px, 1 - py)]


def gather_weights(name, shards):
    n = len(shards)
    hook = gather_hook(shards)

    def body(*refs):
        ins, outs = refs[:n], refs[n:2 * n]
        send_sems, recv_sems = refs[2 * n:]
        hook["start"](ins, outs, send_sems, recv_sems)
        hook["finish"](ins, outs, send_sems, recv_sems)

    return pl.pallas_call(
        body, name=name,
        in_specs=[HBM] * n, out_specs=[HBM] * n,
        out_shape=hook["out_shapes"],
        scratch_shapes=[pltpu.SemaphoreType.DMA(hook["sems"]), pltpu.SemaphoreType.DMA(hook["sems"])],
    )(*shards)


def gather_hook(shards):
    n = len(shards)

    def copies(ins, outs, send_sems, recv_sems):
        px, py, pc = _position()
        chip = 2 * px + py
        sibling = (px, py, 1 - pc)
        own, sends, arrivals, passes, pass_arrivals = [], [], [], [], []
        for a in range(n):
            rows = ins[a].shape[0] // 2
            mine, theirs = pl.ds(pc * rows, rows), pl.ds((1 - pc) * rows, rows)
            own.append(_remote(ins[a], outs[a].at[chip], send_sems.at[a, 6], recv_sems.at[a, 6], sibling))
            for j, (ox, oy) in enumerate(_other_chips(px, py)):
                sends.append(_remote(ins[a].at[mine], outs[a].at[chip, mine],
                                     send_sems.at[a, j], recv_sems.at[a, j], (ox, oy, pc)))
                landed = outs[a].at[2 * ox + oy, mine]
                arrivals.append(_remote(landed, landed, send_sems.at[a, j], recv_sems.at[a, j], sibling))
                passes.append(_remote(landed, landed, send_sems.at[a, 3 + j], recv_sems.at[a, 3 + j], sibling))
                via = outs[a].at[2 * ox + oy, theirs]
                pass_arrivals.append(_remote(via, via, send_sems.at[a, 3 + j], recv_sems.at[a, 3 + j], sibling))
        return own, sends, arrivals, passes, pass_arrivals

    def start(ins, outs, send_sems, recv_sems):
        own, sends, _, _, _ = copies(ins, outs, send_sems, recv_sems)
        for cp in own + sends:
            cp.start()

    def finish(ins, outs, send_sems, recv_sems):
        own, sends, arrivals, passes, pass_arrivals = copies(ins, outs, send_sems, recv_sems)
        for arrival, onward in zip(arrivals, passes):
            arrival.wait_recv()
            onward.start()
        for arrival in pass_arrivals:
            arrival.wait_recv()
        for cp in sends + passes:
            cp.wait_send()
        for cp in own:
            cp.wait()

    return dict(operands=list(shards), out_shapes=[_sds((N_CHIPS,) + s.shape, s.dtype) for s in shards],
                sems=(n, 7), start=start, finish=finish)


def sibling_exchange(name, grads):
    n = len(grads)

    def body(*refs):
        ins, outs = refs[:n], refs[n:2 * n]
        send_sems, recv_sems = refs[2 * n:]
        px, py, pc = _position()
        sibling = (px, py, 1 - pc)
        copies = []
        for a in range(n):
            rows = ins[a].shape[1] // 2
            src = ins[a].at[:, pl.ds((1 - pc) * rows, rows), :]
            cp = _remote(src, outs[a], send_sems.at[a], recv_sems.at[a], sibling)
            cp.start()
            copies.append(cp)
        for cp in copies:
            cp.wait()

    return pl.pallas_call(
        body, name=name,
        in_specs=[HBM] * n, out_specs=[HBM] * n,
        out_shape=[_sds((g.shape[0], g.shape[1] // 2, g.shape[2]), g.dtype) for g in grads],
        scratch_shapes=[pltpu.SemaphoreType.DMA((n,)), pltpu.SemaphoreType.DMA((n,))],
    )(*grads)


def pair_add(name, g, recv, core):
    nb, rows, cols = g.shape
    half = rows // 2
    tr = _tile(half, 256)
    steps = half // tr

    def body(core_ref, g_ref, r_ref, o_ref):
        del core_ref
        o_ref[...] = (g_ref[...].astype(F32) + r_ref[...].astype(F32)).astype(o_ref.dtype)

    return pl.pallas_call(
        body, name=name,
        grid_spec=pltpu.PrefetchScalarGridSpec(
            num_scalar_prefetch=1, grid=(nb, steps),
            in_specs=[pl.BlockSpec((None, tr, cols), lambda j, i, core_ref: (j, core_ref[0] * steps + i, 0)),
                      pl.BlockSpec((None, tr, cols), lambda j, i, core_ref: (j, i, 0))],
            out_specs=pl.BlockSpec((None, tr, cols), lambda j, i, core_ref: (j, i, 0))),
        out_shape=_sds((nb, half, cols), g.dtype),
        compiler_params=_params(("parallel", "parallel")),
    )(core, g, recv)


def scatter_hook(parts):
    n = len(parts)

    def copies(ins, outs, send_sems, recv_sems):
        px, py, pc = _position()
        return [_remote(ins[a].at[2 * ox + oy], outs[a].at[j], send_sems.at[a, j], recv_sems.at[a, j], (ox, oy, pc))
                for a in range(n) for j, (ox, oy) in enumerate(_other_chips(px, py))]

    def start(ins, outs, send_sems, recv_sems):
        for cp in copies(ins, outs, send_sems, recv_sems):
            cp.start()

    def finish(ins, outs, send_sems, recv_sems):
        for cp in copies(ins, outs, send_sems, recv_sems):
            cp.wait()

    return dict(operands=list(parts), out_shapes=[_sds((N_CHIPS - 1,) + p.shape[1:], p.dtype) for p in parts],
                sems=(n, 3), start=start, finish=finish)


def chip_sum(name, part, recv, where, out_shape, lead, dest=None):
    _, half, cols = part.shape
    tr = _tile(half, 256)
    steps = half // tr

    def body(where_ref, p_ref, r_ref, *rest):
        o_ref = rest[-1]
        acc = p_ref[...].astype(F32)
        for j in range(N_CHIPS - 1):
            acc = acc + r_ref[j].astype(F32)
        o_ref[...] = acc

    if lead is None:
        ospec = pl.BlockSpec((tr, cols), lambda i, w: (w[1] * steps + i, 0))
    else:
        ospec = pl.BlockSpec((None, tr, cols), lambda i, w: (lead, w[1] * steps + i, 0))
    in_specs = [pl.BlockSpec((None, tr, cols), lambda i, w: (w[0], i, 0)),
                pl.BlockSpec((N_CHIPS - 1, tr, cols), lambda i, w: (0, i, 0))]
    operands = [where, part, recv]
    aliases = {}
    if dest is not None:
        in_specs.append(pl.BlockSpec(memory_space=pl.ANY))
        operands.append(dest)
        aliases = {3: 0}
    return pl.pallas_call(
        body, name=name,
        grid_spec=pltpu.PrefetchScalarGridSpec(num_scalar_prefetch=1, grid=(steps,),
                                               in_specs=in_specs, out_specs=ospec),
        out_shape=out_shape, input_output_aliases=aliases,
        compiler_params=_params(("parallel",)),
    )(*operands)


def sibling_share(name, slabs, places):
    n = len(slabs)
    k = len(places)

    def body(*refs):
        outs = refs[n:2 * n]
        send_sems, recv_sems = refs[2 * n:]
        px, py, pc = _position()
        sibling = (px, py, 1 - pc)
        copies = []
        for a, (oi, lead, half) in enumerate(places):
            slab = outs[oi] if lead is None else outs[oi].at[lead]
            mine = slab.at[pl.ds(pc * half, half)]
            theirs = slab.at[pl.ds((1 - pc) * half, half)]
            cp = _remote(mine, mine, send_sems.at[a], recv_sems.at[a], sibling)
            cp.start()
            copies.append((cp, _remote(theirs, theirs, send_sems.at[a], recv_sems.at[a], sibling)))
        for cp, arrival in copies:
            cp.wait_send()
            arrival.wait_recv()

    return pl.pallas_call(
        body, name=name,
        in_specs=[HBM] * n, out_specs=[HBM] * n,
        out_shape=[_sds(s.shape, s.dtype) for s in slabs],
        input_output_aliases={a: a for a in range(n)},
        scratch_shapes=[pltpu.SemaphoreType.DMA((k,)), pltpu.SemaphoreType.DMA((k,))],
    )(*slabs)


def _row(a, i):
    return a[i:i + 1]


def _relu_sq(acc):
    r = jnp.maximum(acc, 0.0)
    return r, r * r


def _residual(acc, res, gate):
    return res + gate * acc, acc


def _with_comm(res, comm):
    return res if comm else (res, [])


def _blocked(g):
    return g if g.ndim == 3 else g.reshape(N_CHIPS, g.shape[0] // N_CHIPS, g.shape[1])


def _pre_reduce(tag, named, core):
    glist = [_blocked(g) for _, g in named]
    recv = sibling_exchange("grad_exchange_" + tag, glist)
    return [pair_add("grad_pair_add_" + k, g, r, core) for (k, _), g, r in zip(named, glist, recv)]


def _mlp_fwd(tag, x, gain, mod, w1, w2=None, w2_shard=None, down_comm=None):
    h = norm_mod_fwd(tag + "_mlp_norm", x, gain, _row(mod, 4), _row(mod, 3))
    up_comm = None if w2 is not None else gather_hook([w2_shard])
    (r3, a3), got = _with_comm(mm_nn_b(tag + "_mlp_up", h, w1, [BF16, BF16], _relu_sq, comm=up_comm), up_comm)
    if w2 is None:
        w2 = got[0]
    (x_out, m), got = _with_comm(mm_nn_r(tag + "_mlp_down", a3, w2, [F32, BF16], _residual,
                                         (x, _row(mod, 5)), ("tile", "row"), comm=down_comm), down_comm)
    return x_out, (h, r3, a3, m), w2, got


def _mlp_bwd(tag, dx_out, dm, x, gain, mod, w1, w2, saved, core, branch, down_comm=None):
    h, r3, a3, _ = saved
    (dz3,), got = _with_comm(mm_nt_b(tag + "_mlp_down_dgrad", dm, w2, [BF16],
                                     lambda acc, r: (acc * (2.0 * r.astype(F32)),), (r3,), comm=down_comm),
                             down_comm)
    gw2 = mm_tn(tag + "_mlp_w2_grad", a3, dm[None], BF16)
    part2 = _pre_reduce(tag + "_w2", [(tag + "_w2", gw2)], core)
    gw1, recv2 = mm_tn(tag + "_mlp_w1_grad", h[None], dz3, BF16, comm=scatter_hook(part2))
    part1 = _pre_reduce(tag + "_w1", [(tag + "_w1", gw1)], core)
    dh, recv1 = mm_nt_r(tag + "_mlp_up_dgrad", dz3, w1, F32, comm=scatter_hook(part1))
    dx, dsh, dsc, dgain, dy, dgate = norm_mod_bwd(tag + "_mlp_norm_bwd", dh, x, gain, _row(mod, 4), dx_out, branch)
    return dx, (dy, dgate), (part1[0], recv1[0]), (part2[0], recv2[0]), (dsh, dsc), dgain, got


def local_step(x, target, mod0, mod1, kvmod, norm_mix, norm_mlp, kv_norm, final_norm, lb, out_gain,
               w, shards, core):
    w = dict(w)

    def flat(g):
        return g.reshape(-1, g.shape[-1])

    h1 = norm_mod_fwd("l0_mix_norm", x, _row(norm_mix, 0), _row(mod0, 1), _row(mod0, 0))
    (proj3,), (w["w1_0"],) = mm_nn_b("l0_in_proj", h1, w["a_in"], [F32], comm=gather_hook([shards["w1_0"]]))
    (u, o_h, states), got = hgrn_fwd("l0_hgrn_fwd", proj3, lb, out_gain,
                                     comm=gather_hook([shards["bq"], shards["kv"], shards["bo"]]))
    w["bq"], w["kv"], w["bo"] = (flat(g) for g in got)
    x1, y0 = mm_nn_b("l0_out_proj", u, w["a_out"][None], [F32, BF16], _residual,
                     (x[None], _row(mod0, 2)), ("tile", "row"))
    x1, y0 = x1[0], y0[0]
    x2, mlp0, w["w2_0"], (w["w1_1"],) = _mlp_fwd(
        "l0", x1, _row(norm_mlp, 0), mod0, w["w1_0"], w2_shard=shards["w2_0"],
        down_comm=gather_hook([shards["w1_1"]]))
    hk = norm_mod_fwd("kv_norm", x2, kv_norm, _row(kvmod, 1), _row(kvmod, 0))
    kv = mm_nn_b("kv_proj", hk, w["kv"][None], [BF16])[0][0]
    h3 = norm_mod_fwd("l1_mix_norm", x2, _row(norm_mix, 1), _row(mod1, 1), _row(mod1, 0))
    q = mm_nn_b("l1_q_proj", h3, w["bq"][None], [BF16])[0][0]
    (o_a, totals, visited), (w["w2_1"],) = attn_fwd("l1_attn_fwd", q, kv, comm=gather_hook([shards["w2_1"]]))
    x3, y1 = mm_nn_b("l1_out_proj", o_a, w["bo"][None], [F32, BF16], _residual,
                     (x2[None], _row(mod1, 2)), ("tile", "row"))
    x3, y1 = x3[0], y1[0]
    x4, mlp1, _, _ = _mlp_fwd("l1", x3, _row(norm_mlp, 1), mod1, w["w1_1"], w["w2_1"])
    dx4, d_final, loss, dm1, dgate_mlp1 = final_loss("final_loss", x4, final_norm, target, mlp1[3], _row(mod1, 5))

    reduce = {}
    dx3, (dy1, dgate1), reduce["w1_1"], reduce["w2_1"], dmlp1, d_nmlp1, _ = _mlp_bwd(
        "l1", dx4, dm1, x3, _row(norm_mlp, 1), mod1, w["w1_1"], w["w2_1"], mlp1, core, (y1, _row(mod1, 2)))
    do_a = mm_nt_b("l1_out_dgrad", dy1, w["bo"][None], [F32])[0][0]
    g_bo = mm_tn("l1_out_grad", o_a[None], dy1[None], BF16)[0]
    dq, dkv3 = attn_bwd("l1_attn_bwd", q, kv, totals, visited, do_a)
    g_bq = mm_tn("l1_q_grad", h3[None], dq[None], BF16)[0]
    dh3 = mm_nt_b("l1_q_dgrad", dq, w["bq"][None], [F32])[0][0]
    dx2, dsh, dsc, d_nmix1 = norm_mod_bwd("l1_mix_norm_bwd", dh3, x2, _row(norm_mix, 1), _row(mod1, 1), dx3)
    dmod1 = jnp.concatenate([dsh, dsc, dgate1, *dmlp1, dgate_mlp1], axis=0)
    dkv = jnp.concatenate([dkv3[0], dkv3[1]], axis=1).astype(BF16)
    g_kv = mm_tn("kv_grad", hk[None], dkv[None], BF16)[0]
    dhk = mm_nt_b("kv_dgrad", dkv, w["kv"][None], [F32])[0][0]
    dx2, dsh, dsc, d_nkv, dm0, dgate_mlp0 = norm_mod_bwd("kv_norm_bwd", dhk, x2, kv_norm, _row(kvmod, 1), dx2,
                                                         (mlp0[3], _row(mod0, 5)))
    dkvmod = jnp.concatenate([dsh, dsc], axis=0)
    attn_parts = _pre_reduce("attn", [("bo", g_bo), ("bq", g_bq), ("kv", g_kv)], core)
    dx1, (dy0, dgate0), reduce["w1_0"], reduce["w2_0"], dmlp0, d_nmlp0, attn_recv = _mlp_bwd(
        "l0", dx2, dm0, x1, _row(norm_mlp, 0), mod0, w["w1_0"], w["w2_0"], mlp0, core, (y0, _row(mod0, 2)),
        scatter_hook(attn_parts))
    for k, part, recv in zip(("bo", "bq", "kv"), attn_parts, attn_recv):
        reduce[k] = (part, recv)
    du = mm_nt_b("l0_out_dgrad", dy0, w["a_out"][None], [F32])[0][0]
    g_a_out = mm_tn("l0_out_grad", u[None], dy0[None], BF16)[0]
    dproj3, d_lb, d_out_gain = hgrn_bwd("l0_hgrn_bwd", proj3, lb, out_gain, o_h, du, states)
    part = _pre_reduce("a_out", [("a_out", g_a_out)], core)
    g_a_in, recv = mm_tn("l0_in_grad", h1[None], dproj3, BF16, comm=scatter_hook(part))
    reduce["a_out"] = (part[0], recv[0])
    part = _pre_reduce("a_in", [("a_in", g_a_in)], core)
    dh1, recv = mm_nt_r("l0_in_dgrad", dproj3, w["a_in"], F32, comm=scatter_hook(part))
    reduce["a_in"] = (part[0], recv[0])
    dx0, dsh, dsc, d_nmix0 = norm_mod_bwd("l0_mix_norm_bwd", dh1, x, _row(norm_mix, 0), _row(mod0, 1), dx1)
    dmod0 = jnp.concatenate([dsh, dsc, dgate0, *dmlp0, dgate_mlp0], axis=0)
    small = dict(norm_mix=(d_nmix0, d_nmix1), norm_mlp=(d_nmlp0, d_nmlp1), kv_norm=d_nkv,
                 final_norm=d_final, lb=d_lb, out_gain=d_out_gain)
    return loss, dx0, reduce, dmod0, dmod1, dkvmod, small


BIG = ("a_in", "w1_0", "w1_1", "w2_0", "w2_1", "a_out", "bq", "bo", "kv")


def kernel(x, c, ada_w, ada_b, norm_mix, norm_mlp, a_w_in, a_lb_logits, a_out_gain, a_w_out, kv_ada_w, kv_ada_b, kv_norm, w_kv, b_w_q, b_w_out, mlp_w1, mlp_w2, final_norm, loss_target, m_ada_w, m_ada_b, m_norm_mix, m_norm_mlp, m_a_w_in, m_a_lb_logits, m_a_out_gain, m_a_w_out, m_kv_ada_w, m_kv_ada_b, m_kv_norm, m_w_kv, m_b_w_q, m_b_w_out, m_mlp_w1, m_mlp_w2, m_final_norm, v_ada_w, v_ada_b, v_norm_mix, v_norm_mlp, v_a_w_in, v_a_lb_logits, v_a_out_gain, v_a_w_out, v_kv_ada_w, v_kv_ada_b, v_kv_norm, v_w_kv, v_b_w_q, v_b_w_out, v_mlp_w1, v_mlp_w2, v_final_norm):
    d = x.shape[-1]
    px, py, pc = _position()
    me = 4 * px + 2 * py + pc
    chip = 2 * px + py
    n_ada = ada_w.shape[2]
    n_kvada = kv_ada_w.shape[1]
    shard_cols = d // N_CHIPS

    def as_rows(a):
        return a.reshape(-1, shard_cols)

    pack1 = jnp.concatenate([as_rows(c), a_lb_logits, a_out_gain,
                             jnp.zeros((1, shard_cols), F32)], axis=0)
    got1 = allgather8("gather_cond", pack1)
    n_c = d // shard_cols
    c_all = got1[:, :n_c, :].reshape(N_DEV, d)
    per_chip = got1[0::2]
    logits_full = jnp.swapaxes(per_chip[:, n_c:n_c + 2, :], 0, 1).reshape(2, d)
    out_gain_full = per_chip[:, n_c + 2, :].reshape(1, d)
    lb = lower_bound_fwd("lower_bound", logits_full)

    bias0 = lax.dynamic_slice_in_dim(ada_b, chip * n_ada, n_ada, axis=1)
    bias_kv = lax.dynamic_slice_in_dim(kv_ada_b.reshape(1, -1), chip * n_kvada, n_kvada, axis=1)
    mod_part = jnp.concatenate([
        ada_project("ada_proj_0", c_all, ada_w[0], bias0[0:1]),
        ada_project("ada_proj_1", c_all, ada_w[1], bias0[1:2]),
        ada_project("ada_proj_kv", c_all, kv_ada_w, bias_kv)], axis=1)
    got2 = allgather8("gather_mod", mod_part)
    mine = lax.dynamic_index_in_dim(got2[0::2], me, axis=1, keepdims=False)
    mod0 = mine[:, :n_ada].reshape(6, d)
    mod1 = mine[:, n_ada:2 * n_ada].reshape(6, d)
    kvmod = mine[:, 2 * n_ada:].reshape(2, d)

    shards = dict(a_in=a_w_in[0], w1_0=mlp_w1[0], w1_1=mlp_w1[1], w2_0=mlp_w2[0], w2_1=mlp_w2[1],
                  a_out=a_w_out[0], bq=b_w_q[0], bo=b_w_out[0], kv=w_kv)
    shards = {k: s.astype(BF16) for k, s in shards.items()}
    g_in, g_out = gather_weights("gather_first_weights", [shards["a_in"], shards["a_out"]])
    core = pc.astype(jnp.int32).reshape(1)

    loss, dx0, reduce, dmod0, dmod1, dkvmod, small = local_step(
        x[0], loss_target[0], mod0, mod1, kvmod, norm_mix, norm_mlp, kv_norm.reshape(1, d),
        final_norm.reshape(1, d), lb, out_gain_full,
        dict(a_in=g_in, a_out=g_out.reshape(-1, g_out.shape[-1])), shards, core)

    loss_row = jnp.concatenate([loss, jnp.zeros((1, shard_cols - loss.shape[1]), F32)], axis=1)
    rows = [as_rows(dmod0), as_rows(dmod1), as_rows(dkvmod),
            as_rows(small["norm_mix"][0]), as_rows(small["norm_mix"][1]),
            as_rows(small["norm_mlp"][0]), as_rows(small["norm_mlp"][1]),
            as_rows(small["kv_norm"]), as_rows(small["final_norm"]),
            as_rows(small["lb"]), as_rows(small["out_gain"]), loss_row]
    n_rows = sum(r.shape[0] for r in rows)
    pad = (-n_rows) % 8
    pack3 = jnp.concatenate(rows + [jnp.zeros((pad, shard_cols), F32)], axis=0)
    got3 = allgather8("gather_small_grads", pack3)
    total = device_sum("sum_small_grads", got3)
    n_mod_rows = (12 * d + 2 * d) // shard_cols
    n_gain_rows = 6 * n_c
    loss_out = total[n_mod_rows + n_gain_rows + 2 * n_c, 0]

    dmod_all = got3[:, :n_mod_rows, :].reshape(N_DEV, 14 * d)
    act_t = jnp.zeros((d, 128), F32).at[:, :N_DEV].set(c_all.T)

    def dmod_cols(lo, width):
        part = lax.dynamic_slice_in_dim(dmod_all, lo + chip * width, width, axis=1)
        return jnp.zeros((128, width), F32).at[:N_DEV].set(part)

    g_ada_w, d_ada_w, nm_ada_w, nv_ada_w = ada_grad_adamw(
        "ada_update", act_t, jnp.stack([dmod_cols(0, n_ada), dmod_cols(6 * d, n_ada)]), ada_w, m_ada_w, v_ada_w)
    g_kv_ada_w, d_kv_ada_w, nm_kv_ada_w, nv_kv_ada_w = (a[0] for a in ada_grad_adamw(
        "ada_update_kv", act_t, dmod_cols(12 * d, n_kvada)[None], kv_ada_w[None], m_kv_ada_w[None],
        v_kv_ada_w[None]))

    chip_parts = [reduce[k][0] for k in BIG]
    from_chips = [reduce[k][1] for k in BIG]
    where = jnp.stack([chip, pc]).astype(jnp.int32)
    out_shapes = [_sds(a_w_in.shape, F32), _sds(mlp_w1.shape, F32), _sds(mlp_w2.shape, F32),
                  _sds(a_w_out.shape, F32), _sds(b_w_q.shape, F32), _sds(b_w_out.shape, F32),
                  _sds(w_kv.shape, F32)]
    targets = [(0, 0), (1, 0), (1, 1), (2, 0), (2, 1), (3, 0), (4, 0), (5, 0), (6, None)]
    slabs = [None] * len(out_shapes)
    places = []
    for k, part, recv, (oi, lead) in zip(BIG, chip_parts, from_chips, targets):
        slabs[oi] = chip_sum("grad_chip_sum_" + k, part, recv, where, out_shapes[oi], lead, slabs[oi])
        places.append((oi, lead, part.shape[1]))
    g_a_w_in, g_mlp_w1, g_mlp_w2, g_a_w_out, g_b_w_q, g_b_w_out, g_w_kv = sibling_share(
        "grad_sibling_share", slabs, places)

    def update(name, wgt, g, m_, v_):
        shape = wgt.shape
        f = lambda a: a.reshape(-1, shape[-1])
        return tuple(o.reshape(shape) for o in adamw(name, f(wgt), f(g), f(m_), f(v_)))

    u_a_w_in = update("adamw_a_w_in", a_w_in, g_a_w_in, m_a_w_in, v_a_w_in)
    u_mlp_w1 = update("adamw_mlp_w1", mlp_w1, g_mlp_w1, m_mlp_w1, v_mlp_w1)
    u_mlp_w2 = update("adamw_mlp_w2", mlp_w2, g_mlp_w2, m_mlp_w2, v_mlp_w2)
    u_a_w_out = update("adamw_a_w_out", a_w_out, g_a_w_out, m_a_w_out, v_a_w_out)
    u_b_w_q = update("adamw_b_w_q", b_w_q, g_b_w_q, m_b_w_q, v_b_w_q)
    u_b_w_out = update("adamw_b_w_out", b_w_out, g_b_w_out, m_b_w_out, v_b_w_out)
    u_w_kv = update("adamw_w_kv", w_kv, g_w_kv, m_w_kv, v_w_kv)

    base = n_mod_rows + n_gain_rows
    d_lb_mine = lax.dynamic_slice_in_dim(total, base + chip, 1, axis=0)
    d_gain_mine = lax.dynamic_slice_in_dim(total, base + n_c + chip, 1, axis=0)
    d_logits = lower_bound_bwd("lower_bound_bwd", a_lb_logits, d_lb_mine)

    def pack_small(ada_b_, kv_ada_b_, norm_mix_, norm_mlp_, kv_norm_, final_norm_, lbl_, gain_):
        parts = [as_rows(ada_b_), as_rows(kv_ada_b_), as_rows(norm_mix_), as_rows(norm_mlp_),
                 as_rows(kv_norm_), as_rows(final_norm_), lbl_, gain_]
        n = sum(p.shape[0] for p in parts)
        return jnp.concatenate(parts + [jnp.zeros(((-n) % 8, shard_cols), F32)], axis=0)

    w_small = pack_small(ada_b, kv_ada_b, norm_mix, norm_mlp, kv_norm, final_norm, a_lb_logits, a_out_gain)
    m_small = pack_small(m_ada_b, m_kv_ada_b, m_norm_mix, m_norm_mlp, m_kv_norm, m_final_norm,
                         m_a_lb_logits, m_a_out_gain)
    v_small = pack_small(v_ada_b, v_kv_ada_b, v_norm_mix, v_norm_mlp, v_kv_norm, v_final_norm,
                         v_a_lb_logits, v_a_out_gain)
    n_small = w_small.shape[0]
    g_small = jnp.concatenate([total[:base], d_logits, d_gain_mine,
                               jnp.zeros((n_small - base - 3, shard_cols), F32)], axis=0)
    small_out = (g_small,) + tuple(adamw("adamw_small", w_small, g_small, m_small, v_small))

    def unpack_small(p):
        out, r0 = [], 0
        for ref in (ada_b, kv_ada_b, norm_mix, norm_mlp, kv_norm, final_norm, a_lb_logits, a_out_gain):
            nr = ref.size // shard_cols
            out.append(p[r0:r0 + nr].reshape(ref.shape))
            r0 += nr
        return out

    sm = [unpack_small(p) for p in small_out]

    def leaves(kind, big_ada, big_kv_ada):
        ada_b_, kv_ada_b_, norm_mix_, norm_mlp_, kv_norm_, final_norm_, lbl_, gain_ = sm[kind]
        pick = (lambda u, g: g) if kind == 0 else (lambda u, g: u[kind - 1])
        return [big_ada, ada_b_, norm_mix_, norm_mlp_, pick(u_a_w_in, g_a_w_in), lbl_, gain_,
                pick(u_a_w_out, g_a_w_out), big_kv_ada, kv_ada_b_, kv_norm_, pick(u_w_kv, g_w_kv),
                pick(u_b_w_q, g_b_w_q), pick(u_b_w_out, g_b_w_out), pick(u_mlp_w1, g_mlp_w1),
                pick(u_mlp_w2, g_mlp_w2), final_norm_]

    return (loss_out, dx0[None],
            *leaves(0, g_ada_w, g_kv_ada_w), *leaves(1, d_ada_w, d_kv_ada_w),
            *leaves(2, nm_ada_w, nm_kv_ada_w), *leaves(3, nv_ada_w, nv_kv_ada_w))
```

```python
import functools

import jax
import jax.numpy as jnp
from jax import lax
from jax.experimental import pallas as pl
from jax.experimental.pallas import tpu as pltpu

F32 = jnp.float32
BF16 = jnp.bfloat16
MESH = pl.DeviceIdType.MESH

HEAD_DIM = 128
KV_GROUP = 4
HGRN_CHUNK = 64
HGRN_HEADS = 4
HGRN_ROWS = 256
NORM_EPS = 1e-6
N_CHIPS = 4
N_DEV = 8
ROW_TILE = 256
ATTN_TILE = 256
VMEM_LIMIT = 56 * 1024 * 1024
DEAD_LOG_WEIGHT = -110.0

ADAM_LR = 0.001
ADAM_B1 = 0.9
ADAM_B2 = 0.999
ADAM_EPS = 1e-08
ADAM_WD = 0.01
ADAM_STEP = 10

NN = (((1,), (0,)), ((), ()))
NT = (((1,), (1,)), ((), ()))
TN = (((0,), (0,)), ((), ()))


def _dot(a, b, dims=NN, precision=None):
    return lax.dot_general(a, b, dims, preferred_element_type=F32, precision=precision)


def _bdot(a, b, dims=NN):
    return _dot(a.astype(BF16), b.astype(BF16), dims)


def _sigmoid(x):
    return 1.0 / (1.0 + jnp.exp(-x))


def _log_sigmoid(z):
    return jnp.minimum(z, 0.0) - jnp.log(1.0 + jnp.exp(-jnp.abs(z)))


def _split_bf16(x):
    hi = x.astype(BF16)
    lo = (x - hi.astype(F32)).astype(BF16)
    return hi, lo


def _params(sem=None):
    return pltpu.CompilerParams(dimension_semantics=sem, vmem_limit_bytes=VMEM_LIMIT)


def _tile(n, pref):
    t = min(n, pref)
    assert n % t == 0, (n, pref)
    return t


def _mm(name, a, b, a_spec, b_spec, grid, n_red, dims, out_shapes, out_specs,
        acc_shape, epilogue=None, extras=(), extra_specs=(), comm=None):
    n_extra = len(extras)
    n_out = len(out_shapes)
    n_cin = len(comm["operands"]) if comm else 0
    n_cout = len(comm["out_shapes"]) if comm else 0
    if epilogue is None:
        epilogue = lambda acc: (acc,)

    def body(*refs):
        a_ref, b_ref = refs[:2]
        ex_refs = refs[2:2 + n_extra]
        cin_refs = refs[2 + n_extra:2 + n_extra + n_cin]
        out_refs = refs[2 + n_extra + n_cin:2 + n_extra + n_cin + n_out]
        cout_refs = refs[2 + n_extra + n_cin + n_out:2 + n_extra + n_cin + n_out + n_cout]
        scratch = refs[2 + n_extra + n_cin + n_out + n_cout:]
        pids = [pl.program_id(ax) for ax in range(len(grid))]
        if comm:
            send_sems, recv_sems = scratch[-2:]

            @pl.when(functools.reduce(jnp.logical_and, [p == 0 for p in pids]))
            def _():
                comm["start"](cin_refs, cout_refs, send_sems, recv_sems)

        prod = _bdot(a_ref[...], b_ref[...], dims)

        def finish(acc):
            res = epilogue(acc, *[e[...] for e in ex_refs])
            for o_ref, r in zip(out_refs, res):
                o_ref[...] = r.astype(o_ref.dtype)

        if n_red == 0:
            finish(prod)
        else:
            acc_ref = scratch[0]
            ids = pids[len(grid) - n_red:]
            sizes = grid[len(grid) - n_red:]
            first = functools.reduce(jnp.logical_and, [i == 0 for i in ids])
            last = functools.reduce(jnp.logical_and, [i == s - 1 for i, s in zip(ids, sizes)])

            @pl.when(first)
            def _():
                acc_ref[...] = prod

            @pl.when(jnp.logical_not(first))
            def _():
                acc_ref[...] += prod

            @pl.when(last)
            def _():
                finish(acc_ref[...])

        if comm:
            @pl.when(functools.reduce(jnp.logical_and, [p == s - 1 for p, s in zip(pids, grid)]))
            def _():
                comm["finish"](cin_refs, cout_refs, send_sems, recv_sems)

    if comm:
        sem = ("arbitrary",) * len(grid)
    else:
        sem = ("parallel",) * (len(grid) - n_red) + ("arbitrary",) * n_red
    scratch_shapes = [pltpu.VMEM(acc_shape, F32)] if n_red else []
    if comm:
        scratch_shapes += [pltpu.SemaphoreType.DMA(comm["sems"]), pltpu.SemaphoreType.DMA(comm["sems"])]
    out = pl.pallas_call(
        body, name=name, grid=grid,
        in_specs=[a_spec, b_spec, *extra_specs] + [HBM] * n_cin,
        out_specs=list(out_specs) + [HBM] * n_cout,
        out_shape=list(out_shapes) + (list(comm["out_shapes"]) if comm else []),
        scratch_shapes=scratch_shapes,
        compiler_params=_params(sem),
    )(a, b, *extras, *(comm["operands"] if comm else ()))
    return (out[:n_out], out[n_out:]) if comm else out


def _sds(shape, dtype):
    return jax.ShapeDtypeStruct(shape, dtype)


def mm_nn_b(name, a, w3, out_dtypes, epilogue=None, extras=(), extra_kinds=(), comm=None):
    m, k = a.shape
    nb, _, n = w3.shape
    tm, tn = _tile(m, 1024), _tile(n, 512)
    grid = (nb, m // tm, n // tn)
    nt = n // tn
    especs = []
    for kind in extra_kinds:
        if kind == "tile":
            especs.append(pl.BlockSpec((None, tm, tn), lambda j, i, c: (j, i, c)))
        else:
            especs.append(pl.BlockSpec((1, tn), lambda j, i, c: (0, j * nt + c)))
    return _mm(name, a, w3,
               pl.BlockSpec((tm, k), lambda j, i, c: (i, 0)),
               pl.BlockSpec((None, k, tn), lambda j, i, c: (j, 0, c)),
               grid, 0, NN,
               [_sds((nb, m, n), d) for d in out_dtypes],
               [pl.BlockSpec((None, tm, tn), lambda j, i, c: (j, i, c)) for _ in out_dtypes],
               None, epilogue, extras, especs, comm)


def mm_nn_r(name, a3, w3, out_dtypes, epilogue=None, extras=(), extra_kinds=(), comm=None):
    nb, m, kb = a3.shape
    n = w3.shape[2]
    tm, tn, tk = _tile(m, 1024), _tile(n, 512), _tile(kb, 2048)
    grid = (m // tm, n // tn, nb, kb // tk)
    especs = []
    for kind in extra_kinds:
        if kind == "tile":
            especs.append(pl.BlockSpec((tm, tn), lambda i, c, j, r: (i, c)))
        else:
            especs.append(pl.BlockSpec((1, tn), lambda i, c, j, r: (0, c)))
    return _mm(name, a3, w3,
               pl.BlockSpec((None, tm, tk), lambda i, c, j, r: (j, i, r)),
               pl.BlockSpec((None, tk, tn), lambda i, c, j, r: (j, r, c)),
               grid, 2, NN,
               [_sds((m, n), d) for d in out_dtypes],
               [pl.BlockSpec((tm, tn), lambda i, c, j, r: (i, c)) for _ in out_dtypes],
               (tm, tn), epilogue, extras, especs, comm)


def mm_nt_b(name, a, w3, out_dtypes, epilogue=None, extras=(), comm=None):
    m, n = a.shape
    nb, kb, _ = w3.shape
    tm, tk = _tile(m, 1024), _tile(kb, 512)
    grid = (nb, m // tm, kb // tk)
    especs = [pl.BlockSpec((None, tm, tk), lambda j, i, c: (j, i, c)) for _ in extras]
    return _mm(name, a, w3,
               pl.BlockSpec((tm, n), lambda j, i, c: (i, 0)),
               pl.BlockSpec((None, tk, n), lambda j, i, c: (j, c, 0)),
               grid, 0, NT,
               [_sds((nb, m, kb), d) for d in out_dtypes],
               [pl.BlockSpec((None, tm, tk), lambda j, i, c: (j, i, c)) for _ in out_dtypes],
               None, epilogue, extras, especs, comm)


def _single(res, comm):
    return res[0] if comm is None else (res[0][0], res[1])


def mm_nt_r(name, a3, w3, out_dtype, comm=None):
    nb, m, n = a3.shape
    k = w3.shape[1]
    tm, tk, tc = _tile(m, 1024), _tile(k, 1024), _tile(n, 2048)
    grid = (m // tm, k // tk, nb, n // tc)
    return _single(_mm(name, a3, w3,
                       pl.BlockSpec((None, tm, tc), lambda i, c, j, r: (j, i, r)),
                       pl.BlockSpec((None, tk, tc), lambda i, c, j, r: (j, c, r)),
                       grid, 2, NT,
                       [_sds((m, k), out_dtype)],
                       [pl.BlockSpec((tm, tk), lambda i, c, j, r: (i, c))],
                       (tm, tk), comm=comm), comm)


def mm_tn(name, a3, d3, out_dtype, comm=None, tiles=(512, 512, 4096)):
    na, m, kb = a3.shape
    nd, _, n = d3.shape
    nb = max(na, nd)
    tk, tn, tm = _tile(kb, tiles[0]), _tile(n, tiles[1]), _tile(m, tiles[2])
    ja = (lambda j: j) if na > 1 else (lambda j: 0)
    jd = (lambda j: j) if nd > 1 else (lambda j: 0)
    if tm == m:
        return _single(_mm(name, a3, d3,
                           pl.BlockSpec((None, tm, tk), lambda j, c, e: (ja(j), 0, c)),
                           pl.BlockSpec((None, tm, tn), lambda j, c, e: (jd(j), 0, e)),
                           (nb, kb // tk, n // tn), 0, TN,
                           [_sds((nb, kb, n), out_dtype)],
                           [pl.BlockSpec((None, tk, tn), lambda j, c, e: (j, c, e))],
                           None, comm=comm), comm)
    grid = (nb, kb // tk, n // tn, m // tm)
    return _single(_mm(name, a3, d3,
                       pl.BlockSpec((None, tm, tk), lambda j, c, e, r: (ja(j), r, c)),
                       pl.BlockSpec((None, tm, tn), lambda j, c, e, r: (jd(j), r, e)),
                       grid, 1, TN,
                       [_sds((nb, kb, n), out_dtype)],
                       [pl.BlockSpec((None, tk, tn), lambda j, c, e, r: (j, c, e))],
                       (tk, tn), comm=comm), comm)


def _row_spec(ts, d):
    return pl.BlockSpec((ts, d), lambda i: (i, 0))


def _vec_spec(d):
    return pl.BlockSpec((1, d), lambda i: (0, 0))


def norm_mod_fwd(name, x, gain, scale, shift):
    s, d = x.shape
    ts = _tile(s, ROW_TILE)

    def body(x_ref, g_ref, sc_ref, sh_ref, h_ref):
        xv = x_ref[...]
        inv = lax.rsqrt(jnp.mean(xv * xv, axis=-1, keepdims=True) + NORM_EPS)
        h = (xv * inv) * g_ref[...] * (1.0 + sc_ref[...]) + sh_ref[...]
        h_ref[...] = h.astype(h_ref.dtype)

    return pl.pallas_call(
        body, name=name, grid=(s // ts,),
        in_specs=[_row_spec(ts, d), _vec_spec(d), _vec_spec(d), _vec_spec(d)],
        out_specs=_row_spec(ts, d), out_shape=_sds((s, d), BF16),
        compiler_params=_params(("parallel",)),
    )(x, gain, scale, shift)


def _gate_bwd(dxv, y_ref, gate_ref, dy_ref, dgate_ref):
    dy_ref[...] = (dxv * gate_ref[...]).astype(dy_ref.dtype)
    dgate_ref[...] += jnp.sum(dxv * y_ref[...].astype(F32), axis=0, keepdims=True)


def norm_mod_bwd(name, dh, x, gain, scale, dres, branch=None):
    s, d = x.shape
    ts = _tile(s, ROW_TILE)

    def body(dh_ref, x_ref, g_ref, sc_ref, dres_ref, *rest):
        if branch:
            y_ref, gate_ref, dx_ref, dsh_ref, dsc_ref, dg_ref, dy_ref, dgate_ref = rest
        else:
            dx_ref, dsh_ref, dsc_ref, dg_ref = rest

        @pl.when(pl.program_id(0) == 0)
        def _():
            dsh_ref[...] = jnp.zeros_like(dsh_ref)
            dsc_ref[...] = jnp.zeros_like(dsc_ref)
            dg_ref[...] = jnp.zeros_like(dg_ref)
            if branch:
                dgate_ref[...] = jnp.zeros_like(dgate_ref)

        xv = x_ref[...]
        dhv = dh_ref[...].astype(F32)
        g = g_ref[...]
        inv = lax.rsqrt(jnp.mean(xv * xv, axis=-1, keepdims=True) + NORM_EPS)
        n = xv * inv
        dhn = dhv * (1.0 + sc_ref[...])
        dn = dhn * g
        dx = dres_ref[...] + inv * (dn - n * jnp.mean(dn * n, axis=-1, keepdims=True))
        dx_ref[...] = dx
        dsh_ref[...] += jnp.sum(dhv, axis=0, keepdims=True)
        dsc_ref[...] += jnp.sum(dhv * (n * g), axis=0, keepdims=True)
        dg_ref[...] += jnp.sum(dhn * n, axis=0, keepdims=True)
        if branch:
            _gate_bwd(dx, y_ref, gate_ref, dy_ref, dgate_ref)

    row, vec = _row_spec(ts, d), _vec_spec(d)
    return pl.pallas_call(
        body, name=name, grid=(s // ts,),
        in_specs=[row, row, vec, vec, row] + ([row, vec] if branch else []),
        out_specs=[row, vec, vec, vec] + ([row, vec] if branch else []),
        out_shape=[_sds((s, d), F32), _sds((1, d), F32), _sds((1, d), F32), _sds((1, d), F32)]
        + ([_sds((s, d), BF16), _sds((1, d), F32)] if branch else []),
        compiler_params=_params(("arbitrary",)),
    )(dh, x, gain, scale, dres, *(branch or ()))


def final_loss(name, x, gain, target, y, gate):
    s, d = x.shape
    ts = _tile(s, ROW_TILE)

    def body(x_ref, g_ref, t_ref, y_ref, gate_ref, dx_ref, dg_ref, loss_ref, dy_ref, dgate_ref):
        @pl.when(pl.program_id(0) == 0)
        def _():
            dg_ref[...] = jnp.zeros_like(dg_ref)
            loss_ref[...] = jnp.zeros_like(loss_ref)
            dgate_ref[...] = jnp.zeros_like(dgate_ref)

        xv = x_ref[...]
        g = g_ref[...]
        inv = lax.rsqrt(jnp.mean(xv * xv, axis=-1, keepdims=True) + NORM_EPS)
        n = xv * inv
        diff = n * g - t_ref[...]
        per_tok = jnp.mean(diff * diff, axis=-1, keepdims=True)
        loss_ref[...] += 0.5 * jnp.sum(per_tok, axis=0, keepdims=True)
        dout = diff * (1.0 / d)
        dg_ref[...] += jnp.sum(dout * n, axis=0, keepdims=True)
        dn = dout * g
        dx = inv * (dn - n * jnp.mean(dn * n, axis=-1, keepdims=True))
        dx_ref[...] = dx
        _gate_bwd(dx, y_ref, gate_ref, dy_ref, dgate_ref)

    row, vec = _row_spec(ts, d), _vec_spec(d)
    return pl.pallas_call(
        body, name=name, grid=(s // ts,),
        in_specs=[row, vec, row, row, vec],
        out_specs=[row, vec, _vec_spec(128), row, vec],
        out_shape=[_sds((s, d), F32), _sds((1, d), F32), _sds((1, 128), F32), _sds((s, d), BF16), _sds((1, d), F32)],
        compiler_params=_params(("arbitrary",)),
    )(x, gain, target, y, gate)


def _hgrn_chunk_fwd(qr, fl, lbv, tri):
    sg = _sigmoid(fl)
    sgm = _sigmoid(-fl)
    f = lbv + (1.0 - lbv) * sg
    logf = jnp.log(f)
    k = (1.0 - lbv) * sgm
    cum = _dot(tri, logf, precision=lax.Precision.HIGHEST)
    cl = cum[HGRN_CHUNK - 1:HGRN_CHUNK, :]
    e = jnp.exp(cum)
    en = jnp.exp(-cum)
    es = jnp.exp(cl - cum)
    sq = _sigmoid(qr)
    qs = qr * sq
    return dict(sg=sg, sgm=sgm, f=f, k=k, cum=cum, cl=cl, e=e, en=en, es=es, sq=sq, qs=qs,
                qd=qs * e, ki=k * en, ks=k * es, dec=jnp.exp(cl))


def _tri_masks(strict=False):
    r = lax.broadcasted_iota(jnp.int32, (HGRN_CHUNK, HGRN_CHUNK), 0)
    c = lax.broadcasted_iota(jnp.int32, (HGRN_CHUNK, HGRN_CHUNK), 1)
    return (r > c) if strict else (r >= c)


def _carried(comm, grid, n_in, n_out):
    if not comm:
        return [], [], [], [], lambda refs: (refs, lambda: None, lambda: None)
    n_cin, n_cout = len(comm["operands"]), len(comm["out_shapes"])
    sems = [pltpu.SemaphoreType.DMA(comm["sems"]), pltpu.SemaphoreType.DMA(comm["sems"])]

    def split(refs):
        ins, cin = refs[:n_in], refs[n_in:n_in + n_cin]
        outs = refs[n_in + n_cin:n_in + n_cin + n_out]
        cout = refs[n_in + n_cin + n_out:n_in + n_cin + n_out + n_cout]
        scratch = refs[n_in + n_cin + n_out + n_cout:]
        send_sems, recv_sems = scratch[-2:]
        pids = [pl.program_id(ax) for ax in range(len(grid))]

        def start():
            @pl.when(functools.reduce(jnp.logical_and, [p == 0 for p in pids]))
            def _():
                comm["start"](cin, cout, send_sems, recv_sems)

        def finish():
            @pl.when(functools.reduce(jnp.logical_and, [p == n - 1 for p, n in zip(pids, grid)]))
            def _():
                comm["finish"](cin, cout, send_sems, recv_sems)

        return ins + outs + scratch[:-2], start, finish

    return [HBM] * n_cin, [HBM] * n_cout, list(comm["out_shapes"]), sems, split


def hgrn_fwd(name, proj3, lb, out_gain, comm=None):
    _, s, d = proj3.shape
    heads = d // HEAD_DIM
    t_rows = _tile(s, HGRN_ROWS)
    n_t = s // t_rows
    n_c = t_rows // HGRN_CHUNK
    width = HGRN_HEADS * HEAD_DIM
    grid = (heads // HGRN_HEADS, n_t)
    c_in, c_out, c_shapes, c_sems, split = _carried(comm, grid, 6, 3)

    def body(*refs):
        (q_ref, f_ref, i_ref, g_ref, lb_ref, gain_ref, u_ref, o_ref, st_ref, state), start, finish = split(refs)
        start()
        compute(q_ref, f_ref, i_ref, g_ref, lb_ref, gain_ref, u_ref, o_ref, st_ref, state)
        finish()

    def compute(q_ref, f_ref, i_ref, g_ref, lb_ref, gain_ref, u_ref, o_ref, st_ref, state):
        @pl.when(pl.program_id(1) == 0)
        def _():
            state[...] = jnp.zeros_like(state)

        causal = _tri_masks()
        tri = causal.astype(F32)
        hs = range(HGRN_HEADS)
        col = [pl.ds(hh * HEAD_DIM, HEAD_DIM) for hh in hs]
        for ci in range(n_c):
            rows = pl.ds(ci * HGRN_CHUNK, HGRN_CHUNK)
            c = [_hgrn_chunk_fwd(q_ref[rows, col[hh]], f_ref[rows, col[hh]], lb_ref[:, col[hh]], tri) for hh in hs]
            v = [i_ref[rows, col[hh]] for hh in hs]
            st = [state[hh] for hh in hs]
            scores = [jnp.where(causal, _bdot(c[hh]["qd"], c[hh]["ki"], NT), 0.0) for hh in hs]
            inter = [_bdot(c[hh]["qd"], st[hh], NT) for hh in hs]
            update = [_bdot(v[hh], c[hh]["ks"], TN) for hh in hs]
            o = [_bdot(scores[hh], v[hh]) + inter[hh] for hh in hs]
            for hh in hs:
                st_ref[hh, ci] = st[hh]
                state[hh] = st[hh] * c[hh]["dec"] + update[hh]
            for hh in hs:
                graw = g_ref[rows, col[hh]]
                rms = lax.rsqrt(jnp.mean(o[hh] * o[hh], axis=-1, keepdims=True) + NORM_EPS)
                u = o[hh] * rms * gain_ref[:, col[hh]] * (graw * _sigmoid(graw))
                o_ref[rows, col[hh]] = o[hh]
                u_ref[rows, col[hh]] = u.astype(u_ref.dtype)

    def pspec(blk):
        return pl.BlockSpec((None, t_rows, width), lambda h, t: (blk, t, h))

    hspec = pl.BlockSpec((1, width), lambda h, t: (0, h))
    ospec = pl.BlockSpec((t_rows, width), lambda h, t: (t, h))
    out = pl.pallas_call(
        body, name=name, grid=grid,
        in_specs=[pspec(0), pspec(1), pspec(2), pspec(3), hspec, hspec] + c_in,
        out_specs=[ospec, ospec,
                   pl.BlockSpec((HGRN_HEADS, n_c, HEAD_DIM, HEAD_DIM), lambda h, t: (h, t, 0, 0))] + c_out,
        out_shape=[_sds((s, d), BF16), _sds((s, d), F32),
                   _sds((heads, s // HGRN_CHUNK, HEAD_DIM, HEAD_DIM), F32)] + c_shapes,
        scratch_shapes=[pltpu.VMEM((HGRN_HEADS, HEAD_DIM, HEAD_DIM), F32)] + c_sems,
        compiler_params=_params(("arbitrary", "arbitrary")),
    )(proj3, proj3, proj3, proj3, lb, out_gain, *(comm["operands"] if comm else ()))
    return (out[:3], out[3:]) if comm else out


def hgrn_bwd(name, proj3, lb, out_gain, o, du, states, comm=None):
    _, s, d = proj3.shape
    heads = d // HEAD_DIM
    t_rows = _tile(s, HGRN_ROWS)
    n_t = s // t_rows
    n_c = t_rows // HGRN_CHUNK
    width = HGRN_HEADS * HEAD_DIM
    grid = (heads // HGRN_HEADS, n_t)
    c_in, c_out, c_shapes, c_sems, split = _carried(comm, grid, 9, 3)

    def body(*refs):
        refs, start, finish = split(refs)
        start()
        compute(*refs)
        finish()

    def compute(q_ref, f_ref, i_ref, g_ref, lb_ref, gain_ref, o_ref, du_ref, st_ref,
             dp_ref, dlb_ref, dgain_ref, dstate):
        @pl.when(pl.program_id(1) == 0)
        def _():
            dstate[...] = jnp.zeros_like(dstate)
            dlb_ref[...] = jnp.zeros_like(dlb_ref)
            dgain_ref[...] = jnp.zeros_like(dgain_ref)

        causal = _tri_masks()
        tri = causal.astype(F32)
        tri_t = jnp.logical_not(_tri_masks(strict=True)).astype(F32)
        hs = range(HGRN_HEADS)
        col = [pl.ds(hh * HEAD_DIM, HEAD_DIM) for hh in hs]
        lbv = [lb_ref[:, col[hh]] for hh in hs]
        gain = [gain_ref[:, col[hh]] for hh in hs]
        for ci in reversed(range(n_c)):
            rows = pl.ds(ci * HGRN_CHUNK, HGRN_CHUNK)
            qr = [q_ref[rows, col[hh]] for hh in hs]
            c = [_hgrn_chunk_fwd(qr[hh], f_ref[rows, col[hh]], lbv[hh], tri) for hh in hs]
            v = [i_ref[rows, col[hh]] for hh in hs]
            st = [st_ref[hh, ci] for hh in hs]
            dst = [dstate[hh] for hh in hs]
            do, dgraw = [], []
            for hh in hs:
                ov = o_ref[rows, col[hh]]
                duv = du_ref[rows, col[hh]].astype(F32)
                graw = g_ref[rows, col[hh]]
                sgg = _sigmoid(graw)
                gate = graw * sgg
                rms = lax.rsqrt(jnp.mean(ov * ov, axis=-1, keepdims=True) + NORM_EPS)
                on = ov * rms
                dgain_ref[:, col[hh]] += jnp.sum(duv * on * gate, axis=0, keepdims=True)
                dgraw.append(duv * on * gain[hh] * (sgg * (1.0 + graw * (1.0 - sgg))))
                don = duv * gain[hh] * gate
                do.append(rms * (don - on * jnp.mean(don * on, axis=-1, keepdims=True)))
            qd = [c[hh]["qd"] for hh in hs]
            ki = [c[hh]["ki"] for hh in hs]
            ks = [c[hh]["ks"] for hh in hs]
            p = [jnp.where(causal, _bdot(qd[hh], ki[hh], NT), 0.0) for hh in hs]
            dp = [jnp.where(causal, _bdot(do[hh], v[hh], NT), 0.0) for hh in hs]
            from_state = [_bdot(do[hh], st[hh]) for hh in hs]
            dks = [_bdot(v[hh], dst[hh]) for hh in hs]
            dv_state = [_bdot(ks[hh], dst[hh], NT) for hh in hs]
            dstate_new = [_bdot(do[hh], qd[hh], TN) for hh in hs]
            dqd = [_bdot(dp[hh], ki[hh]) + from_state[hh] for hh in hs]
            dki = [_bdot(dp[hh], qd[hh], TN) for hh in hs]
            dv = [_bdot(p[hh], do[hh], TN) + dv_state[hh] for hh in hs]
            ddec = [jnp.sum(dst[hh] * st[hh], axis=0, keepdims=True) for hh in hs]
            for hh in hs:
                dstate[hh] = dst[hh] * c[hh]["dec"] + dstate_new[hh]
            dcum = [dqd[hh] * qd[hh] - dki[hh] * ki[hh] - dks[hh] * ks[hh] for hh in hs]
            dcl = [jnp.sum(dks[hh] * ks[hh], axis=0, keepdims=True) + ddec[hh] * c[hh]["dec"] for hh in hs]
            dlogf = [_dot(tri_t, dcum[hh], precision=lax.Precision.HIGHEST) + dcl[hh] for hh in hs]
            for hh in hs:
                ch = c[hh]
                dqs = dqd[hh] * ch["e"]
                dk = dki[hh] * ch["en"] + dks[hh] * ch["es"]
                df = dlogf[hh] / ch["f"]
                sg, sgm, sq = ch["sg"], ch["sgm"], ch["sq"]
                one_m_lb = 1.0 - lbv[hh]
                dlb_ref[:, col[hh]] += jnp.sum(df * (1.0 - sg) - dk * sgm, axis=0, keepdims=True)
                dfl = df * one_m_lb * sg * (1.0 - sg) - dk * one_m_lb * sgm * (1.0 - sgm)
                dqr = dqs * (sq * (1.0 + qr[hh] * (1.0 - sq)))
                dp_ref[0, rows, col[hh]] = dqr.astype(dp_ref.dtype)
                dp_ref[1, rows, col[hh]] = dfl.astype(dp_ref.dtype)
                dp_ref[2, rows, col[hh]] = dv[hh].astype(dp_ref.dtype)
                dp_ref[3, rows, col[hh]] = dgraw[hh].astype(dp_ref.dtype)

    def pspec(blk):
        return pl.BlockSpec((None, t_rows, width), lambda h, t: (blk, n_t - 1 - t, h))

    hspec = pl.BlockSpec((1, width), lambda h, t: (0, h))
    ospec = pl.BlockSpec((t_rows, width), lambda h, t: (n_t - 1 - t, h))
    out = pl.pallas_call(
        body, name=name, grid=grid,
        in_specs=[pspec(0), pspec(1), pspec(2), pspec(3), hspec, hspec, ospec, ospec,
                  pl.BlockSpec((HGRN_HEADS, n_c, HEAD_DIM, HEAD_DIM), lambda h, t: (h, n_t - 1 - t, 0, 0))] + c_in,
        out_specs=[pl.BlockSpec((4, t_rows, width), lambda h, t: (0, n_t - 1 - t, h)), hspec, hspec] + c_out,
        out_shape=[_sds((4, s, d), BF16), _sds((1, d), F32), _sds((1, d), F32)] + c_shapes,
        scratch_shapes=[pltpu.VMEM((HGRN_HEADS, HEAD_DIM, HEAD_DIM), F32)] + c_sems,
        compiler_params=_params(("arbitrary", "arbitrary")),
    )(proj3, proj3, proj3, proj3, lb, out_gain, o, du, states, *(comm["operands"] if comm else ()))
    return (out[:3], out[3:]) if comm else out


def attn_fwd(name, q, kv, comm=None):
    s, dq = q.shape
    kvh = kv.shape[1] // (2 * HEAD_DIM)
    assert dq == kvh * KV_GROUP * HEAD_DIM
    tq = _tile(s, ATTN_TILE)
    scale = HEAD_DIM ** -0.5

    grid = (kvh, s // tq)
    c_in, c_out, c_shapes, c_sems, split = _carried(comm, grid, 3, 3)

    def body(*refs):
        (q_ref, k_ref, v_ref, o_ref, tot_ref, cnt_ref), start, finish = split(refs)
        start()
        compute(q_ref, k_ref, v_ref, o_ref, tot_ref, cnt_ref)
        finish()

    def compute(q_ref, k_ref, v_ref, o_ref, tot_ref, cnt_ref):
        i = pl.program_id(1)
        heads = range(KV_GROUP)
        qs = [q_ref[:, g * HEAD_DIM:(g + 1) * HEAD_DIM] for g in heads]
        r_i = lax.broadcasted_iota(jnp.int32, (tq, tq), 0)
        c_i = lax.broadcasted_iota(jnp.int32, (tq, tq), 1)
        later = (r_i > c_i).astype(BF16)
        later2 = jnp.concatenate([later, later], axis=0)
        mask = c_i < r_i
        ones2 = jnp.ones((8, 2 * tq), BF16)

        def block(j, carry, masked):
            rows = pl.ds(pl.multiple_of(j * tq, tq), tq)
            kj = k_ref[rows, :]
            vj = v_ref[rows, :]
            z = [_dot(qs[g], kj, NT) * scale for g in heads]
            lbeta = [_log_sigmoid(z[g]) for g in heads]
            lrest = [lbeta[g] - z[g] for g in heads]
            if masked:
                lrest = [jnp.where(mask, lrest[g], 0.0) for g in heads]
            hl = [jnp.concatenate(_split_bf16(lrest[g]), axis=1) for g in heads]
            between = [_dot(hl[g], later2) + carry[g][0] for g in heads]
            sums = [_dot(ones2, hl[g], NT) for g in heads]
            w = [jnp.exp(lbeta[g] + between[g]) for g in heads]
            if masked:
                w = [jnp.where(mask, w[g], 0.0) for g in heads]
            pv = [_dot(w[g].astype(BF16), vj) for g in heads]
            return tuple((carry[g][0] + jnp.sum(lrest[g], axis=1, keepdims=True), carry[g][1] + pv[g],
                          carry[g][2] + sums[g]) for g in heads)

        def alive(carry):
            top = carry[0][0]
            for g in heads[1:]:
                top = jnp.maximum(top, carry[g][0])
            return jnp.max(top) > DEAD_LOG_WEIGHT

        zero = (jnp.zeros((tq, 1), F32), jnp.zeros((tq, HEAD_DIM), F32), jnp.zeros((8, tq), F32))
        carry = block(i, (zero,) * KV_GROUP, True)

        def step(state):
            jj, _, cr = state
            cr = block(i - 1 - jj, cr, False)
            return jj + 1, alive(cr), cr

        done, _, carry = lax.while_loop(lambda st: jnp.logical_and(st[0] < i, st[1]), step,
                                        (jnp.int32(0), alive(carry), carry))
        for g in heads:
            o_ref[:, g * HEAD_DIM:(g + 1) * HEAD_DIM] = carry[g][1]
            tot_ref[g] = carry[g][2][0:1, :]
        cnt_ref[pl.program_id(0), i] = done

    group = KV_GROUP * HEAD_DIM
    out = pl.pallas_call(
        body, name=name, grid=grid,
        in_specs=[pl.BlockSpec((tq, group), lambda kh, i: (i, kh)),
                  pl.BlockSpec((s, HEAD_DIM), lambda kh, i: (0, kh)),
                  pl.BlockSpec((s, HEAD_DIM), lambda kh, i: (0, kvh + kh))] + c_in,
        out_specs=[pl.BlockSpec((tq, group), lambda kh, i: (i, kh)),
                   pl.BlockSpec((KV_GROUP, 1, tq), lambda kh, i: (kh, 0, i)),
                   pl.BlockSpec(memory_space=pltpu.SMEM)] + c_out,
        out_shape=[_sds((s, dq), F32), _sds((dq // HEAD_DIM, 1, s), F32),
                   _sds((kvh, s // tq), jnp.int32)] + c_shapes,
        scratch_shapes=c_sems,
        compiler_params=_params(("arbitrary", "arbitrary")),
    )(q, kv, kv, *(comm["operands"] if comm else ()))
    return (out[:3], out[3:]) if comm else out


def attn_bwd(name, q, kv, totals, visited, do):
    s, dq_cols = q.shape
    kvh = kv.shape[1] // (2 * HEAD_DIM)
    tq = _tile(s, ATTN_TILE)
    scale = HEAD_DIM ** -0.5

    def body(cnt_ref, q_ref, k_ref, v_ref, tot_ref, do_ref, dq_ref, dkv_ref):
        i = pl.program_id(1)
        first = i - jnp.clip(cnt_ref[pl.program_id(0), i], 0, i)

        @pl.when(i == 0)
        def _():
            dkv_ref[...] = jnp.zeros_like(dkv_ref)

        heads = range(KV_GROUP)
        qs = [q_ref[:, g * HEAD_DIM:(g + 1) * HEAD_DIM] for g in heads]
        dobs = [do_ref[:, g * HEAD_DIM:(g + 1) * HEAD_DIM].astype(BF16) for g in heads]
        tots = [tot_ref[g] for g in heads]
        q_all = jnp.concatenate(qs, axis=0)
        do_all = jnp.concatenate(dobs, axis=0)
        r_i = lax.broadcasted_iota(jnp.int32, (tq, tq), 0)
        c_i = lax.broadcasted_iota(jnp.int32, (tq, tq), 1)
        upto = (c_i <= r_i).astype(BF16)
        before = (c_i < r_i).astype(BF16)
        upto2 = jnp.concatenate([upto, upto], axis=1)
        before2 = jnp.concatenate([before, before], axis=1)
        mask = r_i < c_i

        def block(j, carry, masked):
            rows = pl.ds(pl.multiple_of(j * tq, tq), tq)
            kj = k_ref[rows, :]
            vj = v_ref[rows, :]
            zt = [_dot(kj, qs[g], NT) * scale for g in heads]
            dwt = [_dot(vj, dobs[g], NT) for g in heads]
            lbeta = [_log_sigmoid(zt[g]) for g in heads]
            lrest_raw = [lbeta[g] - zt[g] for g in heads]
            lrest = [jnp.where(mask, lrest_raw[g], 0.0) for g in heads] if masked else lrest_raw
            hl = [jnp.concatenate(_split_bf16(lrest[g]), axis=0) for g in heads]
            upto_sum = [_dot(upto2, hl[g]) for g in heads]
            wt = [jnp.exp(lbeta[g] + (tots[g] - carry[g][0] - upto_sum[g])) for g in heads]
            if masked:
                wt = [jnp.where(mask, wt[g], 0.0) for g in heads]
            dat = [dwt[g] * wt[g] for g in heads]
            earlier = [_dot(before2, jnp.concatenate(_split_bf16(dat[g]), axis=0)) for g in heads]
            dzt = [dat[g] * jnp.exp(lrest_raw[g]) - (carry[g][1] + earlier[g]) * jnp.exp(lbeta[g]) for g in heads]
            if masked:
                dzt = [jnp.where(mask, dzt[g], 0.0) for g in heads]
            dzb = [(dzt[g] * scale).astype(BF16) for g in heads]
            dq_new = [_dot(dzb[g], kj, TN) for g in heads]
            dkv_ref[0, rows, :] += _dot(jnp.concatenate(dzb, axis=1), q_all)
            dkv_ref[1, rows, :] += _dot(jnp.concatenate([wt[g].astype(BF16) for g in heads], axis=1), do_all)
            return tuple((carry[g][0] + jnp.sum(lrest[g], axis=0, keepdims=True),
                          carry[g][1] + jnp.sum(dat[g], axis=0, keepdims=True),
                          carry[g][2] + dq_new[g]) for g in heads)

        zrow = jnp.zeros((1, tq), F32)
        carry = ((zrow, zrow, jnp.zeros((tq, HEAD_DIM), F32)),) * KV_GROUP
        carry = lax.fori_loop(first, i, lambda j, cr: block(j, cr, False), carry)
        carry = block(i, carry, True)
        for g in heads:
            dq_ref[:, g * HEAD_DIM:(g + 1) * HEAD_DIM] = carry[g][2].astype(dq_ref.dtype)

    group = KV_GROUP * HEAD_DIM
    qspec = pl.BlockSpec((tq, group), lambda kh, i, cnt: (i, kh))
    return pl.pallas_call(
        body, name=name,
        grid_spec=pltpu.PrefetchScalarGridSpec(
            num_scalar_prefetch=1, grid=(kvh, s // tq),
            in_specs=[qspec,
                      pl.BlockSpec((s, HEAD_DIM), lambda kh, i, cnt: (0, kh)),
                      pl.BlockSpec((s, HEAD_DIM), lambda kh, i, cnt: (0, kvh + kh)),
                      pl.BlockSpec((KV_GROUP, 1, tq), lambda kh, i, cnt: (kh, 0, i)),
                      qspec],
            out_specs=[qspec, pl.BlockSpec((2, s, HEAD_DIM), lambda kh, i, cnt: (0, 0, kh))]),
        out_shape=[_sds((s, dq_cols), BF16), _sds((2, s, kvh * HEAD_DIM), F32)],
        compiler_params=_params(("parallel", "arbitrary")),
    )(visited, q, kv, kv, totals, do)


def ada_project(name, c_all, w, b):
    bsz, d = c_all.shape
    n = w.shape[1]
    tn = _tile(n, 512)

    def body(c_ref, w_ref, b_ref, o_ref):
        cv = c_ref[...]
        act = cv * _sigmoid(cv)
        o_ref[...] = _bdot(act, w_ref[...]) + b_ref[...]

    return pl.pallas_call(
        body, name=name, grid=(n // tn,),
        in_specs=[pl.BlockSpec((bsz, d), lambda i: (0, 0)),
                  pl.BlockSpec((d, tn), lambda i: (0, i)),
                  pl.BlockSpec((1, tn), lambda i: (0, i))],
        out_specs=pl.BlockSpec((bsz, tn), lambda i: (0, i)),
        out_shape=_sds((bsz, n), F32),
        compiler_params=_params(("parallel",)),
    )(c_all, w, b)


def _adamw_math(w, g, m, v):
    m = ADAM_B1 * m + (1.0 - ADAM_B1) * g
    v = ADAM_B2 * v + (1.0 - ADAM_B2) * (g * g)
    m_hat = m / (1.0 - ADAM_B1 ** ADAM_STEP)
    v_hat = v / (1.0 - ADAM_B2 ** ADAM_STEP)
    delta = -ADAM_LR * (m_hat / (jnp.sqrt(v_hat) + ADAM_EPS) + ADAM_WD * w)
    return delta, m, v


def adamw(name, w, g, m, v):
    r, c = w.shape
    tr = _tile(r, 256)
    tc = _tile(c, 2048)

    def body(w_ref, g_ref, m_ref, v_ref, d_ref, mo_ref, vo_ref):
        delta, mn, vn = _adamw_math(w_ref[...], g_ref[...], m_ref[...], v_ref[...])
        d_ref[...] = delta
        mo_ref[...] = mn
        vo_ref[...] = vn

    spec = pl.BlockSpec((tr, tc), lambda i, j: (i, j))
    return pl.pallas_call(
        body, name=name, grid=(r // tr, c // tc),
        in_specs=[spec] * 4, out_specs=[spec] * 3,
        out_shape=[_sds((r, c), F32)] * 3,
        compiler_params=_params(("parallel", "parallel")),
    )(w, g, m, v)


def ada_grad_adamw(name, c_t, dmod, w, m, v):
    layers, d, n = w.shape
    tr = _tile(d, 256)
    tc = _tile(n, 512)

    def body(a_ref, dm_ref, w_ref, m_ref, v_ref, g_ref, d_ref, mo_ref, vo_ref):
        cv = a_ref[...]
        g = _bdot(cv * _sigmoid(cv), dm_ref[...])
        delta, mn, vn = _adamw_math(w_ref[...], g, m_ref[...], v_ref[...])
        g_ref[...] = g
        d_ref[...] = delta
        mo_ref[...] = mn
        vo_ref[...] = vn

    spec = pl.BlockSpec((None, tr, tc), lambda l, i, j: (l, i, j))
    return pl.pallas_call(
        body, name=name, grid=(layers, d // tr, n // tc),
        in_specs=[pl.BlockSpec((tr, 128), lambda l, i, j: (i, 0)),
                  pl.BlockSpec((None, 128, tc), lambda l, i, j: (l, 0, j)), spec, spec, spec],
        out_specs=[spec] * 4, out_shape=[_sds((layers, d, n), F32)] * 4,
        compiler_params=_params(("parallel", "parallel", "parallel")),
    )(c_t, dmod, w, m, v)


def device_sum(name, gathered):
    _, r, c = gathered.shape

    def body(g_ref, o_ref):
        acc = g_ref[0]
        for dev in range(1, N_DEV):
            acc = acc + g_ref[dev]
        o_ref[...] = acc

    return pl.pallas_call(
        body, name=name,
        in_specs=[pl.BlockSpec(memory_space=pltpu.VMEM)],
        out_specs=pl.BlockSpec(memory_space=pltpu.VMEM),
        out_shape=_sds((r, c), F32),
    )(gathered)


def lower_bound_fwd(name, logits):
    _, d = logits.shape

    def body(l_ref, o_ref):
        l0 = l_ref[0:1, :]
        l1 = l_ref[1:2, :]
        mx = jnp.maximum(l0, l1)
        e0 = jnp.exp(l0 - mx)
        e1 = jnp.exp(l1 - mx)
        o_ref[...] = e0 / (e0 + e1)

    return pl.pallas_call(
        body, name=name,
        in_specs=[pl.BlockSpec(memory_space=pltpu.VMEM)],
        out_specs=pl.BlockSpec(memory_space=pltpu.VMEM),
        out_shape=_sds((1, d), F32),
    )(logits)


def lower_bound_bwd(name, logits, dlb):
    _, d = logits.shape

    def body(l_ref, dlb_ref, o_ref):
        l0 = l_ref[0:1, :]
        l1 = l_ref[1:2, :]
        mx = jnp.maximum(l0, l1)
        e0 = jnp.exp(l0 - mx)
        e1 = jnp.exp(l1 - mx)
        p0 = e0 / (e0 + e1)
        p1 = e1 / (e0 + e1)
        g = dlb_ref[...] * p0 * p1
        o_ref[0:1, :] = g
        o_ref[1:2, :] = -g

    return pl.pallas_call(
        body, name=name,
        in_specs=[pl.BlockSpec(memory_space=pltpu.VMEM)] * 2,
        out_specs=pl.BlockSpec(memory_space=pltpu.VMEM),
        out_shape=_sds((2, d), F32),
    )(logits, dlb)


HBM = pl.BlockSpec(memory_space=pltpu.HBM)


def _position():
    return lax.axis_index("x"), lax.axis_index("y"), lax.axis_index("c")


def _remote(src, dst, send_sem, recv_sem, device):
    return pltpu.make_async_remote_copy(src_ref=src, dst_ref=dst, send_sem=send_sem, recv_sem=recv_sem,
                                        device_id=device, device_id_type=MESH)


def allgather8(name, x):
    r, c = x.shape

    def body(x_ref, out_ref, send_sems, recv_sems):
        px, py, pc = _position()
        me = 4 * px + 2 * py + pc
        out_ref[me] = x_ref[...]
        copies = []
        for k in range(1, N_DEV):
            peer = (1 - px if k & 4 else px, 1 - py if k & 2 else py, 1 - pc if k & 1 else pc)
            cp = _remote(x_ref, out_ref.at[me], send_sems.at[k - 1], recv_sems.at[k - 1], peer)
            cp.start()
            copies.append(cp)
        for cp in copies:
            cp.wait()

    return pl.pallas_call(
        body, name=name,
        in_specs=[pl.BlockSpec(memory_space=pltpu.VMEM)],
        out_specs=pl.BlockSpec(memory_space=pltpu.VMEM),
        out_shape=_sds((N_DEV, r, c), x.dtype),
        scratch_shapes=[pltpu.SemaphoreType.DMA((N_DEV - 1,)), pltpu.SemaphoreType.DMA((N_DEV - 1,))],
    )(x)


def _other_chips(px, py):
    return [(1 - px, py), (px, 1 - py), (1 - px, 1 - py)]


def gather_weights(name, shards):
    n = len(shards)
    hook = gather_hook(shards)

    def body(*refs):
        ins, outs = refs[:n], refs[n:2 * n]
        send_sems, recv_sems = refs[2 * n:]
        hook["start"](ins, outs, send_sems, recv_sems)
        hook["finish"](ins, outs, send_sems, recv_sems)

    return pl.pallas_call(
        body, name=name,
        in_specs=[HBM] * n, out_specs=[HBM] * n,
        out_shape=hook["out_shapes"],
        scratch_shapes=[pltpu.SemaphoreType.DMA(hook["sems"]), pltpu.SemaphoreType.DMA(hook["sems"])],
    )(*shards)


def gather_hook(shards):
    n = len(shards)

    def copies(ins, outs, send_sems, recv_sems):
        px, py, pc = _position()
        chip = 2 * px + py
        sibling = (px, py, 1 - pc)
        own, sends, arrivals, passes, pass_arrivals = [], [], [], [], []
        for a in range(n):
            rows = ins[a].shape[0] // 2
            mine, theirs = pl.ds(pc * rows, rows), pl.ds((1 - pc) * rows, rows)
            own.append(_remote(ins[a], outs[a].at[chip], send_sems.at[a, 6], recv_sems.at[a, 6], sibling))
            for j, (ox, oy) in enumerate(_other_chips(px, py)):
                sends.append(_remote(ins[a].at[mine], outs[a].at[chip, mine],
                                     send_sems.at[a, j], recv_sems.at[a, j], (ox, oy, pc)))
                landed = outs[a].at[2 * ox + oy, mine]
                arrivals.append(_remote(landed, landed, send_sems.at[a, j], recv_sems.at[a, j], sibling))
                passes.append(_remote(landed, landed, send_sems.at[a, 3 + j], recv_sems.at[a, 3 + j], sibling))
                via = outs[a].at[2 * ox + oy, theirs]
                pass_arrivals.append(_remote(via, via, send_sems.at[a, 3 + j], recv_sems.at[a, 3 + j], sibling))
        return own, sends, arrivals, passes, pass_arrivals

    def start(ins, outs, send_sems, recv_sems):
        own, sends, _, _, _ = copies(ins, outs, send_sems, recv_sems)
        for cp in own + sends:
            cp.start()

    def finish(ins, outs, send_sems, recv_sems):
        own, sends, arrivals, passes, pass_arrivals = copies(ins, outs, send_sems, recv_sems)
        for arrival, onward in zip(arrivals, passes):
            arrival.wait_recv()
            onward.start()
        for arrival in pass_arrivals:
            arrival.wait_recv()
        for cp in sends + passes:
            cp.wait_send()
        for cp in own:
            cp.wait()

    return dict(operands=list(shards), out_shapes=[_sds((N_CHIPS,) + s.shape, s.dtype) for s in shards],
                sems=(n, 7), start=start, finish=finish)


def sibling_exchange(name, grads):
    n = len(grads)

    def body(*refs):
        ins, outs = refs[:n], refs[n:2 * n]
        send_sems, recv_sems = refs[2 * n:]
        px, py, pc = _position()
        sibling = (px, py, 1 - pc)
        copies = []
        for a in range(n):
            rows = ins[a].shape[1] // 2
            src = ins[a].at[:, pl.ds((1 - pc) * rows, rows), :]
            cp = _remote(src, outs[a], send_sems.at[a], recv_sems.at[a], sibling)
            cp.start()
            copies.append(cp)
        for cp in copies:
            cp.wait()

    return pl.pallas_call(
        body, name=name,
        in_specs=[HBM] * n, out_specs=[HBM] * n,
        out_shape=[_sds((g.shape[0], g.shape[1] // 2, g.shape[2]), g.dtype) for g in grads],
        scratch_shapes=[pltpu.SemaphoreType.DMA((n,)), pltpu.SemaphoreType.DMA((n,))],
    )(*grads)


def pair_add(name, g, recv, core):
    nb, rows, cols = g.shape
    half = rows // 2
    tr = _tile(half, 256)
    steps = half // tr

    def body(core_ref, g_ref, r_ref, o_ref):
        del core_ref
        o_ref[...] = (g_ref[...].astype(F32) + r_ref[...].astype(F32)).astype(o_ref.dtype)

    return pl.pallas_call(
        body, name=name,
        grid_spec=pltpu.PrefetchScalarGridSpec(
            num_scalar_prefetch=1, grid=(nb, steps),
            in_specs=[pl.BlockSpec((None, tr, cols), lambda j, i, core_ref: (j, core_ref[0] * steps + i, 0)),
                      pl.BlockSpec((None, tr, cols), lambda j, i, core_ref: (j, i, 0))],
            out_specs=pl.BlockSpec((None, tr, cols), lambda j, i, core_ref: (j, i, 0))),
        out_shape=_sds((nb, half, cols), g.dtype),
        compiler_params=_params(("parallel", "parallel")),
    )(core, g, recv)


def scatter_hook(parts):
    n = len(parts)

    def copies(ins, outs, send_sems, recv_sems):
        px, py, pc = _position()
        return [_remote(ins[a].at[2 * ox + oy], outs[a].at[j], send_sems.at[a, j], recv_sems.at[a, j], (ox, oy, pc))
                for a in range(n) for j, (ox, oy) in enumerate(_other_chips(px, py))]

    def start(ins, outs, send_sems, recv_sems):
        for cp in copies(ins, outs, send_sems, recv_sems):
            cp.start()

    def finish(ins, outs, send_sems, recv_sems):
        for cp in copies(ins, outs, send_sems, recv_sems):
            cp.wait()

    return dict(operands=list(parts), out_shapes=[_sds((N_CHIPS - 1,) + p.shape[1:], p.dtype) for p in parts],
                sems=(n, 3), start=start, finish=finish)


def chip_sum(name, part, recv, where, out_shape, lead, dest=None):
    _, half, cols = part.shape
    tr = _tile(half, 256)
    steps = half // tr

    def body(where_ref, p_ref, r_ref, *rest):
        o_ref = rest[-1]
        acc = p_ref[...].astype(F32)
        for j in range(N_CHIPS - 1):
            acc = acc + r_ref[j].astype(F32)
        o_ref[...] = acc

    if lead is None:
        ospec = pl.BlockSpec((tr, cols), lambda i, w: (w[1] * steps + i, 0))
    else:
        ospec = pl.BlockSpec((None, tr, cols), lambda i, w: (lead, w[1] * steps + i, 0))
    in_specs = [pl.BlockSpec((None, tr, cols), lambda i, w: (w[0], i, 0)),
                pl.BlockSpec((N_CHIPS - 1, tr, cols), lambda i, w: (0, i, 0))]
    operands = [where, part, recv]
    aliases = {}
    if dest is not None:
        in_specs.append(pl.BlockSpec(memory_space=pl.ANY))
        operands.append(dest)
        aliases = {3: 0}
    return pl.pallas_call(
        body, name=name,
        grid_spec=pltpu.PrefetchScalarGridSpec(num_scalar_prefetch=1, grid=(steps,),
                                               in_specs=in_specs, out_specs=ospec),
        out_shape=out_shape, input_output_aliases=aliases,
        compiler_params=_params(("parallel",)),
    )(*operands)


def sibling_share(name, slabs, places):
    n = len(slabs)
    k = len(places)

    def body(*refs):
        outs = refs[n:2 * n]
        send_sems, recv_sems = refs[2 * n:]
        px, py, pc = _position()
        sibling = (px, py, 1 - pc)
        copies = []
        for a, (oi, lead, half) in enumerate(places):
            slab = outs[oi] if lead is None else outs[oi].at[lead]
            mine = slab.at[pl.ds(pc * half, half)]
            theirs = slab.at[pl.ds((1 - pc) * half, half)]
            cp = _remote(mine, mine, send_sems.at[a], recv_sems.at[a], sibling)
            cp.start()
            copies.append((cp, _remote(theirs, theirs, send_sems.at[a], recv_sems.at[a], sibling)))
        for cp, arrival in copies:
            cp.wait_send()
            arrival.wait_recv()

    return pl.pallas_call(
        body, name=name,
        in_specs=[HBM] * n, out_specs=[HBM] * n,
        out_shape=[_sds(s.shape, s.dtype) for s in slabs],
        input_output_aliases={a: a for a in range(n)},
        scratch_shapes=[pltpu.SemaphoreType.DMA((k,)), pltpu.SemaphoreType.DMA((k,))],
    )(*slabs)


def _row(a, i):
    return a[i:i + 1]


def _relu_sq(acc):
    r = jnp.maximum(acc, 0.0)
    return r, r * r


def _residual(acc, res, gate):
    return res + gate * acc, acc


def _with_comm(res, comm):
    return res if comm else (res, [])


def _blocked(g):
    return g if g.ndim == 3 else g.reshape(N_CHIPS, g.shape[0] // N_CHIPS, g.shape[1])


def _pre_reduce(tag, named, core):
    glist = [_blocked(g) for _, g in named]
    recv = sibling_exchange("grad_exchange_" + tag, glist)
    return [pair_add("grad_pair_add_" + k, g, r, core) for (k, _), g, r in zip(named, glist, recv)]


def _mlp_fwd(tag, x, gain, mod, w1, w2=None, w2_shard=None, down_comm=None):
    h = norm_mod_fwd(tag + "_mlp_norm", x, gain, _row(mod, 4), _row(mod, 3))
    up_comm = None if w2 is not None else gather_hook([w2_shard])
    (r3, a3), got = _with_comm(mm_nn_b(tag + "_mlp_up", h, w1, [BF16, BF16], _relu_sq, comm=up_comm), up_comm)
    if w2 is None:
        w2 = got[0]
    (x_out, m), got = _with_comm(mm_nn_r(tag + "_mlp_down", a3, w2, [F32, BF16], _residual,
                                         (x, _row(mod, 5)), ("tile", "row"), comm=down_comm), down_comm)
    return x_out, (h, r3, a3, m), w2, got


def _mlp_bwd(tag, dx_out, dm, x, gain, mod, w1, w2, saved, core, branch, down_comm=None, w2_comm=None):
    h, r3, a3, _ = saved
    (dz3,), got_down = _with_comm(mm_nt_b(tag + "_mlp_down_dgrad", dm, w2, [BF16],
                                          lambda acc, r: (acc * (2.0 * r.astype(F32)),), (r3,), comm=down_comm),
                                  down_comm)
    gw2, got_w2 = _with_comm(mm_tn(tag + "_mlp_w2_grad", a3, dm[None], BF16, comm=w2_comm), w2_comm)
    part2 = _pre_reduce(tag + "_w2", [(tag + "_w2", gw2)], core)
    gw1 = mm_tn(tag + "_mlp_w1_grad", h[None], dz3, BF16)
    dh, recv2 = mm_nt_r(tag + "_mlp_up_dgrad", dz3, w1, F32, comm=scatter_hook(part2))
    part1 = _pre_reduce(tag + "_w1", [(tag + "_w1", gw1)], core)
    dx, dsh, dsc, dgain, dy, dgate = norm_mod_bwd(tag + "_mlp_norm_bwd", dh, x, gain, _row(mod, 4), dx_out, branch)
    return dx, (dy, dgate), part1[0], (part2[0], recv2[0]), (dsh, dsc), dgain, got_down, got_w2


def local_step(x, target, mod0, mod1, kvmod, norm_mix, norm_mlp, kv_norm, final_norm, lb, out_gain,
               w, shards, core):
    w = dict(w)

    def flat(g):
        return g.reshape(-1, g.shape[-1])

    h1 = norm_mod_fwd("l0_mix_norm", x, _row(norm_mix, 0), _row(mod0, 1), _row(mod0, 0))
    (proj3,), (w["w1_0"],) = mm_nn_b("l0_in_proj", h1, w["a_in"], [F32], comm=gather_hook([shards["w1_0"]]))
    (u, o_h, states), got = hgrn_fwd("l0_hgrn_fwd", proj3, lb, out_gain,
                                     comm=gather_hook([shards["bq"], shards["kv"], shards["bo"]]))
    w["bq"], w["kv"], w["bo"] = (flat(g) for g in got)
    x1, y0 = mm_nn_b("l0_out_proj", u, w["a_out"][None], [F32, BF16], _residual,
                     (x[None], _row(mod0, 2)), ("tile", "row"))
    x1, y0 = x1[0], y0[0]
    x2, mlp0, w["w2_0"], (w["w1_1"],) = _mlp_fwd(
        "l0", x1, _row(norm_mlp, 0), mod0, w["w1_0"], w2_shard=shards["w2_0"],
        down_comm=gather_hook([shards["w1_1"]]))
    hk = norm_mod_fwd("kv_norm", x2, kv_norm, _row(kvmod, 1), _row(kvmod, 0))
    kv = mm_nn_b("kv_proj", hk, w["kv"][None], [BF16])[0][0]
    h3 = norm_mod_fwd("l1_mix_norm", x2, _row(norm_mix, 1), _row(mod1, 1), _row(mod1, 0))
    q = mm_nn_b("l1_q_proj", h3, w["bq"][None], [BF16])[0][0]
    (o_a, totals, visited), (w["w2_1"],) = attn_fwd("l1_attn_fwd", q, kv, comm=gather_hook([shards["w2_1"]]))
    x3, y1 = mm_nn_b("l1_out_proj", o_a, w["bo"][None], [F32, BF16], _residual,
                     (x2[None], _row(mod1, 2)), ("tile", "row"))
    x3, y1 = x3[0], y1[0]
    x4, mlp1, _, _ = _mlp_fwd("l1", x3, _row(norm_mlp, 1), mod1, w["w1_1"], w["w2_1"])
    dx4, d_final, loss, dm1, dgate_mlp1 = final_loss("final_loss", x4, final_norm, target, mlp1[3], _row(mod1, 5))

    reduce = {}
    dx3, (dy1, dgate1), part_w1_1, reduce["w2_1"], dmlp1, d_nmlp1, _, _ = _mlp_bwd(
        "l1", dx4, dm1, x3, _row(norm_mlp, 1), mod1, w["w1_1"], w["w2_1"], mlp1, core, (y1, _row(mod1, 2)))
    do_a = mm_nt_b("l1_out_dgrad", dy1, w["bo"][None], [F32])[0][0]
    g_bo = mm_tn("l1_out_grad", o_a[None], dy1[None], BF16)[0]
    dq, dkv3 = attn_bwd("l1_attn_bwd", q, kv, totals, visited, do_a)
    g_bq = mm_tn("l1_q_grad", h3[None], dq[None], BF16)[0]
    dh3 = mm_nt_b("l1_q_dgrad", dq, w["bq"][None], [F32])[0][0]
    dx2, dsh, dsc, d_nmix1 = norm_mod_bwd("l1_mix_norm_bwd", dh3, x2, _row(norm_mix, 1), _row(mod1, 1), dx3)
    dmod1 = jnp.concatenate([dsh, dsc, dgate1, *dmlp1, dgate_mlp1], axis=0)
    dkv = jnp.concatenate([dkv3[0], dkv3[1]], axis=1).astype(BF16)
    g_kv = mm_tn("kv_grad", hk[None], dkv[None], BF16)[0]
    dhk = mm_nt_b("kv_dgrad", dkv, w["kv"][None], [F32])[0][0]
    dx2, dsh, dsc, d_nkv, dm0, dgate_mlp0 = norm_mod_bwd("kv_norm_bwd", dhk, x2, kv_norm, _row(kvmod, 1), dx2,
                                                         (mlp0[3], _row(mod0, 5)))
    dkvmod = jnp.concatenate([dsh, dsc], axis=0)
    attn_parts = _pre_reduce("attn", [("bo", g_bo), ("bq", g_bq), ("kv", g_kv)], core)
    dx1, (dy0, dgate0), part_w1_0, reduce["w2_0"], dmlp0, d_nmlp0, (recv,), attn_recv = _mlp_bwd(
        "l0", dx2, dm0, x1, _row(norm_mlp, 0), mod0, w["w1_0"], w["w2_0"], mlp0, core, (y0, _row(mod0, 2)),
        scatter_hook([part_w1_1]), scatter_hook(attn_parts))
    reduce["w1_1"] = (part_w1_1, recv)
    for k, part, recv in zip(("bo", "bq", "kv"), attn_parts, attn_recv):
        reduce[k] = (part, recv)
    du = mm_nt_b("l0_out_dgrad", dy0, w["a_out"][None], [F32])[0][0]
    g_a_out = mm_tn("l0_out_grad", u[None], dy0[None], BF16)[0]
    (dproj3, d_lb, d_out_gain), (recv,) = hgrn_bwd("l0_hgrn_bwd", proj3, lb, out_gain, o_h, du, states,
                                                  comm=scatter_hook([part_w1_0]))
    reduce["w1_0"] = (part_w1_0, recv)
    part = _pre_reduce("a_out", [("a_out", g_a_out)], core)
    g_a_in, recv = mm_tn("l0_in_grad", h1[None], dproj3, BF16, comm=scatter_hook(part))
    reduce["a_out"] = (part[0], recv[0])
    part = _pre_reduce("a_in", [("a_in", g_a_in)], core)
    dh1, recv = mm_nt_r("l0_in_dgrad", dproj3, w["a_in"], F32, comm=scatter_hook(part))
    reduce["a_in"] = (part[0], recv[0])
    dx0, dsh, dsc, d_nmix0 = norm_mod_bwd("l0_mix_norm_bwd", dh1, x, _row(norm_mix, 0), _row(mod0, 1), dx1)
    dmod0 = jnp.concatenate([dsh, dsc, dgate0, *dmlp0, dgate_mlp0], axis=0)
    small = dict(norm_mix=(d_nmix0, d_nmix1), norm_mlp=(d_nmlp0, d_nmlp1), kv_norm=d_nkv,
                 final_norm=d_final, lb=d_lb, out_gain=d_out_gain)
    return loss, dx0, reduce, dmod0, dmod1, dkvmod, small


BIG = ("a_in", "w1_0", "w1_1", "w2_0", "w2_1", "a_out", "bq", "bo", "kv")


def kernel(x, c, ada_w, ada_b, norm_mix, norm_mlp, a_w_in, a_lb_logits, a_out_gain, a_w_out, kv_ada_w, kv_ada_b, kv_norm, w_kv, b_w_q, b_w_out, mlp_w1, mlp_w2, final_norm, loss_target, m_ada_w, m_ada_b, m_norm_mix, m_norm_mlp, m_a_w_in, m_a_lb_logits, m_a_out_gain, m_a_w_out, m_kv_ada_w, m_kv_ada_b, m_kv_norm, m_w_kv, m_b_w_q, m_b_w_out, m_mlp_w1, m_mlp_w2, m_final_norm, v_ada_w, v_ada_b, v_norm_mix, v_norm_mlp, v_a_w_in, v_a_lb_logits, v_a_out_gain, v_a_w_out, v_kv_ada_w, v_kv_ada_b, v_kv_norm, v_w_kv, v_b_w_q, v_b_w_out, v_mlp_w1, v_mlp_w2, v_final_norm):
    d = x.shape[-1]
    px, py, pc = _position()
    me = 4 * px + 2 * py + pc
    chip = 2 * px + py
    n_ada = ada_w.shape[2]
    n_kvada = kv_ada_w.shape[1]
    shard_cols = d // N_CHIPS

    def as_rows(a):
        return a.reshape(-1, shard_cols)

    pack1 = jnp.concatenate([as_rows(c), a_lb_logits, a_out_gain,
                             jnp.zeros((1, shard_cols), F32)], axis=0)
    got1 = allgather8("gather_cond", pack1)
    n_c = d // shard_cols
    c_all = got1[:, :n_c, :].reshape(N_DEV, d)
    per_chip = got1[0::2]
    logits_full = jnp.swapaxes(per_chip[:, n_c:n_c + 2, :], 0, 1).reshape(2, d)
    out_gain_full = per_chip[:, n_c + 2, :].reshape(1, d)
    lb = lower_bound_fwd("lower_bound", logits_full)

    bias0 = lax.dynamic_slice_in_dim(ada_b, chip * n_ada, n_ada, axis=1)
    bias_kv = lax.dynamic_slice_in_dim(kv_ada_b.reshape(1, -1), chip * n_kvada, n_kvada, axis=1)
    mod_part = jnp.concatenate([
        ada_project("ada_proj_0", c_all, ada_w[0], bias0[0:1]),
        ada_project("ada_proj_1", c_all, ada_w[1], bias0[1:2]),
        ada_project("ada_proj_kv", c_all, kv_ada_w, bias_kv)], axis=1)
    got2 = allgather8("gather_mod", mod_part)
    mine = lax.dynamic_index_in_dim(got2[0::2], me, axis=1, keepdims=False)
    mod0 = mine[:, :n_ada].reshape(6, d)
    mod1 = mine[:, n_ada:2 * n_ada].reshape(6, d)
    kvmod = mine[:, 2 * n_ada:].reshape(2, d)

    shards = dict(a_in=a_w_in[0], w1_0=mlp_w1[0], w1_1=mlp_w1[1], w2_0=mlp_w2[0], w2_1=mlp_w2[1],
                  a_out=a_w_out[0], bq=b_w_q[0], bo=b_w_out[0], kv=w_kv)
    shards = {k: s.astype(BF16) for k, s in shards.items()}
    g_in, g_out = gather_weights("gather_first_weights", [shards["a_in"], shards["a_out"]])
    core = pc.astype(jnp.int32).reshape(1)

    loss, dx0, reduce, dmod0, dmod1, dkvmod, small = local_step(
        x[0], loss_target[0], mod0, mod1, kvmod, norm_mix, norm_mlp, kv_norm.reshape(1, d),
        final_norm.reshape(1, d), lb, out_gain_full,
        dict(a_in=g_in, a_out=g_out.reshape(-1, g_out.shape[-1])), shards, core)

    loss_row = jnp.concatenate([loss, jnp.zeros((1, shard_cols - loss.shape[1]), F32)], axis=1)
    rows = [as_rows(dmod0), as_rows(dmod1), as_rows(dkvmod),
            as_rows(small["norm_mix"][0]), as_rows(small["norm_mix"][1]),
            as_rows(small["norm_mlp"][0]), as_rows(small["norm_mlp"][1]),
            as_rows(small["kv_norm"]), as_rows(small["final_norm"]),
            as_rows(small["lb"]), as_rows(small["out_gain"]), loss_row]
    n_rows = sum(r.shape[0] for r in rows)
    pad = (-n_rows) % 8
    pack3 = jnp.concatenate(rows + [jnp.zeros((pad, shard_cols), F32)], axis=0)
    got3 = allgather8("gather_small_grads", pack3)
    total = device_sum("sum_small_grads", got3)
    n_mod_rows = (12 * d + 2 * d) // shard_cols
    n_gain_rows = 6 * n_c
    loss_out = total[n_mod_rows + n_gain_rows + 2 * n_c, 0]

    dmod_all = got3[:, :n_mod_rows, :].reshape(N_DEV, 14 * d)
    act_t = jnp.zeros((d, 128), F32).at[:, :N_DEV].set(c_all.T)

    def dmod_cols(lo, width):
        part = lax.dynamic_slice_in_dim(dmod_all, lo + chip * width, width, axis=1)
        return jnp.zeros((128, width), F32).at[:N_DEV].set(part)

    g_ada_w, d_ada_w, nm_ada_w, nv_ada_w = ada_grad_adamw(
        "ada_update", act_t, jnp.stack([dmod_cols(0, n_ada), dmod_cols(6 * d, n_ada)]), ada_w, m_ada_w, v_ada_w)
    g_kv_ada_w, d_kv_ada_w, nm_kv_ada_w, nv_kv_ada_w = (a[0] for a in ada_grad_adamw(
        "ada_update_kv", act_t, dmod_cols(12 * d, n_kvada)[None], kv_ada_w[None], m_kv_ada_w[None],
        v_kv_ada_w[None]))

    chip_parts = [reduce[k][0] for k in BIG]
    from_chips = [reduce[k][1] for k in BIG]
    where = jnp.stack([chip, pc]).astype(jnp.int32)
    out_shapes = [_sds(a_w_in.shape, F32), _sds(mlp_w1.shape, F32), _sds(mlp_w2.shape, F32),
                  _sds(a_w_out.shape, F32), _sds(b_w_q.shape, F32), _sds(b_w_out.shape, F32),
                  _sds(w_kv.shape, F32)]
    targets = [(0, 0), (1, 0), (1, 1), (2, 0), (2, 1), (3, 0), (4, 0), (5, 0), (6, None)]
    slabs = [None] * len(out_shapes)
    places = []
    for k, part, recv, (oi, lead) in zip(BIG, chip_parts, from_chips, targets):
        slabs[oi] = chip_sum("grad_chip_sum_" + k, part, recv, where, out_shapes[oi], lead, slabs[oi])
        places.append((oi, lead, part.shape[1]))
    g_a_w_in, g_mlp_w1, g_mlp_w2, g_a_w_out, g_b_w_q, g_b_w_out, g_w_kv = sibling_share(
        "grad_sibling_share", slabs, places)

    def update(name, wgt, g, m_, v_):
        shape = wgt.shape
        f = lambda a: a.reshape(-1, shape[-1])
        return tuple(o.reshape(shape) for o in adamw(name, f(wgt), f(g), f(m_), f(v_)))

    u_a_w_in = update("adamw_a_w_in", a_w_in, g_a_w_in, m_a_w_in, v_a_w_in)
    u_mlp_w1 = update("adamw_mlp_w1", mlp_w1, g_mlp_w1, m_mlp_w1, v_mlp_w1)
    u_mlp_w2 = update("adamw_mlp_w2", mlp_w2, g_mlp_w2, m_mlp_w2, v_mlp_w2)
    u_a_w_out = update("adamw_a_w_out", a_w_out, g_a_w_out, m_a_w_out, v_a_w_out)
    u_b_w_q = update("adamw_b_w_q", b_w_q, g_b_w_q, m_b_w_q, v_b_w_q)
    u_b_w_out = update("adamw_b_w_out", b_w_out, g_b_w_out, m_b_w_out, v_b_w_out)
    u_w_kv = update("adamw_w_kv", w_kv, g_w_kv, m_w_kv, v_w_kv)

    base = n_mod_rows + n_gain_rows
    d_lb_mine = lax.dynamic_slice_in_dim(total, base + chip, 1, axis=0)
    d_gain_mine = lax.dynamic_slice_in_dim(total, base + n_c + chip, 1, axis=0)
    d_logits = lower_bound_bwd("lower_bound_bwd", a_lb_logits, d_lb_mine)

    def pack_small(ada_b_, kv_ada_b_, norm_mix_, norm_mlp_, kv_norm_, final_norm_, lbl_, gain_):
        parts = [as_rows(ada_b_), as_rows(kv_ada_b_), as_rows(norm_mix_), as_rows(norm_mlp_),
                 as_rows(kv_norm_), as_rows(final_norm_), lbl_, gain_]
        n = sum(p.shape[0] for p in parts)
        return jnp.concatenate(parts + [jnp.zeros(((-n) % 8, shard_cols), F32)], axis=0)

    w_small = pack_small(ada_b, kv_ada_b, norm_mix, norm_mlp, kv_norm, final_norm, a_lb_logits, a_out_gain)
    m_small = pack_small(m_ada_b, m_kv_ada_b, m_norm_mix, m_norm_mlp, m_kv_norm, m_final_norm,
                         m_a_lb_logits, m_a_out_gain)
    v_small = pack_small(v_ada_b, v_kv_ada_b, v_norm_mix, v_norm_mlp, v_kv_norm, v_final_norm,
                         v_a_lb_logits, v_a_out_gain)
    n_small = w_small.shape[0]
    g_small = jnp.concatenate([total[:base], d_logits, d_gain_mine,
                               jnp.zeros((n_small - base - 3, shard_cols), F32)], axis=0)
    small_out = (g_small,) + tuple(adamw("adamw_small", w_small, g_small, m_small, v_small))

    def unpack_small(p):
        out, r0 = [], 0
        for ref in (ada_b, kv_ada_b, norm_mix, norm_mlp, kv_norm, final_norm, a_lb_logits, a_out_gain):
            nr = ref.size // shard_cols
            out.append(p[r0:r0 + nr].reshape(ref.shape))
            r0 += nr
        return out

    sm = [unpack_small(p) for p in small_out]

    def leaves(kind, big_ada, big_kv_ada):
        ada_b_, kv_ada_b_, norm_mix_, norm_mlp_, kv_norm_, final_norm_, lbl_, gain_ = sm[kind]
        pick = (lambda u, g: g) if kind == 0 else (lambda u, g: u[kind - 1])
        return [big_ada, ada_b_, norm_mix_, norm_mlp_, pick(u_a_w_in, g_a_w_in), lbl_, gain_,
                pick(u_a_w_out, g_a_w_out), big_kv_ada, kv_ada_b_, kv_norm_, pick(u_w_kv, g_w_kv),
                pick(u_b_w_q, g_b_w_q), pick(u_b_w_out, g_b_w_out), pick(u_mlp_w1, g_mlp_w1),
                pick(u_mlp_w2, g_mlp_w2), final_norm_]

    return (loss_out, dx0[None],
            *leaves(0, g_ada_w, g_kv_ada_w), *leaves(1, d_ada_w, d_kv_ada_w),
            *leaves(2, nm_ada_w, nm_kv_ada_w), *leaves(3, nv_ada_w, nv_kv_ada_w))
```

```python
import functools

import jax
import jax.numpy as jnp
from jax import lax
from jax.experimental import pallas as pl
from jax.experimental.pallas import tpu as pltpu

F32 = jnp.float32
BF16 = jnp.bfloat16
MESH = pl.DeviceIdType.MESH

HEAD_DIM = 128
KV_GROUP = 4
HGRN_CHUNK = 64
HGRN_HEADS = 4
HGRN_ROWS = 256
NORM_EPS = 1e-6
N_CHIPS = 4
N_DEV = 8
ROW_TILE = 256
ATTN_TILE = 256
VMEM_LIMIT = 56 * 1024 * 1024
DEAD_LOG_WEIGHT = -110.0

ADAM_LR = 0.001
ADAM_B1 = 0.9
ADAM_B2 = 0.999
ADAM_EPS = 1e-08
ADAM_WD = 0.01
ADAM_STEP = 10

NN = (((1,), (0,)), ((), ()))
NT = (((1,), (1,)), ((), ()))
TN = (((0,), (0,)), ((), ()))


def _dot(a, b, dims=NN, precision=None):
    return lax.dot_general(a, b, dims, preferred_element_type=F32, precision=precision)


def _bdot(a, b, dims=NN):
    return _dot(a.astype(BF16), b.astype(BF16), dims)


def _sigmoid(x):
    return 1.0 / (1.0 + jnp.exp(-x))


def _log_sigmoid(z):
    return jnp.minimum(z, 0.0) - jnp.log(1.0 + jnp.exp(-jnp.abs(z)))


def _split_bf16(x):
    hi = x.astype(BF16)
    lo = (x - hi.astype(F32)).astype(BF16)
    return hi, lo


def _params(sem=None):
    return pltpu.CompilerParams(dimension_semantics=sem, vmem_limit_bytes=VMEM_LIMIT)


def _tile(n, pref):
    t = min(n, pref)
    assert n % t == 0, (n, pref)
    return t


def _mm(name, a, b, a_spec, b_spec, grid, n_red, dims, out_shapes, out_specs,
        acc_shape, epilogue=None, extras=(), extra_specs=(), comm=None):
    n_extra = len(extras)
    n_out = len(out_shapes)
    n_cin = len(comm["operands"]) if comm else 0
    n_cout = len(comm["out_shapes"]) if comm else 0
    if epilogue is None:
        epilogue = lambda acc: (acc,)

    def body(*refs):
        a_ref, b_ref = refs[:2]
        ex_refs = refs[2:2 + n_extra]
        cin_refs = refs[2 + n_extra:2 + n_extra + n_cin]
        out_refs = refs[2 + n_extra + n_cin:2 + n_extra + n_cin + n_out]
        cout_refs = refs[2 + n_extra + n_cin + n_out:2 + n_extra + n_cin + n_out + n_cout]
        scratch = refs[2 + n_extra + n_cin + n_out + n_cout:]
        pids = [pl.program_id(ax) for ax in range(len(grid))]
        if comm:
            send_sems, recv_sems = scratch[-2:]

            @pl.when(functools.reduce(jnp.logical_and, [p == 0 for p in pids]))
            def _():
                comm["start"](cin_refs, cout_refs, send_sems, recv_sems)

        prod = _bdot(a_ref[...], b_ref[...], dims)

        def finish(acc):
            res = epilogue(acc, *[e[...] for e in ex_refs])
            for o_ref, r in zip(out_refs, res):
                o_ref[...] = r.astype(o_ref.dtype)

        if n_red == 0:
            finish(prod)
        else:
            acc_ref = scratch[0]
            ids = pids[len(grid) - n_red:]
            sizes = grid[len(grid) - n_red:]
            first = functools.reduce(jnp.logical_and, [i == 0 for i in ids])
            last = functools.reduce(jnp.logical_and, [i == s - 1 for i, s in zip(ids, sizes)])

            @pl.when(first)
            def _():
                acc_ref[...] = prod

            @pl.when(jnp.logical_not(first))
            def _():
                acc_ref[...] += prod

            @pl.when(last)
            def _():
                finish(acc_ref[...])

        if comm:
            @pl.when(functools.reduce(jnp.logical_and, [p == s - 1 for p, s in zip(pids, grid)]))
            def _():
                comm["finish"](cin_refs, cout_refs, send_sems, recv_sems)

    if comm:
        sem = ("arbitrary",) * len(grid)
    else:
        sem = ("parallel",) * (len(grid) - n_red) + ("arbitrary",) * n_red
    scratch_shapes = [pltpu.VMEM(acc_shape, F32)] if n_red else []
    if comm:
        scratch_shapes += [pltpu.SemaphoreType.DMA(comm["sems"]), pltpu.SemaphoreType.DMA(comm["sems"])]
    out = pl.pallas_call(
        body, name=name, grid=grid,
        in_specs=[a_spec, b_spec, *extra_specs] + [HBM] * n_cin,
        out_specs=list(out_specs) + [HBM] * n_cout,
        out_shape=list(out_shapes) + (list(comm["out_shapes"]) if comm else []),
        scratch_shapes=scratch_shapes,
        compiler_params=_params(sem),
    )(a, b, *extras, *(comm["operands"] if comm else ()))
    return (out[:n_out], out[n_out:]) if comm else out


def _sds(shape, dtype):
    return jax.ShapeDtypeStruct(shape, dtype)


def mm_nn_b(name, a, w3, out_dtypes, epilogue=None, extras=(), extra_kinds=(), comm=None):
    m, k = a.shape
    nb, _, n = w3.shape
    tm, tn = _tile(m, 1024), _tile(n, 512)
    grid = (nb, m // tm, n // tn)
    nt = n // tn
    especs = []
    for kind in extra_kinds:
        if kind == "tile":
            especs.append(pl.BlockSpec((None, tm, tn), lambda j, i, c: (j, i, c)))
        else:
            especs.append(pl.BlockSpec((1, tn), lambda j, i, c: (0, j * nt + c)))
    return _mm(name, a, w3,
               pl.BlockSpec((tm, k), lambda j, i, c: (i, 0)),
               pl.BlockSpec((None, k, tn), lambda j, i, c: (j, 0, c)),
               grid, 0, NN,
               [_sds((nb, m, n), d) for d in out_dtypes],
               [pl.BlockSpec((None, tm, tn), lambda j, i, c: (j, i, c)) for _ in out_dtypes],
               None, epilogue, extras, especs, comm)


def mm_nn_r(name, a3, w3, out_dtypes, epilogue=None, extras=(), extra_kinds=(), comm=None):
    nb, m, kb = a3.shape
    n = w3.shape[2]
    tm, tn, tk = _tile(m, 1024), _tile(n, 512), _tile(kb, 2048)
    grid = (m // tm, n // tn, nb, kb // tk)
    especs = []
    for kind in extra_kinds:
        if kind == "tile":
            especs.append(pl.BlockSpec((tm, tn), lambda i, c, j, r: (i, c)))
        else:
            especs.append(pl.BlockSpec((1, tn), lambda i, c, j, r: (0, c)))
    return _mm(name, a3, w3,
               pl.BlockSpec((None, tm, tk), lambda i, c, j, r: (j, i, r)),
               pl.BlockSpec((None, tk, tn), lambda i, c, j, r: (j, r, c)),
               grid, 2, NN,
               [_sds((m, n), d) for d in out_dtypes],
               [pl.BlockSpec((tm, tn), lambda i, c, j, r: (i, c)) for _ in out_dtypes],
               (tm, tn), epilogue, extras, especs, comm)


def mm_nt_b(name, a, w3, out_dtypes, epilogue=None, extras=(), comm=None):
    m, n = a.shape
    nb, kb, _ = w3.shape
    tm, tk = _tile(m, 1024), _tile(kb, 512)
    grid = (nb, m // tm, kb // tk)
    especs = [pl.BlockSpec((None, tm, tk), lambda j, i, c: (j, i, c)) for _ in extras]
    return _mm(name, a, w3,
               pl.BlockSpec((tm, n), lambda j, i, c: (i, 0)),
               pl.BlockSpec((None, tk, n), lambda j, i, c: (j, c, 0)),
               grid, 0, NT,
               [_sds((nb, m, kb), d) for d in out_dtypes],
               [pl.BlockSpec((None, tm, tk), lambda j, i, c: (j, i, c)) for _ in out_dtypes],
               None, epilogue, extras, especs, comm)


def _single(res, comm):
    return res[0] if comm is None else (res[0][0], res[1])


def mm_nt_r(name, a3, w3, out_dtype, comm=None):
    nb, m, n = a3.shape
    k = w3.shape[1]
    tm, tk, tc = _tile(m, 1024), _tile(k, 1024), _tile(n, 2048)
    grid = (m // tm, k // tk, nb, n // tc)
    return _single(_mm(name, a3, w3,
                       pl.BlockSpec((None, tm, tc), lambda i, c, j, r: (j, i, r)),
                       pl.BlockSpec((None, tk, tc), lambda i, c, j, r: (j, c, r)),
                       grid, 2, NT,
                       [_sds((m, k), out_dtype)],
                       [pl.BlockSpec((tm, tk), lambda i, c, j, r: (i, c))],
                       (tm, tk), comm=comm), comm)


def mm_tn(name, a3, d3, out_dtype, comm=None, tiles=(512, 512, 4096)):
    na, m, kb = a3.shape
    nd, _, n = d3.shape
    nb = max(na, nd)
    tk, tn, tm = _tile(kb, tiles[0]), _tile(n, tiles[1]), _tile(m, tiles[2])
    ja = (lambda j: j) if na > 1 else (lambda j: 0)
    jd = (lambda j: j) if nd > 1 else (lambda j: 0)
    if tm == m:
        return _single(_mm(name, a3, d3,
                           pl.BlockSpec((None, tm, tk), lambda j, c, e: (ja(j), 0, c)),
                           pl.BlockSpec((None, tm, tn), lambda j, c, e: (jd(j), 0, e)),
                           (nb, kb // tk, n // tn), 0, TN,
                           [_sds((nb, kb, n), out_dtype)],
                           [pl.BlockSpec((None, tk, tn), lambda j, c, e: (j, c, e))],
                           None, comm=comm), comm)
    grid = (nb, kb // tk, n // tn, m // tm)
    return _single(_mm(name, a3, d3,
                       pl.BlockSpec((None, tm, tk), lambda j, c, e, r: (ja(j), r, c)),
                       pl.BlockSpec((None, tm, tn), lambda j, c, e, r: (jd(j), r, e)),
                       grid, 1, TN,
                       [_sds((nb, kb, n), out_dtype)],
                       [pl.BlockSpec((None, tk, tn), lambda j, c, e, r: (j, c, e))],
                       (tk, tn), comm=comm), comm)


def _row_spec(ts, d):
    return pl.BlockSpec((ts, d), lambda i: (i, 0))


def _vec_spec(d):
    return pl.BlockSpec((1, d), lambda i: (0, 0))


def norm_mod_fwd(name, x, gain, scale, shift):
    s, d = x.shape
    ts = _tile(s, ROW_TILE)

    def body(x_ref, g_ref, sc_ref, sh_ref, h_ref):
        xv = x_ref[...]
        inv = lax.rsqrt(jnp.mean(xv * xv, axis=-1, keepdims=True) + NORM_EPS)
        h = (xv * inv) * g_ref[...] * (1.0 + sc_ref[...]) + sh_ref[...]
        h_ref[...] = h.astype(h_ref.dtype)

    return pl.pallas_call(
        body, name=name, grid=(s // ts,),
        in_specs=[_row_spec(ts, d), _vec_spec(d), _vec_spec(d), _vec_spec(d)],
        out_specs=_row_spec(ts, d), out_shape=_sds((s, d), BF16),
        compiler_params=_params(("parallel",)),
    )(x, gain, scale, shift)


def _gate_bwd(dxv, y_ref, gate_ref, dy_ref, dgate_ref):
    dy_ref[...] = (dxv * gate_ref[...]).astype(dy_ref.dtype)
    dgate_ref[...] += jnp.sum(dxv * y_ref[...].astype(F32), axis=0, keepdims=True)


def norm_mod_bwd(name, dh, x, gain, scale, dres, branch=None):
    s, d = x.shape
    ts = _tile(s, ROW_TILE)

    def body(dh_ref, x_ref, g_ref, sc_ref, dres_ref, *rest):
        if branch:
            y_ref, gate_ref, dx_ref, dsh_ref, dsc_ref, dg_ref, dy_ref, dgate_ref = rest
        else:
            dx_ref, dsh_ref, dsc_ref, dg_ref = rest

        @pl.when(pl.program_id(0) == 0)
        def _():
            dsh_ref[...] = jnp.zeros_like(dsh_ref)
            dsc_ref[...] = jnp.zeros_like(dsc_ref)
            dg_ref[...] = jnp.zeros_like(dg_ref)
            if branch:
                dgate_ref[...] = jnp.zeros_like(dgate_ref)

        xv = x_ref[...]
        dhv = dh_ref[...].astype(F32)
        g = g_ref[...]
        inv = lax.rsqrt(jnp.mean(xv * xv, axis=-1, keepdims=True) + NORM_EPS)
        n = xv * inv
        dhn = dhv * (1.0 + sc_ref[...])
        dn = dhn * g
        dx = dres_ref[...] + inv * (dn - n * jnp.mean(dn * n, axis=-1, keepdims=True))
        dx_ref[...] = dx
        dsh_ref[...] += jnp.sum(dhv, axis=0, keepdims=True)
        dsc_ref[...] += jnp.sum(dhv * (n * g), axis=0, keepdims=True)
        dg_ref[...] += jnp.sum(dhn * n, axis=0, keepdims=True)
        if branch:
            _gate_bwd(dx, y_ref, gate_ref, dy_ref, dgate_ref)

    row, vec = _row_spec(ts, d), _vec_spec(d)
    return pl.pallas_call(
        body, name=name, grid=(s // ts,),
        in_specs=[row, row, vec, vec, row] + ([row, vec] if branch else []),
        out_specs=[row, vec, vec, vec] + ([row, vec] if branch else []),
        out_shape=[_sds((s, d), F32), _sds((1, d), F32), _sds((1, d), F32), _sds((1, d), F32)]
        + ([_sds((s, d), BF16), _sds((1, d), F32)] if branch else []),
        compiler_params=_params(("arbitrary",)),
    )(dh, x, gain, scale, dres, *(branch or ()))


def final_loss(name, x, gain, target, y, gate):
    s, d = x.shape
    ts = _tile(s, ROW_TILE)

    def body(x_ref, g_ref, t_ref, y_ref, gate_ref, dx_ref, dg_ref, loss_ref, dy_ref, dgate_ref):
        @pl.when(pl.program_id(0) == 0)
        def _():
            dg_ref[...] = jnp.zeros_like(dg_ref)
            loss_ref[...] = jnp.zeros_like(loss_ref)
            dgate_ref[...] = jnp.zeros_like(dgate_ref)

        xv = x_ref[...]
        g = g_ref[...]
        inv = lax.rsqrt(jnp.mean(xv * xv, axis=-1, keepdims=True) + NORM_EPS)
        n = xv * inv
        diff = n * g - t_ref[...]
        per_tok = jnp.mean(diff * diff, axis=-1, keepdims=True)
        loss_ref[...] += 0.5 * jnp.sum(per_tok, axis=0, keepdims=True)
        dout = diff * (1.0 / d)
        dg_ref[...] += jnp.sum(dout * n, axis=0, keepdims=True)
        dn = dout * g
        dx = inv * (dn - n * jnp.mean(dn * n, axis=-1, keepdims=True))
        dx_ref[...] = dx
        _gate_bwd(dx, y_ref, gate_ref, dy_ref, dgate_ref)

    row, vec = _row_spec(ts, d), _vec_spec(d)
    return pl.pallas_call(
        body, name=name, grid=(s // ts,),
        in_specs=[row, vec, row, row, vec],
        out_specs=[row, vec, _vec_spec(128), row, vec],
        out_shape=[_sds((s, d), F32), _sds((1, d), F32), _sds((1, 128), F32), _sds((s, d), BF16), _sds((1, d), F32)],
        compiler_params=_params(("arbitrary",)),
    )(x, gain, target, y, gate)


def _hgrn_chunk_fwd(qr, fl, lbv, tri):
    sg = _sigmoid(fl)
    sgm = _sigmoid(-fl)
    f = lbv + (1.0 - lbv) * sg
    logf = jnp.log(f)
    k = (1.0 - lbv) * sgm
    cum = _dot(tri, logf, precision=lax.Precision.HIGHEST)
    cl = cum[HGRN_CHUNK - 1:HGRN_CHUNK, :]
    e = jnp.exp(cum)
    en = jnp.exp(-cum)
    es = jnp.exp(cl - cum)
    sq = _sigmoid(qr)
    qs = qr * sq
    return dict(sg=sg, sgm=sgm, f=f, k=k, cum=cum, cl=cl, e=e, en=en, es=es, sq=sq, qs=qs,
                qd=qs * e, ki=k * en, ks=k * es, dec=jnp.exp(cl))


def _tri_masks(strict=False):
    r = lax.broadcasted_iota(jnp.int32, (HGRN_CHUNK, HGRN_CHUNK), 0)
    c = lax.broadcasted_iota(jnp.int32, (HGRN_CHUNK, HGRN_CHUNK), 1)
    return (r > c) if strict else (r >= c)


def _carried(comm, grid, n_in, n_out):
    if not comm:
        return [], [], [], [], lambda refs: (refs, lambda: None, lambda: None)
    n_cin, n_cout = len(comm["operands"]), len(comm["out_shapes"])
    sems = [pltpu.SemaphoreType.DMA(comm["sems"]), pltpu.SemaphoreType.DMA(comm["sems"])]

    def split(refs):
        ins, cin = refs[:n_in], refs[n_in:n_in + n_cin]
        outs = refs[n_in + n_cin:n_in + n_cin + n_out]
        cout = refs[n_in + n_cin + n_out:n_in + n_cin + n_out + n_cout]
        scratch = refs[n_in + n_cin + n_out + n_cout:]
        send_sems, recv_sems = scratch[-2:]
        pids = [pl.program_id(ax) for ax in range(len(grid))]

        def start():
            @pl.when(functools.reduce(jnp.logical_and, [p == 0 for p in pids]))
            def _():
                comm["start"](cin, cout, send_sems, recv_sems)

        def finish():
            @pl.when(functools.reduce(jnp.logical_and, [p == n - 1 for p, n in zip(pids, grid)]))
            def _():
                comm["finish"](cin, cout, send_sems, recv_sems)

        return ins + outs + scratch[:-2], start, finish

    return [HBM] * n_cin, [HBM] * n_cout, list(comm["out_shapes"]), sems, split


def hgrn_fwd(name, proj3, lb, out_gain, comm=None):
    _, s, d = proj3.shape
    heads = d // HEAD_DIM
    t_rows = _tile(s, HGRN_ROWS)
    n_t = s // t_rows
    n_c = t_rows // HGRN_CHUNK
    width = HGRN_HEADS * HEAD_DIM
    grid = (heads // HGRN_HEADS, n_t)
    c_in, c_out, c_shapes, c_sems, split = _carried(comm, grid, 6, 3)

    def body(*refs):
        (q_ref, f_ref, i_ref, g_ref, lb_ref, gain_ref, u_ref, o_ref, st_ref, state), start, finish = split(refs)
        start()
        compute(q_ref, f_ref, i_ref, g_ref, lb_ref, gain_ref, u_ref, o_ref, st_ref, state)
        finish()

    def compute(q_ref, f_ref, i_ref, g_ref, lb_ref, gain_ref, u_ref, o_ref, st_ref, state):
        @pl.when(pl.program_id(1) == 0)
        def _():
            state[...] = jnp.zeros_like(state)

        causal = _tri_masks()
        tri = causal.astype(F32)
        hs = range(HGRN_HEADS)
        col = [pl.ds(hh * HEAD_DIM, HEAD_DIM) for hh in hs]
        for ci in range(n_c):
            rows = pl.ds(ci * HGRN_CHUNK, HGRN_CHUNK)
            c = [_hgrn_chunk_fwd(q_ref[rows, col[hh]], f_ref[rows, col[hh]], lb_ref[:, col[hh]], tri) for hh in hs]
            v = [i_ref[rows, col[hh]] for hh in hs]
            st = [state[hh] for hh in hs]
            scores = [jnp.where(causal, _bdot(c[hh]["qd"], c[hh]["ki"], NT), 0.0) for hh in hs]
            inter = [_bdot(c[hh]["qd"], st[hh], NT) for hh in hs]
            update = [_bdot(v[hh], c[hh]["ks"], TN) for hh in hs]
            o = [_bdot(scores[hh], v[hh]) + inter[hh] for hh in hs]
            for hh in hs:
                st_ref[hh, ci] = st[hh]
                state[hh] = st[hh] * c[hh]["dec"] + update[hh]
            for hh in hs:
                graw = g_ref[rows, col[hh]]
                rms = lax.rsqrt(jnp.mean(o[hh] * o[hh], axis=-1, keepdims=True) + NORM_EPS)
                u = o[hh] * rms * gain_ref[:, col[hh]] * (graw * _sigmoid(graw))
                o_ref[rows, col[hh]] = o[hh]
                u_ref[rows, col[hh]] = u.astype(u_ref.dtype)

    def pspec(blk):
        return pl.BlockSpec((None, t_rows, width), lambda h, t: (blk, t, h))

    hspec = pl.BlockSpec((1, width), lambda h, t: (0, h))
    ospec = pl.BlockSpec((t_rows, width), lambda h, t: (t, h))
    out = pl.pallas_call(
        body, name=name, grid=grid,
        in_specs=[pspec(0), pspec(1), pspec(2), pspec(3), hspec, hspec] + c_in,
        out_specs=[ospec, ospec,
                   pl.BlockSpec((HGRN_HEADS, n_c, HEAD_DIM, HEAD_DIM), lambda h, t: (h, t, 0, 0))] + c_out,
        out_shape=[_sds((s, d), BF16), _sds((s, d), F32),
                   _sds((heads, s // HGRN_CHUNK, HEAD_DIM, HEAD_DIM), F32)] + c_shapes,
        scratch_shapes=[pltpu.VMEM((HGRN_HEADS, HEAD_DIM, HEAD_DIM), F32)] + c_sems,
        compiler_params=_params(("arbitrary", "arbitrary")),
    )(proj3, proj3, proj3, proj3, lb, out_gain, *(comm["operands"] if comm else ()))
    return (out[:3], out[3:]) if comm else out


def hgrn_bwd(name, proj3, lb, out_gain, o, du, states, comm=None):
    _, s, d = proj3.shape
    heads = d // HEAD_DIM
    t_rows = _tile(s, HGRN_ROWS)
    n_t = s // t_rows
    n_c = t_rows // HGRN_CHUNK
    width = HGRN_HEADS * HEAD_DIM
    grid = (heads // HGRN_HEADS, n_t)
    c_in, c_out, c_shapes, c_sems, split = _carried(comm, grid, 9, 3)

    def body(*refs):
        refs, start, finish = split(refs)
        start()
        compute(*refs)
        finish()

    def compute(q_ref, f_ref, i_ref, g_ref, lb_ref, gain_ref, o_ref, du_ref, st_ref,
             dp_ref, dlb_ref, dgain_ref, dstate):
        @pl.when(pl.program_id(1) == 0)
        def _():
            dstate[...] = jnp.zeros_like(dstate)
            dlb_ref[...] = jnp.zeros_like(dlb_ref)
            dgain_ref[...] = jnp.zeros_like(dgain_ref)

        causal = _tri_masks()
        tri = causal.astype(F32)
        tri_t = jnp.logical_not(_tri_masks(strict=True)).astype(F32)
        hs = range(HGRN_HEADS)
        col = [pl.ds(hh * HEAD_DIM, HEAD_DIM) for hh in hs]
        lbv = [lb_ref[:, col[hh]] for hh in hs]
        gain = [gain_ref[:, col[hh]] for hh in hs]
        for ci in reversed(range(n_c)):
            rows = pl.ds(ci * HGRN_CHUNK, HGRN_CHUNK)
            qr = [q_ref[rows, col[hh]] for hh in hs]
            c = [_hgrn_chunk_fwd(qr[hh], f_ref[rows, col[hh]], lbv[hh], tri) for hh in hs]
            v = [i_ref[rows, col[hh]] for hh in hs]
            st = [st_ref[hh, ci] for hh in hs]
            dst = [dstate[hh] for hh in hs]
            do, dgraw = [], []
            for hh in hs:
                ov = o_ref[rows, col[hh]]
                duv = du_ref[rows, col[hh]].astype(F32)
                graw = g_ref[rows, col[hh]]
                sgg = _sigmoid(graw)
                gate = graw * sgg
                rms = lax.rsqrt(jnp.mean(ov * ov, axis=-1, keepdims=True) + NORM_EPS)
                on = ov * rms
                dgain_ref[:, col[hh]] += jnp.sum(duv * on * gate, axis=0, keepdims=True)
                dgraw.append(duv * on * gain[hh] * (sgg * (1.0 + graw * (1.0 - sgg))))
                don = duv * gain[hh] * gate
                do.append(rms * (don - on * jnp.mean(don * on, axis=-1, keepdims=True)))
            qd = [c[hh]["qd"] for hh in hs]
            ki = [c[hh]["ki"] for hh in hs]
            ks = [c[hh]["ks"] for hh in hs]
            p = [jnp.where(causal, _bdot(qd[hh], ki[hh], NT), 0.0) for hh in hs]
            dp = [jnp.where(causal, _bdot(do[hh], v[hh], NT), 0.0) for hh in hs]
            from_state = [_bdot(do[hh], st[hh]) for hh in hs]
            dks = [_bdot(v[hh], dst[hh]) for hh in hs]
            dv_state = [_bdot(ks[hh], dst[hh], NT) for hh in hs]
            dstate_new = [_bdot(do[hh], qd[hh], TN) for hh in hs]
            dqd = [_bdot(dp[hh], ki[hh]) + from_state[hh] for hh in hs]
            dki = [_bdot(dp[hh], qd[hh], TN) for hh in hs]
            dv = [_bdot(p[hh], do[hh], TN) + dv_state[hh] for hh in hs]
            ddec = [jnp.sum(dst[hh] * st[hh], axis=0, keepdims=True) for hh in hs]
            for hh in hs:
                dstate[hh] = dst[hh] * c[hh]["dec"] + dstate_new[hh]
            dcum = [dqd[hh] * qd[hh] - dki[hh] * ki[hh] - dks[hh] * ks[hh] for hh in hs]
            dcl = [jnp.sum(dks[hh] * ks[hh], axis=0, keepdims=True) + ddec[hh] * c[hh]["dec"] for hh in hs]
            dlogf = [_dot(tri_t, dcum[hh], precision=lax.Precision.HIGHEST) + dcl[hh] for hh in hs]
            for hh in hs:
                ch = c[hh]
                dqs = dqd[hh] * ch["e"]
                dk = dki[hh] * ch["en"] + dks[hh] * ch["es"]
                df = dlogf[hh] / ch["f"]
                sg, sgm, sq = ch["sg"], ch["sgm"], ch["sq"]
                one_m_lb = 1.0 - lbv[hh]
                dlb_ref[:, col[hh]] += jnp.sum(df * (1.0 - sg) - dk * sgm, axis=0, keepdims=True)
                dfl = df * one_m_lb * sg * (1.0 - sg) - dk * one_m_lb * sgm * (1.0 - sgm)
                dqr = dqs * (sq * (1.0 + qr[hh] * (1.0 - sq)))
                dp_ref[0, rows, col[hh]] = dqr.astype(dp_ref.dtype)
                dp_ref[1, rows, col[hh]] = dfl.astype(dp_ref.dtype)
                dp_ref[2, rows, col[hh]] = dv[hh].astype(dp_ref.dtype)
                dp_ref[3, rows, col[hh]] = dgraw[hh].astype(dp_ref.dtype)

    def pspec(blk):
        return pl.BlockSpec((None, t_rows, width), lambda h, t: (blk, n_t - 1 - t, h))

    hspec = pl.BlockSpec((1, width), lambda h, t: (0, h))
    ospec = pl.BlockSpec((t_rows, width), lambda h, t: (n_t - 1 - t, h))
    out = pl.pallas_call(
        body, name=name, grid=grid,
        in_specs=[pspec(0), pspec(1), pspec(2), pspec(3), hspec, hspec, ospec, ospec,
                  pl.BlockSpec((HGRN_HEADS, n_c, HEAD_DIM, HEAD_DIM), lambda h, t: (h, n_t - 1 - t, 0, 0))] + c_in,
        out_specs=[pl.BlockSpec((4, t_rows, width), lambda h, t: (0, n_t - 1 - t, h)), hspec, hspec] + c_out,
        out_shape=[_sds((4, s, d), BF16), _sds((1, d), F32), _sds((1, d), F32)] + c_shapes,
        scratch_shapes=[pltpu.VMEM((HGRN_HEADS, HEAD_DIM, HEAD_DIM), F32)] + c_sems,
        compiler_params=_params(("arbitrary", "arbitrary")),
    )(proj3, proj3, proj3, proj3, lb, out_gain, o, du, states, *(comm["operands"] if comm else ()))
    return (out[:3], out[3:]) if comm else out


def attn_fwd(name, q, kv, comm=None):
    s, dq = q.shape
    kvh = kv.shape[1] // (2 * HEAD_DIM)
    assert dq == kvh * KV_GROUP * HEAD_DIM
    tq = _tile(s, ATTN_TILE)
    scale = HEAD_DIM ** -0.5

    grid = (kvh, s // tq)
    c_in, c_out, c_shapes, c_sems, split = _carried(comm, grid, 3, 3)

    def body(*refs):
        (q_ref, k_ref, v_ref, o_ref, tot_ref, cnt_ref), start, finish = split(refs)
        start()
        compute(q_ref, k_ref, v_ref, o_ref, tot_ref, cnt_ref)
        finish()

    def compute(q_ref, k_ref, v_ref, o_ref, tot_ref, cnt_ref):
        i = pl.program_id(1)
        heads = range(KV_GROUP)
        qs = [q_ref[:, g * HEAD_DIM:(g + 1) * HEAD_DIM] for g in heads]
        r_i = lax.broadcasted_iota(jnp.int32, (tq, tq), 0)
        c_i = lax.broadcasted_iota(jnp.int32, (tq, tq), 1)
        later = (r_i > c_i).astype(BF16)
        later2 = jnp.concatenate([later, later], axis=0)
        mask = c_i < r_i
        ones2 = jnp.ones((8, 2 * tq), BF16)

        def block(j, carry, masked):
            rows = pl.ds(pl.multiple_of(j * tq, tq), tq)
            kj = k_ref[rows, :]
            vj = v_ref[rows, :]
            z = [_dot(qs[g], kj, NT) * scale for g in heads]
            lbeta = [_log_sigmoid(z[g]) for g in heads]
            lrest = [lbeta[g] - z[g] for g in heads]
            if masked:
                lrest = [jnp.where(mask, lrest[g], 0.0) for g in heads]
            hl = [jnp.concatenate(_split_bf16(lrest[g]), axis=1) for g in heads]
            between = [_dot(hl[g], later2) + carry[g][0] for g in heads]
            sums = [_dot(ones2, hl[g], NT) for g in heads]
            w = [jnp.exp(lbeta[g] + between[g]) for g in heads]
            if masked:
                w = [jnp.where(mask, w[g], 0.0) for g in heads]
            pv = [_dot(w[g].astype(BF16), vj) for g in heads]
            return tuple((carry[g][0] + jnp.sum(lrest[g], axis=1, keepdims=True), carry[g][1] + pv[g],
                          carry[g][2] + sums[g]) for g in heads)

        def alive(carry):
            top = carry[0][0]
            for g in heads[1:]:
                top = jnp.maximum(top, carry[g][0])
            return jnp.max(top) > DEAD_LOG_WEIGHT

        zero = (jnp.zeros((tq, 1), F32), jnp.zeros((tq, HEAD_DIM), F32), jnp.zeros((8, tq), F32))
        carry = block(i, (zero,) * KV_GROUP, True)

        def step(state):
            jj, _, cr = state
            cr = block(i - 1 - jj, cr, False)
            return jj + 1, alive(cr), cr

        done, _, carry = lax.while_loop(lambda st: jnp.logical_and(st[0] < i, st[1]), step,
                                        (jnp.int32(0), alive(carry), carry))
        for g in heads:
            o_ref[:, g * HEAD_DIM:(g + 1) * HEAD_DIM] = carry[g][1]
            tot_ref[g] = carry[g][2][0:1, :]
        cnt_ref[pl.program_id(0), i] = done

    group = KV_GROUP * HEAD_DIM
    out = pl.pallas_call(
        body, name=name, grid=grid,
        in_specs=[pl.BlockSpec((tq, group), lambda kh, i: (i, kh)),
                  pl.BlockSpec((s, HEAD_DIM), lambda kh, i: (0, kh)),
                  pl.BlockSpec((s, HEAD_DIM), lambda kh, i: (0, kvh + kh))] + c_in,
        out_specs=[pl.BlockSpec((tq, group), lambda kh, i: (i, kh)),
                   pl.BlockSpec((KV_GROUP, 1, tq), lambda kh, i: (kh, 0, i)),
                   pl.BlockSpec(memory_space=pltpu.SMEM)] + c_out,
        out_shape=[_sds((s, dq), F32), _sds((dq // HEAD_DIM, 1, s), F32),
                   _sds((kvh, s // tq), jnp.int32)] + c_shapes,
        scratch_shapes=c_sems,
        compiler_params=_params(("arbitrary", "arbitrary")),
    )(q, kv, kv, *(comm["operands"] if comm else ()))
    return (out[:3], out[3:]) if comm else out


def attn_bwd(name, q, kv, totals, visited, do):
    s, dq_cols = q.shape
    kvh = kv.shape[1] // (2 * HEAD_DIM)
    tq = _tile(s, ATTN_TILE)
    scale = HEAD_DIM ** -0.5

    def body(cnt_ref, q_ref, k_ref, v_ref, tot_ref, do_ref, dq_ref, dkv_ref):
        i = pl.program_id(1)
        first = i - jnp.clip(cnt_ref[pl.program_id(0), i], 0, i)

        @pl.when(i == 0)
        def _():
            dkv_ref[...] = jnp.zeros_like(dkv_ref)

        heads = range(KV_GROUP)
        qs = [q_ref[:, g * HEAD_DIM:(g + 1) * HEAD_DIM] for g in heads]
        dobs = [do_ref[:, g * HEAD_DIM:(g + 1) * HEAD_DIM].astype(BF16) for g in heads]
        tots = [tot_ref[g] for g in heads]
        q_all = jnp.concatenate(qs, axis=0)
        do_all = jnp.concatenate(dobs, axis=0)
        r_i = lax.broadcasted_iota(jnp.int32, (tq, tq), 0)
        c_i = lax.broadcasted_iota(jnp.int32, (tq, tq), 1)
        upto = (c_i <= r_i).astype(BF16)
        before = (c_i < r_i).astype(BF16)
        upto2 = jnp.concatenate([upto, upto], axis=1)
        before2 = jnp.concatenate([before, before], axis=1)
        mask = r_i < c_i

        def block(j, carry, masked):
            rows = pl.ds(pl.multiple_of(j * tq, tq), tq)
            kj = k_ref[rows, :]
            vj = v_ref[rows, :]
            zt = [_dot(kj, qs[g], NT) * scale for g in heads]
            dwt = [_dot(vj, dobs[g], NT) for g in heads]
            lbeta = [_log_sigmoid(zt[g]) for g in heads]
            lrest_raw = [lbeta[g] - zt[g] for g in heads]
            lrest = [jnp.where(mask, lrest_raw[g], 0.0) for g in heads] if masked else lrest_raw
            hl = [jnp.concatenate(_split_bf16(lrest[g]), axis=0) for g in heads]
            upto_sum = [_dot(upto2, hl[g]) for g in heads]
            wt = [jnp.exp(lbeta[g] + (tots[g] - carry[g][0] - upto_sum[g])) for g in heads]
            if masked:
                wt = [jnp.where(mask, wt[g], 0.0) for g in heads]
            dat = [dwt[g] * wt[g] for g in heads]
            earlier = [_dot(before2, jnp.concatenate(_split_bf16(dat[g]), axis=0)) for g in heads]
            dzt = [dat[g] * jnp.exp(lrest_raw[g]) - (carry[g][1] + earlier[g]) * jnp.exp(lbeta[g]) for g in heads]
            if masked:
                dzt = [jnp.where(mask, dzt[g], 0.0) for g in heads]
            dzb = [(dzt[g] * scale).astype(BF16) for g in heads]
            dq_new = [_dot(dzb[g], kj, TN) for g in heads]
            dkv_ref[0, rows, :] += _dot(jnp.concatenate(dzb, axis=1), q_all)
            dkv_ref[1, rows, :] += _dot(jnp.concatenate([wt[g].astype(BF16) for g in heads], axis=1), do_all)
            return tuple((carry[g][0] + jnp.sum(lrest[g], axis=0, keepdims=True),
                          carry[g][1] + jnp.sum(dat[g], axis=0, keepdims=True),
                          carry[g][2] + dq_new[g]) for g in heads)

        zrow = jnp.zeros((1, tq), F32)
        carry = ((zrow, zrow, jnp.zeros((tq, HEAD_DIM), F32)),) * KV_GROUP
        carry = lax.fori_loop(first, i, lambda j, cr: block(j, cr, False), carry)
        carry = block(i, carry, True)
        for g in heads:
            dq_ref[:, g * HEAD_DIM:(g + 1) * HEAD_DIM] = carry[g][2].astype(dq_ref.dtype)

    group = KV_GROUP * HEAD_DIM
    qspec = pl.BlockSpec((tq, group), lambda kh, i, cnt: (i, kh))
    return pl.pallas_call(
        body, name=name,
        grid_spec=pltpu.PrefetchScalarGridSpec(
            num_scalar_prefetch=1, grid=(kvh, s // tq),
            in_specs=[qspec,
                      pl.BlockSpec((s, HEAD_DIM), lambda kh, i, cnt: (0, kh)),
                      pl.BlockSpec((s, HEAD_DIM), lambda kh, i, cnt: (0, kvh + kh)),
                      pl.BlockSpec((KV_GROUP, 1, tq), lambda kh, i, cnt: (kh, 0, i)),
                      qspec],
            out_specs=[qspec, pl.BlockSpec((2, s, HEAD_DIM), lambda kh, i, cnt: (0, 0, kh))]),
        out_shape=[_sds((s, dq_cols), BF16), _sds((2, s, kvh * HEAD_DIM), F32)],
        compiler_params=_params(("parallel", "arbitrary")),
    )(visited, q, kv, kv, totals, do)


def ada_project(name, c_all, w, b):
    bsz, d = c_all.shape
    n = w.shape[1]
    tn = _tile(n, 512)

    def body(c_ref, w_ref, b_ref, o_ref):
        cv = c_ref[...]
        act = cv * _sigmoid(cv)
        o_ref[...] = _bdot(act, w_ref[...]) + b_ref[...]

    return pl.pallas_call(
        body, name=name, grid=(n // tn,),
        in_specs=[pl.BlockSpec((bsz, d), lambda i: (0, 0)),
                  pl.BlockSpec((d, tn), lambda i: (0, i)),
                  pl.BlockSpec((1, tn), lambda i: (0, i))],
        out_specs=pl.BlockSpec((bsz, tn), lambda i: (0, i)),
        out_shape=_sds((bsz, n), F32),
        compiler_params=_params(("parallel",)),
    )(c_all, w, b)


def _adamw_math(w, g, m, v):
    m = ADAM_B1 * m + (1.0 - ADAM_B1) * g
    v = ADAM_B2 * v + (1.0 - ADAM_B2) * (g * g)
    m_hat = m / (1.0 - ADAM_B1 ** ADAM_STEP)
    v_hat = v / (1.0 - ADAM_B2 ** ADAM_STEP)
    delta = -ADAM_LR * (m_hat / (jnp.sqrt(v_hat) + ADAM_EPS) + ADAM_WD * w)
    return delta, m, v


def adamw(name, w, g, m, v):
    r, c = w.shape
    tr = _tile(r, 256)
    tc = _tile(c, 2048)

    def body(w_ref, g_ref, m_ref, v_ref, d_ref, mo_ref, vo_ref):
        delta, mn, vn = _adamw_math(w_ref[...], g_ref[...], m_ref[...], v_ref[...])
        d_ref[...] = delta
        mo_ref[...] = mn
        vo_ref[...] = vn

    spec = pl.BlockSpec((tr, tc), lambda i, j: (i, j))
    return pl.pallas_call(
        body, name=name, grid=(r // tr, c // tc),
        in_specs=[spec] * 4, out_specs=[spec] * 3,
        out_shape=[_sds((r, c), F32)] * 3,
        compiler_params=_params(("parallel", "parallel")),
    )(w, g, m, v)


def ada_grad_adamw(name, c_t, dmod, w, m, v):
    layers, d, n = w.shape
    tr = _tile(d, 256)
    tc = _tile(n, 512)

    def body(a_ref, dm_ref, w_ref, m_ref, v_ref, g_ref, d_ref, mo_ref, vo_ref):
        cv = a_ref[...]
        g = _bdot(cv * _sigmoid(cv), dm_ref[...])
        delta, mn, vn = _adamw_math(w_ref[...], g, m_ref[...], v_ref[...])
        g_ref[...] = g
        d_ref[...] = delta
        mo_ref[...] = mn
        vo_ref[...] = vn

    spec = pl.BlockSpec((None, tr, tc), lambda l, i, j: (l, i, j))
    return pl.pallas_call(
        body, name=name, grid=(layers, d // tr, n // tc),
        in_specs=[pl.BlockSpec((tr, 128), lambda l, i, j: (i, 0)),
                  pl.BlockSpec((None, 128, tc), lambda l, i, j: (l, 0, j)), spec, spec, spec],
        out_specs=[spec] * 4, out_shape=[_sds((layers, d, n), F32)] * 4,
        compiler_params=_params(("parallel", "parallel", "parallel")),
    )(c_t, dmod, w, m, v)


def device_sum(name, gathered):
    _, r, c = gathered.shape

    def body(g_ref, o_ref):
        acc = g_ref[0]
        for dev in range(1, N_DEV):
            acc = acc + g_ref[dev]
        o_ref[...] = acc

    return pl.pallas_call(
        body, name=name,
        in_specs=[pl.BlockSpec(memory_space=pltpu.VMEM)],
        out_specs=pl.BlockSpec(memory_space=pltpu.VMEM),
        out_shape=_sds((r, c), F32),
    )(gathered)


def lower_bound_fwd(name, logits):
    _, d = logits.shape

    def body(l_ref, o_ref):
        l0 = l_ref[0:1, :]
        l1 = l_ref[1:2, :]
        mx = jnp.maximum(l0, l1)
        e0 = jnp.exp(l0 - mx)
        e1 = jnp.exp(l1 - mx)
        o_ref[...] = e0 / (e0 + e1)

    return pl.pallas_call(
        body, name=name,
        in_specs=[pl.BlockSpec(memory_space=pltpu.VMEM)],
        out_specs=pl.BlockSpec(memory_space=pltpu.VMEM),
        out_shape=_sds((1, d), F32),
    )(logits)


def lower_bound_bwd(name, logits, dlb):
    _, d = logits.shape

    def body(l_ref, dlb_ref, o_ref):
        l0 = l_ref[0:1, :]
        l1 = l_ref[1:2, :]
        mx = jnp.maximum(l0, l1)
        e0 = jnp.exp(l0 - mx)
        e1 = jnp.exp(l1 - mx)
        p0 = e0 / (e0 + e1)
        p1 = e1 / (e0 + e1)
        g = dlb_ref[...] * p0 * p1
        o_ref[0:1, :] = g
        o_ref[1:2, :] = -g

    return pl.pallas_call(
        body, name=name,
        in_specs=[pl.BlockSpec(memory_space=pltpu.VMEM)] * 2,
        out_specs=pl.BlockSpec(memory_space=pltpu.VMEM),
        out_shape=_sds((2, d), F32),
    )(logits, dlb)


HBM = pl.BlockSpec(memory_space=pltpu.HBM)


def _position():
    return lax.axis_index("x"), lax.axis_index("y"), lax.axis_index("c")


def _remote(src, dst, send_sem, recv_sem, device):
    return pltpu.make_async_remote_copy(src_ref=src, dst_ref=dst, send_sem=send_sem, recv_sem=recv_sem,
                                        device_id=device, device_id_type=MESH)


def allgather8(name, x):
    r, c = x.shape

    def body(x_ref, out_ref, send_sems, recv_sems):
        px, py, pc = _position()
        me = 4 * px + 2 * py + pc
        out_ref[me] = x_ref[...]
        copies = []
        for k in range(1, N_DEV):
            peer = (1 - px if k & 4 else px, 1 - py if k & 2 else py, 1 - pc if k & 1 else pc)
            cp = _remote(x_ref, out_ref.at[me], send_sems.at[k - 1], recv_sems.at[k - 1], peer)
            cp.start()
            copies.append(cp)
        for cp in copies:
            cp.wait()

    return pl.pallas_call(
        body, name=name,
        in_specs=[pl.BlockSpec(memory_space=pltpu.VMEM)],
        out_specs=pl.BlockSpec(memory_space=pltpu.VMEM),
        out_shape=_sds((N_DEV, r, c), x.dtype),
        scratch_shapes=[pltpu.SemaphoreType.DMA((N_DEV - 1,)), pltpu.SemaphoreType.DMA((N_DEV - 1,))],
    )(x)


def _other_chips(px, py):
    return [(1 - px, py), (px, 1 - py), (1 - px, 1 - py)]


def gather_weights(name, shards):
    n = len(shards)
    hook = gather_hook(shards)

    def body(*refs):
        ins, outs = refs[:n], refs[n:2 * n]
        send_sems, recv_sems = refs[2 * n:]
        hook["start"](ins, outs, send_sems, recv_sems)
        hook["finish"](ins, outs, send_sems, recv_sems)

    return pl.pallas_call(
        body, name=name,
        in_specs=[HBM] * n, out_specs=[HBM] * n,
        out_shape=hook["out_shapes"],
        scratch_shapes=[pltpu.SemaphoreType.DMA(hook["sems"]), pltpu.SemaphoreType.DMA(hook["sems"])],
    )(*shards)


def gather_hook(shards):
    n = len(shards)

    def copies(ins, outs, send_sems, recv_sems):
        px, py, pc = _position()
        chip = 2 * px + py
        sibling = (px, py, 1 - pc)
        own, sends, arrivals, passes, pass_arrivals = [], [], [], [], []
        for a in range(n):
            rows = ins[a].shape[0] // 2
            mine, theirs = pl.ds(pc * rows, rows), pl.ds((1 - pc) * rows, rows)
            own.append(_remote(ins[a], outs[a].at[chip], send_sems.at[a, 6], recv_sems.at[a, 6], sibling))
            for j, (ox, oy) in enumerate(_other_chips(px, py)):
                sends.append(_remote(ins[a].at[mine], outs[a].at[chip, mine],
                                     send_sems.at[a, j], recv_sems.at[a, j], (ox, oy, pc)))
                landed = outs[a].at[2 * ox + oy, mine]
                arrivals.append(_remote(landed, landed, send_sems.at[a, j], recv_sems.at[a, j], sibling))
                passes.append(_remote(landed, landed, send_sems.at[a, 3 + j], recv_sems.at[a, 3 + j], sibling))
                via = outs[a].at[2 * ox + oy, theirs]
                pass_arrivals.append(_remote(via, via, send_sems.at[a, 3 + j], recv_sems.at[a, 3 + j], sibling))
        return own, sends, arrivals, passes, pass_arrivals

    def start(ins, outs, send_sems, recv_sems):
        own, sends, _, _, _ = copies(ins, outs, send_sems, recv_sems)
        for cp in own + sends:
            cp.start()

    def finish(ins, outs, send_sems, recv_sems):
        own, sends, arrivals, passes, pass_arrivals = copies(ins, outs, send_sems, recv_sems)
        for arrival, onward in zip(arrivals, passes):
            arrival.wait_recv()
            onward.start()
        for arrival in pass_arrivals:
            arrival.wait_recv()
        for cp in sends + passes:
            cp.wait_send()
        for cp in own:
            cp.wait()

    return dict(operands=list(shards), out_shapes=[_sds((N_CHIPS,) + s.shape, s.dtype) for s in shards],
                sems=(n, 7), start=start, finish=finish)


def sibling_exchange(name, grads):
    n = len(grads)
    hook = exchange_hook(grads)

    def body(*refs):
        ins, outs = refs[:n], refs[n:2 * n]
        send_sems, recv_sems = refs[2 * n:]
        hook["start"](ins, outs, send_sems, recv_sems)
        hook["finish"](ins, outs, send_sems, recv_sems)

    return pl.pallas_call(
        body, name=name,
        in_specs=[HBM] * n, out_specs=[HBM] * n,
        out_shape=hook["out_shapes"],
        scratch_shapes=[pltpu.SemaphoreType.DMA(hook["sems"]), pltpu.SemaphoreType.DMA(hook["sems"])],
    )(*grads)


def exchange_hook(grads, first_sem=0):
    n = len(grads)

    def copies(ins, outs, send_sems, recv_sems):
        px, py, pc = _position()
        made = []
        for a in range(n):
            rows = ins[a].shape[1] // 2
            src = ins[a].at[:, pl.ds((1 - pc) * rows, rows), :]
            made.append(_remote(src, outs[a], send_sems.at[first_sem + a, 0], recv_sems.at[first_sem + a, 0],
                                (px, py, 1 - pc)))
        return made

    def start(ins, outs, send_sems, recv_sems):
        for cp in copies(ins, outs, send_sems, recv_sems):
            cp.start()

    def finish(ins, outs, send_sems, recv_sems):
        for cp in copies(ins, outs, send_sems, recv_sems):
            cp.wait()

    return dict(operands=list(grads),
                out_shapes=[_sds((g.shape[0], g.shape[1] // 2, g.shape[2]), g.dtype) for g in grads],
                sems=(first_sem + n, 3), start=start, finish=finish)


def combine_hooks(first, second):
    n_in, n_out = len(first["operands"]), len(first["out_shapes"])

    def start(ins, outs, send_sems, recv_sems):
        first["start"](ins[:n_in], outs[:n_out], send_sems, recv_sems)
        second["start"](ins[n_in:], outs[n_out:], send_sems, recv_sems)

    def finish(ins, outs, send_sems, recv_sems):
        first["finish"](ins[:n_in], outs[:n_out], send_sems, recv_sems)
        second["finish"](ins[n_in:], outs[n_out:], send_sems, recv_sems)

    return dict(operands=first["operands"] + second["operands"], out_shapes=first["out_shapes"] + second["out_shapes"],
                sems=second["sems"], start=start, finish=finish)


def pair_add(name, g, recv, core):
    nb, rows, cols = g.shape
    half = rows // 2
    tr = _tile(half, 256)
    steps = half // tr

    def body(core_ref, g_ref, r_ref, o_ref):
        del core_ref
        o_ref[...] = (g_ref[...].astype(F32) + r_ref[...].astype(F32)).astype(o_ref.dtype)

    return pl.pallas_call(
        body, name=name,
        grid_spec=pltpu.PrefetchScalarGridSpec(
            num_scalar_prefetch=1, grid=(nb, steps),
            in_specs=[pl.BlockSpec((None, tr, cols), lambda j, i, core_ref: (j, core_ref[0] * steps + i, 0)),
                      pl.BlockSpec((None, tr, cols), lambda j, i, core_ref: (j, i, 0))],
            out_specs=pl.BlockSpec((None, tr, cols), lambda j, i, core_ref: (j, i, 0))),
        out_shape=_sds((nb, half, cols), g.dtype),
        compiler_params=_params(("parallel", "parallel")),
    )(core, g, recv)


def scatter_hook(parts):
    n = len(parts)

    def copies(ins, outs, send_sems, recv_sems):
        px, py, pc = _position()
        return [_remote(ins[a].at[2 * ox + oy], outs[a].at[j], send_sems.at[a, j], recv_sems.at[a, j], (ox, oy, pc))
                for a in range(n) for j, (ox, oy) in enumerate(_other_chips(px, py))]

    def start(ins, outs, send_sems, recv_sems):
        for cp in copies(ins, outs, send_sems, recv_sems):
            cp.start()

    def finish(ins, outs, send_sems, recv_sems):
        for cp in copies(ins, outs, send_sems, recv_sems):
            cp.wait()

    return dict(operands=list(parts), out_shapes=[_sds((N_CHIPS - 1,) + p.shape[1:], p.dtype) for p in parts],
                sems=(n, 3), start=start, finish=finish)


def chip_sum(name, part, recv, where, out_shape, lead, dest=None):
    _, half, cols = part.shape
    tr = _tile(half, 256)
    steps = half // tr

    def body(where_ref, p_ref, r_ref, *rest):
        o_ref = rest[-1]
        acc = p_ref[...].astype(F32)
        for j in range(N_CHIPS - 1):
            acc = acc + r_ref[j].astype(F32)
        o_ref[...] = acc

    if lead is None:
        ospec = pl.BlockSpec((tr, cols), lambda i, w: (w[1] * steps + i, 0))
    else:
        ospec = pl.BlockSpec((None, tr, cols), lambda i, w: (lead, w[1] * steps + i, 0))
    in_specs = [pl.BlockSpec((None, tr, cols), lambda i, w: (w[0], i, 0)),
                pl.BlockSpec((N_CHIPS - 1, tr, cols), lambda i, w: (0, i, 0))]
    operands = [where, part, recv]
    aliases = {}
    if dest is not None:
        in_specs.append(pl.BlockSpec(memory_space=pl.ANY))
        operands.append(dest)
        aliases = {3: 0}
    return pl.pallas_call(
        body, name=name,
        grid_spec=pltpu.PrefetchScalarGridSpec(num_scalar_prefetch=1, grid=(steps,),
                                               in_specs=in_specs, out_specs=ospec),
        out_shape=out_shape, input_output_aliases=aliases,
        compiler_params=_params(("parallel",)),
    )(*operands)


def sibling_share(name, slabs, places):
    n = len(slabs)
    k = len(places)

    def body(*refs):
        outs = refs[n:2 * n]
        send_sems, recv_sems = refs[2 * n:]
        px, py, pc = _position()
        sibling = (px, py, 1 - pc)
        copies = []
        for a, (oi, lead, half) in enumerate(places):
            slab = outs[oi] if lead is None else outs[oi].at[lead]
            mine = slab.at[pl.ds(pc * half, half)]
            theirs = slab.at[pl.ds((1 - pc) * half, half)]
            cp = _remote(mine, mine, send_sems.at[a], recv_sems.at[a], sibling)
            cp.start()
            copies.append((cp, _remote(theirs, theirs, send_sems.at[a], recv_sems.at[a], sibling)))
        for cp, arrival in copies:
            cp.wait_send()
            arrival.wait_recv()

    return pl.pallas_call(
        body, name=name,
        in_specs=[HBM] * n, out_specs=[HBM] * n,
        out_shape=[_sds(s.shape, s.dtype) for s in slabs],
        input_output_aliases={a: a for a in range(n)},
        scratch_shapes=[pltpu.SemaphoreType.DMA((k,)), pltpu.SemaphoreType.DMA((k,))],
    )(*slabs)


def _row(a, i):
    return a[i:i + 1]


def _relu_sq(acc):
    r = jnp.maximum(acc, 0.0)
    return r, r * r


def _residual(acc, res, gate):
    return res + gate * acc, acc


def _with_comm(res, comm):
    return res if comm else (res, [])


def _blocked(g):
    return g if g.ndim == 3 else g.reshape(N_CHIPS, g.shape[0] // N_CHIPS, g.shape[1])


def _pre_reduce(tag, named, core):
    glist = [_blocked(g) for _, g in named]
    recv = sibling_exchange("grad_exchange_" + tag, glist)
    return [pair_add("grad_pair_add_" + k, g, r, core) for (k, _), g, r in zip(named, glist, recv)]


def _mlp_fwd(tag, x, gain, mod, w1, w2=None, w2_shard=None, down_comm=None):
    h = norm_mod_fwd(tag + "_mlp_norm", x, gain, _row(mod, 4), _row(mod, 3))
    up_comm = None if w2 is not None else gather_hook([w2_shard])
    (r3, a3), got = _with_comm(mm_nn_b(tag + "_mlp_up", h, w1, [BF16, BF16], _relu_sq, comm=up_comm), up_comm)
    if w2 is None:
        w2 = got[0]
    (x_out, m), got = _with_comm(mm_nn_r(tag + "_mlp_down", a3, w2, [F32, BF16], _residual,
                                         (x, _row(mod, 5)), ("tile", "row"), comm=down_comm), down_comm)
    return x_out, (h, r3, a3, m), w2, got


def _mlp_bwd(tag, dx_out, dm, x, gain, mod, w1, w2, saved, core, branch, down_comm=None, w2_comm=None):
    h, r3, a3, _ = saved
    (dz3,), got_down = _with_comm(mm_nt_b(tag + "_mlp_down_dgrad", dm, w2, [BF16],
                                          lambda acc, r: (acc * (2.0 * r.astype(F32)),), (r3,), comm=down_comm),
                                  down_comm)
    gw2, got_w2 = _with_comm(mm_tn(tag + "_mlp_w2_grad", a3, dm[None], BF16, comm=w2_comm), w2_comm)
    gw1, (from_sibling,) = mm_tn(tag + "_mlp_w1_grad", h[None], dz3, BF16, comm=exchange_hook([gw2]))
    part2 = pair_add("grad_pair_add_" + tag + "_w2", gw2, from_sibling, core)
    dh, (recv2, from_sibling) = mm_nt_r(tag + "_mlp_up_dgrad", dz3, w1, F32,
                                        comm=combine_hooks(scatter_hook([part2]), exchange_hook([gw1], 1)))
    part1 = pair_add("grad_pair_add_" + tag + "_w1", gw1, from_sibling, core)
    dx, dsh, dsc, dgain, dy, dgate = norm_mod_bwd(tag + "_mlp_norm_bwd", dh, x, gain, _row(mod, 4), dx_out, branch)
    return dx, (dy, dgate), part1, (part2, recv2), (dsh, dsc), dgain, got_down, got_w2


def local_step(x, target, mod0, mod1, kvmod, norm_mix, norm_mlp, kv_norm, final_norm, lb, out_gain,
               w, shards, core):
    w = dict(w)

    def flat(g):
        return g.reshape(-1, g.shape[-1])

    h1 = norm_mod_fwd("l0_mix_norm", x, _row(norm_mix, 0), _row(mod0, 1), _row(mod0, 0))
    (proj3,), (w["w1_0"],) = mm_nn_b("l0_in_proj", h1, w["a_in"], [F32], comm=gather_hook([shards["w1_0"]]))
    (u, o_h, states), got = hgrn_fwd("l0_hgrn_fwd", proj3, lb, out_gain,
                                     comm=gather_hook([shards["bq"], shards["kv"], shards["bo"]]))
    w["bq"], w["kv"], w["bo"] = (flat(g) for g in got)
    x1, y0 = mm_nn_b("l0_out_proj", u, w["a_out"][None], [F32, BF16], _residual,
                     (x[None], _row(mod0, 2)), ("tile", "row"))
    x1, y0 = x1[0], y0[0]
    x2, mlp0, w["w2_0"], (w["w1_1"],) = _mlp_fwd(
        "l0", x1, _row(norm_mlp, 0), mod0, w["w1_0"], w2_shard=shards["w2_0"],
        down_comm=gather_hook([shards["w1_1"]]))
    hk = norm_mod_fwd("kv_norm", x2, kv_norm, _row(kvmod, 1), _row(kvmod, 0))
    kv = mm_nn_b("kv_proj", hk, w["kv"][None], [BF16])[0][0]
    h3 = norm_mod_fwd("l1_mix_norm", x2, _row(norm_mix, 1), _row(mod1, 1), _row(mod1, 0))
    q = mm_nn_b("l1_q_proj", h3, w["bq"][None], [BF16])[0][0]
    (o_a, totals, visited), (w["w2_1"],) = attn_fwd("l1_attn_fwd", q, kv, comm=gather_hook([shards["w2_1"]]))
    x3, y1 = mm_nn_b("l1_out_proj", o_a, w["bo"][None], [F32, BF16], _residual,
                     (x2[None], _row(mod1, 2)), ("tile", "row"))
    x3, y1 = x3[0], y1[0]
    x4, mlp1, _, _ = _mlp_fwd("l1", x3, _row(norm_mlp, 1), mod1, w["w1_1"], w["w2_1"])
    dx4, d_final, loss, dm1, dgate_mlp1 = final_loss("final_loss", x4, final_norm, target, mlp1[3], _row(mod1, 5))

    reduce = {}
    dx3, (dy1, dgate1), part_w1_1, reduce["w2_1"], dmlp1, d_nmlp1, _, _ = _mlp_bwd(
        "l1", dx4, dm1, x3, _row(norm_mlp, 1), mod1, w["w1_1"], w["w2_1"], mlp1, core, (y1, _row(mod1, 2)))
    do_a = mm_nt_b("l1_out_dgrad", dy1, w["bo"][None], [F32])[0][0]
    g_bo = mm_tn("l1_out_grad", o_a[None], dy1[None], BF16)[0]
    dq, dkv3 = attn_bwd("l1_attn_bwd", q, kv, totals, visited, do_a)
    g_bq = mm_tn("l1_q_grad", h3[None], dq[None], BF16)[0]
    dh3 = mm_nt_b("l1_q_dgrad", dq, w["bq"][None], [F32])[0][0]
    dx2, dsh, dsc, d_nmix1 = norm_mod_bwd("l1_mix_norm_bwd", dh3, x2, _row(norm_mix, 1), _row(mod1, 1), dx3)
    dmod1 = jnp.concatenate([dsh, dsc, dgate1, *dmlp1, dgate_mlp1], axis=0)
    dkv = jnp.concatenate([dkv3[0], dkv3[1]], axis=1).astype(BF16)
    g_kv = mm_tn("kv_grad", hk[None], dkv[None], BF16)[0]
    attn_grads = [_blocked(g_bo), _blocked(g_bq), _blocked(g_kv)]
    (dhk,), from_sibling = mm_nt_b("kv_dgrad", dkv, w["kv"][None], [F32], comm=exchange_hook(attn_grads))
    dhk = dhk[0]
    dx2, dsh, dsc, d_nkv, dm0, dgate_mlp0 = norm_mod_bwd("kv_norm_bwd", dhk, x2, kv_norm, _row(kvmod, 1), dx2,
                                                         (mlp0[3], _row(mod0, 5)))
    dkvmod = jnp.concatenate([dsh, dsc], axis=0)
    attn_parts = [pair_add("grad_pair_add_" + k, g, r, core)
                  for k, g, r in zip(("bo", "bq", "kv"), attn_grads, from_sibling)]
    dx1, (dy0, dgate0), part_w1_0, reduce["w2_0"], dmlp0, d_nmlp0, (recv,), attn_recv = _mlp_bwd(
        "l0", dx2, dm0, x1, _row(norm_mlp, 0), mod0, w["w1_0"], w["w2_0"], mlp0, core, (y0, _row(mod0, 2)),
        scatter_hook([part_w1_1]), scatter_hook(attn_parts))
    reduce["w1_1"] = (part_w1_1, recv)
    for k, part, recv in zip(("bo", "bq", "kv"), attn_parts, attn_recv):
        reduce[k] = (part, recv)
    du = mm_nt_b("l0_out_dgrad", dy0, w["a_out"][None], [F32])[0][0]
    g_a_out = mm_tn("l0_out_grad", u[None], dy0[None], BF16)[0]
    g_a_out = _blocked(g_a_out)
    (dproj3, d_lb, d_out_gain), (recv, from_sibling) = hgrn_bwd(
        "l0_hgrn_bwd", proj3, lb, out_gain, o_h, du, states,
        comm=combine_hooks(scatter_hook([part_w1_0]), exchange_hook([g_a_out], 1)))
    reduce["w1_0"] = (part_w1_0, recv)
    part = [pair_add("grad_pair_add_a_out", g_a_out, from_sibling, core)]
    g_a_in, recv = mm_tn("l0_in_grad", h1[None], dproj3, BF16, comm=scatter_hook(part))
    reduce["a_out"] = (part[0], recv[0])
    part = _pre_reduce("a_in", [("a_in", g_a_in)], core)
    dh1, recv = mm_nt_r("l0_in_dgrad", dproj3, w["a_in"], F32, comm=scatter_hook(part))
    reduce["a_in"] = (part[0], recv[0])
    dx0, dsh, dsc, d_nmix0 = norm_mod_bwd("l0_mix_norm_bwd", dh1, x, _row(norm_mix, 0), _row(mod0, 1), dx1)
    dmod0 = jnp.concatenate([dsh, dsc, dgate0, *dmlp0, dgate_mlp0], axis=0)
    small = dict(norm_mix=(d_nmix0, d_nmix1), norm_mlp=(d_nmlp0, d_nmlp1), kv_norm=d_nkv,
                 final_norm=d_final, lb=d_lb, out_gain=d_out_gain)
    return loss, dx0, reduce, dmod0, dmod1, dkvmod, small


BIG = ("a_in", "w1_0", "w1_1", "w2_0", "w2_1", "a_out", "bq", "bo", "kv")


def kernel(x, c, ada_w, ada_b, norm_mix, norm_mlp, a_w_in, a_lb_logits, a_out_gain, a_w_out, kv_ada_w, kv_ada_b, kv_norm, w_kv, b_w_q, b_w_out, mlp_w1, mlp_w2, final_norm, loss_target, m_ada_w, m_ada_b, m_norm_mix, m_norm_mlp, m_a_w_in, m_a_lb_logits, m_a_out_gain, m_a_w_out, m_kv_ada_w, m_kv_ada_b, m_kv_norm, m_w_kv, m_b_w_q, m_b_w_out, m_mlp_w1, m_mlp_w2, m_final_norm, v_ada_w, v_ada_b, v_norm_mix, v_norm_mlp, v_a_w_in, v_a_lb_logits, v_a_out_gain, v_a_w_out, v_kv_ada_w, v_kv_ada_b, v_kv_norm, v_w_kv, v_b_w_q, v_b_w_out, v_mlp_w1, v_mlp_w2, v_final_norm):
    d = x.shape[-1]
    px, py, pc = _position()
    me = 4 * px + 2 * py + pc
    chip = 2 * px + py
    n_ada = ada_w.shape[2]
    n_kvada = kv_ada_w.shape[1]
    shard_cols = d // N_CHIPS

    def as_rows(a):
        return a.reshape(-1, shard_cols)

    pack1 = jnp.concatenate([as_rows(c), a_lb_logits, a_out_gain,
                             jnp.zeros((1, shard_cols), F32)], axis=0)
    got1 = allgather8("gather_cond", pack1)
    n_c = d // shard_cols
    c_all = got1[:, :n_c, :].reshape(N_DEV, d)
    per_chip = got1[0::2]
    logits_full = jnp.swapaxes(per_chip[:, n_c:n_c + 2, :], 0, 1).reshape(2, d)
    out_gain_full = per_chip[:, n_c + 2, :].reshape(1, d)
    lb = lower_bound_fwd("lower_bound", logits_full)

    bias0 = lax.dynamic_slice_in_dim(ada_b, chip * n_ada, n_ada, axis=1)
    bias_kv = lax.dynamic_slice_in_dim(kv_ada_b.reshape(1, -1), chip * n_kvada, n_kvada, axis=1)
    mod_part = jnp.concatenate([
        ada_project("ada_proj_0", c_all, ada_w[0], bias0[0:1]),
        ada_project("ada_proj_1", c_all, ada_w[1], bias0[1:2]),
        ada_project("ada_proj_kv", c_all, kv_ada_w, bias_kv)], axis=1)
    got2 = allgather8("gather_mod", mod_part)
    mine = lax.dynamic_index_in_dim(got2[0::2], me, axis=1, keepdims=False)
    mod0 = mine[:, :n_ada].reshape(6, d)
    mod1 = mine[:, n_ada:2 * n_ada].reshape(6, d)
    kvmod = mine[:, 2 * n_ada:].reshape(2, d)

    shards = dict(a_in=a_w_in[0], w1_0=mlp_w1[0], w1_1=mlp_w1[1], w2_0=mlp_w2[0], w2_1=mlp_w2[1],
                  a_out=a_w_out[0], bq=b_w_q[0], bo=b_w_out[0], kv=w_kv)
    shards = {k: s.astype(BF16) for k, s in shards.items()}
    g_in, g_out = gather_weights("gather_first_weights", [shards["a_in"], shards["a_out"]])
    core = pc.astype(jnp.int32).reshape(1)

    loss, dx0, reduce, dmod0, dmod1, dkvmod, small = local_step(
        x[0], loss_target[0], mod0, mod1, kvmod, norm_mix, norm_mlp, kv_norm.reshape(1, d),
        final_norm.reshape(1, d), lb, out_gain_full,
        dict(a_in=g_in, a_out=g_out.reshape(-1, g_out.shape[-1])), shards, core)

    loss_row = jnp.concatenate([loss, jnp.zeros((1, shard_cols - loss.shape[1]), F32)], axis=1)
    rows = [as_rows(dmod0), as_rows(dmod1), as_rows(dkvmod),
            as_rows(small["norm_mix"][0]), as_rows(small["norm_mix"][1]),
            as_rows(small["norm_mlp"][0]), as_rows(small["norm_mlp"][1]),
            as_rows(small["kv_norm"]), as_rows(small["final_norm"]),
            as_rows(small["lb"]), as_rows(small["out_gain"]), loss_row]
    n_rows = sum(r.shape[0] for r in rows)
    pad = (-n_rows) % 8
    pack3 = jnp.concatenate(rows + [jnp.zeros((pad, shard_cols), F32)], axis=0)
    got3 = allgather8("gather_small_grads", pack3)
    total = device_sum("sum_small_grads", got3)
    n_mod_rows = (12 * d + 2 * d) // shard_cols
    n_gain_rows = 6 * n_c
    loss_out = total[n_mod_rows + n_gain_rows + 2 * n_c, 0]

    dmod_all = got3[:, :n_mod_rows, :].reshape(N_DEV, 14 * d)
    act_t = jnp.zeros((d, 128), F32).at[:, :N_DEV].set(c_all.T)

    def dmod_cols(lo, width):
        part = lax.dynamic_slice_in_dim(dmod_all, lo + chip * width, width, axis=1)
        return jnp.zeros((128, width), F32).at[:N_DEV].set(part)

    g_ada_w, d_ada_w, nm_ada_w, nv_ada_w = ada_grad_adamw(
        "ada_update", act_t, jnp.stack([dmod_cols(0, n_ada), dmod_cols(6 * d, n_ada)]), ada_w, m_ada_w, v_ada_w)
    g_kv_ada_w, d_kv_ada_w, nm_kv_ada_w, nv_kv_ada_w = (a[0] for a in ada_grad_adamw(
        "ada_update_kv", act_t, dmod_cols(12 * d, n_kvada)[None], kv_ada_w[None], m_kv_ada_w[None],
        v_kv_ada_w[None]))

    chip_parts = [reduce[k][0] for k in BIG]
    from_chips = [reduce[k][1] for k in BIG]
    where = jnp.stack([chip, pc]).astype(jnp.int32)
    out_shapes = [_sds(a_w_in.shape, F32), _sds(mlp_w1.shape, F32), _sds(mlp_w2.shape, F32),
                  _sds(a_w_out.shape, F32), _sds(b_w_q.shape, F32), _sds(b_w_out.shape, F32),
                  _sds(w_kv.shape, F32)]
    targets = [(0, 0), (1, 0), (1, 1), (2, 0), (2, 1), (3, 0), (4, 0), (5, 0), (6, None)]
    slabs = [None] * len(out_shapes)
    places = []
    for k, part, recv, (oi, lead) in zip(BIG, chip_parts, from_chips, targets):
        slabs[oi] = chip_sum("grad_chip_sum_" + k, part, recv, where, out_shapes[oi], lead, slabs[oi])
        places.append((oi, lead, part.shape[1]))
    g_a_w_in, g_mlp_w1, g_mlp_w2, g_a_w_out, g_b_w_q, g_b_w_out, g_w_kv = sibling_share(
        "grad_sibling_share", slabs, places)

    def update(name, wgt, g, m_, v_):
        shape = wgt.shape
        f = lambda a: a.reshape(-1, shape[-1])
        return tuple(o.reshape(shape) for o in adamw(name, f(wgt), f(g), f(m_), f(v_)))

    u_a_w_in = update("adamw_a_w_in", a_w_in, g_a_w_in, m_a_w_in, v_a_w_in)
    u_mlp_w1 = update("adamw_mlp_w1", mlp_w1, g_mlp_w1, m_mlp_w1, v_mlp_w1)
    u_mlp_w2 = update("adamw_mlp_w2", mlp_w2, g_mlp_w2, m_mlp_w2, v_mlp_w2)
    u_a_w_out = update("adamw_a_w_out", a_w_out, g_a_w_out, m_a_w_out, v_a_w_out)
    u_b_w_q = update("adamw_b_w_q", b_w_q, g_b_w_q, m_b_w_q, v_b_w_q)
    u_b_w_out = update("adamw_b_w_out", b_w_out, g_b_w_out, m_b_w_out, v_b_w_out)
    u_w_kv = update("adamw_w_kv", w_kv, g_w_kv, m_w_kv, v_w_kv)

    base = n_mod_rows + n_gain_rows
    d_lb_mine = lax.dynamic_slice_in_dim(total, base + chip, 1, axis=0)
    d_gain_mine = lax.dynamic_slice_in_dim(total, base + n_c + chip, 1, axis=0)
    d_logits = lower_bound_bwd("lower_bound_bwd", a_lb_logits, d_lb_mine)

    def pack_small(ada_b_, kv_ada_b_, norm_mix_, norm_mlp_, kv_norm_, final_norm_, lbl_, gain_):
        parts = [as_rows(ada_b_), as_rows(kv_ada_b_), as_rows(norm_mix_), as_rows(norm_mlp_),
                 as_rows(kv_norm_), as_rows(final_norm_), lbl_, gain_]
        n = sum(p.shape[0] for p in parts)
        return jnp.concatenate(parts + [jnp.zeros(((-n) % 8, shard_cols), F32)], axis=0)

    w_small = pack_small(ada_b, kv_ada_b, norm_mix, norm_mlp, kv_norm, final_norm, a_lb_logits, a_out_gain)
    m_small = pack_small(m_ada_b, m_kv_ada_b, m_norm_mix, m_norm_mlp, m_kv_norm, m_final_norm,
                         m_a_lb_logits, m_a_out_gain)
    v_small = pack_small(v_ada_b, v_kv_ada_b, v_norm_mix, v_norm_mlp, v_kv_norm, v_final_norm,
                         v_a_lb_logits, v_a_out_gain)
    n_small = w_small.shape[0]
    g_small = jnp.concatenate([total[:base], d_logits, d_gain_mine,
                               jnp.zeros((n_small - base - 3, shard_cols), F32)], axis=0)
    small_out = (g_small,) + tuple(adamw("adamw_small", w_small, g_small, m_small, v_small))

    def unpack_small(p):
        out, r0 = [], 0
        for ref in (ada_b, kv_ada_b, norm_mix, norm_mlp, kv_norm, final_norm, a_lb_logits, a_out_gain):
            nr = ref.size // shard_cols
            out.append(p[r0:r0 + nr].reshape(ref.shape))
            r0 += nr
        return out

    sm = [unpack_small(p) for p in small_out]

    def leaves(kind, big_ada, big_kv_ada):
        ada_b_, kv_ada_b_, norm_mix_, norm_mlp_, kv_norm_, final_norm_, lbl_, gain_ = sm[kind]
        pick = (lambda u, g: g) if kind == 0 else (lambda u, g: u[kind - 1])
        return [big_ada, ada_b_, norm_mix_, norm_mlp_, pick(u_a_w_in, g_a_w_in), lbl_, gain_,
                pick(u_a_w_out, g_a_w_out), big_kv_ada, kv_ada_b_, kv_norm_, pick(u_w_kv, g_w_kv),
                pick(u_b_w_q, g_b_w_q), pick(u_b_w_out, g_b_w_out), pick(u_mlp_w1, g_mlp_w1),
                pick(u_mlp_w2, g_mlp_w2), final_norm_]

    return (loss_out, dx0[None],
            *leaves(0, g_ada_w, g_kv_ada_w), *leaves(1, d_ada_w, d_kv_ada_w),
            *leaves(2, nm_ada_w, nm_kv_ada_w), *leaves(3, nv_ada_w, nv_kv_ada_w))
```

```python
import functools

import jax
import jax.numpy as jnp
from jax import lax
from jax.experimental import pallas as pl
from jax.experimental.pallas import tpu as pltpu

F32 = jnp.float32
BF16 = jnp.bfloat16
MESH = pl.DeviceIdType.MESH

HEAD_DIM = 128
KV_GROUP = 4
HGRN_CHUNK = 64
HGRN_HEADS = 4
HGRN_ROWS = 256
NORM_EPS = 1e-6
N_CHIPS = 4
N_DEV = 8
ROW_TILE = 256
ATTN_TILE = 256
VMEM_LIMIT = 56 * 1024 * 1024
DEAD_LOG_WEIGHT = -110.0

ADAM_LR = 0.001
ADAM_B1 = 0.9
ADAM_B2 = 0.999
ADAM_EPS = 1e-08
ADAM_WD = 0.01
ADAM_STEP = 10

NN = (((1,), (0,)), ((), ()))
NT = (((1,), (1,)), ((), ()))
TN = (((0,), (0,)), ((), ()))


def _dot(a, b, dims=NN, precision=None):
    return lax.dot_general(a, b, dims, preferred_element_type=F32, precision=precision)


def _bdot(a, b, dims=NN):
    return _dot(a.astype(BF16), b.astype(BF16), dims)


def _sigmoid(x):
    return 1.0 / (1.0 + jnp.exp(-x))


def _log_sigmoid(z):
    return jnp.minimum(z, 0.0) - jnp.log(1.0 + jnp.exp(-jnp.abs(z)))


def _split_bf16(x):
    hi = x.astype(BF16)
    lo = (x - hi.astype(F32)).astype(BF16)
    return hi, lo


def _params(sem=None):
    return pltpu.CompilerParams(dimension_semantics=sem, vmem_limit_bytes=VMEM_LIMIT)


def _tile(n, pref):
    t = min(n, pref)
    assert n % t == 0, (n, pref)
    return t


def _mm(name, a, b, a_spec, b_spec, grid, n_red, dims, out_shapes, out_specs,
        acc_shape, epilogue=None, extras=(), extra_specs=(), comm=None):
    n_extra = len(extras)
    n_out = len(out_shapes)
    n_cin = len(comm["operands"]) if comm else 0
    n_cout = len(comm["out_shapes"]) if comm else 0
    if epilogue is None:
        epilogue = lambda acc: (acc,)

    def body(*refs):
        a_ref, b_ref = refs[:2]
        ex_refs = refs[2:2 + n_extra]
        cin_refs = refs[2 + n_extra:2 + n_extra + n_cin]
        out_refs = refs[2 + n_extra + n_cin:2 + n_extra + n_cin + n_out]
        cout_refs = refs[2 + n_extra + n_cin + n_out:2 + n_extra + n_cin + n_out + n_cout]
        scratch = refs[2 + n_extra + n_cin + n_out + n_cout:]
        pids = [pl.program_id(ax) for ax in range(len(grid))]
        if comm:
            send_sems, recv_sems = scratch[-2:]

            @pl.when(functools.reduce(jnp.logical_and, [p == 0 for p in pids]))
            def _():
                comm["start"](cin_refs, cout_refs, send_sems, recv_sems)

        prod = _bdot(a_ref[...], b_ref[...], dims)

        def finish(acc):
            res = epilogue(acc, *[e[...] for e in ex_refs])
            for o_ref, r in zip(out_refs, res):
                o_ref[...] = r.astype(o_ref.dtype)

        if n_red == 0:
            finish(prod)
        else:
            acc_ref = scratch[0]
            ids = pids[len(grid) - n_red:]
            sizes = grid[len(grid) - n_red:]
            first = functools.reduce(jnp.logical_and, [i == 0 for i in ids])
            last = functools.reduce(jnp.logical_and, [i == s - 1 for i, s in zip(ids, sizes)])

            @pl.when(first)
            def _():
                acc_ref[...] = prod

            @pl.when(jnp.logical_not(first))
            def _():
                acc_ref[...] += prod

            @pl.when(last)
            def _():
                finish(acc_ref[...])

        if comm:
            @pl.when(functools.reduce(jnp.logical_and, [p == s - 1 for p, s in zip(pids, grid)]))
            def _():
                comm["finish"](cin_refs, cout_refs, send_sems, recv_sems)

    if comm:
        sem = ("arbitrary",) * len(grid)
    else:
        sem = ("parallel",) * (len(grid) - n_red) + ("arbitrary",) * n_red
    scratch_shapes = [pltpu.VMEM(acc_shape, F32)] if n_red else []
    if comm:
        scratch_shapes += [pltpu.SemaphoreType.DMA(comm["sems"]), pltpu.SemaphoreType.DMA(comm["sems"])]
    out = pl.pallas_call(
        body, name=name, grid=grid,
        in_specs=[a_spec, b_spec, *extra_specs] + [HBM] * n_cin,
        out_specs=list(out_specs) + [HBM] * n_cout,
        out_shape=list(out_shapes) + (list(comm["out_shapes"]) if comm else []),
        scratch_shapes=scratch_shapes,
        compiler_params=_params(sem),
    )(a, b, *extras, *(comm["operands"] if comm else ()))
    return (out[:n_out], out[n_out:]) if comm else out


def _sds(shape, dtype):
    return jax.ShapeDtypeStruct(shape, dtype)


def mm_nn_b(name, a, w3, out_dtypes, epilogue=None, extras=(), extra_kinds=(), comm=None):
    m, k = a.shape
    nb, _, n = w3.shape
    tm, tn = _tile(m, 1024), _tile(n, 512)
    grid = (nb, m // tm, n // tn)
    nt = n // tn
    especs = []
    for kind in extra_kinds:
        if kind == "tile":
            especs.append(pl.BlockSpec((None, tm, tn), lambda j, i, c: (j, i, c)))
        else:
            especs.append(pl.BlockSpec((1, tn), lambda j, i, c: (0, j * nt + c)))
    return _mm(name, a, w3,
               pl.BlockSpec((tm, k), lambda j, i, c: (i, 0)),
               pl.BlockSpec((None, k, tn), lambda j, i, c: (j, 0, c)),
               grid, 0, NN,
               [_sds((nb, m, n), d) for d in out_dtypes],
               [pl.BlockSpec((None, tm, tn), lambda j, i, c: (j, i, c)) for _ in out_dtypes],
               None, epilogue, extras, especs, comm)


def mm_nn_r(name, a3, w3, out_dtypes, epilogue=None, extras=(), extra_kinds=(), comm=None):
    nb, m, kb = a3.shape
    n = w3.shape[2]
    tm, tn, tk = _tile(m, 1024), _tile(n, 512), _tile(kb, 2048)
    grid = (m // tm, n // tn, nb, kb // tk)
    especs = []
    for kind in extra_kinds:
        if kind == "tile":
            especs.append(pl.BlockSpec((tm, tn), lambda i, c, j, r: (i, c)))
        else:
            especs.append(pl.BlockSpec((1, tn), lambda i, c, j, r: (0, c)))
    return _mm(name, a3, w3,
               pl.BlockSpec((None, tm, tk), lambda i, c, j, r: (j, i, r)),
               pl.BlockSpec((None, tk, tn), lambda i, c, j, r: (j, r, c)),
               grid, 2, NN,
               [_sds((m, n), d) for d in out_dtypes],
               [pl.BlockSpec((tm, tn), lambda i, c, j, r: (i, c)) for _ in out_dtypes],
               (tm, tn), epilogue, extras, especs, comm)


def mm_nt_b(name, a, w3, out_dtypes, epilogue=None, extras=(), comm=None):
    m, n = a.shape
    nb, kb, _ = w3.shape
    tm, tk = _tile(m, 1024), _tile(kb, 512)
    grid = (nb, m // tm, kb // tk)
    especs = [pl.BlockSpec((None, tm, tk), lambda j, i, c: (j, i, c)) for _ in extras]
    return _mm(name, a, w3,
               pl.BlockSpec((tm, n), lambda j, i, c: (i, 0)),
               pl.BlockSpec((None, tk, n), lambda j, i, c: (j, c, 0)),
               grid, 0, NT,
               [_sds((nb, m, kb), d) for d in out_dtypes],
               [pl.BlockSpec((None, tm, tk), lambda j, i, c: (j, i, c)) for _ in out_dtypes],
               None, epilogue, extras, especs, comm)


def _single(res, comm):
    return res[0] if comm is None else (res[0][0], res[1])


def mm_nt_r(name, a3, w3, out_dtype, comm=None):
    nb, m, n = a3.shape
    k = w3.shape[1]
    tm, tk, tc = _tile(m, 1024), _tile(k, 1024), _tile(n, 2048)
    grid = (m // tm, k // tk, nb, n // tc)
    return _single(_mm(name, a3, w3,
                       pl.BlockSpec((None, tm, tc), lambda i, c, j, r: (j, i, r)),
                       pl.BlockSpec((None, tk, tc), lambda i, c, j, r: (j, c, r)),
                       grid, 2, NT,
                       [_sds((m, k), out_dtype)],
                       [pl.BlockSpec((tm, tk), lambda i, c, j, r: (i, c))],
                       (tm, tk), comm=comm), comm)


def mm_tn(name, a3, d3, out_dtype, comm=None, tiles=(512, 1024, 4096)):
    na, m, kb = a3.shape
    nd, _, n = d3.shape
    nb = max(na, nd)
    tk, tn, tm = _tile(kb, tiles[0]), _tile(n, tiles[1]), _tile(m, tiles[2])
    ja = (lambda j: j) if na > 1 else (lambda j: 0)
    jd = (lambda j: j) if nd > 1 else (lambda j: 0)
    if tm == m:
        return _single(_mm(name, a3, d3,
                           pl.BlockSpec((None, tm, tk), lambda j, c, e: (ja(j), 0, c)),
                           pl.BlockSpec((None, tm, tn), lambda j, c, e: (jd(j), 0, e)),
                           (nb, kb // tk, n // tn), 0, TN,
                           [_sds((nb, kb, n), out_dtype)],
                           [pl.BlockSpec((None, tk, tn), lambda j, c, e: (j, c, e))],
                           None, comm=comm), comm)
    grid = (nb, kb // tk, n // tn, m // tm)
    return _single(_mm(name, a3, d3,
                       pl.BlockSpec((None, tm, tk), lambda j, c, e, r: (ja(j), r, c)),
                       pl.BlockSpec((None, tm, tn), lambda j, c, e, r: (jd(j), r, e)),
                       grid, 1, TN,
                       [_sds((nb, kb, n), out_dtype)],
                       [pl.BlockSpec((None, tk, tn), lambda j, c, e, r: (j, c, e))],
                       (tk, tn), comm=comm), comm)


def _row_spec(ts, d):
    return pl.BlockSpec((ts, d), lambda i: (i, 0))


def _vec_spec(d):
    return pl.BlockSpec((1, d), lambda i: (0, 0))


def norm_mod_fwd(name, x, gain, scale, shift):
    s, d = x.shape
    ts = _tile(s, ROW_TILE)

    def body(x_ref, g_ref, sc_ref, sh_ref, h_ref):
        xv = x_ref[...]
        inv = lax.rsqrt(jnp.mean(xv * xv, axis=-1, keepdims=True) + NORM_EPS)
        h = (xv * inv) * g_ref[...] * (1.0 + sc_ref[...]) + sh_ref[...]
        h_ref[...] = h.astype(h_ref.dtype)

    return pl.pallas_call(
        body, name=name, grid=(s // ts,),
        in_specs=[_row_spec(ts, d), _vec_spec(d), _vec_spec(d), _vec_spec(d)],
        out_specs=_row_spec(ts, d), out_shape=_sds((s, d), BF16),
        compiler_params=_params(("parallel",)),
    )(x, gain, scale, shift)


def _gate_bwd(dxv, y_ref, gate_ref, dy_ref, dgate_ref):
    dy_ref[...] = (dxv * gate_ref[...]).astype(dy_ref.dtype)
    dgate_ref[...] += jnp.sum(dxv * y_ref[...].astype(F32), axis=0, keepdims=True)


def norm_mod_bwd(name, dh, x, gain, scale, dres, branch=None):
    s, d = x.shape
    ts = _tile(s, ROW_TILE)

    def body(dh_ref, x_ref, g_ref, sc_ref, dres_ref, *rest):
        if branch:
            y_ref, gate_ref, dx_ref, dsh_ref, dsc_ref, dg_ref, dy_ref, dgate_ref = rest
        else:
            dx_ref, dsh_ref, dsc_ref, dg_ref = rest

        @pl.when(pl.program_id(0) == 0)
        def _():
            dsh_ref[...] = jnp.zeros_like(dsh_ref)
            dsc_ref[...] = jnp.zeros_like(dsc_ref)
            dg_ref[...] = jnp.zeros_like(dg_ref)
            if branch:
                dgate_ref[...] = jnp.zeros_like(dgate_ref)

        xv = x_ref[...]
        dhv = dh_ref[...].astype(F32)
        g = g_ref[...]
        inv = lax.rsqrt(jnp.mean(xv * xv, axis=-1, keepdims=True) + NORM_EPS)
        n = xv * inv
        dhn = dhv * (1.0 + sc_ref[...])
        dn = dhn * g
        dx = dres_ref[...] + inv * (dn - n * jnp.mean(dn * n, axis=-1, keepdims=True))
        dx_ref[...] = dx
        dsh_ref[...] += jnp.sum(dhv, axis=0, keepdims=True)
        dsc_ref[...] += jnp.sum(dhv * (n * g), axis=0, keepdims=True)
        dg_ref[...] += jnp.sum(dhn * n, axis=0, keepdims=True)
        if branch:
            _gate_bwd(dx, y_ref, gate_ref, dy_ref, dgate_ref)

    row, vec = _row_spec(ts, d), _vec_spec(d)
    return pl.pallas_call(
        body, name=name, grid=(s // ts,),
        in_specs=[row, row, vec, vec, row] + ([row, vec] if branch else []),
        out_specs=[row, vec, vec, vec] + ([row, vec] if branch else []),
        out_shape=[_sds((s, d), F32), _sds((1, d), F32), _sds((1, d), F32), _sds((1, d), F32)]
        + ([_sds((s, d), BF16), _sds((1, d), F32)] if branch else []),
        compiler_params=_params(("arbitrary",)),
    )(dh, x, gain, scale, dres, *(branch or ()))


def final_loss(name, x, gain, target, y, gate):
    s, d = x.shape
    ts = _tile(s, ROW_TILE)

    def body(x_ref, g_ref, t_ref, y_ref, gate_ref, dx_ref, dg_ref, loss_ref, dy_ref, dgate_ref):
        @pl.when(pl.program_id(0) == 0)
        def _():
            dg_ref[...] = jnp.zeros_like(dg_ref)
            loss_ref[...] = jnp.zeros_like(loss_ref)
            dgate_ref[...] = jnp.zeros_like(dgate_ref)

        xv = x_ref[...]
        g = g_ref[...]
        inv = lax.rsqrt(jnp.mean(xv * xv, axis=-1, keepdims=True) + NORM_EPS)
        n = xv * inv
        diff = n * g - t_ref[...]
        per_tok = jnp.mean(diff * diff, axis=-1, keepdims=True)
        loss_ref[...] += 0.5 * jnp.sum(per_tok, axis=0, keepdims=True)
        dout = diff * (1.0 / d)
        dg_ref[...] += jnp.sum(dout * n, axis=0, keepdims=True)
        dn = dout * g
        dx = inv * (dn - n * jnp.mean(dn * n, axis=-1, keepdims=True))
        dx_ref[...] = dx
        _gate_bwd(dx, y_ref, gate_ref, dy_ref, dgate_ref)

    row, vec = _row_spec(ts, d), _vec_spec(d)
    return pl.pallas_call(
        body, name=name, grid=(s // ts,),
        in_specs=[row, vec, row, row, vec],
        out_specs=[row, vec, _vec_spec(128), row, vec],
        out_shape=[_sds((s, d), F32), _sds((1, d), F32), _sds((1, 128), F32), _sds((s, d), BF16), _sds((1, d), F32)],
        compiler_params=_params(("arbitrary",)),
    )(x, gain, target, y, gate)


def _hgrn_chunk_fwd(qr, fl, lbv, tri):
    sg = _sigmoid(fl)
    sgm = _sigmoid(-fl)
    f = lbv + (1.0 - lbv) * sg
    logf = jnp.log(f)
    k = (1.0 - lbv) * sgm
    cum = _dot(tri, logf, precision=lax.Precision.HIGHEST)
    cl = cum[HGRN_CHUNK - 1:HGRN_CHUNK, :]
    e = jnp.exp(cum)
    en = jnp.exp(-cum)
    es = jnp.exp(cl - cum)
    sq = _sigmoid(qr)
    qs = qr * sq
    return dict(sg=sg, sgm=sgm, f=f, k=k, cum=cum, cl=cl, e=e, en=en, es=es, sq=sq, qs=qs,
                qd=qs * e, ki=k * en, ks=k * es, dec=jnp.exp(cl))


def _tri_masks(strict=False):
    r = lax.broadcasted_iota(jnp.int32, (HGRN_CHUNK, HGRN_CHUNK), 0)
    c = lax.broadcasted_iota(jnp.int32, (HGRN_CHUNK, HGRN_CHUNK), 1)
    return (r > c) if strict else (r >= c)


def _carried(comm, grid, n_in, n_out):
    if not comm:
        return [], [], [], [], lambda refs: (refs, lambda: None, lambda: None)
    n_cin, n_cout = len(comm["operands"]), len(comm["out_shapes"])
    sems = [pltpu.SemaphoreType.DMA(comm["sems"]), pltpu.SemaphoreType.DMA(comm["sems"])]

    def split(refs):
        ins, cin = refs[:n_in], refs[n_in:n_in + n_cin]
        outs = refs[n_in + n_cin:n_in + n_cin + n_out]
        cout = refs[n_in + n_cin + n_out:n_in + n_cin + n_out + n_cout]
        scratch = refs[n_in + n_cin + n_out + n_cout:]
        send_sems, recv_sems = scratch[-2:]
        pids = [pl.program_id(ax) for ax in range(len(grid))]

        def start():
            @pl.when(functools.reduce(jnp.logical_and, [p == 0 for p in pids]))
            def _():
                comm["start"](cin, cout, send_sems, recv_sems)

        def finish():
            @pl.when(functools.reduce(jnp.logical_and, [p == n - 1 for p, n in zip(pids, grid)]))
            def _():
                comm["finish"](cin, cout, send_sems, recv_sems)

        return ins + outs + scratch[:-2], start, finish

    return [HBM] * n_cin, [HBM] * n_cout, list(comm["out_shapes"]), sems, split


def hgrn_fwd(name, proj3, lb, out_gain, comm=None):
    _, s, d = proj3.shape
    heads = d // HEAD_DIM
    t_rows = _tile(s, HGRN_ROWS)
    n_t = s // t_rows
    n_c = t_rows // HGRN_CHUNK
    width = HGRN_HEADS * HEAD_DIM
    grid = (heads // HGRN_HEADS, n_t)
    c_in, c_out, c_shapes, c_sems, split = _carried(comm, grid, 6, 3)

    def body(*refs):
        (q_ref, f_ref, i_ref, g_ref, lb_ref, gain_ref, u_ref, o_ref, st_ref, state), start, finish = split(refs)
        start()
        compute(q_ref, f_ref, i_ref, g_ref, lb_ref, gain_ref, u_ref, o_ref, st_ref, state)
        finish()

    def compute(q_ref, f_ref, i_ref, g_ref, lb_ref, gain_ref, u_ref, o_ref, st_ref, state):
        @pl.when(pl.program_id(1) == 0)
        def _():
            state[...] = jnp.zeros_like(state)

        causal = _tri_masks()
        tri = causal.astype(F32)
        hs = range(HGRN_HEADS)
        col = [pl.ds(hh * HEAD_DIM, HEAD_DIM) for hh in hs]
        for ci in range(n_c):
            rows = pl.ds(ci * HGRN_CHUNK, HGRN_CHUNK)
            c = [_hgrn_chunk_fwd(q_ref[rows, col[hh]], f_ref[rows, col[hh]], lb_ref[:, col[hh]], tri) for hh in hs]
            v = [i_ref[rows, col[hh]] for hh in hs]
            st = [state[hh] for hh in hs]
            scores = [jnp.where(causal, _bdot(c[hh]["qd"], c[hh]["ki"], NT), 0.0) for hh in hs]
            inter = [_bdot(c[hh]["qd"], st[hh], NT) for hh in hs]
            update = [_bdot(v[hh], c[hh]["ks"], TN) for hh in hs]
            o = [_bdot(scores[hh], v[hh]) + inter[hh] for hh in hs]
            for hh in hs:
                st_ref[hh, ci] = st[hh]
                state[hh] = st[hh] * c[hh]["dec"] + update[hh]
            for hh in hs:
                graw = g_ref[rows, col[hh]]
                rms = lax.rsqrt(jnp.mean(o[hh] * o[hh], axis=-1, keepdims=True) + NORM_EPS)
                u = o[hh] * rms * gain_ref[:, col[hh]] * (graw * _sigmoid(graw))
                o_ref[rows, col[hh]] = o[hh]
                u_ref[rows, col[hh]] = u.astype(u_ref.dtype)

    def pspec(blk):
        return pl.BlockSpec((None, t_rows, width), lambda h, t: (blk, t, h))

    hspec = pl.BlockSpec((1, width), lambda h, t: (0, h))
    ospec = pl.BlockSpec((t_rows, width), lambda h, t: (t, h))
    out = pl.pallas_call(
        body, name=name, grid=grid,
        in_specs=[pspec(0), pspec(1), pspec(2), pspec(3), hspec, hspec] + c_in,
        out_specs=[ospec, ospec,
                   pl.BlockSpec((HGRN_HEADS, n_c, HEAD_DIM, HEAD_DIM), lambda h, t: (h, t, 0, 0))] + c_out,
        out_shape=[_sds((s, d), BF16), _sds((s, d), F32),
                   _sds((heads, s // HGRN_CHUNK, HEAD_DIM, HEAD_DIM), F32)] + c_shapes,
        scratch_shapes=[pltpu.VMEM((HGRN_HEADS, HEAD_DIM, HEAD_DIM), F32)] + c_sems,
        compiler_params=_params(("arbitrary", "arbitrary")),
    )(proj3, proj3, proj3, proj3, lb, out_gain, *(comm["operands"] if comm else ()))
    return (out[:3], out[3:]) if comm else out


def hgrn_bwd(name, proj3, lb, out_gain, o, du, states, comm=None):
    _, s, d = proj3.shape
    heads = d // HEAD_DIM
    t_rows = _tile(s, HGRN_ROWS)
    n_t = s // t_rows
    n_c = t_rows // HGRN_CHUNK
    width = HGRN_HEADS * HEAD_DIM
    grid = (heads // HGRN_HEADS, n_t)
    c_in, c_out, c_shapes, c_sems, split = _carried(comm, grid, 9, 3)

    def body(*refs):
        refs, start, finish = split(refs)
        start()
        compute(*refs)
        finish()

    def compute(q_ref, f_ref, i_ref, g_ref, lb_ref, gain_ref, o_ref, du_ref, st_ref,
             dp_ref, dlb_ref, dgain_ref, dstate):
        @pl.when(pl.program_id(1) == 0)
        def _():
            dstate[...] = jnp.zeros_like(dstate)
            dlb_ref[...] = jnp.zeros_like(dlb_ref)
            dgain_ref[...] = jnp.zeros_like(dgain_ref)

        causal = _tri_masks()
        tri = causal.astype(F32)
        tri_t = jnp.logical_not(_tri_masks(strict=True)).astype(F32)
        hs = range(HGRN_HEADS)
        col = [pl.ds(hh * HEAD_DIM, HEAD_DIM) for hh in hs]
        lbv = [lb_ref[:, col[hh]] for hh in hs]
        gain = [gain_ref[:, col[hh]] for hh in hs]
        for ci in reversed(range(n_c)):
            rows = pl.ds(ci * HGRN_CHUNK, HGRN_CHUNK)
            qr = [q_ref[rows, col[hh]] for hh in hs]
            c = [_hgrn_chunk_fwd(qr[hh], f_ref[rows, col[hh]], lbv[hh], tri) for hh in hs]
            v = [i_ref[rows, col[hh]] for hh in hs]
            st = [st_ref[hh, ci] for hh in hs]
            dst = [dstate[hh] for hh in hs]
            do, dgraw = [], []
            for hh in hs:
                ov = o_ref[rows, col[hh]]
                duv = du_ref[rows, col[hh]].astype(F32)
                graw = g_ref[rows, col[hh]]
                sgg = _sigmoid(graw)
                gate = graw * sgg
                rms = lax.rsqrt(jnp.mean(ov * ov, axis=-1, keepdims=True) + NORM_EPS)
                on = ov * rms
                dgain_ref[:, col[hh]] += jnp.sum(duv * on * gate, axis=0, keepdims=True)
                dgraw.append(duv * on * gain[hh] * (sgg * (1.0 + graw * (1.0 - sgg))))
                don = duv * gain[hh] * gate
                do.append(rms * (don - on * jnp.mean(don * on, axis=-1, keepdims=True)))
            qd = [c[hh]["qd"] for hh in hs]
            ki = [c[hh]["ki"] for hh in hs]
            ks = [c[hh]["ks"] for hh in hs]
            p = [jnp.where(causal, _bdot(qd[hh], ki[hh], NT), 0.0) for hh in hs]
            dp = [jnp.where(causal, _bdot(do[hh], v[hh], NT), 0.0) for hh in hs]
            from_state = [_bdot(do[hh], st[hh]) for hh in hs]
            dks = [_bdot(v[hh], dst[hh]) for hh in hs]
            dv_state = [_bdot(ks[hh], dst[hh], NT) for hh in hs]
            dstate_new = [_bdot(do[hh], qd[hh], TN) for hh in hs]
            dqd = [_bdot(dp[hh], ki[hh]) + from_state[hh] for hh in hs]
            dki = [_bdot(dp[hh], qd[hh], TN) for hh in hs]
            dv = [_bdot(p[hh], do[hh], TN) + dv_state[hh] for hh in hs]
            ddec = [jnp.sum(dst[hh] * st[hh], axis=0, keepdims=True) for hh in hs]
            for hh in hs:
                dstate[hh] = dst[hh] * c[hh]["dec"] + dstate_new[hh]
            dcum = [dqd[hh] * qd[hh] - dki[hh] * ki[hh] - dks[hh] * ks[hh] for hh in hs]
            dcl = [jnp.sum(dks[hh] * ks[hh], axis=0, keepdims=True) + ddec[hh] * c[hh]["dec"] for hh in hs]
            dlogf = [_dot(tri_t, dcum[hh], precision=lax.Precision.HIGHEST) + dcl[hh] for hh in hs]
            for hh in hs:
                ch = c[hh]
                dqs = dqd[hh] * ch["e"]
                dk = dki[hh] * ch["en"] + dks[hh] * ch["es"]
                df = dlogf[hh] / ch["f"]
                sg, sgm, sq = ch["sg"], ch["sgm"], ch["sq"]
                one_m_lb = 1.0 - lbv[hh]
                dlb_ref[:, col[hh]] += jnp.sum(df * (1.0 - sg) - dk * sgm, axis=0, keepdims=True)
                dfl = df * one_m_lb * sg * (1.0 - sg) - dk * one_m_lb * sgm * (1.0 - sgm)
                dqr = dqs * (sq * (1.0 + qr[hh] * (1.0 - sq)))
                dp_ref[0, rows, col[hh]] = dqr.astype(dp_ref.dtype)
                dp_ref[1, rows, col[hh]] = dfl.astype(dp_ref.dtype)
                dp_ref[2, rows, col[hh]] = dv[hh].astype(dp_ref.dtype)
                dp_ref[3, rows, col[hh]] = dgraw[hh].astype(dp_ref.dtype)

    def pspec(blk):
        return pl.BlockSpec((None, t_rows, width), lambda h, t: (blk, n_t - 1 - t, h))

    hspec = pl.BlockSpec((1, width), lambda h, t: (0, h))
    ospec = pl.BlockSpec((t_rows, width), lambda h, t: (n_t - 1 - t, h))
    out = pl.pallas_call(
        body, name=name, grid=grid,
        in_specs=[pspec(0), pspec(1), pspec(2), pspec(3), hspec, hspec, ospec, ospec,
                  pl.BlockSpec((HGRN_HEADS, n_c, HEAD_DIM, HEAD_DIM), lambda h, t: (h, n_t - 1 - t, 0, 0))] + c_in,
        out_specs=[pl.BlockSpec((4, t_rows, width), lambda h, t: (0, n_t - 1 - t, h)), hspec, hspec] + c_out,
        out_shape=[_sds((4, s, d), BF16), _sds((1, d), F32), _sds((1, d), F32)] + c_shapes,
        scratch_shapes=[pltpu.VMEM((HGRN_HEADS, HEAD_DIM, HEAD_DIM), F32)] + c_sems,
        compiler_params=_params(("arbitrary", "arbitrary")),
    )(proj3, proj3, proj3, proj3, lb, out_gain, o, du, states, *(comm["operands"] if comm else ()))
    return (out[:3], out[3:]) if comm else out


def attn_fwd(name, q, kv, comm=None):
    s, dq = q.shape
    kvh = kv.shape[1] // (2 * HEAD_DIM)
    assert dq == kvh * KV_GROUP * HEAD_DIM
    tq = _tile(s, ATTN_TILE)
    scale = HEAD_DIM ** -0.5

    grid = (kvh, s // tq)
    c_in, c_out, c_shapes, c_sems, split = _carried(comm, grid, 3, 3)

    def body(*refs):
        (q_ref, k_ref, v_ref, o_ref, tot_ref, cnt_ref), start, finish = split(refs)
        start()
        compute(q_ref, k_ref, v_ref, o_ref, tot_ref, cnt_ref)
        finish()

    def compute(q_ref, k_ref, v_ref, o_ref, tot_ref, cnt_ref):
        i = pl.program_id(1)
        heads = range(KV_GROUP)
        qs = [q_ref[:, g * HEAD_DIM:(g + 1) * HEAD_DIM] for g in heads]
        r_i = lax.broadcasted_iota(jnp.int32, (tq, tq), 0)
        c_i = lax.broadcasted_iota(jnp.int32, (tq, tq), 1)
        later = (r_i > c_i).astype(BF16)
        later2 = jnp.concatenate([later, later], axis=0)
        mask = c_i < r_i
        ones2 = jnp.ones((8, 2 * tq), BF16)

        def block(j, carry, masked):
            rows = pl.ds(pl.multiple_of(j * tq, tq), tq)
            kj = k_ref[rows, :]
            vj = v_ref[rows, :]
            z = [_dot(qs[g], kj, NT) * scale for g in heads]
            lbeta = [_log_sigmoid(z[g]) for g in heads]
            lrest = [lbeta[g] - z[g] for g in heads]
            if masked:
                lrest = [jnp.where(mask, lrest[g], 0.0) for g in heads]
            hl = [jnp.concatenate(_split_bf16(lrest[g]), axis=1) for g in heads]
            between = [_dot(hl[g], later2) + carry[g][0] for g in heads]
            sums = [_dot(ones2, hl[g], NT) for g in heads]
            w = [jnp.exp(lbeta[g] + between[g]) for g in heads]
            if masked:
                w = [jnp.where(mask, w[g], 0.0) for g in heads]
            pv = [_dot(w[g].astype(BF16), vj) for g in heads]
            return tuple((carry[g][0] + jnp.sum(lrest[g], axis=1, keepdims=True), carry[g][1] + pv[g],
                          carry[g][2] + sums[g]) for g in heads)

        def alive(carry):
            top = carry[0][0]
            for g in heads[1:]:
                top = jnp.maximum(top, carry[g][0])
            return jnp.max(top) > DEAD_LOG_WEIGHT

        zero = (jnp.zeros((tq, 1), F32), jnp.zeros((tq, HEAD_DIM), F32), jnp.zeros((8, tq), F32))
        carry = block(i, (zero,) * KV_GROUP, True)

        def step(state):
            jj, _, cr = state
            cr = block(i - 1 - jj, cr, False)
            return jj + 1, alive(cr), cr

        done, _, carry = lax.while_loop(lambda st: jnp.logical_and(st[0] < i, st[1]), step,
                                        (jnp.int32(0), alive(carry), carry))
        for g in heads:
            o_ref[:, g * HEAD_DIM:(g + 1) * HEAD_DIM] = carry[g][1]
            tot_ref[g] = carry[g][2][0:1, :]
        cnt_ref[pl.program_id(0), i] = done

    group = KV_GROUP * HEAD_DIM
    out = pl.pallas_call(
        body, name=name, grid=grid,
        in_specs=[pl.BlockSpec((tq, group), lambda kh, i: (i, kh)),
                  pl.BlockSpec((s, HEAD_DIM), lambda kh, i: (0, kh)),
                  pl.BlockSpec((s, HEAD_DIM), lambda kh, i: (0, kvh + kh))] + c_in,
        out_specs=[pl.BlockSpec((tq, group), lambda kh, i: (i, kh)),
                   pl.BlockSpec((KV_GROUP, 1, tq), lambda kh, i: (kh, 0, i)),
                   pl.BlockSpec(memory_space=pltpu.SMEM)] + c_out,
        out_shape=[_sds((s, dq), F32), _sds((dq // HEAD_DIM, 1, s), F32),
                   _sds((kvh, s // tq), jnp.int32)] + c_shapes,
        scratch_shapes=c_sems,
        compiler_params=_params(("arbitrary", "arbitrary")),
    )(q, kv, kv, *(comm["operands"] if comm else ()))
    return (out[:3], out[3:]) if comm else out


def attn_bwd(name, q, kv, totals, visited, do):
    s, dq_cols = q.shape
    kvh = kv.shape[1] // (2 * HEAD_DIM)
    tq = _tile(s, ATTN_TILE)
    scale = HEAD_DIM ** -0.5

    def body(cnt_ref, q_ref, k_ref, v_ref, tot_ref, do_ref, dq_ref, dkv_ref):
        i = pl.program_id(1)
        first = i - jnp.clip(cnt_ref[pl.program_id(0), i], 0, i)

        @pl.when(i == 0)
        def _():
            dkv_ref[...] = jnp.zeros_like(dkv_ref)

        heads = range(KV_GROUP)
        qs = [q_ref[:, g * HEAD_DIM:(g + 1) * HEAD_DIM] for g in heads]
        dobs = [do_ref[:, g * HEAD_DIM:(g + 1) * HEAD_DIM].astype(BF16) for g in heads]
        tots = [tot_ref[g] for g in heads]
        q_all = jnp.concatenate(qs, axis=0)
        do_all = jnp.concatenate(dobs, axis=0)
        r_i = lax.broadcasted_iota(jnp.int32, (tq, tq), 0)
        c_i = lax.broadcasted_iota(jnp.int32, (tq, tq), 1)
        upto = (c_i <= r_i).astype(BF16)
        before = (c_i < r_i).astype(BF16)
        upto2 = jnp.concatenate([upto, upto], axis=1)
        before2 = jnp.concatenate([before, before], axis=1)
        mask = r_i < c_i

        def block(j, carry, masked):
            rows = pl.ds(pl.multiple_of(j * tq, tq), tq)
            kj = k_ref[rows, :]
            vj = v_ref[rows, :]
            zt = [_dot(kj, qs[g], NT) * scale for g in heads]
            dwt = [_dot(vj, dobs[g], NT) for g in heads]
            lbeta = [_log_sigmoid(zt[g]) for g in heads]
            lrest_raw = [lbeta[g] - zt[g] for g in heads]
            lrest = [jnp.where(mask, lrest_raw[g], 0.0) for g in heads] if masked else lrest_raw
            hl = [jnp.concatenate(_split_bf16(lrest[g]), axis=0) for g in heads]
            upto_sum = [_dot(upto2, hl[g]) for g in heads]
            wt = [jnp.exp(lbeta[g] + (tots[g] - carry[g][0] - upto_sum[g])) for g in heads]
            if masked:
                wt = [jnp.where(mask, wt[g], 0.0) for g in heads]
            dat = [dwt[g] * wt[g] for g in heads]
            earlier = [_dot(before2, jnp.concatenate(_split_bf16(dat[g]), axis=0)) for g in heads]
            dzt = [dat[g] * jnp.exp(lrest_raw[g]) - (carry[g][1] + earlier[g]) * jnp.exp(lbeta[g]) for g in heads]
            if masked:
                dzt = [jnp.where(mask, dzt[g], 0.0) for g in heads]
            dzb = [(dzt[g] * scale).astype(BF16) for g in heads]
            dq_new = [_dot(dzb[g], kj, TN) for g in heads]
            dkv_ref[0, rows, :] += _dot(jnp.concatenate(dzb, axis=1), q_all)
            dkv_ref[1, rows, :] += _dot(jnp.concatenate([wt[g].astype(BF16) for g in heads], axis=1), do_all)
            return tuple((carry[g][0] + jnp.sum(lrest[g], axis=0, keepdims=True),
                          carry[g][1] + jnp.sum(dat[g], axis=0, keepdims=True),
                          carry[g][2] + dq_new[g]) for g in heads)

        zrow = jnp.zeros((1, tq), F32)
        carry = ((zrow, zrow, jnp.zeros((tq, HEAD_DIM), F32)),) * KV_GROUP
        carry = lax.fori_loop(first, i, lambda j, cr: block(j, cr, False), carry)
        carry = block(i, carry, True)
        for g in heads:
            dq_ref[:, g * HEAD_DIM:(g + 1) * HEAD_DIM] = carry[g][2].astype(dq_ref.dtype)

    group = KV_GROUP * HEAD_DIM
    qspec = pl.BlockSpec((tq, group), lambda kh, i, cnt: (i, kh))
    return pl.pallas_call(
        body, name=name,
        grid_spec=pltpu.PrefetchScalarGridSpec(
            num_scalar_prefetch=1, grid=(kvh, s // tq),
            in_specs=[qspec,
                      pl.BlockSpec((s, HEAD_DIM), lambda kh, i, cnt: (0, kh)),
                      pl.BlockSpec((s, HEAD_DIM), lambda kh, i, cnt: (0, kvh + kh)),
                      pl.BlockSpec((KV_GROUP, 1, tq), lambda kh, i, cnt: (kh, 0, i)),
                      qspec],
            out_specs=[qspec, pl.BlockSpec((2, s, HEAD_DIM), lambda kh, i, cnt: (0, 0, kh))]),
        out_shape=[_sds((s, dq_cols), BF16), _sds((2, s, kvh * HEAD_DIM), F32)],
        compiler_params=_params(("parallel", "arbitrary")),
    )(visited, q, kv, kv, totals, do)


def ada_project(name, c_all, w, b):
    bsz, d = c_all.shape
    n = w.shape[1]
    tn = _tile(n, 512)

    def body(c_ref, w_ref, b_ref, o_ref):
        cv = c_ref[...]
        act = cv * _sigmoid(cv)
        o_ref[...] = _bdot(act, w_ref[...]) + b_ref[...]

    return pl.pallas_call(
        body, name=name, grid=(n // tn,),
        in_specs=[pl.BlockSpec((bsz, d), lambda i: (0, 0)),
                  pl.BlockSpec((d, tn), lambda i: (0, i)),
                  pl.BlockSpec((1, tn), lambda i: (0, i))],
        out_specs=pl.BlockSpec((bsz, tn), lambda i: (0, i)),
        out_shape=_sds((bsz, n), F32),
        compiler_params=_params(("parallel",)),
    )(c_all, w, b)


def _adamw_math(w, g, m, v):
    m = ADAM_B1 * m + (1.0 - ADAM_B1) * g
    v = ADAM_B2 * v + (1.0 - ADAM_B2) * (g * g)
    m_hat = m / (1.0 - ADAM_B1 ** ADAM_STEP)
    v_hat = v / (1.0 - ADAM_B2 ** ADAM_STEP)
    delta = -ADAM_LR * (m_hat / (jnp.sqrt(v_hat) + ADAM_EPS) + ADAM_WD * w)
    return delta, m, v


def adamw(name, w, g, m, v):
    r, c = w.shape
    tr = _tile(r, 256)
    tc = _tile(c, 2048)

    def body(w_ref, g_ref, m_ref, v_ref, d_ref, mo_ref, vo_ref):
        delta, mn, vn = _adamw_math(w_ref[...], g_ref[...], m_ref[...], v_ref[...])
        d_ref[...] = delta
        mo_ref[...] = mn
        vo_ref[...] = vn

    spec = pl.BlockSpec((tr, tc), lambda i, j: (i, j))
    return pl.pallas_call(
        body, name=name, grid=(r // tr, c // tc),
        in_specs=[spec] * 4, out_specs=[spec] * 3,
        out_shape=[_sds((r, c), F32)] * 3,
        compiler_params=_params(("parallel", "parallel")),
    )(w, g, m, v)


def ada_grad_adamw(name, c_t, dmod, w, m, v):
    layers, d, n = w.shape
    tr = _tile(d, 256)
    tc = _tile(n, 512)

    def body(a_ref, dm_ref, w_ref, m_ref, v_ref, g_ref, d_ref, mo_ref, vo_ref):
        cv = a_ref[...]
        g = _bdot(cv * _sigmoid(cv), dm_ref[...])
        delta, mn, vn = _adamw_math(w_ref[...], g, m_ref[...], v_ref[...])
        g_ref[...] = g
        d_ref[...] = delta
        mo_ref[...] = mn
        vo_ref[...] = vn

    spec = pl.BlockSpec((None, tr, tc), lambda l, i, j: (l, i, j))
    return pl.pallas_call(
        body, name=name, grid=(layers, d // tr, n // tc),
        in_specs=[pl.BlockSpec((tr, 128), lambda l, i, j: (i, 0)),
                  pl.BlockSpec((None, 128, tc), lambda l, i, j: (l, 0, j)), spec, spec, spec],
        out_specs=[spec] * 4, out_shape=[_sds((layers, d, n), F32)] * 4,
        compiler_params=_params(("parallel", "parallel", "parallel")),
    )(c_t, dmod, w, m, v)


def device_sum(name, gathered):
    _, r, c = gathered.shape

    def body(g_ref, o_ref):
        acc = g_ref[0]
        for dev in range(1, N_DEV):
            acc = acc + g_ref[dev]
        o_ref[...] = acc

    return pl.pallas_call(
        body, name=name,
        in_specs=[pl.BlockSpec(memory_space=pltpu.VMEM)],
        out_specs=pl.BlockSpec(memory_space=pltpu.VMEM),
        out_shape=_sds((r, c), F32),
    )(gathered)


def lower_bound_fwd(name, logits):
    _, d = logits.shape

    def body(l_ref, o_ref):
        l0 = l_ref[0:1, :]
        l1 = l_ref[1:2, :]
        mx = jnp.maximum(l0, l1)
        e0 = jnp.exp(l0 - mx)
        e1 = jnp.exp(l1 - mx)
        o_ref[...] = e0 / (e0 + e1)

    return pl.pallas_call(
        body, name=name,
        in_specs=[pl.BlockSpec(memory_space=pltpu.VMEM)],
        out_specs=pl.BlockSpec(memory_space=pltpu.VMEM),
        out_shape=_sds((1, d), F32),
    )(logits)


def lower_bound_bwd(name, logits, dlb):
    _, d = logits.shape

    def body(l_ref, dlb_ref, o_ref):
        l0 = l_ref[0:1, :]
        l1 = l_ref[1:2, :]
        mx = jnp.maximum(l0, l1)
        e0 = jnp.exp(l0 - mx)
        e1 = jnp.exp(l1 - mx)
        p0 = e0 / (e0 + e1)
        p1 = e1 / (e0 + e1)
        g = dlb_ref[...] * p0 * p1
        o_ref[0:1, :] = g
        o_ref[1:2, :] = -g

    return pl.pallas_call(
        body, name=name,
        in_specs=[pl.BlockSpec(memory_space=pltpu.VMEM)] * 2,
        out_specs=pl.BlockSpec(memory_space=pltpu.VMEM),
        out_shape=_sds((2, d), F32),
    )(logits, dlb)


HBM = pl.BlockSpec(memory_space=pltpu.HBM)


def _position():
    return lax.axis_index("x"), lax.axis_index("y"), lax.axis_index("c")


def _remote(src, dst, send_sem, recv_sem, device):
    return pltpu.make_async_remote_copy(src_ref=src, dst_ref=dst, send_sem=send_sem, recv_sem=recv_sem,
                                        device_id=device, device_id_type=MESH)


def allgather8(name, x):
    r, c = x.shape

    def body(x_ref, out_ref, send_sems, recv_sems):
        px, py, pc = _position()
        me = 4 * px + 2 * py + pc
        out_ref[me] = x_ref[...]
        copies = []
        for k in range(1, N_DEV):
            peer = (1 - px if k & 4 else px, 1 - py if k & 2 else py, 1 - pc if k & 1 else pc)
            cp = _remote(x_ref, out_ref.at[me], send_sems.at[k - 1], recv_sems.at[k - 1], peer)
            cp.start()
            copies.append(cp)
        for cp in copies:
            cp.wait()

    return pl.pallas_call(
        body, name=name,
        in_specs=[pl.BlockSpec(memory_space=pltpu.VMEM)],
        out_specs=pl.BlockSpec(memory_space=pltpu.VMEM),
        out_shape=_sds((N_DEV, r, c), x.dtype),
        scratch_shapes=[pltpu.SemaphoreType.DMA((N_DEV - 1,)), pltpu.SemaphoreType.DMA((N_DEV - 1,))],
    )(x)


def _other_chips(px, py):
    return [(1 - px, py), (px, 1 - py), (1 - px, 1 - py)]


def gather_weights(name, shards):
    n = len(shards)
    hook = gather_hook(shards)

    def body(*refs):
        ins, outs = refs[:n], refs[n:2 * n]
        send_sems, recv_sems = refs[2 * n:]
        hook["start"](ins, outs, send_sems, recv_sems)
        hook["finish"](ins, outs, send_sems, recv_sems)

    return pl.pallas_call(
        body, name=name,
        in_specs=[HBM] * n, out_specs=[HBM] * n,
        out_shape=hook["out_shapes"],
        scratch_shapes=[pltpu.SemaphoreType.DMA(hook["sems"]), pltpu.SemaphoreType.DMA(hook["sems"])],
    )(*shards)


def gather_hook(shards):
    n = len(shards)

    def copies(ins, outs, send_sems, recv_sems):
        px, py, pc = _position()
        chip = 2 * px + py
        sibling = (px, py, 1 - pc)
        own, sends, arrivals, passes, pass_arrivals = [], [], [], [], []
        for a in range(n):
            rows = ins[a].shape[0] // 2
            mine, theirs = pl.ds(pc * rows, rows), pl.ds((1 - pc) * rows, rows)
            own.append(_remote(ins[a], outs[a].at[chip], send_sems.at[a, 6], recv_sems.at[a, 6], sibling))
            for j, (ox, oy) in enumerate(_other_chips(px, py)):
                sends.append(_remote(ins[a].at[mine], outs[a].at[chip, mine],
                                     send_sems.at[a, j], recv_sems.at[a, j], (ox, oy, pc)))
                landed = outs[a].at[2 * ox + oy, mine]
                arrivals.append(_remote(landed, landed, send_sems.at[a, j], recv_sems.at[a, j], sibling))
                passes.append(_remote(landed, landed, send_sems.at[a, 3 + j], recv_sems.at[a, 3 + j], sibling))
                via = outs[a].at[2 * ox + oy, theirs]
                pass_arrivals.append(_remote(via, via, send_sems.at[a, 3 + j], recv_sems.at[a, 3 + j], sibling))
        return own, sends, arrivals, passes, pass_arrivals

    def start(ins, outs, send_sems, recv_sems):
        own, sends, _, _, _ = copies(ins, outs, send_sems, recv_sems)
        for cp in own + sends:
            cp.start()

    def finish(ins, outs, send_sems, recv_sems):
        own, sends, arrivals, passes, pass_arrivals = copies(ins, outs, send_sems, recv_sems)
        for arrival, onward in zip(arrivals, passes):
            arrival.wait_recv()
            onward.start()
        for arrival in pass_arrivals:
            arrival.wait_recv()
        for cp in sends + passes:
            cp.wait_send()
        for cp in own:
            cp.wait()

    return dict(operands=list(shards), out_shapes=[_sds((N_CHIPS,) + s.shape, s.dtype) for s in shards],
                sems=(n, 7), start=start, finish=finish)


def sibling_exchange(name, grads):
    n = len(grads)
    hook = exchange_hook(grads)

    def body(*refs):
        ins, outs = refs[:n], refs[n:2 * n]
        send_sems, recv_sems = refs[2 * n:]
        hook["start"](ins, outs, send_sems, recv_sems)
        hook["finish"](ins, outs, send_sems, recv_sems)

    return pl.pallas_call(
        body, name=name,
        in_specs=[HBM] * n, out_specs=[HBM] * n,
        out_shape=hook["out_shapes"],
        scratch_shapes=[pltpu.SemaphoreType.DMA(hook["sems"]), pltpu.SemaphoreType.DMA(hook["sems"])],
    )(*grads)


def exchange_hook(grads, first_sem=0):
    n = len(grads)

    def copies(ins, outs, send_sems, recv_sems):
        px, py, pc = _position()
        made = []
        for a in range(n):
            rows = ins[a].shape[1] // 2
            src = ins[a].at[:, pl.ds((1 - pc) * rows, rows), :]
            made.append(_remote(src, outs[a], send_sems.at[first_sem + a, 0], recv_sems.at[first_sem + a, 0],
                                (px, py, 1 - pc)))
        return made

    def start(ins, outs, send_sems, recv_sems):
        for cp in copies(ins, outs, send_sems, recv_sems):
            cp.start()

    def finish(ins, outs, send_sems, recv_sems):
        for cp in copies(ins, outs, send_sems, recv_sems):
            cp.wait()

    return dict(operands=list(grads),
                out_shapes=[_sds((g.shape[0], g.shape[1] // 2, g.shape[2]), g.dtype) for g in grads],
                sems=(first_sem + n, 3), start=start, finish=finish)


def combine_hooks(first, second):
    n_in, n_out = len(first["operands"]), len(first["out_shapes"])

    def start(ins, outs, send_sems, recv_sems):
        first["start"](ins[:n_in], outs[:n_out], send_sems, recv_sems)
        second["start"](ins[n_in:], outs[n_out:], send_sems, recv_sems)

    def finish(ins, outs, send_sems, recv_sems):
        first["finish"](ins[:n_in], outs[:n_out], send_sems, recv_sems)
        second["finish"](ins[n_in:], outs[n_out:], send_sems, recv_sems)

    return dict(operands=first["operands"] + second["operands"], out_shapes=first["out_shapes"] + second["out_shapes"],
                sems=second["sems"], start=start, finish=finish)


def pair_add(name, g, recv, core):
    nb, rows, cols = g.shape
    half = rows // 2
    tr = _tile(half, 256)
    steps = half // tr

    def body(core_ref, g_ref, r_ref, o_ref):
        del core_ref
        o_ref[...] = (g_ref[...].astype(F32) + r_ref[...].astype(F32)).astype(o_ref.dtype)

    return pl.pallas_call(
        body, name=name,
        grid_spec=pltpu.PrefetchScalarGridSpec(
            num_scalar_prefetch=1, grid=(nb, steps),
            in_specs=[pl.BlockSpec((None, tr, cols), lambda j, i, core_ref: (j, core_ref[0] * steps + i, 0)),
                      pl.BlockSpec((None, tr, cols), lambda j, i, core_ref: (j, i, 0))],
            out_specs=pl.BlockSpec((None, tr, cols), lambda j, i, core_ref: (j, i, 0))),
        out_shape=_sds((nb, half, cols), g.dtype),
        compiler_params=_params(("parallel", "parallel")),
    )(core, g, recv)


def scatter_hook(parts):
    n = len(parts)

    def copies(ins, outs, send_sems, recv_sems):
        px, py, pc = _position()
        return [_remote(ins[a].at[2 * ox + oy], outs[a].at[j], send_sems.at[a, j], recv_sems.at[a, j], (ox, oy, pc))
                for a in range(n) for j, (ox, oy) in enumerate(_other_chips(px, py))]

    def start(ins, outs, send_sems, recv_sems):
        for cp in copies(ins, outs, send_sems, recv_sems):
            cp.start()

    def finish(ins, outs, send_sems, recv_sems):
        for cp in copies(ins, outs, send_sems, recv_sems):
            cp.wait()

    return dict(operands=list(parts), out_shapes=[_sds((N_CHIPS - 1,) + p.shape[1:], p.dtype) for p in parts],
                sems=(n, 3), start=start, finish=finish)


def chip_sum(name, part, recv, where, out_shape, lead, dest=None):
    _, half, cols = part.shape
    tr = _tile(half, 256)
    steps = half // tr

    def body(where_ref, p_ref, r_ref, *rest):
        o_ref = rest[-1]
        acc = p_ref[...].astype(F32)
        for j in range(N_CHIPS - 1):
            acc = acc + r_ref[j].astype(F32)
        o_ref[...] = acc

    if lead is None:
        ospec = pl.BlockSpec((tr, cols), lambda i, w: (w[1] * steps + i, 0))
    else:
        ospec = pl.BlockSpec((None, tr, cols), lambda i, w: (lead, w[1] * steps + i, 0))
    in_specs = [pl.BlockSpec((None, tr, cols), lambda i, w: (w[0], i, 0)),
                pl.BlockSpec((N_CHIPS - 1, tr, cols), lambda i, w: (0, i, 0))]
    operands = [where, part, recv]
    aliases = {}
    if dest is not None:
        in_specs.append(pl.BlockSpec(memory_space=pl.ANY))
        operands.append(dest)
        aliases = {3: 0}
    return pl.pallas_call(
        body, name=name,
        grid_spec=pltpu.PrefetchScalarGridSpec(num_scalar_prefetch=1, grid=(steps,),
                                               in_specs=in_specs, out_specs=ospec),
        out_shape=out_shape, input_output_aliases=aliases,
        compiler_params=_params(("parallel",)),
    )(*operands)


def sibling_share(name, slabs, places):
    n = len(slabs)
    k = len(places)

    def body(*refs):
        outs = refs[n:2 * n]
        send_sems, recv_sems = refs[2 * n:]
        px, py, pc = _position()
        sibling = (px, py, 1 - pc)
        copies = []
        for a, (oi, lead, half) in enumerate(places):
            slab = outs[oi] if lead is None else outs[oi].at[lead]
            mine = slab.at[pl.ds(pc * half, half)]
            theirs = slab.at[pl.ds((1 - pc) * half, half)]
            cp = _remote(mine, mine, send_sems.at[a], recv_sems.at[a], sibling)
            cp.start()
            copies.append((cp, _remote(theirs, theirs, send_sems.at[a], recv_sems.at[a], sibling)))
        for cp, arrival in copies:
            cp.wait_send()
            arrival.wait_recv()

    return pl.pallas_call(
        body, name=name,
        in_specs=[HBM] * n, out_specs=[HBM] * n,
        out_shape=[_sds(s.shape, s.dtype) for s in slabs],
        input_output_aliases={a: a for a in range(n)},
        scratch_shapes=[pltpu.SemaphoreType.DMA((k,)), pltpu.SemaphoreType.DMA((k,))],
    )(*slabs)


def _row(a, i):
    return a[i:i + 1]


def _relu_sq(acc):
    r = jnp.maximum(acc, 0.0)
    return r, r * r


def _residual(acc, res, gate):
    return res + gate * acc, acc


def _with_comm(res, comm):
    return res if comm else (res, [])


def _blocked(g):
    return g if g.ndim == 3 else g.reshape(N_CHIPS, g.shape[0] // N_CHIPS, g.shape[1])


def _pre_reduce(tag, named, core):
    glist = [_blocked(g) for _, g in named]
    recv = sibling_exchange("grad_exchange_" + tag, glist)
    return [pair_add("grad_pair_add_" + k, g, r, core) for (k, _), g, r in zip(named, glist, recv)]


def _mlp_fwd(tag, x, gain, mod, w1, w2=None, w2_shard=None, down_comm=None):
    h = norm_mod_fwd(tag + "_mlp_norm", x, gain, _row(mod, 4), _row(mod, 3))
    up_comm = None if w2 is not None else gather_hook([w2_shard])
    (r3, a3), got = _with_comm(mm_nn_b(tag + "_mlp_up", h, w1, [BF16, BF16], _relu_sq, comm=up_comm), up_comm)
    if w2 is None:
        w2 = got[0]
    (x_out, m), got = _with_comm(mm_nn_r(tag + "_mlp_down", a3, w2, [F32, BF16], _residual,
                                         (x, _row(mod, 5)), ("tile", "row"), comm=down_comm), down_comm)
    return x_out, (h, r3, a3, m), w2, got


def _mlp_bwd(tag, dx_out, dm, x, gain, mod, w1, w2, saved, core, branch, down_comm=None, w2_comm=None):
    h, r3, a3, _ = saved
    (dz3,), got_down = _with_comm(mm_nt_b(tag + "_mlp_down_dgrad", dm, w2, [BF16],
                                          lambda acc, r: (acc * (2.0 * r.astype(F32)),), (r3,), comm=down_comm),
                                  down_comm)
    gw2, got_w2 = _with_comm(mm_tn(tag + "_mlp_w2_grad", a3, dm[None], BF16, comm=w2_comm), w2_comm)
    gw1, (from_sibling,) = mm_tn(tag + "_mlp_w1_grad", h[None], dz3, BF16, comm=exchange_hook([gw2]))
    part2 = pair_add("grad_pair_add_" + tag + "_w2", gw2, from_sibling, core)
    dh, (recv2, from_sibling) = mm_nt_r(tag + "_mlp_up_dgrad", dz3, w1, F32,
                                        comm=combine_hooks(scatter_hook([part2]), exchange_hook([gw1], 1)))
    part1 = pair_add("grad_pair_add_" + tag + "_w1", gw1, from_sibling, core)
    dx, dsh, dsc, dgain, dy, dgate = norm_mod_bwd(tag + "_mlp_norm_bwd", dh, x, gain, _row(mod, 4), dx_out, branch)
    return dx, (dy, dgate), part1, (part2, recv2), (dsh, dsc), dgain, got_down, got_w2


def local_step(x, target, mod0, mod1, kvmod, norm_mix, norm_mlp, kv_norm, final_norm, lb, out_gain,
               w, shards, core):
    w = dict(w)

    def flat(g):
        return g.reshape(-1, g.shape[-1])

    h1 = norm_mod_fwd("l0_mix_norm", x, _row(norm_mix, 0), _row(mod0, 1), _row(mod0, 0))
    (proj3,), (w["w1_0"],) = mm_nn_b("l0_in_proj", h1, w["a_in"], [F32], comm=gather_hook([shards["w1_0"]]))
    (u, o_h, states), got = hgrn_fwd("l0_hgrn_fwd", proj3, lb, out_gain,
                                     comm=gather_hook([shards["bq"], shards["kv"], shards["bo"]]))
    w["bq"], w["kv"], w["bo"] = (flat(g) for g in got)
    x1, y0 = mm_nn_b("l0_out_proj", u, w["a_out"][None], [F32, BF16], _residual,
                     (x[None], _row(mod0, 2)), ("tile", "row"))
    x1, y0 = x1[0], y0[0]
    x2, mlp0, w["w2_0"], (w["w1_1"],) = _mlp_fwd(
        "l0", x1, _row(norm_mlp, 0), mod0, w["w1_0"], w2_shard=shards["w2_0"],
        down_comm=gather_hook([shards["w1_1"]]))
    hk = norm_mod_fwd("kv_norm", x2, kv_norm, _row(kvmod, 1), _row(kvmod, 0))
    kv = mm_nn_b("kv_proj", hk, w["kv"][None], [BF16])[0][0]
    h3 = norm_mod_fwd("l1_mix_norm", x2, _row(norm_mix, 1), _row(mod1, 1), _row(mod1, 0))
    q = mm_nn_b("l1_q_proj", h3, w["bq"][None], [BF16])[0][0]
    (o_a, totals, visited), (w["w2_1"],) = attn_fwd("l1_attn_fwd", q, kv, comm=gather_hook([shards["w2_1"]]))
    x3, y1 = mm_nn_b("l1_out_proj", o_a, w["bo"][None], [F32, BF16], _residual,
                     (x2[None], _row(mod1, 2)), ("tile", "row"))
    x3, y1 = x3[0], y1[0]
    x4, mlp1, _, _ = _mlp_fwd("l1", x3, _row(norm_mlp, 1), mod1, w["w1_1"], w["w2_1"])
    dx4, d_final, loss, dm1, dgate_mlp1 = final_loss("final_loss", x4, final_norm, target, mlp1[3], _row(mod1, 5))

    reduce = {}
    dx3, (dy1, dgate1), part_w1_1, reduce["w2_1"], dmlp1, d_nmlp1, _, _ = _mlp_bwd(
        "l1", dx4, dm1, x3, _row(norm_mlp, 1), mod1, w["w1_1"], w["w2_1"], mlp1, core, (y1, _row(mod1, 2)))
    do_a = mm_nt_b("l1_out_dgrad", dy1, w["bo"][None], [F32])[0][0]
    g_bo = mm_tn("l1_out_grad", o_a[None], dy1[None], BF16)[0]
    dq, dkv3 = attn_bwd("l1_attn_bwd", q, kv, totals, visited, do_a)
    g_bq = mm_tn("l1_q_grad", h3[None], dq[None], BF16)[0]
    dh3 = mm_nt_b("l1_q_dgrad", dq, w["bq"][None], [F32])[0][0]
    dx2, dsh, dsc, d_nmix1 = norm_mod_bwd("l1_mix_norm_bwd", dh3, x2, _row(norm_mix, 1), _row(mod1, 1), dx3)
    dmod1 = jnp.concatenate([dsh, dsc, dgate1, *dmlp1, dgate_mlp1], axis=0)
    dkv = jnp.concatenate([dkv3[0], dkv3[1]], axis=1).astype(BF16)
    g_kv = mm_tn("kv_grad", hk[None], dkv[None], BF16)[0]
    attn_grads = [_blocked(g_bo), _blocked(g_bq), _blocked(g_kv)]
    (dhk,), from_sibling = mm_nt_b("kv_dgrad", dkv, w["kv"][None], [F32], comm=exchange_hook(attn_grads))
    dhk = dhk[0]
    dx2, dsh, dsc, d_nkv, dm0, dgate_mlp0 = norm_mod_bwd("kv_norm_bwd", dhk, x2, kv_norm, _row(kvmod, 1), dx2,
                                                         (mlp0[3], _row(mod0, 5)))
    dkvmod = jnp.concatenate([dsh, dsc], axis=0)
    attn_parts = [pair_add("grad_pair_add_" + k, g, r, core)
                  for k, g, r in zip(("bo", "bq", "kv"), attn_grads, from_sibling)]
    dx1, (dy0, dgate0), part_w1_0, reduce["w2_0"], dmlp0, d_nmlp0, (recv,), attn_recv = _mlp_bwd(
        "l0", dx2, dm0, x1, _row(norm_mlp, 0), mod0, w["w1_0"], w["w2_0"], mlp0, core, (y0, _row(mod0, 2)),
        scatter_hook([part_w1_1]), scatter_hook(attn_parts))
    reduce["w1_1"] = (part_w1_1, recv)
    for k, part, recv in zip(("bo", "bq", "kv"), attn_parts, attn_recv):
        reduce[k] = (part, recv)
    du = mm_nt_b("l0_out_dgrad", dy0, w["a_out"][None], [F32])[0][0]
    g_a_out = mm_tn("l0_out_grad", u[None], dy0[None], BF16)[0]
    g_a_out = _blocked(g_a_out)
    (dproj3, d_lb, d_out_gain), (recv, from_sibling) = hgrn_bwd(
        "l0_hgrn_bwd", proj3, lb, out_gain, o_h, du, states,
        comm=combine_hooks(scatter_hook([part_w1_0]), exchange_hook([g_a_out], 1)))
    reduce["w1_0"] = (part_w1_0, recv)
    part = [pair_add("grad_pair_add_a_out", g_a_out, from_sibling, core)]
    g_a_in, recv = mm_tn("l0_in_grad", h1[None], dproj3, BF16, comm=scatter_hook(part))
    reduce["a_out"] = (part[0], recv[0])
    part = _pre_reduce("a_in", [("a_in", g_a_in)], core)
    dh1, recv = mm_nt_r("l0_in_dgrad", dproj3, w["a_in"], F32, comm=scatter_hook(part))
    reduce["a_in"] = (part[0], recv[0])
    dx0, dsh, dsc, d_nmix0 = norm_mod_bwd("l0_mix_norm_bwd", dh1, x, _row(norm_mix, 0), _row(mod0, 1), dx1)
    dmod0 = jnp.concatenate([dsh, dsc, dgate0, *dmlp0, dgate_mlp0], axis=0)
    small = dict(norm_mix=(d_nmix0, d_nmix1), norm_mlp=(d_nmlp0, d_nmlp1), kv_norm=d_nkv,
                 final_norm=d_final, lb=d_lb, out_gain=d_out_gain)
    return loss, dx0, reduce, dmod0, dmod1, dkvmod, small


BIG = ("a_in", "w1_0", "w1_1", "w2_0", "w2_1", "a_out", "bq", "bo", "kv")


def kernel(x, c, ada_w, ada_b, norm_mix, norm_mlp, a_w_in, a_lb_logits, a_out_gain, a_w_out, kv_ada_w, kv_ada_b, kv_norm, w_kv, b_w_q, b_w_out, mlp_w1, mlp_w2, final_norm, loss_target, m_ada_w, m_ada_b, m_norm_mix, m_norm_mlp, m_a_w_in, m_a_lb_logits, m_a_out_gain, m_a_w_out, m_kv_ada_w, m_kv_ada_b, m_kv_norm, m_w_kv, m_b_w_q, m_b_w_out, m_mlp_w1, m_mlp_w2, m_final_norm, v_ada_w, v_ada_b, v_norm_mix, v_norm_mlp, v_a_w_in, v_a_lb_logits, v_a_out_gain, v_a_w_out, v_kv_ada_w, v_kv_ada_b, v_kv_norm, v_w_kv, v_b_w_q, v_b_w_out, v_mlp_w1, v_mlp_w2, v_final_norm):
    d = x.shape[-1]
    px, py, pc = _position()
    me = 4 * px + 2 * py + pc
    chip = 2 * px + py
    n_ada = ada_w.shape[2]
    n_kvada = kv_ada_w.shape[1]
    shard_cols = d // N_CHIPS

    def as_rows(a):
        return a.reshape(-1, shard_cols)

    pack1 = jnp.concatenate([as_rows(c), a_lb_logits, a_out_gain,
                             jnp.zeros((1, shard_cols), F32)], axis=0)
    got1 = allgather8("gather_cond", pack1)
    n_c = d // shard_cols
    c_all = got1[:, :n_c, :].reshape(N_DEV, d)
    per_chip = got1[0::2]
    logits_full = jnp.swapaxes(per_chip[:, n_c:n_c + 2, :], 0, 1).reshape(2, d)
    out_gain_full = per_chip[:, n_c + 2, :].reshape(1, d)
    lb = lower_bound_fwd("lower_bound", logits_full)

    bias0 = lax.dynamic_slice_in_dim(ada_b, chip * n_ada, n_ada, axis=1)
    bias_kv = lax.dynamic_slice_in_dim(kv_ada_b.reshape(1, -1), chip * n_kvada, n_kvada, axis=1)
    mod_part = jnp.concatenate([
        ada_project("ada_proj_0", c_all, ada_w[0], bias0[0:1]),
        ada_project("ada_proj_1", c_all, ada_w[1], bias0[1:2]),
        ada_project("ada_proj_kv", c_all, kv_ada_w, bias_kv)], axis=1)
    got2 = allgather8("gather_mod", mod_part)
    mine = lax.dynamic_index_in_dim(got2[0::2], me, axis=1, keepdims=False)
    mod0 = mine[:, :n_ada].reshape(6, d)
    mod1 = mine[:, n_ada:2 * n_ada].reshape(6, d)
    kvmod = mine[:, 2 * n_ada:].reshape(2, d)

    shards = dict(a_in=a_w_in[0], w1_0=mlp_w1[0], w1_1=mlp_w1[1], w2_0=mlp_w2[0], w2_1=mlp_w2[1],
                  a_out=a_w_out[0], bq=b_w_q[0], bo=b_w_out[0], kv=w_kv)
    shards = {k: s.astype(BF16) for k, s in shards.items()}
    g_in, g_out = gather_weights("gather_first_weights", [shards["a_in"], shards["a_out"]])
    core = pc.astype(jnp.int32).reshape(1)

    loss, dx0, reduce, dmod0, dmod1, dkvmod, small = local_step(
        x[0], loss_target[0], mod0, mod1, kvmod, norm_mix, norm_mlp, kv_norm.reshape(1, d),
        final_norm.reshape(1, d), lb, out_gain_full,
        dict(a_in=g_in, a_out=g_out.reshape(-1, g_out.shape[-1])), shards, core)

    loss_row = jnp.concatenate([loss, jnp.zeros((1, shard_cols - loss.shape[1]), F32)], axis=1)
    rows = [as_rows(dmod0), as_rows(dmod1), as_rows(dkvmod),
            as_rows(small["norm_mix"][0]), as_rows(small["norm_mix"][1]),
            as_rows(small["norm_mlp"][0]), as_rows(small["norm_mlp"][1]),
            as_rows(small["kv_norm"]), as_rows(small["final_norm"]),
            as_rows(small["lb"]), as_rows(small["out_gain"]), loss_row]
    n_rows = sum(r.shape[0] for r in rows)
    pad = (-n_rows) % 8
    pack3 = jnp.concatenate(rows + [jnp.zeros((pad, shard_cols), F32)], axis=0)
    got3 = allgather8("gather_small_grads", pack3)
    total = device_sum("sum_small_grads", got3)
    n_mod_rows = (12 * d + 2 * d) // shard_cols
    n_gain_rows = 6 * n_c
    loss_out = total[n_mod_rows + n_gain_rows + 2 * n_c, 0]

    dmod_all = got3[:, :n_mod_rows, :].reshape(N_DEV, 14 * d)
    act_t = jnp.zeros((d, 128), F32).at[:, :N_DEV].set(c_all.T)

    def dmod_cols(lo, width):
        part = lax.dynamic_slice_in_dim(dmod_all, lo + chip * width, width, axis=1)
        return jnp.zeros((128, width), F32).at[:N_DEV].set(part)

    g_ada_w, d_ada_w, nm_ada_w, nv_ada_w = ada_grad_adamw(
        "ada_update", act_t, jnp.stack([dmod_cols(0, n_ada), dmod_cols(6 * d, n_ada)]), ada_w, m_ada_w, v_ada_w)
    g_kv_ada_w, d_kv_ada_w, nm_kv_ada_w, nv_kv_ada_w = (a[0] for a in ada_grad_adamw(
        "ada_update_kv", act_t, dmod_cols(12 * d, n_kvada)[None], kv_ada_w[None], m_kv_ada_w[None],
        v_kv_ada_w[None]))

    chip_parts = [reduce[k][0] for k in BIG]
    from_chips = [reduce[k][1] for k in BIG]
    where = jnp.stack([chip, pc]).astype(jnp.int32)
    out_shapes = [_sds(a_w_in.shape, F32), _sds(mlp_w1.shape, F32), _sds(mlp_w2.shape, F32),
                  _sds(a_w_out.shape, F32), _sds(b_w_q.shape, F32), _sds(b_w_out.shape, F32),
                  _sds(w_kv.shape, F32)]
    targets = [(0, 0), (1, 0), (1, 1), (2, 0), (2, 1), (3, 0), (4, 0), (5, 0), (6, None)]
    slabs = [None] * len(out_shapes)
    places = []
    for k, part, recv, (oi, lead) in zip(BIG, chip_parts, from_chips, targets):
        slabs[oi] = chip_sum("grad_chip_sum_" + k, part, recv, where, out_shapes[oi], lead, slabs[oi])
        places.append((oi, lead, part.shape[1]))
    g_a_w_in, g_mlp_w1, g_mlp_w2, g_a_w_out, g_b_w_q, g_b_w_out, g_w_kv = sibling_share(
        "grad_sibling_share", slabs, places)

    def update(name, wgt, g, m_, v_):
        shape = wgt.shape
        f = lambda a: a.reshape(-1, shape[-1])
        return tuple(o.reshape(shape) for o in adamw(name, f(wgt), f(g), f(m_), f(v_)))

    u_a_w_in = update("adamw_a_w_in", a_w_in, g_a_w_in, m_a_w_in, v_a_w_in)
    u_mlp_w1 = update("adamw_mlp_w1", mlp_w1, g_mlp_w1, m_mlp_w1, v_mlp_w1)
    u_mlp_w2 = update("adamw_mlp_w2", mlp_w2, g_mlp_w2, m_mlp_w2, v_mlp_w2)
    u_a_w_out = update("adamw_a_w_out", a_w_out, g_a_w_out, m_a_w_out, v_a_w_out)
    u_b_w_q = update("adamw_b_w_q", b_w_q, g_b_w_q, m_b_w_q, v_b_w_q)
    u_b_w_out = update("adamw_b_w_out", b_w_out, g_b_w_out, m_b_w_out, v_b_w_out)
    u_w_kv = update("adamw_w_kv", w_kv, g_w_kv, m_w_kv, v_w_kv)

    base = n_mod_rows + n_gain_rows
    d_lb_mine = lax.dynamic_slice_in_dim(total, base + chip, 1, axis=0)
    d_gain_mine = lax.dynamic_slice_in_dim(total, base + n_c + chip, 1, axis=0)
    d_logits = lower_bound_bwd("lower_bound_bwd", a_lb_logits, d_lb_mine)

    def pack_small(ada_b_, kv_ada_b_, norm_mix_, norm_mlp_, kv_norm_, final_norm_, lbl_, gain_):
        parts = [as_rows(ada_b_), as_rows(kv_ada_b_), as_rows(norm_mix_), as_rows(norm_mlp_),
                 as_rows(kv_norm_), as_rows(final_norm_), lbl_, gain_]
        n = sum(p.shape[0] for p in parts)
        return jnp.concatenate(parts + [jnp.zeros(((-n) % 8, shard_cols), F32)], axis=0)

    w_small = pack_small(ada_b, kv_ada_b, norm_mix, norm_mlp, kv_norm, final_norm, a_lb_logits, a_out_gain)
    m_small = pack_small(m_ada_b, m_kv_ada_b, m_norm_mix, m_norm_mlp, m_kv_norm, m_final_norm,
                         m_a_lb_logits, m_a_out_gain)
    v_small = pack_small(v_ada_b, v_kv_ada_b, v_norm_mix, v_norm_mlp, v_kv_norm, v_final_norm,
                         v_a_lb_logits, v_a_out_gain)
    n_small = w_small.shape[0]
    g_small = jnp.concatenate([total[:base], d_logits, d_gain_mine,
                               jnp.zeros((n_small - base - 3, shard_cols), F32)], axis=0)
    small_out = (g_small,) + tuple(adamw("adamw_small", w_small, g_small, m_small, v_small))

    def unpack_small(p):
        out, r0 = [], 0
        for ref in (ada_b, kv_ada_b, norm_mix, norm_mlp, kv_norm, final_norm, a_lb_logits, a_out_gain):
            nr = ref.size // shard_cols
            out.append(p[r0:r0 + nr].reshape(ref.shape))
            r0 += nr
        return out

    sm = [unpack_small(p) for p in small_out]

    def leaves(kind, big_ada, big_kv_ada):
        ada_b_, kv_ada_b_, norm_mix_, norm_mlp_, kv_norm_, final_norm_, lbl_, gain_ = sm[kind]
        pick = (lambda u, g: g) if kind == 0 else (lambda u, g: u[kind - 1])
        return [big_ada, ada_b_, norm_mix_, norm_mlp_, pick(u_a_w_in, g_a_w_in), lbl_, gain_,
                pick(u_a_w_out, g_a_w_out), big_kv_ada, kv_ada_b_, kv_norm_, pick(u_w_kv, g_w_kv),
                pick(u_b_w_q, g_b_w_q), pick(u_b_w_out, g_b_w_out), pick(u_mlp_w1, g_mlp_w1),
                pick(u_mlp_w2, g_mlp_w2), final_norm_]

    return (loss_out, dx0[None],
            *leaves(0, g_ada_w, g_kv_ada_w), *leaves(1, d_ada_w, d_kv_ada_w),
            *leaves(2, nm_ada_w, nm_kv_ada_w), *leaves(3, nv_ada_w, nv_kv_ada_w))
```

```python
import functools

import jax
import jax.numpy as jnp
from jax import lax
from jax.experimental import pallas as pl
from jax.experimental.pallas import tpu as pltpu

F32 = jnp.float32
BF16 = jnp.bfloat16
MESH = pl.DeviceIdType.MESH

HEAD_DIM = 128
KV_GROUP = 4
HGRN_CHUNK = 64
HGRN_HEADS = 4
HGRN_ROWS = 256
NORM_EPS = 1e-6
N_CHIPS = 4
N_DEV = 8
ROW_TILE = 256
ATTN_TILE = 256
VMEM_LIMIT = 56 * 1024 * 1024
DEAD_LOG_WEIGHT = -110.0

ADAM_LR = 0.001
ADAM_B1 = 0.9
ADAM_B2 = 0.999
ADAM_EPS = 1e-08
ADAM_WD = 0.01
ADAM_STEP = 10

NN = (((1,), (0,)), ((), ()))
NT = (((1,), (1,)), ((), ()))
TN = (((0,), (0,)), ((), ()))


def _dot(a, b, dims=NN, precision=None):
    return lax.dot_general(a, b, dims, preferred_element_type=F32, precision=precision)


def _bdot(a, b, dims=NN):
    return _dot(a.astype(BF16), b.astype(BF16), dims)


def _sigmoid(x):
    return 1.0 / (1.0 + jnp.exp(-x))


def _log_sigmoid(z):
    return jnp.minimum(z, 0.0) - jnp.log(1.0 + jnp.exp(-jnp.abs(z)))


def _split_bf16(x):
    hi = x.astype(BF16)
    lo = (x - hi.astype(F32)).astype(BF16)
    return hi, lo


def _params(sem=None):
    return pltpu.CompilerParams(dimension_semantics=sem, vmem_limit_bytes=VMEM_LIMIT)


def _tile(n, pref):
    t = min(n, pref)
    assert n % t == 0, (n, pref)
    return t


def _mm(name, a, b, a_spec, b_spec, grid, n_red, dims, out_shapes, out_specs,
        acc_shape, epilogue=None, extras=(), extra_specs=(), comm=None):
    n_extra = len(extras)
    n_out = len(out_shapes)
    n_cin = len(comm["operands"]) if comm else 0
    n_cout = len(comm["out_shapes"]) if comm else 0
    if epilogue is None:
        epilogue = lambda acc: (acc,)

    def body(*refs):
        a_ref, b_ref = refs[:2]
        ex_refs = refs[2:2 + n_extra]
        cin_refs = refs[2 + n_extra:2 + n_extra + n_cin]
        out_refs = refs[2 + n_extra + n_cin:2 + n_extra + n_cin + n_out]
        cout_refs = refs[2 + n_extra + n_cin + n_out:2 + n_extra + n_cin + n_out + n_cout]
        scratch = refs[2 + n_extra + n_cin + n_out + n_cout:]
        pids = [pl.program_id(ax) for ax in range(len(grid))]
        if comm:
            send_sems, recv_sems = scratch[-2:]

            @pl.when(functools.reduce(jnp.logical_and, [p == 0 for p in pids]))
            def _():
                comm["start"](cin_refs, cout_refs, send_sems, recv_sems)

        prod = _bdot(a_ref[...], b_ref[...], dims)

        def finish(acc):
            res = epilogue(acc, *[e[...] for e in ex_refs])
            for o_ref, r in zip(out_refs, res):
                o_ref[...] = r.astype(o_ref.dtype)

        if n_red == 0:
            finish(prod)
        else:
            acc_ref = scratch[0]
            ids = pids[len(grid) - n_red:]
            sizes = grid[len(grid) - n_red:]
            first = functools.reduce(jnp.logical_and, [i == 0 for i in ids])
            last = functools.reduce(jnp.logical_and, [i == s - 1 for i, s in zip(ids, sizes)])

            @pl.when(first)
            def _():
                acc_ref[...] = prod

            @pl.when(jnp.logical_not(first))
            def _():
                acc_ref[...] += prod

            @pl.when(last)
            def _():
                finish(acc_ref[...])

        if comm:
            @pl.when(functools.reduce(jnp.logical_and, [p == s - 1 for p, s in zip(pids, grid)]))
            def _():
                comm["finish"](cin_refs, cout_refs, send_sems, recv_sems)

    if comm:
        sem = ("arbitrary",) * len(grid)
    else:
        sem = ("parallel",) * (len(grid) - n_red) + ("arbitrary",) * n_red
    scratch_shapes = [pltpu.VMEM(acc_shape, F32)] if n_red else []
    if comm:
        scratch_shapes += [pltpu.SemaphoreType.DMA(comm["sems"]), pltpu.SemaphoreType.DMA(comm["sems"])]
    out = pl.pallas_call(
        body, name=name, grid=grid,
        in_specs=[a_spec, b_spec, *extra_specs] + [HBM] * n_cin,
        out_specs=list(out_specs) + [HBM] * n_cout,
        out_shape=list(out_shapes) + (list(comm["out_shapes"]) if comm else []),
        scratch_shapes=scratch_shapes,
        compiler_params=_params(sem),
    )(a, b, *extras, *(comm["operands"] if comm else ()))
    return (out[:n_out], out[n_out:]) if comm else out


def _sds(shape, dtype):
    return jax.ShapeDtypeStruct(shape, dtype)


def mm_nn_b(name, a, w3, out_dtypes, epilogue=None, extras=(), extra_kinds=(), comm=None):
    m, k = a.shape
    nb, _, n = w3.shape
    tm, tn = _tile(m, 1024), _tile(n, 1024)
    grid = (nb, m // tm, n // tn)
    nt = n // tn
    especs = []
    for kind in extra_kinds:
        if kind == "tile":
            especs.append(pl.BlockSpec((None, tm, tn), lambda j, i, c: (j, i, c)))
        else:
            especs.append(pl.BlockSpec((1, tn), lambda j, i, c: (0, j * nt + c)))
    return _mm(name, a, w3,
               pl.BlockSpec((tm, k), lambda j, i, c: (i, 0)),
               pl.BlockSpec((None, k, tn), lambda j, i, c: (j, 0, c)),
               grid, 0, NN,
               [_sds((nb, m, n), d) for d in out_dtypes],
               [pl.BlockSpec((None, tm, tn), lambda j, i, c: (j, i, c)) for _ in out_dtypes],
               None, epilogue, extras, especs, comm)


def mm_nn_r(name, a3, w3, out_dtypes, epilogue=None, extras=(), extra_kinds=(), comm=None):
    nb, m, kb = a3.shape
    n = w3.shape[2]
    tm, tn, tk = _tile(m, 1024), _tile(n, 512), _tile(kb, 2048)
    grid = (m // tm, n // tn, nb, kb // tk)
    especs = []
    for kind in extra_kinds:
        if kind == "tile":
            especs.append(pl.BlockSpec((tm, tn), lambda i, c, j, r: (i, c)))
        else:
            especs.append(pl.BlockSpec((1, tn), lambda i, c, j, r: (0, c)))
    return _mm(name, a3, w3,
               pl.BlockSpec((None, tm, tk), lambda i, c, j, r: (j, i, r)),
               pl.BlockSpec((None, tk, tn), lambda i, c, j, r: (j, r, c)),
               grid, 2, NN,
               [_sds((m, n), d) for d in out_dtypes],
               [pl.BlockSpec((tm, tn), lambda i, c, j, r: (i, c)) for _ in out_dtypes],
               (tm, tn), epilogue, extras, especs, comm)


def mm_nt_b(name, a, w3, out_dtypes, epilogue=None, extras=(), comm=None):
    m, n = a.shape
    nb, kb, _ = w3.shape
    tm, tk = _tile(m, 1024), _tile(kb, 1024)
    grid = (nb, m // tm, kb // tk)
    especs = [pl.BlockSpec((None, tm, tk), lambda j, i, c: (j, i, c)) for _ in extras]
    return _mm(name, a, w3,
               pl.BlockSpec((tm, n), lambda j, i, c: (i, 0)),
               pl.BlockSpec((None, tk, n), lambda j, i, c: (j, c, 0)),
               grid, 0, NT,
               [_sds((nb, m, kb), d) for d in out_dtypes],
               [pl.BlockSpec((None, tm, tk), lambda j, i, c: (j, i, c)) for _ in out_dtypes],
               None, epilogue, extras, especs, comm)


def _single(res, comm):
    return res[0] if comm is None else (res[0][0], res[1])


def mm_nt_r(name, a3, w3, out_dtype, comm=None):
    nb, m, n = a3.shape
    k = w3.shape[1]
    tm, tk, tc = _tile(m, 1024), _tile(k, 1024), _tile(n, 2048)
    grid = (m // tm, k // tk, nb, n // tc)
    return _single(_mm(name, a3, w3,
                       pl.BlockSpec((None, tm, tc), lambda i, c, j, r: (j, i, r)),
                       pl.BlockSpec((None, tk, tc), lambda i, c, j, r: (j, c, r)),
                       grid, 2, NT,
                       [_sds((m, k), out_dtype)],
                       [pl.BlockSpec((tm, tk), lambda i, c, j, r: (i, c))],
                       (tm, tk), comm=comm), comm)


def mm_tn(name, a3, d3, out_dtype, comm=None, tiles=(512, 1024, 4096)):
    na, m, kb = a3.shape
    nd, _, n = d3.shape
    nb = max(na, nd)
    tk, tn, tm = _tile(kb, tiles[0]), _tile(n, tiles[1]), _tile(m, tiles[2])
    ja = (lambda j: j) if na > 1 else (lambda j: 0)
    jd = (lambda j: j) if nd > 1 else (lambda j: 0)
    if tm == m:
        return _single(_mm(name, a3, d3,
                           pl.BlockSpec((None, tm, tk), lambda j, c, e: (ja(j), 0, c)),
                           pl.BlockSpec((None, tm, tn), lambda j, c, e: (jd(j), 0, e)),
                           (nb, kb // tk, n // tn), 0, TN,
                           [_sds((nb, kb, n), out_dtype)],
                           [pl.BlockSpec((None, tk, tn), lambda j, c, e: (j, c, e))],
                           None, comm=comm), comm)
    grid = (nb, kb // tk, n // tn, m // tm)
    return _single(_mm(name, a3, d3,
                       pl.BlockSpec((None, tm, tk), lambda j, c, e, r: (ja(j), r, c)),
                       pl.BlockSpec((None, tm, tn), lambda j, c, e, r: (jd(j), r, e)),
                       grid, 1, TN,
                       [_sds((nb, kb, n), out_dtype)],
                       [pl.BlockSpec((None, tk, tn), lambda j, c, e, r: (j, c, e))],
                       (tk, tn), comm=comm), comm)


def _row_spec(ts, d):
    return pl.BlockSpec((ts, d), lambda i: (i, 0))


def _vec_spec(d):
    return pl.BlockSpec((1, d), lambda i: (0, 0))


def norm_mod_fwd(name, x, gain, scale, shift):
    s, d = x.shape
    ts = _tile(s, ROW_TILE)

    def body(x_ref, g_ref, sc_ref, sh_ref, h_ref):
        xv = x_ref[...]
        inv = lax.rsqrt(jnp.mean(xv * xv, axis=-1, keepdims=True) + NORM_EPS)
        h = (xv * inv) * g_ref[...] * (1.0 + sc_ref[...]) + sh_ref[...]
        h_ref[...] = h.astype(h_ref.dtype)

    return pl.pallas_call(
        body, name=name, grid=(s // ts,),
        in_specs=[_row_spec(ts, d), _vec_spec(d), _vec_spec(d), _vec_spec(d)],
        out_specs=_row_spec(ts, d), out_shape=_sds((s, d), BF16),
        compiler_params=_params(("parallel",)),
    )(x, gain, scale, shift)


def _gate_bwd(dxv, y_ref, gate_ref, dy_ref, dgate_ref):
    dy_ref[...] = (dxv * gate_ref[...]).astype(dy_ref.dtype)
    dgate_ref[...] += jnp.sum(dxv * y_ref[...].astype(F32), axis=0, keepdims=True)


def norm_mod_bwd(name, dh, x, gain, scale, dres, branch=None):
    s, d = x.shape
    ts = _tile(s, ROW_TILE)

    def body(dh_ref, x_ref, g_ref, sc_ref, dres_ref, *rest):
        if branch:
            y_ref, gate_ref, dx_ref, dsh_ref, dsc_ref, dg_ref, dy_ref, dgate_ref = rest
        else:
            dx_ref, dsh_ref, dsc_ref, dg_ref = rest

        @pl.when(pl.program_id(0) == 0)
        def _():
            dsh_ref[...] = jnp.zeros_like(dsh_ref)
            dsc_ref[...] = jnp.zeros_like(dsc_ref)
            dg_ref[...] = jnp.zeros_like(dg_ref)
            if branch:
                dgate_ref[...] = jnp.zeros_like(dgate_ref)

        xv = x_ref[...]
        dhv = dh_ref[...].astype(F32)
        g = g_ref[...]
        inv = lax.rsqrt(jnp.mean(xv * xv, axis=-1, keepdims=True) + NORM_EPS)
        n = xv * inv
        dhn = dhv * (1.0 + sc_ref[...])
        dn = dhn * g
        dx = dres_ref[...] + inv * (dn - n * jnp.mean(dn * n, axis=-1, keepdims=True))
        dx_ref[...] = dx
        dsh_ref[...] += jnp.sum(dhv, axis=0, keepdims=True)
        dsc_ref[...] += jnp.sum(dhv * (n * g), axis=0, keepdims=True)
        dg_ref[...] += jnp.sum(dhn * n, axis=0, keepdims=True)
        if branch:
            _gate_bwd(dx, y_ref, gate_ref, dy_ref, dgate_ref)

    row, vec = _row_spec(ts, d), _vec_spec(d)
    return pl.pallas_call(
        body, name=name, grid=(s // ts,),
        in_specs=[row, row, vec, vec, row] + ([row, vec] if branch else []),
        out_specs=[row, vec, vec, vec] + ([row, vec] if branch else []),
        out_shape=[_sds((s, d), F32), _sds((1, d), F32), _sds((1, d), F32), _sds((1, d), F32)]
        + ([_sds((s, d), BF16), _sds((1, d), F32)] if branch else []),
        compiler_params=_params(("arbitrary",)),
    )(dh, x, gain, scale, dres, *(branch or ()))


def final_loss(name, x, gain, target, y, gate):
    s, d = x.shape
    ts = _tile(s, ROW_TILE)

    def body(x_ref, g_ref, t_ref, y_ref, gate_ref, dx_ref, dg_ref, loss_ref, dy_ref, dgate_ref):
        @pl.when(pl.program_id(0) == 0)
        def _():
            dg_ref[...] = jnp.zeros_like(dg_ref)
            loss_ref[...] = jnp.zeros_like(loss_ref)
            dgate_ref[...] = jnp.zeros_like(dgate_ref)

        xv = x_ref[...]
        g = g_ref[...]
        inv = lax.rsqrt(jnp.mean(xv * xv, axis=-1, keepdims=True) + NORM_EPS)
        n = xv * inv
        diff = n * g - t_ref[...]
        per_tok = jnp.mean(diff * diff, axis=-1, keepdims=True)
        loss_ref[...] += 0.5 * jnp.sum(per_tok, axis=0, keepdims=True)
        dout = diff * (1.0 / d)
        dg_ref[...] += jnp.sum(dout * n, axis=0, keepdims=True)
        dn = dout * g
        dx = inv * (dn - n * jnp.mean(dn * n, axis=-1, keepdims=True))
        dx_ref[...] = dx
        _gate_bwd(dx, y_ref, gate_ref, dy_ref, dgate_ref)

    row, vec = _row_spec(ts, d), _vec_spec(d)
    return pl.pallas_call(
        body, name=name, grid=(s // ts,),
        in_specs=[row, vec, row, row, vec],
        out_specs=[row, vec, _vec_spec(128), row, vec],
        out_shape=[_sds((s, d), F32), _sds((1, d), F32), _sds((1, 128), F32), _sds((s, d), BF16), _sds((1, d), F32)],
        compiler_params=_params(("arbitrary",)),
    )(x, gain, target, y, gate)


def _hgrn_chunk_fwd(qr, fl, lbv, tri):
    sg = _sigmoid(fl)
    sgm = _sigmoid(-fl)
    f = lbv + (1.0 - lbv) * sg
    logf = jnp.log(f)
    k = (1.0 - lbv) * sgm
    cum = _dot(tri, logf, precision=lax.Precision.HIGHEST)
    cl = cum[HGRN_CHUNK - 1:HGRN_CHUNK, :]
    e = jnp.exp(cum)
    en = jnp.exp(-cum)
    es = jnp.exp(cl - cum)
    sq = _sigmoid(qr)
    qs = qr * sq
    return dict(sg=sg, sgm=sgm, f=f, k=k, cum=cum, cl=cl, e=e, en=en, es=es, sq=sq, qs=qs,
                qd=qs * e, ki=k * en, ks=k * es, dec=jnp.exp(cl))


def _tri_masks(strict=False):
    r = lax.broadcasted_iota(jnp.int32, (HGRN_CHUNK, HGRN_CHUNK), 0)
    c = lax.broadcasted_iota(jnp.int32, (HGRN_CHUNK, HGRN_CHUNK), 1)
    return (r > c) if strict else (r >= c)


def _carried(comm, grid, n_in, n_out):
    if not comm:
        return [], [], [], [], lambda refs: (refs, lambda: None, lambda: None)
    n_cin, n_cout = len(comm["operands"]), len(comm["out_shapes"])
    sems = [pltpu.SemaphoreType.DMA(comm["sems"]), pltpu.SemaphoreType.DMA(comm["sems"])]

    def split(refs):
        ins, cin = refs[:n_in], refs[n_in:n_in + n_cin]
        outs = refs[n_in + n_cin:n_in + n_cin + n_out]
        cout = refs[n_in + n_cin + n_out:n_in + n_cin + n_out + n_cout]
        scratch = refs[n_in + n_cin + n_out + n_cout:]
        send_sems, recv_sems = scratch[-2:]
        pids = [pl.program_id(ax) for ax in range(len(grid))]

        def start():
            @pl.when(functools.reduce(jnp.logical_and, [p == 0 for p in pids]))
            def _():
                comm["start"](cin, cout, send_sems, recv_sems)

        def finish():
            @pl.when(functools.reduce(jnp.logical_and, [p == n - 1 for p, n in zip(pids, grid)]))
            def _():
                comm["finish"](cin, cout, send_sems, recv_sems)

        return ins + outs + scratch[:-2], start, finish

    return [HBM] * n_cin, [HBM] * n_cout, list(comm["out_shapes"]), sems, split


def hgrn_fwd(name, proj3, lb, out_gain, comm=None):
    _, s, d = proj3.shape
    heads = d // HEAD_DIM
    t_rows = _tile(s, HGRN_ROWS)
    n_t = s // t_rows
    n_c = t_rows // HGRN_CHUNK
    width = HGRN_HEADS * HEAD_DIM
    grid = (heads // HGRN_HEADS, n_t)
    c_in, c_out, c_shapes, c_sems, split = _carried(comm, grid, 6, 3)

    def body(*refs):
        (q_ref, f_ref, i_ref, g_ref, lb_ref, gain_ref, u_ref, o_ref, st_ref, state), start, finish = split(refs)
        start()
        compute(q_ref, f_ref, i_ref, g_ref, lb_ref, gain_ref, u_ref, o_ref, st_ref, state)
        finish()

    def compute(q_ref, f_ref, i_ref, g_ref, lb_ref, gain_ref, u_ref, o_ref, st_ref, state):
        @pl.when(pl.program_id(1) == 0)
        def _():
            state[...] = jnp.zeros_like(state)

        causal = _tri_masks()
        tri = causal.astype(F32)
        hs = range(HGRN_HEADS)
        col = [pl.ds(hh * HEAD_DIM, HEAD_DIM) for hh in hs]
        for ci in range(n_c):
            rows = pl.ds(ci * HGRN_CHUNK, HGRN_CHUNK)
            c = [_hgrn_chunk_fwd(q_ref[rows, col[hh]], f_ref[rows, col[hh]], lb_ref[:, col[hh]], tri) for hh in hs]
            v = [i_ref[rows, col[hh]] for hh in hs]
            st = [state[hh] for hh in hs]
            scores = [jnp.where(causal, _bdot(c[hh]["qd"], c[hh]["ki"], NT), 0.0) for hh in hs]
            inter = [_bdot(c[hh]["qd"], st[hh], NT) for hh in hs]
            update = [_bdot(v[hh], c[hh]["ks"], TN) for hh in hs]
            o = [_bdot(scores[hh], v[hh]) + inter[hh] for hh in hs]
            for hh in hs:
                st_ref[hh, ci] = st[hh]
                state[hh] = st[hh] * c[hh]["dec"] + update[hh]
            for hh in hs:
                graw = g_ref[rows, col[hh]]
                rms = lax.rsqrt(jnp.mean(o[hh] * o[hh], axis=-1, keepdims=True) + NORM_EPS)
                u = o[hh] * rms * gain_ref[:, col[hh]] * (graw * _sigmoid(graw))
                o_ref[rows, col[hh]] = o[hh]
                u_ref[rows, col[hh]] = u.astype(u_ref.dtype)

    def pspec(blk):
        return pl.BlockSpec((None, t_rows, width), lambda h, t: (blk, t, h))

    hspec = pl.BlockSpec((1, width), lambda h, t: (0, h))
    ospec = pl.BlockSpec((t_rows, width), lambda h, t: (t, h))
    out = pl.pallas_call(
        body, name=name, grid=grid,
        in_specs=[pspec(0), pspec(1), pspec(2), pspec(3), hspec, hspec] + c_in,
        out_specs=[ospec, ospec,
                   pl.BlockSpec((HGRN_HEADS, n_c, HEAD_DIM, HEAD_DIM), lambda h, t: (h, t, 0, 0))] + c_out,
        out_shape=[_sds((s, d), BF16), _sds((s, d), F32),
                   _sds((heads, s // HGRN_CHUNK, HEAD_DIM, HEAD_DIM), F32)] + c_shapes,
        scratch_shapes=[pltpu.VMEM((HGRN_HEADS, HEAD_DIM, HEAD_DIM), F32)] + c_sems,
        compiler_params=_params(("arbitrary", "arbitrary")),
    )(proj3, proj3, proj3, proj3, lb, out_gain, *(comm["operands"] if comm else ()))
    return (out[:3], out[3:]) if comm else out


def hgrn_bwd(name, proj3, lb, out_gain, o, du, states, comm=None):
    _, s, d = proj3.shape
    heads = d // HEAD_DIM
    t_rows = _tile(s, HGRN_ROWS)
    n_t = s // t_rows
    n_c = t_rows // HGRN_CHUNK
    width = HGRN_HEADS * HEAD_DIM
    grid = (heads // HGRN_HEADS, n_t)
    c_in, c_out, c_shapes, c_sems, split = _carried(comm, grid, 9, 3)

    def body(*refs):
        refs, start, finish = split(refs)
        start()
        compute(*refs)
        finish()

    def compute(q_ref, f_ref, i_ref, g_ref, lb_ref, gain_ref, o_ref, du_ref, st_ref,
             dp_ref, dlb_ref, dgain_ref, dstate):
        @pl.when(pl.program_id(1) == 0)
        def _():
            dstate[...] = jnp.zeros_like(dstate)
            dlb_ref[...] = jnp.zeros_like(dlb_ref)
            dgain_ref[...] = jnp.zeros_like(dgain_ref)

        causal = _tri_masks()
        tri = causal.astype(F32)
        tri_t = jnp.logical_not(_tri_masks(strict=True)).astype(F32)
        hs = range(HGRN_HEADS)
        col = [pl.ds(hh * HEAD_DIM, HEAD_DIM) for hh in hs]
        lbv = [lb_ref[:, col[hh]] for hh in hs]
        gain = [gain_ref[:, col[hh]] for hh in hs]
        for ci in reversed(range(n_c)):
            rows = pl.ds(ci * HGRN_CHUNK, HGRN_CHUNK)
            qr = [q_ref[rows, col[hh]] for hh in hs]
            c = [_hgrn_chunk_fwd(qr[hh], f_ref[rows, col[hh]], lbv[hh], tri) for hh in hs]
            v = [i_ref[rows, col[hh]] for hh in hs]
            st = [st_ref[hh, ci] for hh in hs]
            dst = [dstate[hh] for hh in hs]
            do, dgraw = [], []
            for hh in hs:
                ov = o_ref[rows, col[hh]]
                duv = du_ref[rows, col[hh]].astype(F32)
                graw = g_ref[rows, col[hh]]
                sgg = _sigmoid(graw)
                gate = graw * sgg
                rms = lax.rsqrt(jnp.mean(ov * ov, axis=-1, keepdims=True) + NORM_EPS)
                on = ov * rms
                dgain_ref[:, col[hh]] += jnp.sum(duv * on * gate, axis=0, keepdims=True)
                dgraw.append(duv * on * gain[hh] * (sgg * (1.0 + graw * (1.0 - sgg))))
                don = duv * gain[hh] * gate
                do.append(rms * (don - on * jnp.mean(don * on, axis=-1, keepdims=True)))
            qd = [c[hh]["qd"] for hh in hs]
            ki = [c[hh]["ki"] for hh in hs]
            ks = [c[hh]["ks"] for hh in hs]
            p = [jnp.where(causal, _bdot(qd[hh], ki[hh], NT), 0.0) for hh in hs]
            dp = [jnp.where(causal, _bdot(do[hh], v[hh], NT), 0.0) for hh in hs]
            from_state = [_bdot(do[hh], st[hh]) for hh in hs]
            dks = [_bdot(v[hh], dst[hh]) for hh in hs]
            dv_state = [_bdot(ks[hh], dst[hh], NT) for hh in hs]
            dstate_new = [_bdot(do[hh], qd[hh], TN) for hh in hs]
            dqd = [_bdot(dp[hh], ki[hh]) + from_state[hh] for hh in hs]
            dki = [_bdot(dp[hh], qd[hh], TN) for hh in hs]
            dv = [_bdot(p[hh], do[hh], TN) + dv_state[hh] for hh in hs]
            ddec = [jnp.sum(dst[hh] * st[hh], axis=0, keepdims=True) for hh in hs]
            for hh in hs:
                dstate[hh] = dst[hh] * c[hh]["dec"] + dstate_new[hh]
            dcum = [dqd[hh] * qd[hh] - dki[hh] * ki[hh] - dks[hh] * ks[hh] for hh in hs]
            dcl = [jnp.sum(dks[hh] * ks[hh], axis=0, keepdims=True) + ddec[hh] * c[hh]["dec"] for hh in hs]
            dlogf = [_dot(tri_t, dcum[hh], precision=lax.Precision.HIGHEST) + dcl[hh] for hh in hs]
            for hh in hs:
                ch = c[hh]
                dqs = dqd[hh] * ch["e"]
                dk = dki[hh] * ch["en"] + dks[hh] * ch["es"]
                df = dlogf[hh] / ch["f"]
                sg, sgm, sq = ch["sg"], ch["sgm"], ch["sq"]
                one_m_lb = 1.0 - lbv[hh]
                dlb_ref[:, col[hh]] += jnp.sum(df * (1.0 - sg) - dk * sgm, axis=0, keepdims=True)
                dfl = df * one_m_lb * sg * (1.0 - sg) - dk * one_m_lb * sgm * (1.0 - sgm)
                dqr = dqs * (sq * (1.0 + qr[hh] * (1.0 - sq)))
                dp_ref[0, rows, col[hh]] = dqr.astype(dp_ref.dtype)
                dp_ref[1, rows, col[hh]] = dfl.astype(dp_ref.dtype)
                dp_ref[2, rows, col[hh]] = dv[hh].astype(dp_ref.dtype)
                dp_ref[3, rows, col[hh]] = dgraw[hh].astype(dp_ref.dtype)

    def pspec(blk):
        return pl.BlockSpec((None, t_rows, width), lambda h, t: (blk, n_t - 1 - t, h))

    hspec = pl.BlockSpec((1, width), lambda h, t: (0, h))
    ospec = pl.BlockSpec((t_rows, width), lambda h, t: (n_t - 1 - t, h))
    out = pl.pallas_call(
        body, name=name, grid=grid,
        in_specs=[pspec(0), pspec(1), pspec(2), pspec(3), hspec, hspec, ospec, ospec,
                  pl.BlockSpec((HGRN_HEADS, n_c, HEAD_DIM, HEAD_DIM), lambda h, t: (h, n_t - 1 - t, 0, 0))] + c_in,
        out_specs=[pl.BlockSpec((4, t_rows, width), lambda h, t: (0, n_t - 1 - t, h)), hspec, hspec] + c_out,
        out_shape=[_sds((4, s, d), BF16), _sds((1, d), F32), _sds((1, d), F32)] + c_shapes,
        scratch_shapes=[pltpu.VMEM((HGRN_HEADS, HEAD_DIM, HEAD_DIM), F32)] + c_sems,
        compiler_params=_params(("arbitrary", "arbitrary")),
    )(proj3, proj3, proj3, proj3, lb, out_gain, o, du, states, *(comm["operands"] if comm else ()))
    return (out[:3], out[3:]) if comm else out


def attn_fwd(name, q, kv, comm=None):
    s, dq = q.shape
    kvh = kv.shape[1] // (2 * HEAD_DIM)
    assert dq == kvh * KV_GROUP * HEAD_DIM
    tq = _tile(s, ATTN_TILE)
    scale = HEAD_DIM ** -0.5

    grid = (kvh, s // tq)
    c_in, c_out, c_shapes, c_sems, split = _carried(comm, grid, 3, 3)

    def body(*refs):
        (q_ref, k_ref, v_ref, o_ref, tot_ref, cnt_ref), start, finish = split(refs)
        start()
        compute(q_ref, k_ref, v_ref, o_ref, tot_ref, cnt_ref)
        finish()

    def compute(q_ref, k_ref, v_ref, o_ref, tot_ref, cnt_ref):
        i = pl.program_id(1)
        heads = range(KV_GROUP)
        qs = [q_ref[:, g * HEAD_DIM:(g + 1) * HEAD_DIM] for g in heads]
        r_i = lax.broadcasted_iota(jnp.int32, (tq, tq), 0)
        c_i = lax.broadcasted_iota(jnp.int32, (tq, tq), 1)
        later = (r_i > c_i).astype(BF16)
        later2 = jnp.concatenate([later, later], axis=0)
        mask = c_i < r_i
        ones2 = jnp.ones((8, 2 * tq), BF16)

        def block(j, carry, masked):
            rows = pl.ds(pl.multiple_of(j * tq, tq), tq)
            kj = k_ref[rows, :]
            vj = v_ref[rows, :]
            z = [_dot(qs[g], kj, NT) * scale for g in heads]
            lbeta = [_log_sigmoid(z[g]) for g in heads]
            lrest = [lbeta[g] - z[g] for g in heads]
            if masked:
                lrest = [jnp.where(mask, lrest[g], 0.0) for g in heads]
            hl = [jnp.concatenate(_split_bf16(lrest[g]), axis=1) for g in heads]
            between = [_dot(hl[g], later2) + carry[g][0] for g in heads]
            sums = [_dot(ones2, hl[g], NT) for g in heads]
            w = [jnp.exp(lbeta[g] + between[g]) for g in heads]
            if masked:
                w = [jnp.where(mask, w[g], 0.0) for g in heads]
            pv = [_dot(w[g].astype(BF16), vj) for g in heads]
            return tuple((carry[g][0] + jnp.sum(lrest[g], axis=1, keepdims=True), carry[g][1] + pv[g],
                          carry[g][2] + sums[g]) for g in heads)

        def alive(carry):
            top = carry[0][0]
            for g in heads[1:]:
                top = jnp.maximum(top, carry[g][0])
            return jnp.max(top) > DEAD_LOG_WEIGHT

        zero = (jnp.zeros((tq, 1), F32), jnp.zeros((tq, HEAD_DIM), F32), jnp.zeros((8, tq), F32))
        carry = block(i, (zero,) * KV_GROUP, True)

        def step(state):
            jj, _, cr = state
            cr = block(i - 1 - jj, cr, False)
            return jj + 1, alive(cr), cr

        done, _, carry = lax.while_loop(lambda st: jnp.logical_and(st[0] < i, st[1]), step,
                                        (jnp.int32(0), alive(carry), carry))
        for g in heads:
            o_ref[:, g * HEAD_DIM:(g + 1) * HEAD_DIM] = carry[g][1]
            tot_ref[g] = carry[g][2][0:1, :]
        cnt_ref[pl.program_id(0), i] = done

    group = KV_GROUP * HEAD_DIM
    out = pl.pallas_call(
        body, name=name, grid=grid,
        in_specs=[pl.BlockSpec((tq, group), lambda kh, i: (i, kh)),
                  pl.BlockSpec((s, HEAD_DIM), lambda kh, i: (0, kh)),
                  pl.BlockSpec((s, HEAD_DIM), lambda kh, i: (0, kvh + kh))] + c_in,
        out_specs=[pl.BlockSpec((tq, group), lambda kh, i: (i, kh)),
                   pl.BlockSpec((KV_GROUP, 1, tq), lambda kh, i: (kh, 0, i)),
                   pl.BlockSpec(memory_space=pltpu.SMEM)] + c_out,
        out_shape=[_sds((s, dq), F32), _sds((dq // HEAD_DIM, 1, s), F32),
                   _sds((kvh, s // tq), jnp.int32)] + c_shapes,
        scratch_shapes=c_sems,
        compiler_params=_params(("arbitrary", "arbitrary")),
    )(q, kv, kv, *(comm["operands"] if comm else ()))
    return (out[:3], out[3:]) if comm else out


def attn_bwd(name, q, kv, totals, visited, do):
    s, dq_cols = q.shape
    kvh = kv.shape[1] // (2 * HEAD_DIM)
    tq = _tile(s, ATTN_TILE)
    scale = HEAD_DIM ** -0.5

    def body(cnt_ref, q_ref, k_ref, v_ref, tot_ref, do_ref, dq_ref, dkv_ref):
        i = pl.program_id(1)
        first = i - jnp.clip(cnt_ref[pl.program_id(0), i], 0, i)

        @pl.when(i == 0)
        def _():
            dkv_ref[...] = jnp.zeros_like(dkv_ref)

        heads = range(KV_GROUP)
        qs = [q_ref[:, g * HEAD_DIM:(g + 1) * HEAD_DIM] for g in heads]
        dobs = [do_ref[:, g * HEAD_DIM:(g + 1) * HEAD_DIM].astype(BF16) for g in heads]
        tots = [tot_ref[g] for g in heads]
        q_all = jnp.concatenate(qs, axis=0)
        do_all = jnp.concatenate(dobs, axis=0)
        r_i = lax.broadcasted_iota(jnp.int32, (tq, tq), 0)
        c_i = lax.broadcasted_iota(jnp.int32, (tq, tq), 1)
        upto = (c_i <= r_i).astype(BF16)
        before = (c_i < r_i).astype(BF16)
        upto2 = jnp.concatenate([upto, upto], axis=1)
        before2 = jnp.concatenate([before, before], axis=1)
        mask = r_i < c_i

        def block(j, carry, masked):
            rows = pl.ds(pl.multiple_of(j * tq, tq), tq)
            kj = k_ref[rows, :]
            vj = v_ref[rows, :]
            zt = [_dot(kj, qs[g], NT) * scale for g in heads]
            dwt = [_dot(vj, dobs[g], NT) for g in heads]
            lbeta = [_log_sigmoid(zt[g]) for g in heads]
            lrest_raw = [lbeta[g] - zt[g] for g in heads]
            lrest = [jnp.where(mask, lrest_raw[g], 0.0) for g in heads] if masked else lrest_raw
            hl = [jnp.concatenate(_split_bf16(lrest[g]), axis=0) for g in heads]
            upto_sum = [_dot(upto2, hl[g]) for g in heads]
            wt = [jnp.exp(lbeta[g] + (tots[g] - carry[g][0] - upto_sum[g])) for g in heads]
            if masked:
                wt = [jnp.where(mask, wt[g], 0.0) for g in heads]
            dat = [dwt[g] * wt[g] for g in heads]
            earlier = [_dot(before2, jnp.concatenate(_split_bf16(dat[g]), axis=0)) for g in heads]
            dzt = [dat[g] * jnp.exp(lrest_raw[g]) - (carry[g][1] + earlier[g]) * jnp.exp(lbeta[g]) for g in heads]
            if masked:
                dzt = [jnp.where(mask, dzt[g], 0.0) for g in heads]
            dzb = [(dzt[g] * scale).astype(BF16) for g in heads]
            dq_new = [_dot(dzb[g], kj, TN) for g in heads]
            dkv_ref[0, rows, :] += _dot(jnp.concatenate(dzb, axis=1), q_all)
            dkv_ref[1, rows, :] += _dot(jnp.concatenate([wt[g].astype(BF16) for g in heads], axis=1), do_all)
            return tuple((carry[g][0] + jnp.sum(lrest[g], axis=0, keepdims=True),
                          carry[g][1] + jnp.sum(dat[g], axis=0, keepdims=True),
                          carry[g][2] + dq_new[g]) for g in heads)

        zrow = jnp.zeros((1, tq), F32)
        carry = ((zrow, zrow, jnp.zeros((tq, HEAD_DIM), F32)),) * KV_GROUP
        carry = lax.fori_loop(first, i, lambda j, cr: block(j, cr, False), carry)
        carry = block(i, carry, True)
        for g in heads:
            dq_ref[:, g * HEAD_DIM:(g + 1) * HEAD_DIM] = carry[g][2].astype(dq_ref.dtype)

    group = KV_GROUP * HEAD_DIM
    qspec = pl.BlockSpec((tq, group), lambda kh, i, cnt: (i, kh))
    return pl.pallas_call(
        body, name=name,
        grid_spec=pltpu.PrefetchScalarGridSpec(
            num_scalar_prefetch=1, grid=(kvh, s // tq),
            in_specs=[qspec,
                      pl.BlockSpec((s, HEAD_DIM), lambda kh, i, cnt: (0, kh)),
                      pl.BlockSpec((s, HEAD_DIM), lambda kh, i, cnt: (0, kvh + kh)),
                      pl.BlockSpec((KV_GROUP, 1, tq), lambda kh, i, cnt: (kh, 0, i)),
                      qspec],
            out_specs=[qspec, pl.BlockSpec((2, s, HEAD_DIM), lambda kh, i, cnt: (0, 0, kh))]),
        out_shape=[_sds((s, dq_cols), BF16), _sds((2, s, kvh * HEAD_DIM), F32)],
        compiler_params=_params(("parallel", "arbitrary")),
    )(visited, q, kv, kv, totals, do)


def ada_project(name, c_all, w, b):
    bsz, d = c_all.shape
    n = w.shape[1]
    tn = _tile(n, 512)

    def body(c_ref, w_ref, b_ref, o_ref):
        cv = c_ref[...]
        act = cv * _sigmoid(cv)
        o_ref[...] = _bdot(act, w_ref[...]) + b_ref[...]

    return pl.pallas_call(
        body, name=name, grid=(n // tn,),
        in_specs=[pl.BlockSpec((bsz, d), lambda i: (0, 0)),
                  pl.BlockSpec((d, tn), lambda i: (0, i)),
                  pl.BlockSpec((1, tn), lambda i: (0, i))],
        out_specs=pl.BlockSpec((bsz, tn), lambda i: (0, i)),
        out_shape=_sds((bsz, n), F32),
        compiler_params=_params(("parallel",)),
    )(c_all, w, b)


def _adamw_math(w, g, m, v):
    m = ADAM_B1 * m + (1.0 - ADAM_B1) * g
    v = ADAM_B2 * v + (1.0 - ADAM_B2) * (g * g)
    m_hat = m / (1.0 - ADAM_B1 ** ADAM_STEP)
    v_hat = v / (1.0 - ADAM_B2 ** ADAM_STEP)
    delta = -ADAM_LR * (m_hat / (jnp.sqrt(v_hat) + ADAM_EPS) + ADAM_WD * w)
    return delta, m, v


def adamw(name, w, g, m, v):
    r, c = w.shape
    tr = _tile(r, 256)
    tc = _tile(c, 2048)

    def body(w_ref, g_ref, m_ref, v_ref, d_ref, mo_ref, vo_ref):
        delta, mn, vn = _adamw_math(w_ref[...], g_ref[...], m_ref[...], v_ref[...])
        d_ref[...] = delta
        mo_ref[...] = mn
        vo_ref[...] = vn

    spec = pl.BlockSpec((tr, tc), lambda i, j: (i, j))
    return pl.pallas_call(
        body, name=name, grid=(r // tr, c // tc),
        in_specs=[spec] * 4, out_specs=[spec] * 3,
        out_shape=[_sds((r, c), F32)] * 3,
        compiler_params=_params(("parallel", "parallel")),
    )(w, g, m, v)


def ada_grad_adamw(name, c_t, dmod, w, m, v):
    layers, d, n = w.shape
    tr = _tile(d, 256)
    tc = _tile(n, 512)

    def body(a_ref, dm_ref, w_ref, m_ref, v_ref, g_ref, d_ref, mo_ref, vo_ref):
        cv = a_ref[...]
        g = _bdot(cv * _sigmoid(cv), dm_ref[...])
        delta, mn, vn = _adamw_math(w_ref[...], g, m_ref[...], v_ref[...])
        g_ref[...] = g
        d_ref[...] = delta
        mo_ref[...] = mn
        vo_ref[...] = vn

    spec = pl.BlockSpec((None, tr, tc), lambda l, i, j: (l, i, j))
    return pl.pallas_call(
        body, name=name, grid=(layers, d // tr, n // tc),
        in_specs=[pl.BlockSpec((tr, 128), lambda l, i, j: (i, 0)),
                  pl.BlockSpec((None, 128, tc), lambda l, i, j: (l, 0, j)), spec, spec, spec],
        out_specs=[spec] * 4, out_shape=[_sds((layers, d, n), F32)] * 4,
        compiler_params=_params(("parallel", "parallel", "parallel")),
    )(c_t, dmod, w, m, v)


def device_sum(name, gathered):
    _, r, c = gathered.shape

    def body(g_ref, o_ref):
        acc = g_ref[0]
        for dev in range(1, N_DEV):
            acc = acc + g_ref[dev]
        o_ref[...] = acc

    return pl.pallas_call(
        body, name=name,
        in_specs=[pl.BlockSpec(memory_space=pltpu.VMEM)],
        out_specs=pl.BlockSpec(memory_space=pltpu.VMEM),
        out_shape=_sds((r, c), F32),
    )(gathered)


def lower_bound_fwd(name, logits):
    _, d = logits.shape

    def body(l_ref, o_ref):
        l0 = l_ref[0:1, :]
        l1 = l_ref[1:2, :]
        mx = jnp.maximum(l0, l1)
        e0 = jnp.exp(l0 - mx)
        e1 = jnp.exp(l1 - mx)
        o_ref[...] = e0 / (e0 + e1)

    return pl.pallas_call(
        body, name=name,
        in_specs=[pl.BlockSpec(memory_space=pltpu.VMEM)],
        out_specs=pl.BlockSpec(memory_space=pltpu.VMEM),
        out_shape=_sds((1, d), F32),
    )(logits)


def lower_bound_bwd(name, logits, dlb):
    _, d = logits.shape

    def body(l_ref, dlb_ref, o_ref):
        l0 = l_ref[0:1, :]
        l1 = l_ref[1:2, :]
        mx = jnp.maximum(l0, l1)
        e0 = jnp.exp(l0 - mx)
        e1 = jnp.exp(l1 - mx)
        p0 = e0 / (e0 + e1)
        p1 = e1 / (e0 + e1)
        g = dlb_ref[...] * p0 * p1
        o_ref[0:1, :] = g
        o_ref[1:2, :] = -g

    return pl.pallas_call(
        body, name=name,
        in_specs=[pl.BlockSpec(memory_space=pltpu.VMEM)] * 2,
        out_specs=pl.BlockSpec(memory_space=pltpu.VMEM),
        out_shape=_sds((2, d), F32),
    )(logits, dlb)


HBM = pl.BlockSpec(memory_space=pltpu.HBM)


def _position():
    return lax.axis_index("x"), lax.axis_index("y"), lax.axis_index("c")


def _remote(src, dst, send_sem, recv_sem, device):
    return pltpu.make_async_remote_copy(src_ref=src, dst_ref=dst, send_sem=send_sem, recv_sem=recv_sem,
                                        device_id=device, device_id_type=MESH)


def allgather8(name, x):
    r, c = x.shape

    def body(x_ref, out_ref, send_sems, recv_sems):
        px, py, pc = _position()
        me = 4 * px + 2 * py + pc
        out_ref[me] = x_ref[...]
        copies = []
        for k in range(1, N_DEV):
            peer = (1 - px if k & 4 else px, 1 - py if k & 2 else py, 1 - pc if k & 1 else pc)
            cp = _remote(x_ref, out_ref.at[me], send_sems.at[k - 1], recv_sems.at[k - 1], peer)
            cp.start()
            copies.append(cp)
        for cp in copies:
            cp.wait()

    return pl.pallas_call(
        body, name=name,
        in_specs=[pl.BlockSpec(memory_space=pltpu.VMEM)],
        out_specs=pl.BlockSpec(memory_space=pltpu.VMEM),
        out_shape=_sds((N_DEV, r, c), x.dtype),
        scratch_shapes=[pltpu.SemaphoreType.DMA((N_DEV - 1,)), pltpu.SemaphoreType.DMA((N_DEV - 1,))],
    )(x)


def _other_chips(px, py):
    return [(1 - px, py), (px, 1 - py), (1 - px, 1 - py)]


def gather_weights(name, shards):
    n = len(shards)
    hook = gather_hook(shards)

    def body(*refs):
        ins, outs = refs[:n], refs[n:2 * n]
        send_sems, recv_sems = refs[2 * n:]
        hook["start"](ins, outs, send_sems, recv_sems)
        hook["finish"](ins, outs, send_sems, recv_sems)

    return pl.pallas_call(
        body, name=name,
        in_specs=[HBM] * n, out_specs=[HBM] * n,
        out_shape=hook["out_shapes"],
        scratch_shapes=[pltpu.SemaphoreType.DMA(hook["sems"]), pltpu.SemaphoreType.DMA(hook["sems"])],
    )(*shards)


def gather_hook(shards):
    n = len(shards)

    def copies(ins, outs, send_sems, recv_sems):
        px, py, pc = _position()
        chip = 2 * px + py
        sibling = (px, py, 1 - pc)
        own, sends, arrivals, passes, pass_arrivals = [], [], [], [], []
        for a in range(n):
            rows = ins[a].shape[0] // 2
            mine, theirs = pl.ds(pc * rows, rows), pl.ds((1 - pc) * rows, rows)
            own.append(_remote(ins[a], outs[a].at[chip], send_sems.at[a, 6], recv_sems.at[a, 6], sibling))
            for j, (ox, oy) in enumerate(_other_chips(px, py)):
                sends.append(_remote(ins[a].at[mine], outs[a].at[chip, mine],
                                     send_sems.at[a, j], recv_sems.at[a, j], (ox, oy, pc)))
                landed = outs[a].at[2 * ox + oy, mine]
                arrivals.append(_remote(landed, landed, send_sems.at[a, j], recv_sems.at[a, j], sibling))
                passes.append(_remote(landed, landed, send_sems.at[a, 3 + j], recv_sems.at[a, 3 + j], sibling))
                via = outs[a].at[2 * ox + oy, theirs]
                pass_arrivals.append(_remote(via, via, send_sems.at[a, 3 + j], recv_sems.at[a, 3 + j], sibling))
        return own, sends, arrivals, passes, pass_arrivals

    def start(ins, outs, send_sems, recv_sems):
        own, sends, _, _, _ = copies(ins, outs, send_sems, recv_sems)
        for cp in own + sends:
            cp.start()

    def finish(ins, outs, send_sems, recv_sems):
        own, sends, arrivals, passes, pass_arrivals = copies(ins, outs, send_sems, recv_sems)
        for arrival, onward in zip(arrivals, passes):
            arrival.wait_recv()
            onward.start()
        for arrival in pass_arrivals:
            arrival.wait_recv()
        for cp in sends + passes:
            cp.wait_send()
        for cp in own:
            cp.wait()

    return dict(operands=list(shards), out_shapes=[_sds((N_CHIPS,) + s.shape, s.dtype) for s in shards],
                sems=(n, 7), start=start, finish=finish)


def sibling_exchange(name, grads):
    n = len(grads)
    hook = exchange_hook(grads)

    def body(*refs):
        ins, outs = refs[:n], refs[n:2 * n]
        send_sems, recv_sems = refs[2 * n:]
        hook["start"](ins, outs, send_sems, recv_sems)
        hook["finish"](ins, outs, send_sems, recv_sems)

    return pl.pallas_call(
        body, name=name,
        in_specs=[HBM] * n, out_specs=[HBM] * n,
        out_shape=hook["out_shapes"],
        scratch_shapes=[pltpu.SemaphoreType.DMA(hook["sems"]), pltpu.SemaphoreType.DMA(hook["sems"])],
    )(*grads)


def exchange_hook(grads, first_sem=0):
    n = len(grads)

    def copies(ins, outs, send_sems, recv_sems):
        px, py, pc = _position()
        made = []
        for a in range(n):
            rows = ins[a].shape[1] // 2
            src = ins[a].at[:, pl.ds((1 - pc) * rows, rows), :]
            made.append(_remote(src, outs[a], send_sems.at[first_sem + a, 0], recv_sems.at[first_sem + a, 0],
                                (px, py, 1 - pc)))
        return made

    def start(ins, outs, send_sems, recv_sems):
        for cp in copies(ins, outs, send_sems, recv_sems):
            cp.start()

    def finish(ins, outs, send_sems, recv_sems):
        for cp in copies(ins, outs, send_sems, recv_sems):
            cp.wait()

    return dict(operands=list(grads),
                out_shapes=[_sds((g.shape[0], g.shape[1] // 2, g.shape[2]), g.dtype) for g in grads],
                sems=(first_sem + n, 3), start=start, finish=finish)


def combine_hooks(first, second):
    n_in, n_out = len(first["operands"]), len(first["out_shapes"])

    def start(ins, outs, send_sems, recv_sems):
        first["start"](ins[:n_in], outs[:n_out], send_sems, recv_sems)
        second["start"](ins[n_in:], outs[n_out:], send_sems, recv_sems)

    def finish(ins, outs, send_sems, recv_sems):
        first["finish"](ins[:n_in], outs[:n_out], send_sems, recv_sems)
        second["finish"](ins[n_in:], outs[n_out:], send_sems, recv_sems)

    return dict(operands=first["operands"] + second["operands"], out_shapes=first["out_shapes"] + second["out_shapes"],
                sems=second["sems"], start=start, finish=finish)


def pair_add(name, g, recv, core):
    nb, rows, cols = g.shape
    half = rows // 2
    tr = _tile(half, 256)
    steps = half // tr

    def body(core_ref, g_ref, r_ref, o_ref):
        del core_ref
        o_ref[...] = (g_ref[...].astype(F32) + r_ref[...].astype(F32)).astype(o_ref.dtype)

    return pl.pallas_call(
        body, name=name,
        grid_spec=pltpu.PrefetchScalarGridSpec(
            num_scalar_prefetch=1, grid=(nb, steps),
            in_specs=[pl.BlockSpec((None, tr, cols), lambda j, i, core_ref: (j, core_ref[0] * steps + i, 0)),
                      pl.BlockSpec((None, tr, cols), lambda j, i, core_ref: (j, i, 0))],
            out_specs=pl.BlockSpec((None, tr, cols), lambda j, i, core_ref: (j, i, 0))),
        out_shape=_sds((nb, half, cols), g.dtype),
        compiler_params=_params(("parallel", "parallel")),
    )(core, g, recv)


def scatter_hook(parts):
    n = len(parts)

    def copies(ins, outs, send_sems, recv_sems):
        px, py, pc = _position()
        return [_remote(ins[a].at[2 * ox + oy], outs[a].at[j], send_sems.at[a, j], recv_sems.at[a, j], (ox, oy, pc))
                for a in range(n) for j, (ox, oy) in enumerate(_other_chips(px, py))]

    def start(ins, outs, send_sems, recv_sems):
        for cp in copies(ins, outs, send_sems, recv_sems):
            cp.start()

    def finish(ins, outs, send_sems, recv_sems):
        for cp in copies(ins, outs, send_sems, recv_sems):
            cp.wait()

    return dict(operands=list(parts), out_shapes=[_sds((N_CHIPS - 1,) + p.shape[1:], p.dtype) for p in parts],
                sems=(n, 3), start=start, finish=finish)


def chip_sum(name, part, recv, where, out_shape, lead, dest=None):
    _, half, cols = part.shape
    tr = _tile(half, 256)
    steps = half // tr

    def body(where_ref, p_ref, r_ref, *rest):
        o_ref = rest[-1]
        acc = p_ref[...].astype(F32)
        for j in range(N_CHIPS - 1):
            acc = acc + r_ref[j].astype(F32)
        o_ref[...] = acc

    if lead is None:
        ospec = pl.BlockSpec((tr, cols), lambda i, w: (w[1] * steps + i, 0))
    else:
        ospec = pl.BlockSpec((None, tr, cols), lambda i, w: (lead, w[1] * steps + i, 0))
    in_specs = [pl.BlockSpec((None, tr, cols), lambda i, w: (w[0], i, 0)),
                pl.BlockSpec((N_CHIPS - 1, tr, cols), lambda i, w: (0, i, 0))]
    operands = [where, part, recv]
    aliases = {}
    if dest is not None:
        in_specs.append(pl.BlockSpec(memory_space=pl.ANY))
        operands.append(dest)
        aliases = {3: 0}
    return pl.pallas_call(
        body, name=name,
        grid_spec=pltpu.PrefetchScalarGridSpec(num_scalar_prefetch=1, grid=(steps,),
                                               in_specs=in_specs, out_specs=ospec),
        out_shape=out_shape, input_output_aliases=aliases,
        compiler_params=_params(("parallel",)),
    )(*operands)


def sibling_share(name, slabs, places):
    n = len(slabs)
    k = len(places)

    def body(*refs):
        outs = refs[n:2 * n]
        send_sems, recv_sems = refs[2 * n:]
        px, py, pc = _position()
        sibling = (px, py, 1 - pc)
        copies = []
        for a, (oi, lead, half) in enumerate(places):
            slab = outs[oi] if lead is None else outs[oi].at[lead]
            mine = slab.at[pl.ds(pc * half, half)]
            theirs = slab.at[pl.ds((1 - pc) * half, half)]
            cp = _remote(mine, mine, send_sems.at[a], recv_sems.at[a], sibling)
            cp.start()
            copies.append((cp, _remote(theirs, theirs, send_sems.at[a], recv_sems.at[a], sibling)))
        for cp, arrival in copies:
            cp.wait_send()
            arrival.wait_recv()

    return pl.pallas_call(
        body, name=name,
        in_specs=[HBM] * n, out_specs=[HBM] * n,
        out_shape=[_sds(s.shape, s.dtype) for s in slabs],
        input_output_aliases={a: a for a in range(n)},
        scratch_shapes=[pltpu.SemaphoreType.DMA((k,)), pltpu.SemaphoreType.DMA((k,))],
    )(*slabs)


def _row(a, i):
    return a[i:i + 1]


def _relu_sq(acc):
    r = jnp.maximum(acc, 0.0)
    return r, r * r


def _residual(acc, res, gate):
    return res + gate * acc, acc


def _with_comm(res, comm):
    return res if comm else (res, [])


def _blocked(g):
    return g if g.ndim == 3 else g.reshape(N_CHIPS, g.shape[0] // N_CHIPS, g.shape[1])


def _pre_reduce(tag, named, core):
    glist = [_blocked(g) for _, g in named]
    recv = sibling_exchange("grad_exchange_" + tag, glist)
    return [pair_add("grad_pair_add_" + k, g, r, core) for (k, _), g, r in zip(named, glist, recv)]


def _mlp_fwd(tag, x, gain, mod, w1, w2=None, w2_shard=None, down_comm=None):
    h = norm_mod_fwd(tag + "_mlp_norm", x, gain, _row(mod, 4), _row(mod, 3))
    up_comm = None if w2 is not None else gather_hook([w2_shard])
    (r3, a3), got = _with_comm(mm_nn_b(tag + "_mlp_up", h, w1, [BF16, BF16], _relu_sq, comm=up_comm), up_comm)
    if w2 is None:
        w2 = got[0]
    (x_out, m), got = _with_comm(mm_nn_r(tag + "_mlp_down", a3, w2, [F32, BF16], _residual,
                                         (x, _row(mod, 5)), ("tile", "row"), comm=down_comm), down_comm)
    return x_out, (h, r3, a3, m), w2, got


def _mlp_bwd(tag, dx_out, dm, x, gain, mod, w1, w2, saved, core, branch, down_comm=None, w2_comm=None):
    h, r3, a3, _ = saved
    (dz3,), got_down = _with_comm(mm_nt_b(tag + "_mlp_down_dgrad", dm, w2, [BF16],
                                          lambda acc, r: (acc * (2.0 * r.astype(F32)),), (r3,), comm=down_comm),
                                  down_comm)
    gw2, got_w2 = _with_comm(mm_tn(tag + "_mlp_w2_grad", a3, dm[None], BF16, comm=w2_comm), w2_comm)
    gw1, (from_sibling,) = mm_tn(tag + "_mlp_w1_grad", h[None], dz3, BF16, comm=exchange_hook([gw2]))
    part2 = pair_add("grad_pair_add_" + tag + "_w2", gw2, from_sibling, core)
    dh, (recv2, from_sibling) = mm_nt_r(tag + "_mlp_up_dgrad", dz3, w1, F32,
                                        comm=combine_hooks(scatter_hook([part2]), exchange_hook([gw1], 1)))
    part1 = pair_add("grad_pair_add_" + tag + "_w1", gw1, from_sibling, core)
    dx, dsh, dsc, dgain, dy, dgate = norm_mod_bwd(tag + "_mlp_norm_bwd", dh, x, gain, _row(mod, 4), dx_out, branch)
    return dx, (dy, dgate), part1, (part2, recv2), (dsh, dsc), dgain, got_down, got_w2


def local_step(x, target, mod0, mod1, kvmod, norm_mix, norm_mlp, kv_norm, final_norm, lb, out_gain,
               w, shards, core):
    w = dict(w)

    def flat(g):
        return g.reshape(-1, g.shape[-1])

    h1 = norm_mod_fwd("l0_mix_norm", x, _row(norm_mix, 0), _row(mod0, 1), _row(mod0, 0))
    (proj3,), (w["w1_0"],) = mm_nn_b("l0_in_proj", h1, w["a_in"], [F32], comm=gather_hook([shards["w1_0"]]))
    (u, o_h, states), got = hgrn_fwd("l0_hgrn_fwd", proj3, lb, out_gain,
                                     comm=gather_hook([shards["bq"], shards["kv"], shards["bo"]]))
    w["bq"], w["kv"], w["bo"] = (flat(g) for g in got)
    x1, y0 = mm_nn_b("l0_out_proj", u, w["a_out"][None], [F32, BF16], _residual,
                     (x[None], _row(mod0, 2)), ("tile", "row"))
    x1, y0 = x1[0], y0[0]
    x2, mlp0, w["w2_0"], (w["w1_1"],) = _mlp_fwd(
        "l0", x1, _row(norm_mlp, 0), mod0, w["w1_0"], w2_shard=shards["w2_0"],
        down_comm=gather_hook([shards["w1_1"]]))
    hk = norm_mod_fwd("kv_norm", x2, kv_norm, _row(kvmod, 1), _row(kvmod, 0))
    kv = mm_nn_b("kv_proj", hk, w["kv"][None], [BF16])[0][0]
    h3 = norm_mod_fwd("l1_mix_norm", x2, _row(norm_mix, 1), _row(mod1, 1), _row(mod1, 0))
    q = mm_nn_b("l1_q_proj", h3, w["bq"][None], [BF16])[0][0]
    (o_a, totals, visited), (w["w2_1"],) = attn_fwd("l1_attn_fwd", q, kv, comm=gather_hook([shards["w2_1"]]))
    x3, y1 = mm_nn_b("l1_out_proj", o_a, w["bo"][None], [F32, BF16], _residual,
                     (x2[None], _row(mod1, 2)), ("tile", "row"))
    x3, y1 = x3[0], y1[0]
    x4, mlp1, _, _ = _mlp_fwd("l1", x3, _row(norm_mlp, 1), mod1, w["w1_1"], w["w2_1"])
    dx4, d_final, loss, dm1, dgate_mlp1 = final_loss("final_loss", x4, final_norm, target, mlp1[3], _row(mod1, 5))

    reduce = {}
    dx3, (dy1, dgate1), part_w1_1, reduce["w2_1"], dmlp1, d_nmlp1, _, _ = _mlp_bwd(
        "l1", dx4, dm1, x3, _row(norm_mlp, 1), mod1, w["w1_1"], w["w2_1"], mlp1, core, (y1, _row(mod1, 2)))
    do_a = mm_nt_b("l1_out_dgrad", dy1, w["bo"][None], [F32])[0][0]
    g_bo = mm_tn("l1_out_grad", o_a[None], dy1[None], BF16)[0]
    dq, dkv3 = attn_bwd("l1_attn_bwd", q, kv, totals, visited, do_a)
    g_bq = mm_tn("l1_q_grad", h3[None], dq[None], BF16)[0]
    dh3 = mm_nt_b("l1_q_dgrad", dq, w["bq"][None], [F32])[0][0]
    dx2, dsh, dsc, d_nmix1 = norm_mod_bwd("l1_mix_norm_bwd", dh3, x2, _row(norm_mix, 1), _row(mod1, 1), dx3)
    dmod1 = jnp.concatenate([dsh, dsc, dgate1, *dmlp1, dgate_mlp1], axis=0)
    dkv = jnp.concatenate([dkv3[0], dkv3[1]], axis=1).astype(BF16)
    g_kv = mm_tn("kv_grad", hk[None], dkv[None], BF16)[0]
    attn_grads = [_blocked(g_bo), _blocked(g_bq), _blocked(g_kv)]
    (dhk,), from_sibling = mm_nt_b("kv_dgrad", dkv, w["kv"][None], [F32], comm=exchange_hook(attn_grads))
    dhk = dhk[0]
    dx2, dsh, dsc, d_nkv, dm0, dgate_mlp0 = norm_mod_bwd("kv_norm_bwd", dhk, x2, kv_norm, _row(kvmod, 1), dx2,
                                                         (mlp0[3], _row(mod0, 5)))
    dkvmod = jnp.concatenate([dsh, dsc], axis=0)
    attn_parts = [pair_add("grad_pair_add_" + k, g, r, core)
                  for k, g, r in zip(("bo", "bq", "kv"), attn_grads, from_sibling)]
    dx1, (dy0, dgate0), part_w1_0, reduce["w2_0"], dmlp0, d_nmlp0, (recv,), attn_recv = _mlp_bwd(
        "l0", dx2, dm0, x1, _row(norm_mlp, 0), mod0, w["w1_0"], w["w2_0"], mlp0, core, (y0, _row(mod0, 2)),
        scatter_hook([part_w1_1]), scatter_hook(attn_parts))
    reduce["w1_1"] = (part_w1_1, recv)
    for k, part, recv in zip(("bo", "bq", "kv"), attn_parts, attn_recv):
        reduce[k] = (part, recv)
    du = mm_nt_b("l0_out_dgrad", dy0, w["a_out"][None], [F32])[0][0]
    g_a_out = mm_tn("l0_out_grad", u[None], dy0[None], BF16)[0]
    g_a_out = _blocked(g_a_out)
    (dproj3, d_lb, d_out_gain), (recv, from_sibling) = hgrn_bwd(
        "l0_hgrn_bwd", proj3, lb, out_gain, o_h, du, states,
        comm=combine_hooks(scatter_hook([part_w1_0]), exchange_hook([g_a_out], 1)))
    reduce["w1_0"] = (part_w1_0, recv)
    part = [pair_add("grad_pair_add_a_out", g_a_out, from_sibling, core)]
    g_a_in, recv = mm_tn("l0_in_grad", h1[None], dproj3, BF16, comm=scatter_hook(part))
    reduce["a_out"] = (part[0], recv[0])
    part = _pre_reduce("a_in", [("a_in", g_a_in)], core)
    dh1, recv = mm_nt_r("l0_in_dgrad", dproj3, w["a_in"], F32, comm=scatter_hook(part))
    reduce["a_in"] = (part[0], recv[0])
    dx0, dsh, dsc, d_nmix0 = norm_mod_bwd("l0_mix_norm_bwd", dh1, x, _row(norm_mix, 0), _row(mod0, 1), dx1)
    dmod0 = jnp.concatenate([dsh, dsc, dgate0, *dmlp0, dgate_mlp0], axis=0)
    small = dict(norm_mix=(d_nmix0, d_nmix1), norm_mlp=(d_nmlp0, d_nmlp1), kv_norm=d_nkv,
                 final_norm=d_final, lb=d_lb, out_gain=d_out_gain)
    return loss, dx0, reduce, dmod0, dmod1, dkvmod, small


BIG = ("a_in", "w1_0", "w1_1", "w2_0", "w2_1", "a_out", "bq", "bo", "kv")


def kernel(x, c, ada_w, ada_b, norm_mix, norm_mlp, a_w_in, a_lb_logits, a_out_gain, a_w_out, kv_ada_w, kv_ada_b, kv_norm, w_kv, b_w_q, b_w_out, mlp_w1, mlp_w2, final_norm, loss_target, m_ada_w, m_ada_b, m_norm_mix, m_norm_mlp, m_a_w_in, m_a_lb_logits, m_a_out_gain, m_a_w_out, m_kv_ada_w, m_kv_ada_b, m_kv_norm, m_w_kv, m_b_w_q, m_b_w_out, m_mlp_w1, m_mlp_w2, m_final_norm, v_ada_w, v_ada_b, v_norm_mix, v_norm_mlp, v_a_w_in, v_a_lb_logits, v_a_out_gain, v_a_w_out, v_kv_ada_w, v_kv_ada_b, v_kv_norm, v_w_kv, v_b_w_q, v_b_w_out, v_mlp_w1, v_mlp_w2, v_final_norm):
    d = x.shape[-1]
    px, py, pc = _position()
    me = 4 * px + 2 * py + pc
    chip = 2 * px + py
    n_ada = ada_w.shape[2]
    n_kvada = kv_ada_w.shape[1]
    shard_cols = d // N_CHIPS

    def as_rows(a):
        return a.reshape(-1, shard_cols)

    pack1 = jnp.concatenate([as_rows(c), a_lb_logits, a_out_gain,
                             jnp.zeros((1, shard_cols), F32)], axis=0)
    got1 = allgather8("gather_cond", pack1)
    n_c = d // shard_cols
    c_all = got1[:, :n_c, :].reshape(N_DEV, d)
    per_chip = got1[0::2]
    logits_full = jnp.swapaxes(per_chip[:, n_c:n_c + 2, :], 0, 1).reshape(2, d)
    out_gain_full = per_chip[:, n_c + 2, :].reshape(1, d)
    lb = lower_bound_fwd("lower_bound", logits_full)

    bias0 = lax.dynamic_slice_in_dim(ada_b, chip * n_ada, n_ada, axis=1)
    bias_kv = lax.dynamic_slice_in_dim(kv_ada_b.reshape(1, -1), chip * n_kvada, n_kvada, axis=1)
    mod_part = jnp.concatenate([
        ada_project("ada_proj_0", c_all, ada_w[0], bias0[0:1]),
        ada_project("ada_proj_1", c_all, ada_w[1], bias0[1:2]),
        ada_project("ada_proj_kv", c_all, kv_ada_w, bias_kv)], axis=1)
    got2 = allgather8("gather_mod", mod_part)
    mine = lax.dynamic_index_in_dim(got2[0::2], me, axis=1, keepdims=False)
    mod0 = mine[:, :n_ada].reshape(6, d)
    mod1 = mine[:, n_ada:2 * n_ada].reshape(6, d)
    kvmod = mine[:, 2 * n_ada:].reshape(2, d)

    shards = dict(a_in=a_w_in[0], w1_0=mlp_w1[0], w1_1=mlp_w1[1], w2_0=mlp_w2[0], w2_1=mlp_w2[1],
                  a_out=a_w_out[0], bq=b_w_q[0], bo=b_w_out[0], kv=w_kv)
    shards = {k: s.astype(BF16) for k, s in shards.items()}
    g_in, g_out = gather_weights("gather_first_weights", [shards["a_in"], shards["a_out"]])
    core = pc.astype(jnp.int32).reshape(1)

    loss, dx0, reduce, dmod0, dmod1, dkvmod, small = local_step(
        x[0], loss_target[0], mod0, mod1, kvmod, norm_mix, norm_mlp, kv_norm.reshape(1, d),
        final_norm.reshape(1, d), lb, out_gain_full,
        dict(a_in=g_in, a_out=g_out.reshape(-1, g_out.shape[-1])), shards, core)

    loss_row = jnp.concatenate([loss, jnp.zeros((1, shard_cols - loss.shape[1]), F32)], axis=1)
    rows = [as_rows(dmod0), as_rows(dmod1), as_rows(dkvmod),
            as_rows(small["norm_mix"][0]), as_rows(small["norm_mix"][1]),
            as_rows(small["norm_mlp"][0]), as_rows(small["norm_mlp"][1]),
            as_rows(small["kv_norm"]), as_rows(small["final_norm"]),
            as_rows(small["lb"]), as_rows(small["out_gain"]), loss_row]
    n_rows = sum(r.shape[0] for r in rows)
    pad = (-n_rows) % 8
    pack3 = jnp.concatenate(rows + [jnp.zeros((pad, shard_cols), F32)], axis=0)
    got3 = allgather8("gather_small_grads", pack3)
    total = device_sum("sum_small_grads", got3)
    n_mod_rows = (12 * d + 2 * d) // shard_cols
    n_gain_rows = 6 * n_c
    loss_out = total[n_mod_rows + n_gain_rows + 2 * n_c, 0]

    dmod_all = got3[:, :n_mod_rows, :].reshape(N_DEV, 14 * d)
    act_t = jnp.zeros((d, 128), F32).at[:, :N_DEV].set(c_all.T)

    def dmod_cols(lo, width):
        part = lax.dynamic_slice_in_dim(dmod_all, lo + chip * width, width, axis=1)
        return jnp.zeros((128, width), F32).at[:N_DEV].set(part)

    g_ada_w, d_ada_w, nm_ada_w, nv_ada_w = ada_grad_adamw(
        "ada_update", act_t, jnp.stack([dmod_cols(0, n_ada), dmod_cols(6 * d, n_ada)]), ada_w, m_ada_w, v_ada_w)
    g_kv_ada_w, d_kv_ada_w, nm_kv_ada_w, nv_kv_ada_w = (a[0] for a in ada_grad_adamw(
        "ada_update_kv", act_t, dmod_cols(12 * d, n_kvada)[None], kv_ada_w[None], m_kv_ada_w[None],
        v_kv_ada_w[None]))

    chip_parts = [reduce[k][0] for k in BIG]
    from_chips = [reduce[k][1] for k in BIG]
    where = jnp.stack([chip, pc]).astype(jnp.int32)
    out_shapes = [_sds(a_w_in.shape, F32), _sds(mlp_w1.shape, F32), _sds(mlp_w2.shape, F32),
                  _sds(a_w_out.shape, F32), _sds(b_w_q.shape, F32), _sds(b_w_out.shape, F32),
                  _sds(w_kv.shape, F32)]
    targets = [(0, 0), (1, 0), (1, 1), (2, 0), (2, 1), (3, 0), (4, 0), (5, 0), (6, None)]
    slabs = [None] * len(out_shapes)
    places = []
    for k, part, recv, (oi, lead) in zip(BIG, chip_parts, from_chips, targets):
        slabs[oi] = chip_sum("grad_chip_sum_" + k, part, recv, where, out_shapes[oi], lead, slabs[oi])
        places.append((oi, lead, part.shape[1]))
    g_a_w_in, g_mlp_w1, g_mlp_w2, g_a_w_out, g_b_w_q, g_b_w_out, g_w_kv = sibling_share(
        "grad_sibling_share", slabs, places)

    def update(name, wgt, g, m_, v_):
        shape = wgt.shape
        f = lambda a: a.reshape(-1, shape[-1])
        return tuple(o.reshape(shape) for o in adamw(name, f(wgt), f(g), f(m_), f(v_)))

    u_a_w_in = update("adamw_a_w_in", a_w_in, g_a_w_in, m_a_w_in, v_a_w_in)
    u_mlp_w1 = update("adamw_mlp_w1", mlp_w1, g_mlp_w1, m_mlp_w1, v_mlp_w1)
    u_mlp_w2 = update("adamw_mlp_w2", mlp_w2, g_mlp_w2, m_mlp_w2, v_mlp_w2)
    u_a_w_out = update("adamw_a_w_out", a_w_out, g_a_w_out, m_a_w_out, v_a_w_out)
    u_b_w_q = update("adamw_b_w_q", b_w_q, g_b_w_q, m_b_w_q, v_b_w_q)
    u_b_w_out = update("adamw_b_w_out", b_w_out, g_b_w_out, m_b_w_out, v_b_w_out)
    u_w_kv = update("adamw_w_kv", w_kv, g_w_kv, m_w_kv, v_w_kv)

    base = n_mod_rows + n_gain_rows
    d_lb_mine = lax.dynamic_slice_in_dim(total, base + chip, 1, axis=0)
    d_gain_mine = lax.dynamic_slice_in_dim(total, base + n_c + chip, 1, axis=0)
    d_logits = lower_bound_bwd("lower_bound_bwd", a_lb_logits, d_lb_mine)

    def pack_small(ada_b_, kv_ada_b_, norm_mix_, norm_mlp_, kv_norm_, final_norm_, lbl_, gain_):
        parts = [as_rows(ada_b_), as_rows(kv_ada_b_), as_rows(norm_mix_), as_rows(norm_mlp_),
                 as_rows(kv_norm_), as_rows(final_norm_), lbl_, gain_]
        n = sum(p.shape[0] for p in parts)
        return jnp.concatenate(parts + [jnp.zeros(((-n) % 8, shard_cols), F32)], axis=0)

    w_small = pack_small(ada_b, kv_ada_b, norm_mix, norm_mlp, kv_norm, final_norm, a_lb_logits, a_out_gain)
    m_small = pack_small(m_ada_b, m_kv_ada_b, m_norm_mix, m_norm_mlp, m_kv_norm, m_final_norm,
                         m_a_lb_logits, m_a_out_gain)
    v_small = pack_small(v_ada_b, v_kv_ada_b, v_norm_mix, v_norm_mlp, v_kv_norm, v_final_norm,
                         v_a_lb_logits, v_a_out_gain)
    n_small = w_small.shape[0]
    g_small = jnp.concatenate([total[:base], d_logits, d_gain_mine,
                               jnp.zeros((n_small - base - 3, shard_cols), F32)], axis=0)
    small_out = (g_small,) + tuple(adamw("adamw_small", w_small, g_small, m_small, v_small))

    def unpack_small(p):
        out, r0 = [], 0
        for ref in (ada_b, kv_ada_b, norm_mix, norm_mlp, kv_norm, final_norm, a_lb_logits, a_out_gain):
            nr = ref.size // shard_cols
            out.append(p[r0:r0 + nr].reshape(ref.shape))
            r0 += nr
        return out

    sm = [unpack_small(p) for p in small_out]

    def leaves(kind, big_ada, big_kv_ada):
        ada_b_, kv_ada_b_, norm_mix_, norm_mlp_, kv_norm_, final_norm_, lbl_, gain_ = sm[kind]
        pick = (lambda u, g: g) if kind == 0 else (lambda u, g: u[kind - 1])
        return [big_ada, ada_b_, norm_mix_, norm_mlp_, pick(u_a_w_in, g_a_w_in), lbl_, gain_,
                pick(u_a_w_out, g_a_w_out), big_kv_ada, kv_ada_b_, kv_norm_, pick(u_w_kv, g_w_kv),
                pick(u_b_w_q, g_b_w_q), pick(u_b_w_out, g_b_w_out), pick(u_mlp_w1, g_mlp_w1),
                pick(u_mlp_w2, g_mlp_w2), final_norm_]

    return (loss_out, dx0[None],
            *leaves(0, g_ada_w, g_kv_ada_w), *leaves(1, d_ada_w, d_kv_ada_w),
            *leaves(2, nm_ada_w, nm_kv_ada_w), *leaves(3, nv_ada_w, nv_kv_ada_w))
```
